```python
import math
import jax
import jax.numpy as jnp
from jax import lax
import numpy as np

D_MODEL = 2048
BATCH = 4
SEQ = 4096
DEPTH = 2

HEAD_DIM = 128
N_HEADS = D_MODEL // HEAD_DIM
MIX_WIDTH = N_HEADS * HEAD_DIM
HEADS_A = N_HEADS // 2
HEADS_B = N_HEADS // 4
HEADS_C = N_HEADS - HEADS_A - HEADS_B
DILATED_PATTERNS = ((128, 1), (512, 4), (2048, 16))
DIFF_HEAD_DIM = HEAD_DIM // 2
DIFF_QBLK = 128
GRID_W = 64
NA_ROWS = 8
NA_COLS = 16
N_EXPERTS = 64
TOP_K = 8
N_GROUPS = 8
TOPK_GROUPS = 4
EXPERT_DIM = D_MODEL // 4
SHARED_DIM = EXPERT_DIM
ROUTED_SCALE = 2.5
MOE_BLOCK = 256
NORM_EPS = 1e-6
SUBLN_EPS = 1e-5
NEG_INF = -1e30

kernel_name = "hybrid_dilated_diff_neighborhood_moe_encoder"

F32 = jnp.float32


def rms_norm(x, g, eps=NORM_EPS):
    xf = x.astype(F32)
    y = xf * lax.rsqrt(jnp.mean(xf * xf, axis=-1, keepdims=True) + eps)
    return (y * g.astype(F32)).astype(x.dtype)


def modulate(h, shift, scale):
    return h * (1 + scale[:, None, :]) + shift[:, None, :]


def alibi_slopes(n):
    def pow2(m):
        start = 2.0 ** (-8.0 / m)
        return [start ** (i + 1) for i in range(m)]
    p = 2 ** int(math.floor(math.log2(n)))
    s = pow2(p) + pow2(2 * p)[0::2][: n - p]
    return jnp.asarray(s, dtype=F32)


def dilated_branch(q, k, v, slopes, window, dilation):
    B, H, S, hd = q.shape
    half = window // (2 * dilation)
    L = S // dilation
    nb = -(-L // half)
    Lp = nb * half

    def to_sub(z):
        return z.reshape(B, H, L, dilation, hd).transpose(0, 1, 3, 2, 4)

    qs = jnp.pad(to_sub(q), ((0, 0), (0, 0), (0, 0), (0, Lp - L), (0, 0)))
    qs = qs.reshape(B, H, dilation, nb, half, hd)

    def key_blocks(z):
        zp = jnp.pad(to_sub(z), ((0, 0), (0, 0), (0, 0), (half, Lp - L + half), (0, 0)))
        zp = zp.reshape(B, H, dilation, nb + 2, half, hd)
        return jnp.concatenate([zp[:, :, :, :-2], zp[:, :, :, 1:-1], zp[:, :, :, 2:]], axis=4)

    kb = key_blocks(k)
    vb = key_blocks(v)
    qi = jnp.arange(nb)[:, None] * half + jnp.arange(half)[None, :]
    ki = jnp.arange(nb)[:, None] * half - half + jnp.arange(3 * half)[None, :]
    rel = jnp.abs(ki[:, None, :] - qi[:, :, None])
    valid = (rel <= half) & (ki[:, None, :] >= 0) & (ki[:, None, :] < L)
    dist = (rel * dilation).astype(F32)
    s = jnp.einsum('bhrnqd,bhrnkd->bhrnqk', qs, kb).astype(F32) * (hd ** -0.5)
    s = s - slopes[None, :, None, None, None, None] * dist
    s = jnp.where(valid, s, NEG_INF)
    lse = jax.nn.logsumexp(s, axis=-1)
    p = jnp.exp(s - lse[..., None])
    o = jnp.einsum('bhrnqk,bhrnkd->bhrnqd', p.astype(v.dtype), vb)
    o = o.reshape(B, H, dilation, Lp, hd)[:, :, :, :L].transpose(0, 1, 3, 2, 4).reshape(B, H, S, hd)
    lse = lse.reshape(B, H, dilation, Lp)[:, :, :, :L].transpose(0, 1, 3, 2).reshape(B, H, S)
    return o, lse


def dilated_mixture_attention(q, k, v, slopes):
    outs, lses = [], []
    for window, dilation in DILATED_PATTERNS:
        o_p, lse_p = dilated_branch(q, k, v, slopes, window, dilation)
        outs.append(o_p)
        lses.append(lse_p)
    wts = jax.nn.softmax(jnp.stack(lses, axis=0), axis=0)
    o = jnp.sum(wts[..., None] * jnp.stack(outs, axis=0).astype(F32), axis=0)
    return o.astype(v.dtype)


def differential_attention(q, k, v, slopes, lam, lambda_init, subln_g):
    B, H, S, _, dh = q.shape
    nq = S // DIFF_QBLK
    qb = q.reshape(B, H, nq, DIFF_QBLK, 2, dh).transpose(2, 0, 1, 3, 4, 5)
    kpos = jnp.arange(S)

    def block(args):
        qi, start = args
        s = jnp.einsum('bhqcd,bhkcd->bhcqk', qi, k).astype(F32) * (dh ** -0.5)
        dist = jnp.abs(start + jnp.arange(DIFF_QBLK)[:, None] - kpos[None, :]).astype(F32)
        s = s - slopes[None, :, None, None, None] * dist
        p = jax.nn.softmax(s, axis=-1)
        a = p[:, :, 0] - lam * p[:, :, 1]
        return jnp.einsum('bhqk,bhkd->bhqd', a.astype(v.dtype), v)

    o = lax.map(block, (qb, jnp.arange(nq) * DIFF_QBLK))
    o = o.transpose(1, 2, 0, 3, 4).reshape(B, H, S, 2 * dh)
    o = rms_norm(o, subln_g, SUBLN_EPS).astype(F32) * (1.0 - lambda_init)
    return o.astype(v.dtype)


def neighborhood_attention(q, k, v, rpb):
    B, H, S, hd = q.shape
    R = S // GRID_W
    kr = min(NA_ROWS, R)
    r = jnp.arange(R)
    rows_idx = jnp.clip(r - kr // 2, 0, R - kr)[:, None] + jnp.arange(kr)[None, :]
    col = jnp.arange(GRID_W)
    cstart = jnp.clip(col - NA_COLS // 2, 0, GRID_W - NA_COLS)
    col_ok = (col[None, :] >= cstart[:, None]) & (col[None, :] < cstart[:, None] + NA_COLS)
    dr_idx = rows_idx - r[:, None] + NA_ROWS - 1
    dc_idx = jnp.clip(col[None, :] - col[:, None] + NA_COLS - 1, 0, 2 * NA_COLS - 2)
    bias = rpb[:, dr_idx[:, None, :, None], dc_idx[None, :, None, :]].astype(F32)
    qg = q.reshape(B, H, R, GRID_W, hd)
    kg = k.reshape(B, H, R, GRID_W, hd)[:, :, rows_idx]
    vg = v.reshape(B, H, R, GRID_W, hd)[:, :, rows_idx]
    s = jnp.einsum('bhrqd,bhrjkd->bhrqjk', qg, kg).astype(F32) * (hd ** -0.5) + bias[None]
    s = jnp.where(col_ok[:, None, :], s, NEG_INF)
    p = jax.nn.softmax(s.reshape(B, H, R, GRID_W, kr * GRID_W), axis=-1).reshape(s.shape)
    o = jnp.einsum('bhrqjk,bhrjkd->bhrqd', p.astype(v.dtype), vg)
    return o.reshape(B, H, S, hd)


def moe_ffn(h, router_w, router_bias, w_gate, w_up, w_down, s_gate, s_up, s_down):
    N, D = h.shape
    E = w_gate.shape[0]
    scores = jax.nn.sigmoid((h @ router_w).astype(F32))
    sel = scores + router_bias.astype(F32)
    grp_score = jnp.sum(lax.top_k(sel.reshape(N, N_GROUPS, E // N_GROUPS), 2)[0], axis=-1)
    _, gidx = lax.top_k(grp_score, TOPK_GROUPS)
    gmask = jnp.any(gidx[:, :, None] == jnp.arange(N_GROUPS)[None, None, :], axis=1)
    sel = jnp.where(jnp.repeat(gmask, E // N_GROUPS, axis=1), sel, NEG_INF)
    _, idx = lax.top_k(sel, TOP_K)
    gate = jnp.take_along_axis(scores, idx, axis=-1)
    gate = gate / jnp.sum(gate, axis=-1, keepdims=True) * ROUTED_SCALE

    A = N * TOP_K
    e_flat = idx.reshape(A)
    order = jnp.argsort(e_flat)
    e_sorted = e_flat[order]
    tok_sorted = (order // TOP_K).astype(jnp.int32)
    gate_sorted = gate.reshape(A)[order]
    counts = jnp.bincount(e_flat, length=E)
    starts = jnp.cumsum(counts) - counts
    padded = (counts + MOE_BLOCK - 1) // MOE_BLOCK * MOE_BLOCK
    pend = jnp.cumsum(padded)
    pstart = pend - padded
    dest = pstart[e_sorted] + jnp.arange(A) - starts[e_sorted]
    n_blocks = -(-(A + E * (MOE_BLOCK - 1)) // MOE_BLOCK)
    P = n_blocks * MOE_BLOCK
    tok_buf = jnp.zeros((P,), jnp.int32).at[dest].set(tok_sorted)
    gate_buf = jnp.zeros((P,), F32).at[dest].set(gate_sorted)
    block_expert = jnp.minimum(
        jnp.searchsorted(pend, jnp.arange(n_blocks) * MOE_BLOCK, side='right'), E - 1)

    def expert_block(args):
        rows, gts, e = args
        xb = h[rows]
        u = jax.nn.silu(xb @ w_gate[e]) * (xb @ w_up[e])
        return (u @ w_down[e]).astype(F32) * gts[:, None]

    y = lax.map(expert_block, (tok_buf.reshape(n_blocks, MOE_BLOCK),
                               gate_buf.reshape(n_blocks, MOE_BLOCK), block_expert))
    routed = jnp.zeros((N, D), F32).at[tok_buf].add(y.reshape(P, D))
    shared = (jax.nn.silu(h @ s_gate) * (h @ s_up)) @ s_down
    return (routed + shared.astype(F32)).astype(h.dtype)


def setup_inputs(seed: int = 0) -> dict:
    key = jax.random.key(seed)
    ks = jax.random.split(key, 32)
    L, D = DEPTH, D_MODEL

    def nrm(k, shape, scale):
        return jax.random.normal(k, shape, F32) * scale

    return {
        'x': nrm(ks[0], (BATCH, SEQ, D), 1.0),
        'c': nrm(ks[1], (BATCH, D), 1.0),
        'ada_w': nrm(ks[2], (L, D, 6 * D), 0.5 * D ** -0.5),
        'ada_b': nrm(ks[3], (L, 6 * D), 0.02),
        'norm_mix_g': 1.0 + nrm(ks[4], (L, D), 0.02),
        'norm_ffn_g': 1.0 + nrm(ks[5], (L, D), 0.02),
        'w_in': nrm(ks[6], (L, D, 3 * MIX_WIDTH), D ** -0.5),
        'lambda_q1': nrm(ks[7], (L, DIFF_HEAD_DIM), 0.1),
        'lambda_k1': nrm(ks[8], (L, DIFF_HEAD_DIM), 0.1),
        'lambda_q2': nrm(ks[9], (L, DIFF_HEAD_DIM), 0.1),
        'lambda_k2': nrm(ks[10], (L, DIFF_HEAD_DIM), 0.1),
        'subln_g': 1.0 + nrm(ks[11], (L, HEAD_DIM), 0.02),
        'rpb': nrm(ks[12], (L, HEADS_C, 2 * NA_ROWS - 1, 2 * NA_COLS - 1), 0.1),
        'w_out': nrm(ks[13], (L, MIX_WIDTH, D), MIX_WIDTH ** -0.5),
        'router_w': nrm(ks[14], (L, D, N_EXPERTS), D ** -0.5),
        'router_bias': nrm(ks[15], (L, N_EXPERTS), 0.01),
        'exp_w_gate': nrm(ks[16], (L, N_EXPERTS, D, EXPERT_DIM), D ** -0.5),
        'exp_w_up': nrm(ks[17], (L, N_EXPERTS, D, EXPERT_DIM), D ** -0.5),
        'exp_w_down': nrm(ks[18], (L, N_EXPERTS, EXPERT_DIM, D), EXPERT_DIM ** -0.5),
        'sh_w_gate': nrm(ks[19], (L, D, SHARED_DIM), D ** -0.5),
        'sh_w_up': nrm(ks[20], (L, D, SHARED_DIM), D ** -0.5),
        'sh_w_down': nrm(ks[21], (L, SHARED_DIM, D), SHARED_DIM ** -0.5),
        'final_g': 1.0 + nrm(ks[22], (D,), 0.02),
    }


def reference(x, c, ada_w, ada_b, norm_mix_g, norm_ffn_g, w_in, lambda_q1, lambda_k1,
              lambda_q2, lambda_k2, subln_g, rpb, w_out, router_w, router_bias,
              exp_w_gate, exp_w_up, exp_w_down, sh_w_gate, sh_w_up, sh_w_down, final_g):
    B, S, D = x.shape
    slopes = alibi_slopes(HEADS_A + HEADS_B)
    slopes_a, slopes_b = slopes[:HEADS_A], slopes[HEADS_A:]
    cond = jax.nn.silu(c)
    for l in range(DEPTH):
        mod = cond @ ada_w[l] + ada_b[l]
        sh1, sc1, g1, sh2, sc2, g2 = jnp.split(mod, 6, axis=-1)

        h = modulate(rms_norm(x, norm_mix_g[l]), sh1, sc1)
        qkv = (h @ w_in[l]).reshape(B, S, 3, N_HEADS, HEAD_DIM).transpose(2, 0, 3, 1, 4)
        q, k, v = qkv[0], qkv[1], qkv[2]
        a0, b0 = HEADS_A, HEADS_A + HEADS_B

        o_a = dilated_mixture_attention(q[:, :a0], k[:, :a0], v[:, :a0], slopes_a)

        lambda_init = 0.8 - 0.6 * math.exp(-0.3 * l)
        lam = (jnp.exp(jnp.sum(lambda_q1[l].astype(F32) * lambda_k1[l].astype(F32)))
               - jnp.exp(jnp.sum(lambda_q2[l].astype(F32) * lambda_k2[l].astype(F32)))
               + lambda_init)
        qd = q[:, a0:b0].reshape(B, HEADS_B, S, 2, DIFF_HEAD_DIM)
        kd = k[:, a0:b0].reshape(B, HEADS_B, S, 2, DIFF_HEAD_DIM)
        o_b = differential_attention(qd, kd, v[:, a0:b0], slopes_b, lam, lambda_init, subln_g[l])

        o_c = neighborhood_attention(q[:, b0:], k[:, b0:], v[:, b0:], rpb[l])

        o = jnp.concatenate([o_a, o_b, o_c], axis=1).transpose(0, 2, 1, 3).reshape(B, S, MIX_WIDTH)
        x = x + g1[:, None, :] * (o @ w_out[l])

        h2 = modulate(rms_norm(x, norm_ffn_g[l]), sh2, sc2)
        f = moe_ffn(h2.reshape(B * S, D), router_w[l], router_bias[l], exp_w_gate[l], exp_w_up[l],
                    exp_w_down[l], sh_w_gate[l], sh_w_up[l], sh_w_down[l])
        x = x + g2[:, None, :] * f.reshape(B, S, D)
    return rms_norm(x, final_g)
```

```python
import functools
import math

import jax
import jax.numpy as jnp
from jax import lax
from jax.experimental import pallas as pl
from jax.experimental.pallas import tpu as pltpu

F32 = jnp.float32
BF16 = jnp.bfloat16
I32 = jnp.int32
HIGHEST = lax.Precision.HIGHEST
_NT = (((1,), (1,)), ((), ()))

HEAD_DIM = 128
HEADS_A_FRAC, HEADS_B_FRAC = 2, 4
DILATED_PATTERNS = ((128, 1), (512, 4), (2048, 16))
GRID_W = 64
NA_ROWS = 8
NA_COLS = 16
N_GROUPS = 8
TOPK_GROUPS = 4
TOP_K = 8
ROUTED_SCALE = 2.5
NORM_EPS = 1e-6
SUBLN_EPS = 1e-5
NEG_INF = -1e30

LANES = 128
VMEM_BYTES_V7X = 64 << 20

MOE_BLOCK = 256


def _params(semantics, vmem_mib):
    return pltpu.CompilerParams(dimension_semantics=semantics,
                                vmem_limit_bytes=min(vmem_mib << 20, VMEM_BYTES_V7X - (4 << 20)))


def _alibi_slopes(n):
    def pow2(m):
        start = 2.0 ** (-8.0 / m)
        return [start ** (i + 1) for i in range(m)]
    p = 2 ** int(math.floor(math.log2(n)))
    return pow2(p) + pow2(2 * p)[0::2][: n - p]


def _rms(x, eps):
    return x * lax.rsqrt(jnp.mean(x * x, axis=-1, keepdims=True) + eps)


def _ada_kernel(c_ref, w_ref, b_ref, o_ref):
    c = c_ref[...]
    sc = c * jax.nn.sigmoid(c)
    o_ref[0] = jnp.dot(sc, w_ref[0], preferred_element_type=F32, precision=HIGHEST) + b_ref[0]


def _ada(c, ada_w, ada_b):
    B, D = c.shape
    L, _, D6 = ada_w.shape
    rows = 8
    cp = jnp.zeros((rows, D), F32).at[:B].set(c)
    tn = 1024
    out = pl.pallas_call(
        _ada_kernel,
        grid=(L, D6 // tn),
        in_specs=[pl.BlockSpec((rows, D), lambda l, j: (0, 0)),
                  pl.BlockSpec((1, D, tn), lambda l, j: (l, 0, j)),
                  pl.BlockSpec((1, 1, tn), lambda l, j: (l, 0, j))],
        out_specs=pl.BlockSpec((1, rows, tn), lambda l, j: (l, 0, j)),
        out_shape=jax.ShapeDtypeStruct((L, rows, D6), F32),
        compiler_params=_params(("arbitrary", "arbitrary"), 32),
        name="ada_mod",
    )(cp, ada_w, ada_b.reshape(L, 1, D6))
    return out[:, :B]


def _qkv_kernel(x_ref, g_ref, sc_ref, sh_ref, w_ref, o_ref, h_scr, *, heads_per_step):
    @pl.when(pl.program_id(1) == 0)
    def _():
        y = _rms(x_ref[...], NORM_EPS) * g_ref[...]
        h_scr[...] = (y * (1.0 + sc_ref[0]) + sh_ref[0]).astype(BF16)

    res = jnp.dot(h_scr[...], w_ref[...], preferred_element_type=F32)
    for hh in range(heads_per_step):
        o_ref[hh] = res[:, hh * HEAD_DIM:(hh + 1) * HEAD_DIM].astype(BF16)


def _qkv(x2d, g, sc, sh, w_bf, S, tm=512, tn=1536):
    N, D = x2d.shape
    W3 = w_bf.shape[1]
    hps = tn // HEAD_DIM
    tpb = S // tm
    return pl.pallas_call(
        functools.partial(_qkv_kernel, heads_per_step=hps),
        grid=(N // tm, W3 // tn),
        in_specs=[pl.BlockSpec((tm, D), lambda i, j: (i, 0)),
                  pl.BlockSpec((1, D), lambda i, j: (0, 0)),
                  pl.BlockSpec((1, 1, D), lambda i, j: (i // tpb, 0, 0)),
                  pl.BlockSpec((1, 1, D), lambda i, j: (i // tpb, 0, 0)),
                  pl.BlockSpec((D, tn), lambda i, j: (0, j))],
        out_specs=pl.BlockSpec((hps, tm, HEAD_DIM), lambda i, j: (j, i, 0)),
        out_shape=jax.ShapeDtypeStruct((W3 // HEAD_DIM, N, HEAD_DIM), BF16),
        scratch_shapes=[pltpu.VMEM((tm, D), BF16)],
        compiler_params=_params(("arbitrary", "arbitrary"), 48),
        name="qkv_proj",
    )(x2d, g.reshape(1, D), sc, sh, w_bf)


def _attn_a_kernel(slopes_ref, q_ref, k_ref, v_ref, o_ref, tab_ref, *, S, Tq, reach):
    h = pl.program_id(1)
    i = pl.program_id(2)
    wk = Tq + 2 * reach
    max_off = wk - Tq

    @pl.when(i == 0)
    def _():
        shape = (Tq, wk + max_off)
        a = lax.broadcasted_iota(I32, shape, 0)
        u = lax.broadcasted_iota(I32, shape, 1)
        d = u - max_off - a
        ad = jnp.maximum(d, -d)
        mult = jnp.zeros(shape, F32)
        for window, dil in DILATED_PATTERNS:
            half = window // (2 * dil)
            mult = mult + jnp.where((ad <= half * dil) & ((ad & (dil - 1)) == 0), 1.0, 0.0)
        bias = jnp.log(jnp.maximum(mult, 1.0)) - slopes_ref[h] * ad.astype(F32)
        tab_ref[...] = jnp.where(mult > 0.0, bias, NEG_INF)

    t0 = i * Tq
    start = pl.multiple_of(jnp.clip(t0 - reach, 0, S - wk), LANES)
    off = t0 - start
    q = q_ref[0]
    kw = k_ref[0, pl.ds(start, wk), :]
    vw = v_ref[0, pl.ds(start, wk), :]
    s = lax.dot_general(q, kw, _NT, preferred_element_type=F32) * (HEAD_DIM ** -0.5)
    s = s + tab_ref[:, pl.ds(pl.multiple_of(max_off - off, LANES), wk)]
    m = jnp.max(s, axis=-1, keepdims=True)
    p = jnp.exp(s - m)
    l = jnp.sum(p, axis=-1, keepdims=True)
    o = jnp.dot(p.astype(BF16), vw, preferred_element_type=F32) / l
    o_ref[...] = o.astype(BF16)


def _attn_a(qkv, slopes, B, S, n_heads, heads_a, Tq=128):
    N = B * S
    reach = max((w // (2 * d)) * d for w, d in DILATED_PATTERNS)
    wk = Tq + 2 * reach
    assert wk <= S and reach % LANES == 0
    nq = S // Tq
    return pl.pallas_call(
        functools.partial(_attn_a_kernel, S=S, Tq=Tq, reach=reach),
        grid=(B, heads_a, nq),
        in_specs=[pl.BlockSpec(memory_space=pltpu.SMEM),
                  pl.BlockSpec((1, Tq, HEAD_DIM), lambda b, h, i: (h, b * nq + i, 0)),
                  pl.BlockSpec((1, S, HEAD_DIM), lambda b, h, i: (n_heads + h, b, 0)),
                  pl.BlockSpec((1, S, HEAD_DIM), lambda b, h, i: (2 * n_heads + h, b, 0))],
        out_specs=pl.BlockSpec((Tq, HEAD_DIM), lambda b, h, i: (b * nq + i, h)),
        out_shape=jax.ShapeDtypeStruct((N, heads_a * HEAD_DIM), BF16),
        scratch_shapes=[pltpu.VMEM((Tq, 2 * wk - Tq), F32)],
        compiler_params=_params(("arbitrary", "arbitrary", "arbitrary"), 32),
        name="attn_dilated",
    )(slopes, qkv, qkv, qkv)


def _attn_b_kernel(slopes_ref, lam_ref, q_ref, k_ref, v_ref, g_ref, o_ref, tab_ref, *,
                   S, Tq, lambda_init):
    h = pl.program_id(1)
    i = pl.program_id(2)
    dh = HEAD_DIM // 2

    @pl.when(i == 0)
    def _():
        shape = (Tq, 2 * S - Tq)
        a = lax.broadcasted_iota(I32, shape, 0)
        u = lax.broadcasted_iota(I32, shape, 1)
        d = a - u + (S - Tq)
        tab_ref[...] = -slopes_ref[h] * jnp.maximum(d, -d).astype(F32)

    lv = lam_ref[...]
    lam = (jnp.exp(jnp.sum(lv[0:1] * lv[1:2], axis=-1, keepdims=True))
           - jnp.exp(jnp.sum(lv[2:3] * lv[3:4], axis=-1, keepdims=True)) + lambda_init)

    t0 = i * Tq
    bias = tab_ref[:, pl.ds(pl.multiple_of(S - Tq - t0, LANES), S)]
    q = q_ref[0]
    k = k_ref[0]

    def softmax_part(c):
        s = lax.dot_general(q[:, c * dh:(c + 1) * dh], k[:, c * dh:(c + 1) * dh], _NT,
                            preferred_element_type=F32) * (dh ** -0.5) + bias
        m = jnp.max(s, axis=-1, keepdims=True)
        p = jnp.exp(s - m)
        return p, jnp.sum(p, axis=-1, keepdims=True)

    p1, l1 = softmax_part(0)
    p2, l2 = softmax_part(1)
    a = p1 * (1.0 / l1) - p2 * (lam / l2)
    o = jnp.dot(a.astype(BF16), v_ref[0], preferred_element_type=F32)
    o = _rms(o, SUBLN_EPS) * g_ref[...] * (1.0 - lambda_init)
    o_ref[...] = o.astype(BF16)


def _attn_b(qkv, slopes, lam_vecs, subln_g, B, S, n_heads, head0, heads_b, lambda_init, Tq=128):
    N = B * S
    nq = S // Tq
    return pl.pallas_call(
        functools.partial(_attn_b_kernel, S=S, Tq=Tq, lambda_init=lambda_init),
        grid=(B, heads_b, nq),
        in_specs=[pl.BlockSpec(memory_space=pltpu.SMEM),
                  pl.BlockSpec((4, HEAD_DIM // 2), lambda b, h, i: (0, 0)),
                  pl.BlockSpec((1, Tq, HEAD_DIM), lambda b, h, i: (head0 + h, b * nq + i, 0)),
                  pl.BlockSpec((1, S, HEAD_DIM), lambda b, h, i: (n_heads + head0 + h, b, 0)),
                  pl.BlockSpec((1, S, HEAD_DIM), lambda b, h, i: (2 * n_heads + head0 + h, b, 0)),
                  pl.BlockSpec((1, HEAD_DIM), lambda b, h, i: (0, 0))],
        out_specs=pl.BlockSpec((Tq, HEAD_DIM), lambda b, h, i: (b * nq + i, h)),
        out_shape=jax.ShapeDtypeStruct((N, heads_b * HEAD_DIM), BF16),
        scratch_shapes=[pltpu.VMEM((Tq, 2 * S - Tq), F32)],
        compiler_params=_params(("arbitrary", "arbitrary", "arbitrary"), 48),
        name="attn_diff",
    )(slopes, lam_vecs, qkv, qkv, qkv, subln_g.reshape(1, HEAD_DIM))


def _attn_c_kernel(rpb_ref, q_ref, k_ref, v_ref, o_ref, tab_ref, *, R):
    h = pl.program_id(1)
    W = GRID_W
    kr = min(NA_ROWS, R)
    n_dr = 2 * NA_ROWS - 1
    n_dc = 2 * NA_COLS - 1

    c_io = lax.broadcasted_iota(I32, (W, 2 * W), 0)
    l_io = lax.broadcasted_iota(I32, (W, 2 * W), 1)
    cp = l_io & (W - 1)
    dcm = cp - c_io + (NA_COLS - 1)
    cstart = jnp.clip(c_io - NA_COLS // 2, 0, W - NA_COLS)
    ok = (cp >= cstart) & (cp < cstart + NA_COLS)
    blocks = []
    for dr in range(n_dr):
        blk = jnp.zeros((W, 2 * W), F32)
        for dc in range(n_dc):
            blk = jnp.where(dcm == dc, rpb_ref[(h * n_dr + dr) * n_dc + dc], blk)
        blocks.append(jnp.where(ok, blk, NEG_INF))
    for o in range(NA_ROWS):
        for jj in range(kr // 2):
            tab_ref[o, :, jj * 2 * W:(jj + 1) * 2 * W] = jnp.where(
                l_io < W, blocks[o + 2 * jj], blocks[o + 2 * jj + 1])

    def row(r, carry):
        rs = jnp.clip(r - kr // 2, 0, R - kr)
        q = q_ref[0, pl.ds(pl.multiple_of(r * W, W), W), :]
        kw = k_ref[0, pl.ds(pl.multiple_of(rs * W, W), kr * W), :]
        vw = v_ref[0, pl.ds(pl.multiple_of(rs * W, W), kr * W), :]
        s = lax.dot_general(q, kw, _NT, preferred_element_type=F32) * (HEAD_DIM ** -0.5)
        s = s + tab_ref[rs - r + (NA_ROWS - 1)]
        m = jnp.max(s, axis=-1, keepdims=True)
        p = jnp.exp(s - m)
        l = jnp.sum(p, axis=-1, keepdims=True)
        o = jnp.dot(p.astype(BF16), vw, preferred_element_type=F32) / l
        o_ref[pl.ds(pl.multiple_of(r * W, W), W), :] = o.astype(BF16)
        return carry

    lax.fori_loop(0, R, row, 0)


def _attn_c(qkv, rpb_flat, B, S, n_heads, head0, heads_c):
    N = B * S
    R = S // GRID_W
    kr = min(NA_ROWS, R)
    assert kr == NA_ROWS and kr % 2 == 0
    return pl.pallas_call(
        functools.partial(_attn_c_kernel, R=R),
        grid=(B, heads_c),
        in_specs=[pl.BlockSpec(memory_space=pltpu.SMEM),
                  pl.BlockSpec((1, S, HEAD_DIM), lambda b, h: (head0 + h, b, 0)),
                  pl.BlockSpec((1, S, HEAD_DIM), lambda b, h: (n_heads + head0 + h, b, 0)),
                  pl.BlockSpec((1, S, HEAD_DIM), lambda b, h: (2 * n_heads + head0 + h, b, 0))],
        out_specs=pl.BlockSpec((S, HEAD_DIM), lambda b, h: (b, h)),
        out_shape=jax.ShapeDtypeStruct((N, heads_c * HEAD_DIM), BF16),
        scratch_shapes=[pltpu.VMEM((NA_ROWS, GRID_W, kr * GRID_W), F32)],
        compiler_params=_params(("arbitrary", "arbitrary"), 32),
        name="attn_nbr",
    )(rpb_flat, qkv, qkv, qkv)


def _out_kernel(oa_ref, ob_ref, oc_ref, wa_ref, wb_ref, wc_ref, x_ref, g1_ref, ng_ref, sc_ref,
                sh_ref, rw_ref, x1_ref, h2_ref, lg_ref):
    acc = jnp.dot(oa_ref[...], wa_ref[...], preferred_element_type=F32)
    acc = acc + jnp.dot(ob_ref[...], wb_ref[...], preferred_element_type=F32)
    acc = acc + jnp.dot(oc_ref[...], wc_ref[...], preferred_element_type=F32)
    x1 = x_ref[...] + g1_ref[0] * acc
    x1_ref[...] = x1
    h2 = (_rms(x1, NORM_EPS) * ng_ref[...]) * (1.0 + sc_ref[0]) + sh_ref[0]
    h2_ref[...] = h2
    lg_ref[...] = lax.dot_general(rw_ref[...], h2, _NT, preferred_element_type=F32,
                                  precision=HIGHEST)


def _out_proj(oa, ob, oc, w_bf, x2d, g1, ng, sc, sh, rw_t, S, tm=512):
    N, D = x2d.shape
    wa, wb, wc = oa.shape[1], ob.shape[1], oc.shape[1]
    E = rw_t.shape[0]
    tpb = S // tm
    const = dict(pipeline_mode=pl.Buffered(1))
    row = lambda i: (i, 0)
    per_b = lambda i: (i // tpb, 0, 0)
    return pl.pallas_call(
        _out_kernel,
        grid=(N // tm,),
        in_specs=[pl.BlockSpec((tm, wa), row), pl.BlockSpec((tm, wb), row), pl.BlockSpec((tm, wc), row),
                  pl.BlockSpec((wa, D), lambda i: (0, 0), **const),
                  pl.BlockSpec((wb, D), lambda i: (wa // wb, 0), **const),
                  pl.BlockSpec((wc, D), lambda i: ((wa + wb) // wc, 0), **const),
                  pl.BlockSpec((tm, D), row),
                  pl.BlockSpec((1, 1, D), per_b),
                  pl.BlockSpec((1, D), lambda i: (0, 0)),
                  pl.BlockSpec((1, 1, D), per_b),
                  pl.BlockSpec((1, 1, D), per_b),
                  pl.BlockSpec((E, D), lambda i: (0, 0), **const)],
        out_specs=[pl.BlockSpec((tm, D), row), pl.BlockSpec((tm, D), row),
                   pl.BlockSpec((E, tm), lambda i: (0, i))],
        out_shape=[jax.ShapeDtypeStruct((N, D), F32), jax.ShapeDtypeStruct((N, D), F32),
                   jax.ShapeDtypeStruct((E, N), F32)],
        compiler_params=_params(("arbitrary",), 52),
        name="out_proj",
    )(oa, ob, oc, w_bf, w_bf, w_bf, x2d, g1, ng.reshape(1, D), sc, sh, rw_t)


def _first_argmax(vals, iota, big):
    mx = jnp.max(vals, axis=0, keepdims=True)
    idx = jnp.min(jnp.where(vals == mx, iota, big), axis=0, keepdims=True)
    return mx, idx


def _route_kernel(lg_ref, rb_ref, idx_ref, gate_ref, rank_ref, cnt_ref, tri_ref, carry_ref, *, Tt):
    i = pl.program_id(0)
    E = lg_ref.shape[0]
    gs = E // N_GROUPS

    @pl.when(i == 0)
    def _():
        r = lax.broadcasted_iota(I32, (Tt, Tt), 0)
        c = lax.broadcasted_iota(I32, (Tt, Tt), 1)
        tri_ref[...] = jnp.where(r < c, 1.0, 0.0).astype(BF16)
        carry_ref[...] = jnp.zeros_like(carry_ref)

    scores = jax.nn.sigmoid(lg_ref[...])
    sel = scores + rb_ref[...]
    e_io = lax.broadcasted_iota(I32, (E, Tt), 0).astype(F32)
    s_io = lax.broadcasted_iota(I32, (gs, Tt), 0).astype(F32)
    g_io = lax.broadcasted_iota(I32, (N_GROUPS, Tt), 0).astype(F32)

    grp = []
    for g in range(N_GROUPS):
        blk = sel[g * gs:(g + 1) * gs]
        m1, a1 = _first_argmax(blk, s_io, float(gs))
        m2 = jnp.max(jnp.where(s_io == a1, -jnp.inf, blk), axis=0, keepdims=True)
        grp.append(m1 + m2)
    grp = jnp.concatenate(grp, axis=0)

    gsel = jnp.zeros((N_GROUPS, Tt), F32)
    for _ in range(TOPK_GROUPS):
        _, gi = _first_argmax(grp, g_io, float(N_GROUPS))
        hit = g_io == gi
        gsel = jnp.where(hit, 1.0, gsel)
        grp = jnp.where(hit, -jnp.inf, grp)

    esel = jnp.concatenate(
        [jnp.broadcast_to(gsel[g:g + 1], (gs, Tt)) for g in range(N_GROUPS)], axis=0)
    cur = jnp.where(esel > 0.0, sel, NEG_INF)

    idxs, gates = [], []
    onehot = jnp.zeros((E, Tt), F32)
    for _ in range(TOP_K):
        _, ei = _first_argmax(cur, e_io, float(E))
        hit = e_io == ei
        idxs.append(ei)
        gates.append(jnp.sum(jnp.where(hit, scores, 0.0), axis=0, keepdims=True))
        onehot = jnp.where(hit, 1.0, onehot)
        cur = jnp.where(hit, -jnp.inf, cur)
    gate = jnp.concatenate(gates, axis=0)
    gate = gate / jnp.sum(gate, axis=0, keepdims=True) * ROUTED_SCALE
    idx_ref[...] = jnp.concatenate(idxs, axis=0).astype(I32)
    gate_ref[...] = gate

    oh = onehot.astype(BF16)
    before = jnp.dot(oh, tri_ref[...], preferred_element_type=F32)
    tile_cnt = jnp.dot(oh, jnp.ones((Tt, LANES), BF16), preferred_element_type=F32)
    carry = carry_ref[...]
    rank_mat = before + jnp.concatenate([carry] * (Tt // LANES), axis=1)
    ranks = [jnp.sum(jnp.where(e_io == ei, rank_mat, 0.0), axis=0, keepdims=True) for ei in idxs]
    rank_ref[...] = jnp.concatenate(ranks, axis=0).astype(I32)
    carry_ref[...] = carry + tile_cnt
    cnt_ref[...] = (carry + tile_cnt).astype(I32)


def _route(logits_t, router_bias, Tt=512):
    E, N = logits_t.shape
    tok = lambda i: (0, i)
    return pl.pallas_call(
        functools.partial(_route_kernel, Tt=Tt),
        grid=(N // Tt,),
        in_specs=[pl.BlockSpec((E, Tt), tok), pl.BlockSpec((E, 1), lambda i: (0, 0))],
        out_specs=[pl.BlockSpec((TOP_K, Tt), tok), pl.BlockSpec((TOP_K, Tt), tok),
                   pl.BlockSpec((TOP_K, Tt), tok), pl.BlockSpec((E, LANES), lambda i: (0, 0))],
        out_shape=[jax.ShapeDtypeStruct((TOP_K, N), I32), jax.ShapeDtypeStruct((TOP_K, N), F32),
                   jax.ShapeDtypeStruct((TOP_K, N), I32), jax.ShapeDtypeStruct((E, LANES), I32)],
        scratch_shapes=[pltpu.VMEM((Tt, Tt), BF16), pltpu.VMEM((E, LANES), F32)],
        compiler_params=_params(("arbitrary",), 32),
        name="route_topk",
    )(logits_t, router_bias.reshape(E, 1))


def _dest_kernel(pstart_ref, idx_ref, rank_ref, dest_ref):
    idx = idx_ref[...]
    base = jnp.zeros(idx.shape, I32)
    for e in range(pstart_ref.shape[0]):
        base = jnp.where(idx == e, pstart_ref[e], base)
    dest_ref[...] = base + rank_ref[...]


def _dest(pstart, idx, rank):
    return pl.pallas_call(
        _dest_kernel,
        in_specs=[pl.BlockSpec(memory_space=pltpu.SMEM), pl.BlockSpec(memory_space=pltpu.VMEM),
                  pl.BlockSpec(memory_space=pltpu.VMEM)],
        out_specs=pl.BlockSpec(memory_space=pltpu.VMEM),
        out_shape=jax.ShapeDtypeStruct(idx.shape, I32),
        name="route_dest",
    )(pstart, idx, rank)


def _dispatch_kernel(dest_hbm, h_ref, xs_hbm, dest_smem, idx_sem, row_sem, *, Td):
    i = pl.program_id(0)
    n = Td * TOP_K
    cp = pltpu.make_async_copy(dest_hbm.at[pl.ds(i * n, n)], dest_smem, idx_sem)
    cp.start()
    cp.wait()

    def row_copy(t, k):
        return pltpu.make_async_copy(h_ref.at[pl.ds(t, 1)],
                                     xs_hbm.at[pl.ds(dest_smem[t * TOP_K + k], 1)], row_sem)

    def issue(t, carry):
        for k in range(TOP_K):
            row_copy(t, k).start()
        return carry

    lax.fori_loop(0, Td, issue, 0)

    def drain(t, carry):
        for k in range(TOP_K):
            row_copy(t, k).wait()
        return carry

    lax.fori_loop(0, Td, drain, 0)


def _dispatch(dest_flat, h2, P, Td=256):
    N, D = h2.shape
    return pl.pallas_call(
        functools.partial(_dispatch_kernel, Td=Td),
        grid=(N // Td,),
        in_specs=[pl.BlockSpec(memory_space=pl.ANY), pl.BlockSpec((Td, D), lambda i: (i, 0))],
        out_specs=pl.BlockSpec(memory_space=pl.ANY),
        out_shape=jax.ShapeDtypeStruct((P, D), h2.dtype),
        scratch_shapes=[pltpu.SMEM((Td * TOP_K,), I32), pltpu.SemaphoreType.DMA,
                        pltpu.SemaphoreType.DMA],
        compiler_params=_params(("arbitrary",), 32),
        name="moe_dispatch",
    )(dest_flat, h2)


def _moe_kernel(be_ref, nv_ref, nb_ref, xs_ref, wg_ref, wu_ref, wd_ref, ys_ref):
    b = pl.program_id(0)

    @pl.when(b < nb_ref[0])
    def _():
        rows = lax.broadcasted_iota(I32, xs_ref.shape, 0)
        x = jnp.where(rows < nv_ref[b], xs_ref[...], 0.0).astype(BF16)
        g = jnp.dot(x, wg_ref[0], preferred_element_type=F32)
        u = jnp.dot(x, wu_ref[0], preferred_element_type=F32)
        a = (g * jax.nn.sigmoid(g) * u).astype(BF16)
        ys_ref[...] = jnp.dot(a, wd_ref[0], preferred_element_type=F32)


def _moe(block_expert, nvalid, nblocks, xs, wg, wu, wd):
    P, D = xs.shape
    E, _, F = wg.shape
    n_blocks = P // MOE_BLOCK
    blk = lambda b, be, nv, nb: (jnp.minimum(b, nb[0] - 1), 0)
    return pl.pallas_call(
        _moe_kernel,
        grid_spec=pltpu.PrefetchScalarGridSpec(
            num_scalar_prefetch=3,
            grid=(n_blocks,),
            in_specs=[pl.BlockSpec((MOE_BLOCK, D), blk),
                      pl.BlockSpec((1, D, F), lambda b, be, nv, nb: (be[b], 0, 0)),
                      pl.BlockSpec((1, D, F), lambda b, be, nv, nb: (be[b], 0, 0)),
                      pl.BlockSpec((1, F, D), lambda b, be, nv, nb: (be[b], 0, 0))],
            out_specs=pl.BlockSpec((MOE_BLOCK, D), blk)),
        out_shape=jax.ShapeDtypeStruct((P, D), F32),
        compiler_params=_params(("arbitrary",), 48),
        name="moe_experts",
    )(block_expert, nvalid, nblocks, xs, wg, wu, wd)


def _combine_kernel(dest_hbm, ys_hbm, gate_ref, h_ref, x_ref, g2_ref, sg_ref, su_ref, sd_ref,
                    fg_ref, o_ref, dest_smem, ybuf, idx_sem, row_sem, *, Tc, final_norm):
    i = pl.program_id(0)
    n = Tc * TOP_K
    cp = pltpu.make_async_copy(dest_hbm.at[pl.ds(i * n, n)], dest_smem, idx_sem)
    cp.start()
    cp.wait()

    def row_copy(t, k):
        return pltpu.make_async_copy(ys_hbm.at[pl.ds(dest_smem[t * TOP_K + k], 1)],
                                     ybuf.at[k, pl.ds(t, 1)], row_sem)

    def issue(t, carry):
        for k in range(TOP_K):
            row_copy(t, k).start()
        return carry

    lax.fori_loop(0, Tc, issue, 0)

    h = h_ref[...].astype(BF16)
    g = jnp.dot(h, sg_ref[...], preferred_element_type=F32)
    u = jnp.dot(h, su_ref[...], preferred_element_type=F32)
    a = (g * jax.nn.sigmoid(g) * u).astype(BF16)
    f = jnp.dot(a, sd_ref[...], preferred_element_type=F32)

    def drain(t, carry):
        for k in range(TOP_K):
            row_copy(t, k).wait()
        return carry

    lax.fori_loop(0, Tc, drain, 0)

    gate = gate_ref[...]
    for k in range(TOP_K):
        f = f + gate[:, k:k + 1] * ybuf[k]
    x2 = x_ref[...] + g2_ref[0] * f
    if final_norm:
        x2 = _rms(x2, NORM_EPS) * fg_ref[...]
    o_ref[...] = x2


def _combine(dest_flat, ys, gate_t, h2, x1, g2, sg, su, sd, final_g, S, final_norm, Tc=256):
    N, D = x1.shape
    F = sg.shape[1]
    tpb = S // Tc
    const = dict(pipeline_mode=pl.Buffered(1))
    row = lambda i: (i, 0)
    return pl.pallas_call(
        functools.partial(_combine_kernel, Tc=Tc, final_norm=final_norm),
        grid=(N // Tc,),
        in_specs=[pl.BlockSpec(memory_space=pl.ANY), pl.BlockSpec(memory_space=pl.ANY),
                  pl.BlockSpec((Tc, TOP_K), row),
                  pl.BlockSpec((Tc, D), row), pl.BlockSpec((Tc, D), row),
                  pl.BlockSpec((1, 1, D), lambda i: (i // tpb, 0, 0)),
                  pl.BlockSpec((D, F), lambda i: (0, 0), **const),
                  pl.BlockSpec((D, F), lambda i: (0, 0), **const),
                  pl.BlockSpec((F, D), lambda i: (0, 0), **const),
                  pl.BlockSpec((1, D), lambda i: (0, 0))],
        out_specs=pl.BlockSpec((Tc, D), row),
        out_shape=jax.ShapeDtypeStruct((N, D), F32),
        scratch_shapes=[pltpu.SMEM((Tc * TOP_K,), I32), pltpu.VMEM((TOP_K, Tc, D), F32),
                        pltpu.SemaphoreType.DMA, pltpu.SemaphoreType.DMA],
        compiler_params=_params(("arbitrary",), 52),
        name="moe_combine",
    )(dest_flat, ys, gate_t, h2, x1, g2, sg, su, sd, final_g.reshape(1, D))


def _moe_ffn(h2, logits_t, x1, g2, router_bias, wg, wu, wd, sg, su, sd, final_g, S, final_norm):
    N, D = h2.shape
    E = wg.shape[0]
    idx, gate, rank, cnt = _route(logits_t, router_bias)
    counts = cnt[:, 0]
    padded = (counts + MOE_BLOCK - 1) // MOE_BLOCK * MOE_BLOCK
    pend = jnp.cumsum(padded)
    pstart = (pend - padded).astype(I32)
    n_blocks = -(-(N * TOP_K + E * (MOE_BLOCK - 1)) // MOE_BLOCK)
    bstart = jnp.arange(n_blocks, dtype=I32) * MOE_BLOCK
    block_expert = jnp.minimum(jnp.searchsorted(pend, bstart, side='right'), E - 1).astype(I32)
    nvalid = jnp.clip(pstart[block_expert] + counts[block_expert] - bstart, 0, MOE_BLOCK).astype(I32)
    nblocks = (pend[-1:] // MOE_BLOCK).astype(I32)

    dest = _dest(pstart, idx, rank)
    dest_flat = dest.T.reshape(N * TOP_K)
    xs = _dispatch(dest_flat, h2, n_blocks * MOE_BLOCK)
    ys = _moe(block_expert, nvalid, nblocks, xs, wg, wu, wd)
    return _combine(dest_flat, ys, gate.T, h2, x1, g2, sg, su, sd, final_g, S, final_norm)


def kernel(x, c, ada_w, ada_b, norm_mix_g, norm_ffn_g, w_in, lambda_q1, lambda_k1, lambda_q2,
           lambda_k2, subln_g, rpb, w_out, router_w, router_bias, exp_w_gate, exp_w_up,
           exp_w_down, sh_w_gate, sh_w_up, sh_w_down, final_g):
    B, S, D = x.shape
    L = ada_w.shape[0]
    N = B * S
    n_heads = w_in.shape[2] // (3 * HEAD_DIM)
    heads_a = n_heads // HEADS_A_FRAC
    heads_b = n_heads // HEADS_B_FRAC
    heads_c = n_heads - heads_a - heads_b
    slopes = _alibi_slopes(heads_a + heads_b)
    slopes_a = jnp.asarray(slopes[:heads_a], F32)
    slopes_b = jnp.asarray(slopes[heads_a:], F32)

    mod = _ada(c, ada_w, ada_b).reshape(L, B, 6, 1, D)
    xf = x.reshape(N, D)
    for l in range(L):
        sh1, sc1, g1, sh2, sc2, g2 = (mod[l, :, j] for j in range(6))
        qkv = _qkv(xf, norm_mix_g[l], sc1, sh1, w_in[l].astype(BF16), S)
        oa = _attn_a(qkv, slopes_a, B, S, n_heads, heads_a)
        lambda_init = 0.8 - 0.6 * math.exp(-0.3 * l)
        lam_vecs = jnp.stack([lambda_q1[l], lambda_k1[l], lambda_q2[l], lambda_k2[l]]).astype(F32)
        ob = _attn_b(qkv, slopes_b, lam_vecs, subln_g[l], B, S, n_heads, heads_a, heads_b, lambda_init)
        oc = _attn_c(qkv, rpb[l].reshape(-1), B, S, n_heads, heads_a + heads_b, heads_c)
        x1, h2, logits_t = _out_proj(oa, ob, oc, w_out[l].astype(BF16), xf, g1, norm_ffn_g[l],
                                     sc2, sh2, router_w[l].T, S)
        xf = _moe_ffn(h2, logits_t, x1, g2, router_bias[l], exp_w_gate[l].astype(BF16),
                      exp_w_up[l].astype(BF16), exp_w_down[l].astype(BF16),
                      sh_w_gate[l].astype(BF16), sh_w_up[l].astype(BF16),
                      sh_w_down[l].astype(BF16), final_g, S, l == L - 1)
    return xf.reshape(B, S, D)
```

```python
import functools
import math

import jax
import jax.numpy as jnp
from jax import lax
from jax.experimental import pallas as pl
from jax.experimental.pallas import tpu as pltpu

F32 = jnp.float32
BF16 = jnp.bfloat16
I32 = jnp.int32
HIGHEST = lax.Precision.HIGHEST
_NT = (((1,), (1,)), ((), ()))

HEAD_DIM = 128
HEADS_A_FRAC, HEADS_B_FRAC = 2, 4
DILATED_PATTERNS = ((128, 1), (512, 4), (2048, 16))
GRID_W = 64
NA_ROWS = 8
NA_COLS = 16
N_GROUPS = 8
TOPK_GROUPS = 4
TOP_K = 8
ROUTED_SCALE = 2.5
NORM_EPS = 1e-6
SUBLN_EPS = 1e-5
NEG_INF = -1e30
LOG2E = math.log2(math.e)

LANES = 128
VMEM_BYTES_V7X = 64 << 20

MOE_BLOCK = 256


def _params(semantics, vmem_mib):
    return pltpu.CompilerParams(dimension_semantics=semantics,
                                vmem_limit_bytes=min(vmem_mib << 20, VMEM_BYTES_V7X - (4 << 20)))


def _alibi_slopes(n):
    def pow2(m):
        start = 2.0 ** (-8.0 / m)
        return [start ** (i + 1) for i in range(m)]
    p = 2 ** int(math.floor(math.log2(n)))
    return pow2(p) + pow2(2 * p)[0::2][: n - p]


def _rms(x, eps):
    return x * lax.rsqrt(jnp.mean(x * x, axis=-1, keepdims=True) + eps)


def _ada_kernel(c_ref, w_ref, b_ref, o_ref):
    c = c_ref[...]
    sc = c * jax.nn.sigmoid(c)
    o_ref[0] = jnp.dot(sc, w_ref[0], preferred_element_type=F32, precision=HIGHEST) + b_ref[0]


def _ada(c, ada_w, ada_b):
    B, D = c.shape
    L, _, D6 = ada_w.shape
    rows = 8
    cp = jnp.zeros((rows, D), F32).at[:B].set(c)
    tn = 1024
    out = pl.pallas_call(
        _ada_kernel,
        grid=(L, D6 // tn),
        in_specs=[pl.BlockSpec((rows, D), lambda l, j: (0, 0)),
                  pl.BlockSpec((1, D, tn), lambda l, j: (l, 0, j)),
                  pl.BlockSpec((1, 1, tn), lambda l, j: (l, 0, j))],
        out_specs=pl.BlockSpec((1, rows, tn), lambda l, j: (l, 0, j)),
        out_shape=jax.ShapeDtypeStruct((L, rows, D6), F32),
        compiler_params=_params(("arbitrary", "arbitrary"), 32),
        name="ada_mod",
    )(cp, ada_w, ada_b.reshape(L, 1, D6))
    return out[:, :B]


def _qkv_kernel(x_ref, g_ref, sc_ref, sh_ref, w_ref, o_ref, h_scr, *, heads_per_step):
    @pl.when(pl.program_id(1) == 0)
    def _():
        y = _rms(x_ref[...], NORM_EPS) * g_ref[...]
        h_scr[...] = (y * (1.0 + sc_ref[0]) + sh_ref[0]).astype(BF16)

    res = jnp.dot(h_scr[...], w_ref[...], preferred_element_type=F32)
    for hh in range(heads_per_step):
        o_ref[hh] = res[:, hh * HEAD_DIM:(hh + 1) * HEAD_DIM].astype(BF16)


def _qkv(x2d, g, sc, sh, w_bf, S, tm=512, tn=1536):
    N, D = x2d.shape
    W3 = w_bf.shape[1]
    hps = tn // HEAD_DIM
    tpb = S // tm
    return pl.pallas_call(
        functools.partial(_qkv_kernel, heads_per_step=hps),
        grid=(N // tm, W3 // tn),
        in_specs=[pl.BlockSpec((tm, D), lambda i, j: (i, 0)),
                  pl.BlockSpec((1, D), lambda i, j: (0, 0)),
                  pl.BlockSpec((1, 1, D), lambda i, j: (i // tpb, 0, 0)),
                  pl.BlockSpec((1, 1, D), lambda i, j: (i // tpb, 0, 0)),
                  pl.BlockSpec((D, tn), lambda i, j: (0, j))],
        out_specs=pl.BlockSpec((hps, tm, HEAD_DIM), lambda i, j: (j, i, 0)),
        out_shape=jax.ShapeDtypeStruct((W3 // HEAD_DIM, N, HEAD_DIM), BF16),
        scratch_shapes=[pltpu.VMEM((tm, D), BF16)],
        compiler_params=_params(("arbitrary", "arbitrary"), 48),
        name="qkv_proj",
    )(x2d, g.reshape(1, D), sc, sh, w_bf)


def _attn_a_kernel(slopes_ref, q_ref, k_ref, v_ref, o_ref, tab_ref, *, S, Tq, reach):
    h = pl.program_id(1)
    i = pl.program_id(2)
    wk = Tq + 2 * reach
    max_off = wk - Tq

    @pl.when(i == 0)
    def _():
        shape = (Tq, wk + max_off)
        a = lax.broadcasted_iota(I32, shape, 0)
        u = lax.broadcasted_iota(I32, shape, 1)
        d = u - max_off - a
        ad = jnp.maximum(d, -d)
        mult = jnp.zeros(shape, F32)
        for window, dil in DILATED_PATTERNS:
            half = window // (2 * dil)
            mult = mult + jnp.where((ad <= half * dil) & ((ad & (dil - 1)) == 0), 1.0, 0.0)
        bias = jnp.log(jnp.maximum(mult, 1.0)) - slopes_ref[h] * ad.astype(F32)
        tab_ref[...] = jnp.where(mult > 0.0, bias, NEG_INF)

    t0 = i * Tq
    start = pl.multiple_of(jnp.clip(t0 - reach, 0, S - wk), LANES)
    off = t0 - start
    q = q_ref[0]
    kw = k_ref[0, pl.ds(start, wk), :]
    vw = v_ref[0, pl.ds(start, wk), :]
    s = lax.dot_general(q, kw, _NT, preferred_element_type=F32) * (HEAD_DIM ** -0.5)
    s = s + tab_ref[:, pl.ds(pl.multiple_of(max_off - off, LANES), wk)]
    m = jnp.max(s, axis=-1, keepdims=True)
    p = jnp.exp(s - m)
    l = jnp.sum(p, axis=-1, keepdims=True)
    o = jnp.dot(p.astype(BF16), vw, preferred_element_type=F32) / l
    o_ref[...] = o.astype(BF16)


def _attn_a(qkv, slopes, B, S, n_heads, heads_a, Tq=128):
    N = B * S
    reach = max((w // (2 * d)) * d for w, d in DILATED_PATTERNS)
    wk = Tq + 2 * reach
    assert wk <= S and reach % LANES == 0
    nq = S // Tq
    return pl.pallas_call(
        functools.partial(_attn_a_kernel, S=S, Tq=Tq, reach=reach),
        grid=(B, heads_a, nq),
        in_specs=[pl.BlockSpec(memory_space=pltpu.SMEM),
                  pl.BlockSpec((1, Tq, HEAD_DIM), lambda b, h, i: (h, b * nq + i, 0)),
                  pl.BlockSpec((1, S, HEAD_DIM), lambda b, h, i: (n_heads + h, b, 0)),
                  pl.BlockSpec((1, S, HEAD_DIM), lambda b, h, i: (2 * n_heads + h, b, 0))],
        out_specs=pl.BlockSpec((Tq, HEAD_DIM), lambda b, h, i: (b * nq + i, h)),
        out_shape=jax.ShapeDtypeStruct((N, heads_a * HEAD_DIM), BF16),
        scratch_shapes=[pltpu.VMEM((Tq, 2 * wk - Tq), F32)],
        compiler_params=_params(("arbitrary", "arbitrary", "arbitrary"), 32),
        name="attn_dilated",
    )(slopes, qkv, qkv, qkv)


def _attn_b_kernel(slopes_ref, lam_ref, q_ref, k_ref, v_ref, g_ref, o_ref, tab_ref, *,
                   S, Tq, lambda_init):
    h = pl.program_id(1)
    i = pl.program_id(2)
    dh = HEAD_DIM // 2

    @pl.when(i == 0)
    def _():
        shape = (Tq, 2 * S - Tq)
        a = lax.broadcasted_iota(I32, shape, 0)
        u = lax.broadcasted_iota(I32, shape, 1)
        d = a - u + (S - Tq)
        tab_ref[...] = (-LOG2E * slopes_ref[h]) * jnp.maximum(d, -d).astype(F32)

    lv = lam_ref[...]
    lam = (jnp.exp(jnp.sum(lv[0:1] * lv[1:2], axis=-1, keepdims=True))
           - jnp.exp(jnp.sum(lv[2:3] * lv[3:4], axis=-1, keepdims=True)) + lambda_init)

    t0 = i * Tq
    bias = tab_ref[:, pl.ds(pl.multiple_of(S - Tq - t0, LANES), S)]
    q = (q_ref[0].astype(F32) * (LOG2E * dh ** -0.5)).astype(BF16)
    k = k_ref[0]

    def softmax_part(c):
        s = lax.dot_general(q[:, c * dh:(c + 1) * dh], k[:, c * dh:(c + 1) * dh], _NT,
                            preferred_element_type=F32) + bias
        m = jnp.max(s, axis=-1, keepdims=True)
        p = jnp.exp2(s - m)
        return p.astype(BF16), jnp.sum(p, axis=-1, keepdims=True)

    p1, l1 = softmax_part(0)
    p2, l2 = softmax_part(1)
    pv = jnp.dot(jnp.concatenate([p1, p2], axis=0), v_ref[0], preferred_element_type=F32)
    o = pv[:Tq] / l1 - pv[Tq:] * (lam / l2)
    o = _rms(o, SUBLN_EPS) * g_ref[...] * (1.0 - lambda_init)
    o_ref[...] = o.astype(BF16)


def _attn_b(qkv, slopes, lam_vecs, subln_g, B, S, n_heads, head0, heads_b, lambda_init, Tq=128):
    N = B * S
    nq = S // Tq
    return pl.pallas_call(
        functools.partial(_attn_b_kernel, S=S, Tq=Tq, lambda_init=lambda_init),
        grid=(B, heads_b, nq),
        in_specs=[pl.BlockSpec(memory_space=pltpu.SMEM),
                  pl.BlockSpec((4, HEAD_DIM // 2), lambda b, h, i: (0, 0)),
                  pl.BlockSpec((1, Tq, HEAD_DIM), lambda b, h, i: (head0 + h, b * nq + i, 0)),
                  pl.BlockSpec((1, S, HEAD_DIM), lambda b, h, i: (n_heads + head0 + h, b, 0)),
                  pl.BlockSpec((1, S, HEAD_DIM), lambda b, h, i: (2 * n_heads + head0 + h, b, 0)),
                  pl.BlockSpec((1, HEAD_DIM), lambda b, h, i: (0, 0))],
        out_specs=pl.BlockSpec((Tq, HEAD_DIM), lambda b, h, i: (b * nq + i, h)),
        out_shape=jax.ShapeDtypeStruct((N, heads_b * HEAD_DIM), BF16),
        scratch_shapes=[pltpu.VMEM((Tq, 2 * S - Tq), F32)],
        compiler_params=_params(("arbitrary", "arbitrary", "arbitrary"), 48),
        name="attn_diff",
    )(slopes, lam_vecs, qkv, qkv, qkv, subln_g.reshape(1, HEAD_DIM))


def _attn_c_kernel(rpb_ref, q_ref, k_ref, v_ref, o_ref, tab_ref, *, R):
    h = pl.program_id(1)
    W = GRID_W
    kr = min(NA_ROWS, R)
    n_dr = 2 * NA_ROWS - 1
    n_dc = 2 * NA_COLS - 1

    c_io = lax.broadcasted_iota(I32, (W, 2 * W), 0)
    l_io = lax.broadcasted_iota(I32, (W, 2 * W), 1)
    cp = l_io & (W - 1)
    dcm = cp - c_io + (NA_COLS - 1)
    cstart = jnp.clip(c_io - NA_COLS // 2, 0, W - NA_COLS)
    ok = (cp >= cstart) & (cp < cstart + NA_COLS)
    blocks = []
    for dr in range(n_dr):
        blk = jnp.zeros((W, 2 * W), F32)
        for dc in range(n_dc):
            blk = jnp.where(dcm == dc, rpb_ref[(h * n_dr + dr) * n_dc + dc], blk)
        blocks.append(jnp.where(ok, blk, NEG_INF))
    for o in range(NA_ROWS):
        for jj in range(kr // 2):
            tab_ref[o, :, jj * 2 * W:(jj + 1) * 2 * W] = jnp.where(
                l_io < W, blocks[o + 2 * jj], blocks[o + 2 * jj + 1])

    def row(r, carry):
        rs = jnp.clip(r - kr // 2, 0, R - kr)
        q = q_ref[0, pl.ds(pl.multiple_of(r * W, W), W), :]
        kw = k_ref[0, pl.ds(pl.multiple_of(rs * W, W), kr * W), :]
        vw = v_ref[0, pl.ds(pl.multiple_of(rs * W, W), kr * W), :]
        s = lax.dot_general(q, kw, _NT, preferred_element_type=F32) * (HEAD_DIM ** -0.5)
        s = s + tab_ref[rs - r + (NA_ROWS - 1)]
        m = jnp.max(s, axis=-1, keepdims=True)
        p = jnp.exp(s - m)
        l = jnp.sum(p, axis=-1, keepdims=True)
        o = jnp.dot(p.astype(BF16), vw, preferred_element_type=F32) / l
        o_ref[pl.ds(pl.multiple_of(r * W, W), W), :] = o.astype(BF16)
        return carry

    lax.fori_loop(0, R, row, 0)


def _attn_c(qkv, rpb_flat, B, S, n_heads, head0, heads_c):
    N = B * S
    R = S // GRID_W
    kr = min(NA_ROWS, R)
    assert kr == NA_ROWS and kr % 2 == 0
    return pl.pallas_call(
        functools.partial(_attn_c_kernel, R=R),
        grid=(B, heads_c),
        in_specs=[pl.BlockSpec(memory_space=pltpu.SMEM),
                  pl.BlockSpec((1, S, HEAD_DIM), lambda b, h: (head0 + h, b, 0)),
                  pl.BlockSpec((1, S, HEAD_DIM), lambda b, h: (n_heads + head0 + h, b, 0)),
                  pl.BlockSpec((1, S, HEAD_DIM), lambda b, h: (2 * n_heads + head0 + h, b, 0))],
        out_specs=pl.BlockSpec((S, HEAD_DIM), lambda b, h: (b, h)),
        out_shape=jax.ShapeDtypeStruct((N, heads_c * HEAD_DIM), BF16),
        scratch_shapes=[pltpu.VMEM((NA_ROWS, GRID_W, kr * GRID_W), F32)],
        compiler_params=_params(("arbitrary", "arbitrary"), 32),
        name="attn_nbr",
    )(rpb_flat, qkv, qkv, qkv)


def _out_kernel(oa_ref, ob_ref, oc_ref, wa_ref, wb_ref, wc_ref, x_ref, g1_ref, ng_ref, sc_ref,
                sh_ref, rw_ref, x1_ref, h2_ref, lg_ref):
    acc = jnp.dot(oa_ref[...], wa_ref[...], preferred_element_type=F32)
    acc = acc + jnp.dot(ob_ref[...], wb_ref[...], preferred_element_type=F32)
    acc = acc + jnp.dot(oc_ref[...], wc_ref[...], preferred_element_type=F32)
    x1 = x_ref[...] + g1_ref[0] * acc
    x1_ref[...] = x1
    h2 = (_rms(x1, NORM_EPS) * ng_ref[...]) * (1.0 + sc_ref[0]) + sh_ref[0]
    h2_ref[...] = h2
    lg_ref[...] = lax.dot_general(rw_ref[...], h2, _NT, preferred_element_type=F32,
                                  precision=HIGHEST)


def _out_proj(oa, ob, oc, w_bf, x2d, g1, ng, sc, sh, rw_t, S, tm=512):
    N, D = x2d.shape
    wa, wb, wc = oa.shape[1], ob.shape[1], oc.shape[1]
    E = rw_t.shape[0]
    tpb = S // tm
    const = dict(pipeline_mode=pl.Buffered(1))
    row = lambda i: (i, 0)
    per_b = lambda i: (i // tpb, 0, 0)
    return pl.pallas_call(
        _out_kernel,
        grid=(N // tm,),
        in_specs=[pl.BlockSpec((tm, wa), row), pl.BlockSpec((tm, wb), row), pl.BlockSpec((tm, wc), row),
                  pl.BlockSpec((wa, D), lambda i: (0, 0), **const),
                  pl.BlockSpec((wb, D), lambda i: (wa // wb, 0), **const),
                  pl.BlockSpec((wc, D), lambda i: ((wa + wb) // wc, 0), **const),
                  pl.BlockSpec((tm, D), row),
                  pl.BlockSpec((1, 1, D), per_b),
                  pl.BlockSpec((1, D), lambda i: (0, 0)),
                  pl.BlockSpec((1, 1, D), per_b),
                  pl.BlockSpec((1, 1, D), per_b),
                  pl.BlockSpec((E, D), lambda i: (0, 0), **const)],
        out_specs=[pl.BlockSpec((tm, D), row), pl.BlockSpec((tm, D), row),
                   pl.BlockSpec((E, tm), lambda i: (0, i))],
        out_shape=[jax.ShapeDtypeStruct((N, D), F32), jax.ShapeDtypeStruct((N, D), F32),
                   jax.ShapeDtypeStruct((E, N), F32)],
        compiler_params=_params(("arbitrary",), 52),
        name="out_proj",
    )(oa, ob, oc, w_bf, w_bf, w_bf, x2d, g1, ng.reshape(1, D), sc, sh, rw_t)


def _first_argmax(vals, iota, big):
    mx = jnp.max(vals, axis=0, keepdims=True)
    idx = jnp.min(jnp.where(vals == mx, iota, big), axis=0, keepdims=True)
    return mx, idx


def _route_kernel(lg_ref, rb_ref, idx_ref, gate_ref, rank_ref, cnt_ref, tri_ref, carry_ref, *, Tt):
    i = pl.program_id(0)
    E = lg_ref.shape[0]
    gs = E // N_GROUPS

    @pl.when(i == 0)
    def _():
        r = lax.broadcasted_iota(I32, (Tt, Tt), 0)
        c = lax.broadcasted_iota(I32, (Tt, Tt), 1)
        tri_ref[...] = jnp.where(r < c, 1.0, 0.0).astype(BF16)
        carry_ref[...] = jnp.zeros_like(carry_ref)

    scores = jax.nn.sigmoid(lg_ref[...])
    sel = scores + rb_ref[...]
    e_io = lax.broadcasted_iota(I32, (E, Tt), 0).astype(F32)
    s_io = lax.broadcasted_iota(I32, (gs, Tt), 0).astype(F32)
    g_io = lax.broadcasted_iota(I32, (N_GROUPS, Tt), 0).astype(F32)

    grp = []
    for g in range(N_GROUPS):
        blk = sel[g * gs:(g + 1) * gs]
        m1, a1 = _first_argmax(blk, s_io, float(gs))
        m2 = jnp.max(jnp.where(s_io == a1, -jnp.inf, blk), axis=0, keepdims=True)
        grp.append(m1 + m2)
    grp = jnp.concatenate(grp, axis=0)

    gsel = jnp.zeros((N_GROUPS, Tt), F32)
    for _ in range(TOPK_GROUPS):
        _, gi = _first_argmax(grp, g_io, float(N_GROUPS))
        hit = g_io == gi
        gsel = jnp.where(hit, 1.0, gsel)
        grp = jnp.where(hit, -jnp.inf, grp)

    esel = jnp.concatenate(
        [jnp.broadcast_to(gsel[g:g + 1], (gs, Tt)) for g in range(N_GROUPS)], axis=0)
    cur = jnp.where(esel > 0.0, sel, NEG_INF)

    idxs, gates = [], []
    onehot = jnp.zeros((E, Tt), F32)
    for _ in range(TOP_K):
        _, ei = _first_argmax(cur, e_io, float(E))
        hit = e_io == ei
        idxs.append(ei)
        gates.append(jnp.sum(jnp.where(hit, scores, 0.0), axis=0, keepdims=True))
        onehot = jnp.where(hit, 1.0, onehot)
        cur = jnp.where(hit, -jnp.inf, cur)
    gate = jnp.concatenate(gates, axis=0)
    gate = gate / jnp.sum(gate, axis=0, keepdims=True) * ROUTED_SCALE
    idx_ref[...] = jnp.concatenate(idxs, axis=0).astype(I32)
    gate_ref[...] = gate

    oh = onehot.astype(BF16)
    before = jnp.dot(oh, tri_ref[...], preferred_element_type=F32)
    tile_cnt = jnp.dot(oh, jnp.ones((Tt, LANES), BF16), preferred_element_type=F32)
    carry = carry_ref[...]
    rank_mat = before + jnp.concatenate([carry] * (Tt // LANES), axis=1)
    ranks = [jnp.sum(jnp.where(e_io == ei, rank_mat, 0.0), axis=0, keepdims=True) for ei in idxs]
    rank_ref[...] = jnp.concatenate(ranks, axis=0).astype(I32)
    carry_ref[...] = carry + tile_cnt
    cnt_ref[...] = (carry + tile_cnt).astype(I32)


def _route(logits_t, router_bias, Tt=512):
    E, N = logits_t.shape
    tok = lambda i: (0, i)
    return pl.pallas_call(
        functools.partial(_route_kernel, Tt=Tt),
        grid=(N // Tt,),
        in_specs=[pl.BlockSpec((E, Tt), tok), pl.BlockSpec((E, 1), lambda i: (0, 0))],
        out_specs=[pl.BlockSpec((TOP_K, Tt), tok), pl.BlockSpec((TOP_K, Tt), tok),
                   pl.BlockSpec((TOP_K, Tt), tok), pl.BlockSpec((E, LANES), lambda i: (0, 0))],
        out_shape=[jax.ShapeDtypeStruct((TOP_K, N), I32), jax.ShapeDtypeStruct((TOP_K, N), F32),
                   jax.ShapeDtypeStruct((TOP_K, N), I32), jax.ShapeDtypeStruct((E, LANES), I32)],
        scratch_shapes=[pltpu.VMEM((Tt, Tt), BF16), pltpu.VMEM((E, LANES), F32)],
        compiler_params=_params(("arbitrary",), 32),
        name="route_topk",
    )(logits_t, router_bias.reshape(E, 1))


def _dest_kernel(pstart_ref, idx_ref, rank_ref, dest_ref):
    idx = idx_ref[...]
    base = jnp.zeros(idx.shape, I32)
    for e in range(pstart_ref.shape[0]):
        base = jnp.where(idx == e, pstart_ref[e], base)
    dest_ref[...] = base + rank_ref[...]


def _dest(pstart, idx, rank):
    return pl.pallas_call(
        _dest_kernel,
        in_specs=[pl.BlockSpec(memory_space=pltpu.SMEM), pl.BlockSpec(memory_space=pltpu.VMEM),
                  pl.BlockSpec(memory_space=pltpu.VMEM)],
        out_specs=pl.BlockSpec(memory_space=pltpu.VMEM),
        out_shape=jax.ShapeDtypeStruct(idx.shape, I32),
        name="route_dest",
    )(pstart, idx, rank)


def _dispatch_kernel(dest_hbm, h_ref, xs_hbm, dest_smem, idx_sem, row_sem, *, Td):
    i = pl.program_id(0)
    n = Td * TOP_K
    cp = pltpu.make_async_copy(dest_hbm.at[pl.ds(i * n, n)], dest_smem, idx_sem)
    cp.start()
    cp.wait()

    def row_copy(t, k):
        return pltpu.make_async_copy(h_ref.at[pl.ds(t, 1)],
                                     xs_hbm.at[pl.ds(dest_smem[t * TOP_K + k], 1)], row_sem)

    def issue(t, carry):
        for k in range(TOP_K):
            row_copy(t, k).start()
        return carry

    lax.fori_loop(0, Td, issue, 0)

    def drain(t, carry):
        for k in range(TOP_K):
            row_copy(t, k).wait()
        return carry

    lax.fori_loop(0, Td, drain, 0)


def _dispatch(dest_flat, h2, P, Td=256):
    N, D = h2.shape
    return pl.pallas_call(
        functools.partial(_dispatch_kernel, Td=Td),
        grid=(N // Td,),
        in_specs=[pl.BlockSpec(memory_space=pl.ANY), pl.BlockSpec((Td, D), lambda i: (i, 0))],
        out_specs=pl.BlockSpec(memory_space=pl.ANY),
        out_shape=jax.ShapeDtypeStruct((P, D), h2.dtype),
        scratch_shapes=[pltpu.SMEM((Td * TOP_K,), I32), pltpu.SemaphoreType.DMA,
                        pltpu.SemaphoreType.DMA],
        compiler_params=_params(("arbitrary",), 32),
        name="moe_dispatch",
    )(dest_flat, h2)


def _moe_kernel(be_ref, nv_ref, nb_ref, xs_ref, wg_ref, wu_ref, wd_ref, ys_ref, wg_s, wu_s, wd_s):
    b = pl.program_id(0)
    active = b < nb_ref[0]
    new_expert = (b == 0) | (be_ref[b] != be_ref[jnp.maximum(b - 1, 0)])

    @pl.when(active & new_expert)
    def _():
        wg_s[...] = wg_ref[0].astype(BF16)
        wu_s[...] = wu_ref[0].astype(BF16)
        wd_s[...] = wd_ref[0].astype(BF16)

    @pl.when(active)
    def _():
        rows = lax.broadcasted_iota(I32, xs_ref.shape, 0)
        x = jnp.where(rows < nv_ref[b], xs_ref[...], 0.0).astype(BF16)
        g = jnp.dot(x, wg_s[...], preferred_element_type=F32)
        u = jnp.dot(x, wu_s[...], preferred_element_type=F32)
        a = (g * jax.nn.sigmoid(g) * u).astype(BF16)
        ys_ref[...] = jnp.dot(a, wd_s[...], preferred_element_type=F32)


def _moe(block_expert, nvalid, nblocks, xs, wg, wu, wd):
    P, D = xs.shape
    E, _, F = wg.shape
    n_blocks = P // MOE_BLOCK
    blk = lambda b, be, nv, nb: (jnp.minimum(b, nb[0] - 1), 0)
    return pl.pallas_call(
        _moe_kernel,
        grid_spec=pltpu.PrefetchScalarGridSpec(
            num_scalar_prefetch=3,
            grid=(n_blocks,),
            in_specs=[pl.BlockSpec((MOE_BLOCK, D), blk),
                      pl.BlockSpec((1, D, F), lambda b, be, nv, nb: (be[b], 0, 0)),
                      pl.BlockSpec((1, D, F), lambda b, be, nv, nb: (be[b], 0, 0)),
                      pl.BlockSpec((1, F, D), lambda b, be, nv, nb: (be[b], 0, 0))],
            out_specs=pl.BlockSpec((MOE_BLOCK, D), blk),
            scratch_shapes=[pltpu.VMEM((D, F), BF16), pltpu.VMEM((D, F), BF16),
                            pltpu.VMEM((F, D), BF16)]),
        out_shape=jax.ShapeDtypeStruct((P, D), F32),
        compiler_params=_params(("arbitrary",), 52),
        name="moe_experts",
    )(block_expert, nvalid, nblocks, xs, wg, wu, wd)


def _combine_kernel(dest_hbm, ys_hbm, gate_ref, h_ref, x_ref, g2_ref, sg_ref, su_ref, sd_ref,
                    fg_ref, o_ref, dest_smem, ybuf, idx_sem, row_sem, *, Tc, final_norm):
    i = pl.program_id(0)
    n = Tc * TOP_K
    cp = pltpu.make_async_copy(dest_hbm.at[pl.ds(i * n, n)], dest_smem, idx_sem)
    cp.start()
    cp.wait()

    def row_copy(t, k):
        return pltpu.make_async_copy(ys_hbm.at[pl.ds(dest_smem[t * TOP_K + k], 1)],
                                     ybuf.at[k, pl.ds(t, 1)], row_sem)

    def issue(t, carry):
        for k in range(TOP_K):
            row_copy(t, k).start()
        return carry

    lax.fori_loop(0, Tc, issue, 0)

    h = h_ref[...].astype(BF16)
    g = jnp.dot(h, sg_ref[...], preferred_element_type=F32)
    u = jnp.dot(h, su_ref[...], preferred_element_type=F32)
    a = (g * jax.nn.sigmoid(g) * u).astype(BF16)
    f = jnp.dot(a, sd_ref[...], preferred_element_type=F32)

    def drain(t, carry):
        for k in range(TOP_K):
            row_copy(t, k).wait()
        return carry

    lax.fori_loop(0, Tc, drain, 0)

    gate = gate_ref[...]
    for k in range(TOP_K):
        f = f + gate[:, k:k + 1] * ybuf[k]
    x2 = x_ref[...] + g2_ref[0] * f
    if final_norm:
        x2 = _rms(x2, NORM_EPS) * fg_ref[...]
    o_ref[...] = x2


def _combine(dest_flat, ys, gate_t, h2, x1, g2, sg, su, sd, final_g, S, final_norm, Tc=256):
    N, D = x1.shape
    F = sg.shape[1]
    tpb = S // Tc
    const = dict(pipeline_mode=pl.Buffered(1))
    row = lambda i: (i, 0)
    return pl.pallas_call(
        functools.partial(_combine_kernel, Tc=Tc, final_norm=final_norm),
        grid=(N // Tc,),
        in_specs=[pl.BlockSpec(memory_space=pl.ANY), pl.BlockSpec(memory_space=pl.ANY),
                  pl.BlockSpec((Tc, TOP_K), row),
                  pl.BlockSpec((Tc, D), row), pl.BlockSpec((Tc, D), row),
                  pl.BlockSpec((1, 1, D), lambda i: (i // tpb, 0, 0)),
                  pl.BlockSpec((D, F), lambda i: (0, 0), **const),
                  pl.BlockSpec((D, F), lambda i: (0, 0), **const),
                  pl.BlockSpec((F, D), lambda i: (0, 0), **const),
                  pl.BlockSpec((1, D), lambda i: (0, 0))],
        out_specs=pl.BlockSpec((Tc, D), row),
        out_shape=jax.ShapeDtypeStruct((N, D), F32),
        scratch_shapes=[pltpu.SMEM((Tc * TOP_K,), I32), pltpu.VMEM((TOP_K, Tc, D), F32),
                        pltpu.SemaphoreType.DMA, pltpu.SemaphoreType.DMA],
        compiler_params=_params(("arbitrary",), 52),
        name="moe_combine",
    )(dest_flat, ys, gate_t, h2, x1, g2, sg, su, sd, final_g.reshape(1, D))


def _moe_ffn(h2, logits_t, x1, g2, router_bias, wg, wu, wd, e_off, sg, su, sd, final_g, S,
             final_norm):
    N, D = h2.shape
    E = logits_t.shape[0]
    idx, gate, rank, cnt = _route(logits_t, router_bias)
    counts = cnt[:, 0]
    padded = (counts + MOE_BLOCK - 1) // MOE_BLOCK * MOE_BLOCK
    pend = jnp.cumsum(padded)
    pstart = (pend - padded).astype(I32)
    n_blocks = -(-(N * TOP_K + E * (MOE_BLOCK - 1)) // MOE_BLOCK)
    bstart = jnp.arange(n_blocks, dtype=I32) * MOE_BLOCK
    block_expert = jnp.minimum(jnp.sum(pend[None, :] <= bstart[:, None], axis=1), E - 1).astype(I32)
    nvalid = jnp.clip(pstart[block_expert] + counts[block_expert] - bstart, 0, MOE_BLOCK).astype(I32)
    nblocks = (pend[-1:] // MOE_BLOCK).astype(I32)

    dest = _dest(pstart, idx, rank)
    dest_flat = dest.T.reshape(N * TOP_K)
    xs = _dispatch(dest_flat, h2, n_blocks * MOE_BLOCK)
    ys = _moe(block_expert + e_off, nvalid, nblocks, xs, wg, wu, wd)
    return _combine(dest_flat, ys, gate.T, h2, x1, g2, sg, su, sd, final_g, S, final_norm)


def kernel(x, c, ada_w, ada_b, norm_mix_g, norm_ffn_g, w_in, lambda_q1, lambda_k1, lambda_q2,
           lambda_k2, subln_g, rpb, w_out, router_w, router_bias, exp_w_gate, exp_w_up,
           exp_w_down, sh_w_gate, sh_w_up, sh_w_down, final_g):
    B, S, D = x.shape
    L = ada_w.shape[0]
    N = B * S
    n_heads = w_in.shape[2] // (3 * HEAD_DIM)
    heads_a = n_heads // HEADS_A_FRAC
    heads_b = n_heads // HEADS_B_FRAC
    heads_c = n_heads - heads_a - heads_b
    slopes = _alibi_slopes(heads_a + heads_b)
    slopes_a = jnp.asarray(slopes[:heads_a], F32)
    slopes_b = jnp.asarray(slopes[heads_a:], F32)

    E, F = exp_w_gate.shape[1], exp_w_gate.shape[3]
    wg_all = exp_w_gate.reshape(L * E, D, F)
    wu_all = exp_w_up.reshape(L * E, D, F)
    wd_all = exp_w_down.reshape(L * E, F, D)

    mod = _ada(c, ada_w, ada_b).reshape(L, B, 6, 1, D)
    xf = x.reshape(N, D)
    for l in range(L):
        sh1, sc1, g1, sh2, sc2, g2 = (mod[l, :, j] for j in range(6))
        qkv = _qkv(xf, norm_mix_g[l], sc1, sh1, w_in[l].astype(BF16), S)
        oa = _attn_a(qkv, slopes_a, B, S, n_heads, heads_a)
        lambda_init = 0.8 - 0.6 * math.exp(-0.3 * l)
        lam_vecs = jnp.stack([lambda_q1[l], lambda_k1[l], lambda_q2[l], lambda_k2[l]]).astype(F32)
        ob = _attn_b(qkv, slopes_b, lam_vecs, subln_g[l], B, S, n_heads, heads_a, heads_b, lambda_init)
        oc = _attn_c(qkv, rpb[l].reshape(-1), B, S, n_heads, heads_a + heads_b, heads_c)
        x1, h2, logits_t = _out_proj(oa, ob, oc, w_out[l].astype(BF16), xf, g1, norm_ffn_g[l],
                                     sc2, sh2, router_w[l].T, S)
        xf = _moe_ffn(h2, logits_t, x1, g2, router_bias[l], wg_all, wu_all, wd_all, l * E,
                      sh_w_gate[l].astype(BF16), sh_w_up[l].astype(BF16),
                      sh_w_down[l].astype(BF16), final_g, S, l == L - 1)
    return xf.reshape(B, S, D)
```

```python
import functools
import math

import jax
import jax.numpy as jnp
from jax import lax
from jax.experimental import pallas as pl
from jax.experimental.pallas import tpu as pltpu

F32 = jnp.float32
BF16 = jnp.bfloat16
I32 = jnp.int32
U32 = jnp.uint32
HIGHEST = lax.Precision.HIGHEST
_NT = (((1,), (1,)), ((), ()))

HEAD_DIM = 128
HEADS_A_FRAC, HEADS_B_FRAC = 2, 4
DILATED_PATTERNS = ((128, 1), (512, 4), (2048, 16))
GRID_W = 64
NA_ROWS = 8
NA_COLS = 16
N_GROUPS = 8
TOPK_GROUPS = 4
TOP_K = 8
ROUTED_SCALE = 2.5
NORM_EPS = 1e-6
SUBLN_EPS = 1e-5
NEG_INF = -1e30
LOG2E = math.log2(math.e)

LANES = 128
VMEM_BYTES_V7X = 64 << 20

MOE_BLOCK = 256


def _params(semantics, vmem_mib):
    return pltpu.CompilerParams(dimension_semantics=semantics,
                                vmem_limit_bytes=min(vmem_mib << 20, VMEM_BYTES_V7X - (4 << 20)))


def _alibi_slopes(n):
    def pow2(m):
        start = 2.0 ** (-8.0 / m)
        return [start ** (i + 1) for i in range(m)]
    p = 2 ** int(math.floor(math.log2(n)))
    return pow2(p) + pow2(2 * p)[0::2][: n - p]


def _rms(x, eps):
    return x * lax.rsqrt(jnp.mean(x * x, axis=-1, keepdims=True) + eps)


def _pack_pairs(x):
    half = x.shape[1] // 2
    hi = lax.bitcast_convert_type(x[:, :half].astype(BF16).astype(F32), U32)
    lo = lax.bitcast_convert_type(x[:, half:].astype(BF16).astype(F32), U32)
    return hi | (lo >> 16)


def _unpack_pairs(w):
    hi = lax.bitcast_convert_type(w & jnp.uint32(0xFFFF0000), F32)
    lo = lax.bitcast_convert_type(w << 16, F32)
    return hi, lo


def _ada_kernel(c_ref, w_ref, b_ref, o_ref):
    c = c_ref[...]
    sc = c * jax.nn.sigmoid(c)
    o_ref[0] = jnp.dot(sc, w_ref[0], preferred_element_type=F32, precision=HIGHEST) + b_ref[0]


def _ada(c, ada_w, ada_b):
    B, D = c.shape
    L, _, D6 = ada_w.shape
    rows = 8
    cp = jnp.zeros((rows, D), F32).at[:B].set(c)
    tn = 1024
    out = pl.pallas_call(
        _ada_kernel,
        grid=(L, D6 // tn),
        in_specs=[pl.BlockSpec((rows, D), lambda l, j: (0, 0)),
                  pl.BlockSpec((1, D, tn), lambda l, j: (l, 0, j)),
                  pl.BlockSpec((1, 1, tn), lambda l, j: (l, 0, j))],
        out_specs=pl.BlockSpec((1, rows, tn), lambda l, j: (l, 0, j)),
        out_shape=jax.ShapeDtypeStruct((L, rows, D6), F32),
        compiler_params=_params(("arbitrary", "arbitrary"), 32),
        name="ada_mod",
    )(cp, ada_w, ada_b.reshape(L, 1, D6))
    return out[:, :B]


def _qkv_kernel(x_ref, g_ref, sc_ref, sh_ref, w_ref, o_ref, h_scr, *, heads_per_step):
    @pl.when(pl.program_id(1) == 0)
    def _():
        y = _rms(x_ref[...], NORM_EPS) * g_ref[...]
        h_scr[...] = (y * (1.0 + sc_ref[0]) + sh_ref[0]).astype(BF16)

    res = jnp.dot(h_scr[...], w_ref[...], preferred_element_type=F32)
    for hh in range(heads_per_step):
        o_ref[hh] = res[:, hh * HEAD_DIM:(hh + 1) * HEAD_DIM].astype(BF16)


def _qkv(x2d, g, sc, sh, w_bf, S, tm=512, tn=1536):
    N, D = x2d.shape
    W3 = w_bf.shape[1]
    hps = tn // HEAD_DIM
    tpb = S // tm
    return pl.pallas_call(
        functools.partial(_qkv_kernel, heads_per_step=hps),
        grid=(N // tm, W3 // tn),
        in_specs=[pl.BlockSpec((tm, D), lambda i, j: (i, 0)),
                  pl.BlockSpec((1, D), lambda i, j: (0, 0)),
                  pl.BlockSpec((1, 1, D), lambda i, j: (i // tpb, 0, 0)),
                  pl.BlockSpec((1, 1, D), lambda i, j: (i // tpb, 0, 0)),
                  pl.BlockSpec((D, tn), lambda i, j: (0, j))],
        out_specs=pl.BlockSpec((hps, tm, HEAD_DIM), lambda i, j: (j, i, 0)),
        out_shape=jax.ShapeDtypeStruct((W3 // HEAD_DIM, N, HEAD_DIM), BF16),
        scratch_shapes=[pltpu.VMEM((tm, D), BF16)],
        compiler_params=_params(("arbitrary", "arbitrary"), 48),
        name="qkv_proj",
    )(x2d, g.reshape(1, D), sc, sh, w_bf)


def _col_to_row(col, eye):
    return jnp.sum(jnp.where(eye, col, 0.0), axis=0, keepdims=True)


def _attn_a_kernel(slopes_ref, q_ref, k_ref, v_ref, o_ref, lse_ref, *, L, dil, half, Tq):
    h = pl.program_id(1)
    wk = 2 * Tq
    n_tiles = L // Tq
    c_dist = LOG2E * slopes_ref[h] * dil
    a = lax.broadcasted_iota(I32, (Tq, wk), 0)
    u = lax.broadcasted_iota(I32, (Tq, wk), 1)
    eye = lax.broadcasted_iota(I32, (Tq, Tq), 0) == lax.broadcasted_iota(I32, (Tq, Tq), 1)

    def table(off):
        d = u - a - off
        ad = jnp.maximum(d, -d)
        return jnp.where(ad <= half, -c_dist * ad.astype(F32), NEG_INF)

    tabs = {off: table(off) for off in sorted({0, half, wk - Tq})}
    for r in range(dil):
        lanes = slice(r * HEAD_DIM, (r + 1) * HEAD_DIM)
        for j in range(n_tiles):
            start = min(max(j * Tq - half, 0), L - wk)
            rows = slice(j * Tq, (j + 1) * Tq)
            q = (q_ref[0, rows, lanes].astype(F32) * (LOG2E * HEAD_DIM ** -0.5)).astype(BF16)
            kw = k_ref[0, start:start + wk, lanes]
            vw = v_ref[0, start:start + wk, lanes]
            s = lax.dot_general(q, kw, _NT, preferred_element_type=F32) + tabs[j * Tq - start]
            m = jnp.max(s, axis=-1, keepdims=True)
            p = jnp.exp2(s - m)
            l = jnp.sum(p, axis=-1, keepdims=True)
            o = jnp.dot(p.astype(BF16), vw, preferred_element_type=F32) / l
            o_ref[0, rows, lanes] = o.astype(BF16)
            lse_ref[0, 0, r:r + 1, rows] = _col_to_row(m + jnp.log2(l), eye)


def _attn_a_pattern(qkv, slopes, B, S, n_heads, heads_a, window, dil, Tq=128):
    N = B * S
    L = S // dil
    half = window // (2 * dil)
    assert L % Tq == 0 and L >= 2 * Tq and 2 * half <= Tq
    view = qkv.reshape(qkv.shape[0], N // dil, dil * HEAD_DIM)
    blk = (1, L, dil * HEAD_DIM)
    o, lse = pl.pallas_call(
        functools.partial(_attn_a_kernel, L=L, dil=dil, half=half, Tq=Tq),
        grid=(B, heads_a),
        in_specs=[pl.BlockSpec(memory_space=pltpu.SMEM),
                  pl.BlockSpec(blk, lambda b, h: (h, b, 0)),
                  pl.BlockSpec(blk, lambda b, h: (n_heads + h, b, 0)),
                  pl.BlockSpec(blk, lambda b, h: (2 * n_heads + h, b, 0))],
        out_specs=[pl.BlockSpec(blk, lambda b, h: (h, b, 0)),
                   pl.BlockSpec((1, 1, dil, L), lambda b, h: (b, h, 0, 0))],
        out_shape=[jax.ShapeDtypeStruct((heads_a, N // dil, dil * HEAD_DIM), BF16),
                   jax.ShapeDtypeStruct((B, heads_a, dil, L), F32)],
        compiler_params=_params(("arbitrary", "arbitrary"), 32),
        name=f"attn_dilated_d{dil}",
    )(slopes, view, view, view)
    lse_nat = lse.transpose(1, 0, 3, 2).reshape(heads_a, N)
    return o.reshape(heads_a, N, HEAD_DIM), lse_nat


def _mix_a_kernel(*refs, n_pat, tm):
    o_refs, lse_refs, out_ref = refs[:n_pat], refs[n_pat:2 * n_pat], refs[2 * n_pat]
    heads = lse_refs[0].shape[0]
    lses = [r[...] for r in lse_refs]
    mx = functools.reduce(jnp.maximum, lses)
    es = [jnp.exp2(x - mx) for x in lses]
    den = functools.reduce(lambda x, y: x + y, es)
    ws = [e / den for e in es]
    pad = jnp.zeros((LANES - n_pat * heads, LANES), F32)
    for c in range(tm // LANES):
        cols = slice(c * LANES, (c + 1) * LANES)
        w_rows = jnp.concatenate([w[:, cols] for w in ws] + [pad], axis=0)
        w_cols = w_rows.T
        for hh in range(heads):
            acc = jnp.zeros((LANES, HEAD_DIM), F32)
            for p in range(n_pat):
                wc = w_cols[:, p * heads + hh:p * heads + hh + 1]
                acc = acc + wc * o_refs[p][hh, cols, :].astype(F32)
            out_ref[cols, hh * HEAD_DIM:(hh + 1) * HEAD_DIM] = acc.astype(BF16)


def _mix_a(os, lses, tm=512):
    heads, N, _ = os[0].shape
    n_pat = len(os)
    assert n_pat * heads <= LANES
    return pl.pallas_call(
        functools.partial(_mix_a_kernel, n_pat=n_pat, tm=tm),
        grid=(N // tm,),
        in_specs=[pl.BlockSpec((heads, tm, HEAD_DIM), lambda i: (0, i, 0))] * n_pat
        + [pl.BlockSpec((heads, tm), lambda i: (0, i))] * n_pat,
        out_specs=pl.BlockSpec((tm, heads * HEAD_DIM), lambda i: (i, 0)),
        out_shape=jax.ShapeDtypeStruct((N, heads * HEAD_DIM), BF16),
        compiler_params=_params(("arbitrary",), 32),
        name="attn_dilated_mix",
    )(*os, *lses)


def _attn_a(qkv, slopes, B, S, n_heads, heads_a):
    parts = [_attn_a_pattern(qkv, slopes, B, S, n_heads, heads_a, w, d) for w, d in DILATED_PATTERNS]
    return _mix_a([p[0] for p in parts], [p[1] for p in parts])


def _attn_b_kernel(slopes_ref, lam_ref, q_ref, k_ref, v_ref, g_ref, o_ref, tab_ref, *,
                   S, Tq, lambda_init):
    h = pl.program_id(1)
    i = pl.program_id(2)
    dh = HEAD_DIM // 2

    @pl.when(i == 0)
    def _():
        shape = (Tq, 2 * S - Tq)
        a = lax.broadcasted_iota(I32, shape, 0)
        u = lax.broadcasted_iota(I32, shape, 1)
        d = a - u + (S - Tq)
        tab_ref[...] = (-LOG2E * slopes_ref[h]) * jnp.maximum(d, -d).astype(F32)

    lv = lam_ref[...]
    lam = (jnp.exp(jnp.sum(lv[0:1] * lv[1:2], axis=-1, keepdims=True))
           - jnp.exp(jnp.sum(lv[2:3] * lv[3:4], axis=-1, keepdims=True)) + lambda_init)

    t0 = i * Tq
    bias = tab_ref[:, pl.ds(pl.multiple_of(S - Tq - t0, LANES), S)]
    q = (q_ref[0].astype(F32) * (LOG2E * dh ** -0.5)).astype(BF16)
    k = k_ref[0]

    def softmax_part(c):
        s = lax.dot_general(q[:, c * dh:(c + 1) * dh], k[:, c * dh:(c + 1) * dh], _NT,
                            preferred_element_type=F32) + bias
        m = jnp.max(s, axis=-1, keepdims=True)
        p = jnp.exp2(s - m)
        return p.astype(BF16), jnp.sum(p, axis=-1, keepdims=True)

    p1, l1 = softmax_part(0)
    p2, l2 = softmax_part(1)
    pv = jnp.dot(jnp.concatenate([p1, p2], axis=0), v_ref[0], preferred_element_type=F32)
    o = pv[:Tq] / l1 - pv[Tq:] * (lam / l2)
    o = _rms(o, SUBLN_EPS) * g_ref[...] * (1.0 - lambda_init)
    o_ref[...] = o.astype(BF16)


def _attn_b(qkv, slopes, lam_vecs, subln_g, B, S, n_heads, head0, heads_b, lambda_init, Tq=128):
    N = B * S
    nq = S // Tq
    return pl.pallas_call(
        functools.partial(_attn_b_kernel, S=S, Tq=Tq, lambda_init=lambda_init),
        grid=(B, heads_b, nq),
        in_specs=[pl.BlockSpec(memory_space=pltpu.SMEM),
                  pl.BlockSpec((4, HEAD_DIM // 2), lambda b, h, i: (0, 0)),
                  pl.BlockSpec((1, Tq, HEAD_DIM), lambda b, h, i: (head0 + h, b * nq + i, 0)),
                  pl.BlockSpec((1, S, HEAD_DIM), lambda b, h, i: (n_heads + head0 + h, b, 0)),
                  pl.BlockSpec((1, S, HEAD_DIM), lambda b, h, i: (2 * n_heads + head0 + h, b, 0)),
                  pl.BlockSpec((1, HEAD_DIM), lambda b, h, i: (0, 0))],
        out_specs=pl.BlockSpec((Tq, HEAD_DIM), lambda b, h, i: (b * nq + i, h)),
        out_shape=jax.ShapeDtypeStruct((N, heads_b * HEAD_DIM), BF16),
        scratch_shapes=[pltpu.VMEM((Tq, 2 * S - Tq), F32)],
        compiler_params=_params(("arbitrary", "arbitrary", "arbitrary"), 48),
        name="attn_diff",
    )(slopes, lam_vecs, qkv, qkv, qkv, subln_g.reshape(1, HEAD_DIM))


def _attn_c_kernel(rpb_ref, q_ref, k_ref, v_ref, o_ref, tab_ref, *, R):
    h = pl.program_id(1)
    W = GRID_W
    kr = min(NA_ROWS, R)
    n_dr = 2 * NA_ROWS - 1
    n_dc = 2 * NA_COLS - 1

    c_io = lax.broadcasted_iota(I32, (W, 2 * W), 0)
    l_io = lax.broadcasted_iota(I32, (W, 2 * W), 1)
    cp = l_io & (W - 1)
    dcm = cp - c_io + (NA_COLS - 1)
    cstart = jnp.clip(c_io - NA_COLS // 2, 0, W - NA_COLS)
    ok = (cp >= cstart) & (cp < cstart + NA_COLS)
    blocks = []
    for dr in range(n_dr):
        blk = jnp.zeros((W, 2 * W), F32)
        for dc in range(n_dc):
            blk = jnp.where(dcm == dc, rpb_ref[(h * n_dr + dr) * n_dc + dc], blk)
        blocks.append(jnp.where(ok, blk, NEG_INF))
    for o in range(NA_ROWS):
        for jj in range(kr // 2):
            tab_ref[o, :, jj * 2 * W:(jj + 1) * 2 * W] = jnp.where(
                l_io < W, blocks[o + 2 * jj], blocks[o + 2 * jj + 1])

    def row(r, carry):
        rs = jnp.clip(r - kr // 2, 0, R - kr)
        q = q_ref[0, pl.ds(pl.multiple_of(r * W, W), W), :]
        kw = k_ref[0, pl.ds(pl.multiple_of(rs * W, W), kr * W), :]
        vw = v_ref[0, pl.ds(pl.multiple_of(rs * W, W), kr * W), :]
        s = lax.dot_general(q, kw, _NT, preferred_element_type=F32) * (HEAD_DIM ** -0.5)
        s = s + tab_ref[rs - r + (NA_ROWS - 1)]
        m = jnp.max(s, axis=-1, keepdims=True)
        p = jnp.exp(s - m)
        l = jnp.sum(p, axis=-1, keepdims=True)
        o = jnp.dot(p.astype(BF16), vw, preferred_element_type=F32) / l
        o_ref[pl.ds(pl.multiple_of(r * W, W), W), :] = o.astype(BF16)
        return carry

    lax.fori_loop(0, R, row, 0)


def _attn_c(qkv, rpb_flat, B, S, n_heads, head0, heads_c):
    N = B * S
    R = S // GRID_W
    kr = min(NA_ROWS, R)
    assert kr == NA_ROWS and kr % 2 == 0
    return pl.pallas_call(
        functools.partial(_attn_c_kernel, R=R),
        grid=(B, heads_c),
        in_specs=[pl.BlockSpec(memory_space=pltpu.SMEM),
                  pl.BlockSpec((1, S, HEAD_DIM), lambda b, h: (head0 + h, b, 0)),
                  pl.BlockSpec((1, S, HEAD_DIM), lambda b, h: (n_heads + head0 + h, b, 0)),
                  pl.BlockSpec((1, S, HEAD_DIM), lambda b, h: (2 * n_heads + head0 + h, b, 0))],
        out_specs=pl.BlockSpec((S, HEAD_DIM), lambda b, h: (b, h)),
        out_shape=jax.ShapeDtypeStruct((N, heads_c * HEAD_DIM), BF16),
        scratch_shapes=[pltpu.VMEM((NA_ROWS, GRID_W, kr * GRID_W), F32)],
        compiler_params=_params(("arbitrary", "arbitrary"), 32),
        name="attn_nbr",
    )(rpb_flat, qkv, qkv, qkv)


def _out_kernel(oa_ref, ob_ref, oc_ref, wa_ref, wb_ref, wc_ref, x_ref, g1_ref, ng_ref, sc_ref,
                sh_ref, rw_ref, x1_ref, h2_ref, lg_ref):
    acc = jnp.dot(oa_ref[...], wa_ref[...], preferred_element_type=F32)
    acc = acc + jnp.dot(ob_ref[...], wb_ref[...], preferred_element_type=F32)
    acc = acc + jnp.dot(oc_ref[...], wc_ref[...], preferred_element_type=F32)
    x1 = x_ref[...] + g1_ref[0] * acc
    x1_ref[...] = x1
    h2 = (_rms(x1, NORM_EPS) * ng_ref[...]) * (1.0 + sc_ref[0]) + sh_ref[0]
    h2_ref[...] = _pack_pairs(h2)
    lg_ref[...] = lax.dot_general(rw_ref[...], h2, _NT, preferred_element_type=F32,
                                  precision=HIGHEST)


def _out_proj(oa, ob, oc, w_bf, x2d, g1, ng, sc, sh, rw_t, S, tm=512):
    N, D = x2d.shape
    wa, wb, wc = oa.shape[1], ob.shape[1], oc.shape[1]
    E = rw_t.shape[0]
    tpb = S // tm
    const = dict(pipeline_mode=pl.Buffered(1))
    row = lambda i: (i, 0)
    per_b = lambda i: (i // tpb, 0, 0)
    return pl.pallas_call(
        _out_kernel,
        grid=(N // tm,),
        in_specs=[pl.BlockSpec((tm, wa), row), pl.BlockSpec((tm, wb), row), pl.BlockSpec((tm, wc), row),
                  pl.BlockSpec((wa, D), lambda i: (0, 0), **const),
                  pl.BlockSpec((wb, D), lambda i: (wa // wb, 0), **const),
                  pl.BlockSpec((wc, D), lambda i: ((wa + wb) // wc, 0), **const),
                  pl.BlockSpec((tm, D), row),
                  pl.BlockSpec((1, 1, D), per_b),
                  pl.BlockSpec((1, D), lambda i: (0, 0)),
                  pl.BlockSpec((1, 1, D), per_b),
                  pl.BlockSpec((1, 1, D), per_b),
                  pl.BlockSpec((E, D), lambda i: (0, 0), **const)],
        out_specs=[pl.BlockSpec((tm, D), row), pl.BlockSpec((tm, D // 2), row),
                   pl.BlockSpec((E, tm), lambda i: (0, i))],
        out_shape=[jax.ShapeDtypeStruct((N, D), F32), jax.ShapeDtypeStruct((N, D // 2), U32),
                   jax.ShapeDtypeStruct((E, N), F32)],
        compiler_params=_params(("arbitrary",), 52),
        name="out_proj",
    )(oa, ob, oc, w_bf, w_bf, w_bf, x2d, g1, ng.reshape(1, D), sc, sh, rw_t)


def _first_argmax(vals, iota, big):
    mx = jnp.max(vals, axis=0, keepdims=True)
    idx = jnp.min(jnp.where(vals == mx, iota, big), axis=0, keepdims=True)
    return mx, idx


def _route_kernel(lg_ref, rb_ref, idx_ref, gate_ref, rank_ref, cnt_ref, tri_ref, carry_ref, *, Tt):
    i = pl.program_id(0)
    E = lg_ref.shape[0]
    gs = E // N_GROUPS

    @pl.when(i == 0)
    def _():
        r = lax.broadcasted_iota(I32, (Tt, Tt), 0)
        c = lax.broadcasted_iota(I32, (Tt, Tt), 1)
        tri_ref[...] = jnp.where(r < c, 1.0, 0.0).astype(BF16)
        carry_ref[...] = jnp.zeros_like(carry_ref)

    scores = jax.nn.sigmoid(lg_ref[...])
    sel = scores + rb_ref[...]
    e_io = lax.broadcasted_iota(I32, (E, Tt), 0).astype(F32)
    s_io = lax.broadcasted_iota(I32, (gs, Tt), 0).astype(F32)
    g_io = lax.broadcasted_iota(I32, (N_GROUPS, Tt), 0).astype(F32)

    grp = []
    for g in range(N_GROUPS):
        blk = sel[g * gs:(g + 1) * gs]
        m1, a1 = _first_argmax(blk, s_io, float(gs))
        m2 = jnp.max(jnp.where(s_io == a1, -jnp.inf, blk), axis=0, keepdims=True)
        grp.append(m1 + m2)
    grp = jnp.concatenate(grp, axis=0)

    gsel = jnp.zeros((N_GROUPS, Tt), F32)
    for _ in range(TOPK_GROUPS):
        _, gi = _first_argmax(grp, g_io, float(N_GROUPS))
        hit = g_io == gi
        gsel = jnp.where(hit, 1.0, gsel)
        grp = jnp.where(hit, -jnp.inf, grp)

    esel = jnp.concatenate(
        [jnp.broadcast_to(gsel[g:g + 1], (gs, Tt)) for g in range(N_GROUPS)], axis=0)
    cur = jnp.where(esel > 0.0, sel, NEG_INF)

    idxs, gates = [], []
    onehot = jnp.zeros((E, Tt), F32)
    for _ in range(TOP_K):
        _, ei = _first_argmax(cur, e_io, float(E))
        hit = e_io == ei
        idxs.append(ei)
        gates.append(jnp.sum(jnp.where(hit, scores, 0.0), axis=0, keepdims=True))
        onehot = jnp.where(hit, 1.0, onehot)
        cur = jnp.where(hit, -jnp.inf, cur)
    gate = jnp.concatenate(gates, axis=0)
    gate = gate / jnp.sum(gate, axis=0, keepdims=True) * ROUTED_SCALE
    idx_ref[...] = jnp.concatenate(idxs, axis=0).astype(I32)
    gate_ref[...] = gate

    oh = onehot.astype(BF16)
    before = jnp.dot(oh, tri_ref[...], preferred_element_type=F32)
    tile_cnt = jnp.dot(oh, jnp.ones((Tt, LANES), BF16), preferred_element_type=F32)
    carry = carry_ref[...]
    rank_mat = before + jnp.concatenate([carry] * (Tt // LANES), axis=1)
    ranks = [jnp.sum(jnp.where(e_io == ei, rank_mat, 0.0), axis=0, keepdims=True) for ei in idxs]
    rank_ref[...] = jnp.concatenate(ranks, axis=0).astype(I32)
    carry_ref[...] = carry + tile_cnt
    cnt_ref[...] = (carry + tile_cnt).astype(I32)


def _route(logits_t, router_bias, Tt=512):
    E, N = logits_t.shape
    tok = lambda i: (0, i)
    return pl.pallas_call(
        functools.partial(_route_kernel, Tt=Tt),
        grid=(N // Tt,),
        in_specs=[pl.BlockSpec((E, Tt), tok), pl.BlockSpec((E, 1), lambda i: (0, 0))],
        out_specs=[pl.BlockSpec((TOP_K, Tt), tok), pl.BlockSpec((TOP_K, Tt), tok),
                   pl.BlockSpec((TOP_K, Tt), tok), pl.BlockSpec((E, LANES), lambda i: (0, 0))],
        out_shape=[jax.ShapeDtypeStruct((TOP_K, N), I32), jax.ShapeDtypeStruct((TOP_K, N), F32),
                   jax.ShapeDtypeStruct((TOP_K, N), I32), jax.ShapeDtypeStruct((E, LANES), I32)],
        scratch_shapes=[pltpu.VMEM((Tt, Tt), BF16), pltpu.VMEM((E, LANES), F32)],
        compiler_params=_params(("arbitrary",), 32),
        name="route_topk",
    )(logits_t, router_bias.reshape(E, 1))


def _dest_kernel(pstart_ref, idx_ref, rank_ref, dest_ref):
    idx = idx_ref[...]
    base = jnp.zeros(idx.shape, I32)
    for e in range(pstart_ref.shape[0]):
        base = jnp.where(idx == e, pstart_ref[e], base)
    dest_ref[...] = base + rank_ref[...]


def _dest(pstart, idx, rank):
    return pl.pallas_call(
        _dest_kernel,
        in_specs=[pl.BlockSpec(memory_space=pltpu.SMEM), pl.BlockSpec(memory_space=pltpu.VMEM),
                  pl.BlockSpec(memory_space=pltpu.VMEM)],
        out_specs=pl.BlockSpec(memory_space=pltpu.VMEM),
        out_shape=jax.ShapeDtypeStruct(idx.shape, I32),
        name="route_dest",
    )(pstart, idx, rank)


def _dispatch_kernel(dest_hbm, h_ref, xs_hbm, dest_smem, idx_sem, row_sem, *, Td):
    i = pl.program_id(0)
    n = Td * TOP_K
    cp = pltpu.make_async_copy(dest_hbm.at[pl.ds(i * n, n)], dest_smem, idx_sem)
    cp.start()
    cp.wait()

    def row_copy(t, k):
        return pltpu.make_async_copy(h_ref.at[pl.ds(t, 1)],
                                     xs_hbm.at[pl.ds(dest_smem[t * TOP_K + k], 1)], row_sem)

    def issue(t, carry):
        for k in range(TOP_K):
            row_copy(t, k).start(priority=k % 2)
        return carry

    lax.fori_loop(0, Td, issue, 0)

    def drain(t, carry):
        for k in range(TOP_K):
            row_copy(t, k).wait()
        return carry

    lax.fori_loop(0, Td, drain, 0)


def _dispatch(dest_flat, h2, P, Td=256):
    N, D = h2.shape
    return pl.pallas_call(
        functools.partial(_dispatch_kernel, Td=Td),
        grid=(N // Td,),
        in_specs=[pl.BlockSpec(memory_space=pl.ANY), pl.BlockSpec((Td, D), lambda i: (i, 0))],
        out_specs=pl.BlockSpec(memory_space=pl.ANY),
        out_shape=jax.ShapeDtypeStruct((P, D), h2.dtype),
        scratch_shapes=[pltpu.SMEM((Td * TOP_K,), I32), pltpu.SemaphoreType.DMA,
                        pltpu.SemaphoreType.DMA],
        compiler_params=_params(("arbitrary",), 32),
        name="moe_dispatch",
    )(dest_flat, h2)


def _moe_kernel(be_ref, nv_ref, nb_ref, xs_ref, wg_ref, wu_ref, wd_ref, ys_ref, wg_s, wu_s, wd_s):
    b = pl.program_id(0)
    active = b < nb_ref[0]
    new_expert = (b == 0) | (be_ref[b] != be_ref[jnp.maximum(b - 1, 0)])

    @pl.when(active & new_expert)
    def _():
        wg_s[...] = wg_ref[0].astype(BF16)
        wu_s[...] = wu_ref[0].astype(BF16)
        wd_s[...] = wd_ref[0].astype(BF16)

    @pl.when(active)
    def _():
        rows = lax.broadcasted_iota(I32, xs_ref.shape, 0)
        hi, lo = _unpack_pairs(jnp.where(rows < nv_ref[b], xs_ref[...], jnp.uint32(0)))
        x = jnp.concatenate([hi.astype(BF16), lo.astype(BF16)], axis=1)
        g = jnp.dot(x, wg_s[...], preferred_element_type=F32)
        u = jnp.dot(x, wu_s[...], preferred_element_type=F32)
        a = (g * jax.nn.sigmoid(g) * u).astype(BF16)
        ys_ref[...] = _pack_pairs(jnp.dot(a, wd_s[...], preferred_element_type=F32))


def _moe(block_expert, nvalid, nblocks, xs, wg, wu, wd):
    P, Dp = xs.shape
    _, D, F = wg.shape
    n_blocks = P // MOE_BLOCK
    blk = lambda b, be, nv, nb: (jnp.minimum(b, nb[0] - 1), 0)
    return pl.pallas_call(
        _moe_kernel,
        grid_spec=pltpu.PrefetchScalarGridSpec(
            num_scalar_prefetch=3,
            grid=(n_blocks,),
            in_specs=[pl.BlockSpec((MOE_BLOCK, Dp), blk),
                      pl.BlockSpec((1, D, F), lambda b, be, nv, nb: (be[b], 0, 0)),
                      pl.BlockSpec((1, D, F), lambda b, be, nv, nb: (be[b], 0, 0)),
                      pl.BlockSpec((1, F, D), lambda b, be, nv, nb: (be[b], 0, 0))],
            out_specs=pl.BlockSpec((MOE_BLOCK, Dp), blk),
            scratch_shapes=[pltpu.VMEM((D, F), BF16), pltpu.VMEM((D, F), BF16),
                            pltpu.VMEM((F, D), BF16)]),
        out_shape=jax.ShapeDtypeStruct((P, Dp), U32),
        compiler_params=_params(("arbitrary",), 52),
        name="moe_experts",
    )(block_expert, nvalid, nblocks, xs, wg, wu, wd)


def _combine_kernel(dest_hbm, ys_hbm, gate_ref, h_ref, x_ref, g2_ref, sg_ref, su_ref, sd_ref,
                    fg_ref, o_ref, dest_smem, ybuf, idx_sem, row_sem, *, Tc, final_norm):
    i = pl.program_id(0)
    n = Tc * TOP_K
    cp = pltpu.make_async_copy(dest_hbm.at[pl.ds(i * n, n)], dest_smem, idx_sem)
    cp.start()
    cp.wait()

    def row_copy(t, k):
        return pltpu.make_async_copy(ys_hbm.at[pl.ds(dest_smem[t * TOP_K + k], 1)],
                                     ybuf.at[k, pl.ds(t, 1)], row_sem)

    def issue(t, carry):
        for k in range(TOP_K):
            row_copy(t, k).start(priority=k % 2)
        return carry

    lax.fori_loop(0, Tc, issue, 0)

    hh, hl = _unpack_pairs(h_ref[...])
    h = jnp.concatenate([hh.astype(BF16), hl.astype(BF16)], axis=1)
    g = jnp.dot(h, sg_ref[...], preferred_element_type=F32)
    u = jnp.dot(h, su_ref[...], preferred_element_type=F32)
    a = (g * jax.nn.sigmoid(g) * u).astype(BF16)
    f = jnp.dot(a, sd_ref[...], preferred_element_type=F32)

    def drain(t, carry):
        for k in range(TOP_K):
            row_copy(t, k).wait()
        return carry

    lax.fori_loop(0, Tc, drain, 0)

    gate = gate_ref[...]
    half = f.shape[1] // 2
    f_hi, f_lo = f[:, :half], f[:, half:]
    for k in range(TOP_K):
        y_hi, y_lo = _unpack_pairs(ybuf[k])
        f_hi = f_hi + gate[:, k:k + 1] * y_hi
        f_lo = f_lo + gate[:, k:k + 1] * y_lo
    f = jnp.concatenate([f_hi, f_lo], axis=1)
    x2 = x_ref[...] + g2_ref[0] * f
    if final_norm:
        x2 = _rms(x2, NORM_EPS) * fg_ref[...]
    o_ref[...] = x2


def _combine(dest_flat, ys, gate_t, h2, x1, g2, sg, su, sd, final_g, S, final_norm, Tc=256):
    N, D = x1.shape
    F = sg.shape[1]
    tpb = S // Tc
    const = dict(pipeline_mode=pl.Buffered(1))
    row = lambda i: (i, 0)
    return pl.pallas_call(
        functools.partial(_combine_kernel, Tc=Tc, final_norm=final_norm),
        grid=(N // Tc,),
        in_specs=[pl.BlockSpec(memory_space=pl.ANY), pl.BlockSpec(memory_space=pl.ANY),
                  pl.BlockSpec((Tc, TOP_K), row),
                  pl.BlockSpec((Tc, D // 2), row), pl.BlockSpec((Tc, D), row),
                  pl.BlockSpec((1, 1, D), lambda i: (i // tpb, 0, 0)),
                  pl.BlockSpec((D, F), lambda i: (0, 0), **const),
                  pl.BlockSpec((D, F), lambda i: (0, 0), **const),
                  pl.BlockSpec((F, D), lambda i: (0, 0), **const),
                  pl.BlockSpec((1, D), lambda i: (0, 0))],
        out_specs=pl.BlockSpec((Tc, D), row),
        out_shape=jax.ShapeDtypeStruct((N, D), F32),
        scratch_shapes=[pltpu.SMEM((Tc * TOP_K,), I32), pltpu.VMEM((TOP_K, Tc, D // 2), U32),
                        pltpu.SemaphoreType.DMA, pltpu.SemaphoreType.DMA],
        compiler_params=_params(("arbitrary",), 52),
        name="moe_combine",
    )(dest_flat, ys, gate_t, h2, x1, g2, sg, su, sd, final_g.reshape(1, D))


def _moe_ffn(h2, logits_t, x1, g2, router_bias, wg, wu, wd, e_off, sg, su, sd, final_g, S,
             final_norm):
    N = h2.shape[0]
    E = logits_t.shape[0]
    idx, gate, rank, cnt = _route(logits_t, router_bias)
    counts = cnt[:, 0]
    padded = (counts + MOE_BLOCK - 1) // MOE_BLOCK * MOE_BLOCK
    pend = jnp.cumsum(padded)
    pstart = (pend - padded).astype(I32)
    n_blocks = -(-(N * TOP_K + E * (MOE_BLOCK - 1)) // MOE_BLOCK)
    bstart = jnp.arange(n_blocks, dtype=I32) * MOE_BLOCK
    block_expert = jnp.minimum(jnp.sum(pend[None, :] <= bstart[:, None], axis=1), E - 1).astype(I32)
    nvalid = jnp.clip(pstart[block_expert] + counts[block_expert] - bstart, 0, MOE_BLOCK).astype(I32)
    nblocks = (pend[-1:] // MOE_BLOCK).astype(I32)

    dest = _dest(pstart, idx, rank)
    dest_flat = dest.T.reshape(N * TOP_K)
    xs = _dispatch(dest_flat, h2, n_blocks * MOE_BLOCK)
    ys = _moe(block_expert + e_off, nvalid, nblocks, xs, wg, wu, wd)
    return _combine(dest_flat, ys, gate.T, h2, x1, g2, sg, su, sd, final_g, S, final_norm)


def kernel(x, c, ada_w, ada_b, norm_mix_g, norm_ffn_g, w_in, lambda_q1, lambda_k1, lambda_q2,
           lambda_k2, subln_g, rpb, w_out, router_w, router_bias, exp_w_gate, exp_w_up,
           exp_w_down, sh_w_gate, sh_w_up, sh_w_down, final_g):
    B, S, D = x.shape
    L = ada_w.shape[0]
    N = B * S
    n_heads = w_in.shape[2] // (3 * HEAD_DIM)
    heads_a = n_heads // HEADS_A_FRAC
    heads_b = n_heads // HEADS_B_FRAC
    heads_c = n_heads - heads_a - heads_b
    slopes = _alibi_slopes(heads_a + heads_b)
    slopes_a = jnp.asarray(slopes[:heads_a], F32)
    slopes_b = jnp.asarray(slopes[heads_a:], F32)

    E, F = exp_w_gate.shape[1], exp_w_gate.shape[3]
    wg_all = exp_w_gate.reshape(L * E, D, F)
    wu_all = exp_w_up.reshape(L * E, D, F)
    wd_all = exp_w_down.reshape(L * E, F, D)

    mod = _ada(c, ada_w, ada_b).reshape(L, B, 6, 1, D)
    xf = x.reshape(N, D)
    for l in range(L):
        sh1, sc1, g1, sh2, sc2, g2 = (mod[l, :, j] for j in range(6))
        qkv = _qkv(xf, norm_mix_g[l], sc1, sh1, w_in[l].astype(BF16), S)
        oa = _attn_a(qkv, slopes_a, B, S, n_heads, heads_a)
        lambda_init = 0.8 - 0.6 * math.exp(-0.3 * l)
        lam_vecs = jnp.stack([lambda_q1[l], lambda_k1[l], lambda_q2[l], lambda_k2[l]]).astype(F32)
        ob = _attn_b(qkv, slopes_b, lam_vecs, subln_g[l], B, S, n_heads, heads_a, heads_b, lambda_init)
        oc = _attn_c(qkv, rpb[l].reshape(-1), B, S, n_heads, heads_a + heads_b, heads_c)
        x1, h2, logits_t = _out_proj(oa, ob, oc, w_out[l].astype(BF16), xf, g1, norm_ffn_g[l],
                                     sc2, sh2, router_w[l].T, S)
        xf = _moe_ffn(h2, logits_t, x1, g2, router_bias[l], wg_all, wu_all, wd_all, l * E,
                      sh_w_gate[l].astype(BF16), sh_w_up[l].astype(BF16),
                      sh_w_down[l].astype(BF16), final_g, S, l == L - 1)
    return xf.reshape(B, S, D)
```

```python
import functools
import math

import jax
import jax.numpy as jnp
from jax import lax
from jax.experimental import pallas as pl
from jax.experimental.pallas import tpu as pltpu

F32 = jnp.float32
BF16 = jnp.bfloat16
I32 = jnp.int32
U32 = jnp.uint32
HIGHEST = lax.Precision.HIGHEST
_NT = (((1,), (1,)), ((), ()))

HEAD_DIM = 128
HEADS_A_FRAC, HEADS_B_FRAC = 2, 4
DILATED_PATTERNS = ((128, 1), (512, 4), (2048, 16))
GRID_W = 64
NA_ROWS = 8
NA_COLS = 16
N_GROUPS = 8
TOPK_GROUPS = 4
TOP_K = 8
ROUTED_SCALE = 2.5
NORM_EPS = 1e-6
SUBLN_EPS = 1e-5
NEG_INF = -1e30
LOG2E = math.log2(math.e)

LANES = 128
VMEM_BYTES_V7X = 64 << 20

MOE_BLOCK = 256


def _params(semantics, vmem_mib):
    return pltpu.CompilerParams(dimension_semantics=semantics,
                                vmem_limit_bytes=min(vmem_mib << 20, VMEM_BYTES_V7X - (4 << 20)))


def _alibi_slopes(n):
    def pow2(m):
        start = 2.0 ** (-8.0 / m)
        return [start ** (i + 1) for i in range(m)]
    p = 2 ** int(math.floor(math.log2(n)))
    return pow2(p) + pow2(2 * p)[0::2][: n - p]


def _rms(x, eps):
    return x * lax.rsqrt(jnp.mean(x * x, axis=-1, keepdims=True) + eps)


def _pack_pairs(x):
    half = x.shape[1] // 2
    hi = lax.bitcast_convert_type(x[:, :half].astype(BF16).astype(F32), U32)
    lo = lax.bitcast_convert_type(x[:, half:].astype(BF16).astype(F32), U32)
    return hi | (lo >> 16)


def _unpack_pairs(w):
    hi = lax.bitcast_convert_type(w & jnp.uint32(0xFFFF0000), F32)
    lo = lax.bitcast_convert_type(w << 16, F32)
    return hi, lo


def _ada_kernel(c_ref, w_ref, b_ref, o_ref):
    c = c_ref[...]
    sc = c * jax.nn.sigmoid(c)
    o_ref[0] = jnp.dot(sc, w_ref[0], preferred_element_type=F32, precision=HIGHEST) + b_ref[0]


def _ada(c, ada_w, ada_b):
    B, D = c.shape
    L, _, D6 = ada_w.shape
    rows = 8
    cp = jnp.zeros((rows, D), F32).at[:B].set(c)
    tn = 1024
    out = pl.pallas_call(
        _ada_kernel,
        grid=(L, D6 // tn),
        in_specs=[pl.BlockSpec((rows, D), lambda l, j: (0, 0)),
                  pl.BlockSpec((1, D, tn), lambda l, j: (l, 0, j)),
                  pl.BlockSpec((1, 1, tn), lambda l, j: (l, 0, j))],
        out_specs=pl.BlockSpec((1, rows, tn), lambda l, j: (l, 0, j)),
        out_shape=jax.ShapeDtypeStruct((L, rows, D6), F32),
        compiler_params=_params(("arbitrary", "arbitrary"), 32),
        name="ada_mod",
    )(cp, ada_w, ada_b.reshape(L, 1, D6))
    return out[:, :B]


def _qkv_kernel(x_ref, g_ref, sc_ref, sh_ref, w_ref, o_ref, h_scr, *, heads_per_step):
    @pl.when(pl.program_id(1) == 0)
    def _():
        y = _rms(x_ref[...], NORM_EPS) * g_ref[...]
        h_scr[...] = (y * (1.0 + sc_ref[0]) + sh_ref[0]).astype(BF16)

    res = jnp.dot(h_scr[...], w_ref[...], preferred_element_type=F32)
    for hh in range(heads_per_step):
        o_ref[hh] = res[:, hh * HEAD_DIM:(hh + 1) * HEAD_DIM].astype(BF16)


def _qkv(x2d, g, sc, sh, w_bf, S, tm=512, tn=1536):
    N, D = x2d.shape
    W3 = w_bf.shape[1]
    hps = tn // HEAD_DIM
    tpb = S // tm
    return pl.pallas_call(
        functools.partial(_qkv_kernel, heads_per_step=hps),
        grid=(N // tm, W3 // tn),
        in_specs=[pl.BlockSpec((tm, D), lambda i, j: (i, 0)),
                  pl.BlockSpec((1, D), lambda i, j: (0, 0)),
                  pl.BlockSpec((1, 1, D), lambda i, j: (i // tpb, 0, 0)),
                  pl.BlockSpec((1, 1, D), lambda i, j: (i // tpb, 0, 0)),
                  pl.BlockSpec((D, tn), lambda i, j: (0, j))],
        out_specs=pl.BlockSpec((hps, tm, HEAD_DIM), lambda i, j: (j, i, 0)),
        out_shape=jax.ShapeDtypeStruct((W3 // HEAD_DIM, N, HEAD_DIM), BF16),
        scratch_shapes=[pltpu.VMEM((tm, D), BF16)],
        compiler_params=_params(("arbitrary", "arbitrary"), 48),
        name="qkv_proj",
    )(x2d, g.reshape(1, D), sc, sh, w_bf)


def _col_to_row(col, eye):
    return jnp.sum(jnp.where(eye, col, 0.0), axis=0, keepdims=True)


def _attn_a_kernel(slopes_ref, q_ref, k_ref, v_ref, o_ref, lse_ref, nat, cq, ck, cv, onat, *,
                   L, dil, half, Tq):
    h = pl.program_id(1)
    wk = 2 * Tq
    n_tiles = L // Tq
    c_dist = LOG2E * slopes_ref[h] * dil
    a = lax.broadcasted_iota(I32, (Tq, wk), 0)
    u = lax.broadcasted_iota(I32, (Tq, wk), 1)
    eye = lax.broadcasted_iota(I32, (Tq, Tq), 0) == lax.broadcasted_iota(I32, (Tq, Tq), 1)

    def table(off):
        d = u - a - off
        ad = jnp.maximum(d, -d)
        return jnp.where(ad <= half, -c_dist * ad.astype(F32), NEG_INF)

    tabs = {off: table(off) for off in sorted({0, half, wk - Tq})}

    for src, dst, scale in ((q_ref, cq, LOG2E * HEAD_DIM ** -0.5), (k_ref, ck, None), (v_ref, cv, None)):
        x = src[0].astype(F32)
        nat[...] = x if scale is None else x * scale
        for r in range(dil):
            dst[r * L:(r + 1) * L, :] = nat[pl.ds(r, L, stride=dil), :].astype(BF16)

    for r in range(dil):
        for j in range(n_tiles):
            start = min(max(j * Tq - half, 0), L - wk)
            q = cq[r * L + j * Tq:r * L + (j + 1) * Tq, :]
            kw = ck[r * L + start:r * L + start + wk, :]
            vw = cv[r * L + start:r * L + start + wk, :]
            s = lax.dot_general(q, kw, _NT, preferred_element_type=F32) + tabs[j * Tq - start]
            m = jnp.max(s, axis=-1, keepdims=True)
            p = jnp.exp2(s - m)
            l = jnp.sum(p, axis=-1, keepdims=True)
            o = jnp.dot(p.astype(BF16), vw, preferred_element_type=F32) / l
            onat[pl.ds(r + j * Tq * dil, Tq, stride=dil), :] = o
            lse_ref[0, 0, r:r + 1, j * Tq:(j + 1) * Tq] = _col_to_row(m + jnp.log2(l), eye)
    o_ref[0] = onat[...].astype(BF16)


def _attn_a_pattern(qkv, slopes, B, S, n_heads, heads_a, window, dil, Tq=128):
    N = B * S
    L = S // dil
    half = window // (2 * dil)
    assert L % Tq == 0 and L >= 2 * Tq and 2 * half <= Tq
    blk = (1, S, HEAD_DIM)
    o, lse = pl.pallas_call(
        functools.partial(_attn_a_kernel, L=L, dil=dil, half=half, Tq=Tq),
        grid=(B, heads_a),
        in_specs=[pl.BlockSpec(memory_space=pltpu.SMEM),
                  pl.BlockSpec(blk, lambda b, h: (h, b, 0)),
                  pl.BlockSpec(blk, lambda b, h: (n_heads + h, b, 0)),
                  pl.BlockSpec(blk, lambda b, h: (2 * n_heads + h, b, 0))],
        out_specs=[pl.BlockSpec(blk, lambda b, h: (h, b, 0)),
                   pl.BlockSpec((1, 1, dil, L), lambda b, h: (b, h, 0, 0))],
        out_shape=[jax.ShapeDtypeStruct((heads_a, N, HEAD_DIM), BF16),
                   jax.ShapeDtypeStruct((B, heads_a, dil, L), F32)],
        scratch_shapes=[pltpu.VMEM((S, HEAD_DIM), F32)] + [pltpu.VMEM((S, HEAD_DIM), BF16)] * 3
        + [pltpu.VMEM((S, HEAD_DIM), F32)],
        compiler_params=_params(("arbitrary", "arbitrary"), 32),
        name=f"attn_dilated_d{dil}",
    )(slopes, qkv, qkv, qkv)
    lse_nat = lse.transpose(1, 0, 3, 2).reshape(heads_a, N)
    return o, lse_nat


def _mix_a_kernel(*refs, n_pat, tm):
    o_refs, lse_refs, out_ref = refs[:n_pat], refs[n_pat:2 * n_pat], refs[2 * n_pat]
    heads = lse_refs[0].shape[0]
    lses = [r[...] for r in lse_refs]
    mx = functools.reduce(jnp.maximum, lses)
    es = [jnp.exp2(x - mx) for x in lses]
    den = functools.reduce(lambda x, y: x + y, es)
    ws = [e / den for e in es]
    pad = jnp.zeros((LANES - n_pat * heads, LANES), F32)
    for c in range(tm // LANES):
        cols = slice(c * LANES, (c + 1) * LANES)
        w_rows = jnp.concatenate([w[:, cols] for w in ws] + [pad], axis=0)
        w_cols = w_rows.T
        for hh in range(heads):
            acc = jnp.zeros((LANES, HEAD_DIM), F32)
            for p in range(n_pat):
                wc = w_cols[:, p * heads + hh:p * heads + hh + 1]
                acc = acc + wc * o_refs[p][hh, cols, :].astype(F32)
            out_ref[cols, hh * HEAD_DIM:(hh + 1) * HEAD_DIM] = acc.astype(BF16)


def _mix_a(os, lses, tm=512):
    heads, N, _ = os[0].shape
    n_pat = len(os)
    assert n_pat * heads <= LANES
    return pl.pallas_call(
        functools.partial(_mix_a_kernel, n_pat=n_pat, tm=tm),
        grid=(N // tm,),
        in_specs=[pl.BlockSpec((heads, tm, HEAD_DIM), lambda i: (0, i, 0))] * n_pat
        + [pl.BlockSpec((heads, tm), lambda i: (0, i))] * n_pat,
        out_specs=pl.BlockSpec((tm, heads * HEAD_DIM), lambda i: (i, 0)),
        out_shape=jax.ShapeDtypeStruct((N, heads * HEAD_DIM), BF16),
        compiler_params=_params(("arbitrary",), 32),
        name="attn_dilated_mix",
    )(*os, *lses)


def _attn_a(qkv, slopes, B, S, n_heads, heads_a):
    parts = [_attn_a_pattern(qkv, slopes, B, S, n_heads, heads_a, w, d) for w, d in DILATED_PATTERNS]
    return _mix_a([p[0] for p in parts], [p[1] for p in parts])


def _attn_b_kernel(slopes_ref, lam_ref, q_ref, k_ref, v_ref, g_ref, o_ref, tab_ref, *,
                   S, Tq, lambda_init):
    h = pl.program_id(1)
    i = pl.program_id(2)
    dh = HEAD_DIM // 2

    @pl.when(i == 0)
    def _():
        shape = (Tq, 2 * S - Tq)
        a = lax.broadcasted_iota(I32, shape, 0)
        u = lax.broadcasted_iota(I32, shape, 1)
        d = a - u + (S - Tq)
        tab_ref[...] = (-LOG2E * slopes_ref[h]) * jnp.maximum(d, -d).astype(F32)

    lv = lam_ref[...]
    lam = (jnp.exp(jnp.sum(lv[0:1] * lv[1:2], axis=-1, keepdims=True))
           - jnp.exp(jnp.sum(lv[2:3] * lv[3:4], axis=-1, keepdims=True)) + lambda_init)

    t0 = i * Tq
    bias = tab_ref[:, pl.ds(pl.multiple_of(S - Tq - t0, LANES), S)]
    q = (q_ref[0].astype(F32) * (LOG2E * dh ** -0.5)).astype(BF16)
    k = k_ref[0]

    def softmax_part(c):
        s = lax.dot_general(q[:, c * dh:(c + 1) * dh], k[:, c * dh:(c + 1) * dh], _NT,
                            preferred_element_type=F32) + bias
        m = jnp.max(s, axis=-1, keepdims=True)
        p = jnp.exp2(s - m)
        return p.astype(BF16), jnp.sum(p, axis=-1, keepdims=True)

    p1, l1 = softmax_part(0)
    p2, l2 = softmax_part(1)
    pv = jnp.dot(jnp.concatenate([p1, p2], axis=0), v_ref[0], preferred_element_type=F32)
    o = pv[:Tq] / l1 - pv[Tq:] * (lam / l2)
    o = _rms(o, SUBLN_EPS) * g_ref[...] * (1.0 - lambda_init)
    o_ref[...] = o.astype(BF16)


def _attn_b(qkv, slopes, lam_vecs, subln_g, B, S, n_heads, head0, heads_b, lambda_init, Tq=128):
    N = B * S
    nq = S // Tq
    return pl.pallas_call(
        functools.partial(_attn_b_kernel, S=S, Tq=Tq, lambda_init=lambda_init),
        grid=(B, heads_b, nq),
        in_specs=[pl.BlockSpec(memory_space=pltpu.SMEM),
                  pl.BlockSpec((4, HEAD_DIM // 2), lambda b, h, i: (0, 0)),
                  pl.BlockSpec((1, Tq, HEAD_DIM), lambda b, h, i: (head0 + h, b * nq + i, 0)),
                  pl.BlockSpec((1, S, HEAD_DIM), lambda b, h, i: (n_heads + head0 + h, b, 0)),
                  pl.BlockSpec((1, S, HEAD_DIM), lambda b, h, i: (2 * n_heads + head0 + h, b, 0)),
                  pl.BlockSpec((1, HEAD_DIM), lambda b, h, i: (0, 0))],
        out_specs=pl.BlockSpec((Tq, HEAD_DIM), lambda b, h, i: (b * nq + i, h)),
        out_shape=jax.ShapeDtypeStruct((N, heads_b * HEAD_DIM), BF16),
        scratch_shapes=[pltpu.VMEM((Tq, 2 * S - Tq), F32)],
        compiler_params=_params(("arbitrary", "arbitrary", "arbitrary"), 48),
        name="attn_diff",
    )(slopes, lam_vecs, qkv, qkv, qkv, subln_g.reshape(1, HEAD_DIM))


def _attn_c_kernel(rpb_ref, q_ref, k_ref, v_ref, o_ref, tab_ref, *, R):
    h = pl.program_id(1)
    W = GRID_W
    kr = min(NA_ROWS, R)
    n_dr = 2 * NA_ROWS - 1
    n_dc = 2 * NA_COLS - 1

    c_io = lax.broadcasted_iota(I32, (W, 2 * W), 0)
    l_io = lax.broadcasted_iota(I32, (W, 2 * W), 1)
    cp = l_io & (W - 1)
    dcm = cp - c_io + (NA_COLS - 1)
    cstart = jnp.clip(c_io - NA_COLS // 2, 0, W - NA_COLS)
    ok = (cp >= cstart) & (cp < cstart + NA_COLS)
    blocks = []
    for dr in range(n_dr):
        blk = jnp.zeros((W, 2 * W), F32)
        for dc in range(n_dc):
            blk = jnp.where(dcm == dc, rpb_ref[(h * n_dr + dr) * n_dc + dc], blk)
        blocks.append(jnp.where(ok, blk, NEG_INF))
    for o in range(NA_ROWS):
        for jj in range(kr // 2):
            tab_ref[o, :, jj * 2 * W:(jj + 1) * 2 * W] = jnp.where(
                l_io < W, blocks[o + 2 * jj], blocks[o + 2 * jj + 1])

    def row(r, carry):
        rs = jnp.clip(r - kr // 2, 0, R - kr)
        q = q_ref[0, pl.ds(pl.multiple_of(r * W, W), W), :]
        kw = k_ref[0, pl.ds(pl.multiple_of(rs * W, W), kr * W), :]
        vw = v_ref[0, pl.ds(pl.multiple_of(rs * W, W), kr * W), :]
        s = lax.dot_general(q, kw, _NT, preferred_element_type=F32) * (HEAD_DIM ** -0.5)
        s = s + tab_ref[rs - r + (NA_ROWS - 1)]
        m = jnp.max(s, axis=-1, keepdims=True)
        p = jnp.exp(s - m)
        l = jnp.sum(p, axis=-1, keepdims=True)
        o = jnp.dot(p.astype(BF16), vw, preferred_element_type=F32) / l
        o_ref[pl.ds(pl.multiple_of(r * W, W), W), :] = o.astype(BF16)
        return carry

    lax.fori_loop(0, R, row, 0, unroll=8 if R % 8 == 0 else 1)


def _attn_c(qkv, rpb_flat, B, S, n_heads, head0, heads_c):
    N = B * S
    R = S // GRID_W
    kr = min(NA_ROWS, R)
    assert kr == NA_ROWS and kr % 2 == 0
    return pl.pallas_call(
        functools.partial(_attn_c_kernel, R=R),
        grid=(B, heads_c),
        in_specs=[pl.BlockSpec(memory_space=pltpu.SMEM),
                  pl.BlockSpec((1, S, HEAD_DIM), lambda b, h: (head0 + h, b, 0)),
                  pl.BlockSpec((1, S, HEAD_DIM), lambda b, h: (n_heads + head0 + h, b, 0)),
                  pl.BlockSpec((1, S, HEAD_DIM), lambda b, h: (2 * n_heads + head0 + h, b, 0))],
        out_specs=pl.BlockSpec((S, HEAD_DIM), lambda b, h: (b, h)),
        out_shape=jax.ShapeDtypeStruct((N, heads_c * HEAD_DIM), BF16),
        scratch_shapes=[pltpu.VMEM((NA_ROWS, GRID_W, kr * GRID_W), F32)],
        compiler_params=_params(("arbitrary", "arbitrary"), 32),
        name="attn_nbr",
    )(rpb_flat, qkv, qkv, qkv)


def _out_kernel(oa_ref, ob_ref, oc_ref, wa_ref, wb_ref, wc_ref, x_ref, g1_ref, ng_ref, sc_ref,
                sh_ref, rw_ref, x1_ref, h2_ref, lg_ref):
    acc = jnp.dot(oa_ref[...], wa_ref[...], preferred_element_type=F32)
    acc = acc + jnp.dot(ob_ref[...], wb_ref[...], preferred_element_type=F32)
    acc = acc + jnp.dot(oc_ref[...], wc_ref[...], preferred_element_type=F32)
    x1 = x_ref[...] + g1_ref[0] * acc
    x1_ref[...] = x1
    h2 = (_rms(x1, NORM_EPS) * ng_ref[...]) * (1.0 + sc_ref[0]) + sh_ref[0]
    h2_ref[...] = _pack_pairs(h2)
    lg_ref[...] = lax.dot_general(rw_ref[...], h2, _NT, preferred_element_type=F32,
                                  precision=HIGHEST)


def _out_proj(oa, ob, oc, w_bf, x2d, g1, ng, sc, sh, rw_t, S, tm=512):
    N, D = x2d.shape
    wa, wb, wc = oa.shape[1], ob.shape[1], oc.shape[1]
    E = rw_t.shape[0]
    tpb = S // tm
    const = dict(pipeline_mode=pl.Buffered(1))
    row = lambda i: (i, 0)
    per_b = lambda i: (i // tpb, 0, 0)
    return pl.pallas_call(
        _out_kernel,
        grid=(N // tm,),
        in_specs=[pl.BlockSpec((tm, wa), row), pl.BlockSpec((tm, wb), row), pl.BlockSpec((tm, wc), row),
                  pl.BlockSpec((wa, D), lambda i: (0, 0), **const),
                  pl.BlockSpec((wb, D), lambda i: (wa // wb, 0), **const),
                  pl.BlockSpec((wc, D), lambda i: ((wa + wb) // wc, 0), **const),
                  pl.BlockSpec((tm, D), row),
                  pl.BlockSpec((1, 1, D), per_b),
                  pl.BlockSpec((1, D), lambda i: (0, 0)),
                  pl.BlockSpec((1, 1, D), per_b),
                  pl.BlockSpec((1, 1, D), per_b),
                  pl.BlockSpec((E, D), lambda i: (0, 0), **const)],
        out_specs=[pl.BlockSpec((tm, D), row), pl.BlockSpec((tm, D // 2), row),
                   pl.BlockSpec((E, tm), lambda i: (0, i))],
        out_shape=[jax.ShapeDtypeStruct((N, D), F32), jax.ShapeDtypeStruct((N, D // 2), U32),
                   jax.ShapeDtypeStruct((E, N), F32)],
        compiler_params=_params(("arbitrary",), 52),
        name="out_proj",
    )(oa, ob, oc, w_bf, w_bf, w_bf, x2d, g1, ng.reshape(1, D), sc, sh, rw_t)


def _first_argmax(vals, iota, big):
    mx = jnp.max(vals, axis=0, keepdims=True)
    idx = jnp.min(jnp.where(vals == mx, iota, big), axis=0, keepdims=True)
    return mx, idx


def _route_kernel(lg_ref, rb_ref, idx_ref, gate_ref, rank_ref, cnt_ref, tri_ref, carry_ref, *, Tt):
    i = pl.program_id(0)
    E = lg_ref.shape[0]
    gs = E // N_GROUPS

    @pl.when(i == 0)
    def _():
        r = lax.broadcasted_iota(I32, (Tt, Tt), 0)
        c = lax.broadcasted_iota(I32, (Tt, Tt), 1)
        tri_ref[...] = jnp.where(r < c, 1.0, 0.0).astype(BF16)
        carry_ref[...] = jnp.zeros_like(carry_ref)

    scores = jax.nn.sigmoid(lg_ref[...])
    sel = scores + rb_ref[...]
    e_io = lax.broadcasted_iota(I32, (E, Tt), 0).astype(F32)
    s_io = lax.broadcasted_iota(I32, (gs, Tt), 0).astype(F32)
    g_io = lax.broadcasted_iota(I32, (N_GROUPS, Tt), 0).astype(F32)

    grp = []
    for g in range(N_GROUPS):
        blk = sel[g * gs:(g + 1) * gs]
        m1, a1 = _first_argmax(blk, s_io, float(gs))
        m2 = jnp.max(jnp.where(s_io == a1, -jnp.inf, blk), axis=0, keepdims=True)
        grp.append(m1 + m2)
    grp = jnp.concatenate(grp, axis=0)

    gsel = jnp.zeros((N_GROUPS, Tt), F32)
    for _ in range(TOPK_GROUPS):
        _, gi = _first_argmax(grp, g_io, float(N_GROUPS))
        hit = g_io == gi
        gsel = jnp.where(hit, 1.0, gsel)
        grp = jnp.where(hit, -jnp.inf, grp)

    esel = jnp.concatenate(
        [jnp.broadcast_to(gsel[g:g + 1], (gs, Tt)) for g in range(N_GROUPS)], axis=0)
    cur = jnp.where(esel > 0.0, sel, NEG_INF)

    idxs, gates = [], []
    onehot = jnp.zeros((E, Tt), F32)
    for _ in range(TOP_K):
        _, ei = _first_argmax(cur, e_io, float(E))
        hit = e_io == ei
        idxs.append(ei)
        gates.append(jnp.sum(jnp.where(hit, scores, 0.0), axis=0, keepdims=True))
        onehot = jnp.where(hit, 1.0, onehot)
        cur = jnp.where(hit, -jnp.inf, cur)
    gate = jnp.concatenate(gates, axis=0)
    gate = gate / jnp.sum(gate, axis=0, keepdims=True) * ROUTED_SCALE
    idx_ref[...] = jnp.concatenate(idxs, axis=0).astype(I32)
    gate_ref[...] = gate

    oh = onehot.astype(BF16)
    before = jnp.dot(oh, tri_ref[...], preferred_element_type=F32)
    tile_cnt = jnp.dot(oh, jnp.ones((Tt, LANES), BF16), preferred_element_type=F32)
    carry = carry_ref[...]
    rank_mat = before + jnp.concatenate([carry] * (Tt // LANES), axis=1)
    ranks = [jnp.sum(jnp.where(e_io == ei, rank_mat, 0.0), axis=0, keepdims=True) for ei in idxs]
    rank_ref[...] = jnp.concatenate(ranks, axis=0).astype(I32)
    carry_ref[...] = carry + tile_cnt
    cnt_ref[...] = (carry + tile_cnt).astype(I32)


def _route(logits_t, router_bias, Tt=512):
    E, N = logits_t.shape
    tok = lambda i: (0, i)
    return pl.pallas_call(
        functools.partial(_route_kernel, Tt=Tt),
        grid=(N // Tt,),
        in_specs=[pl.BlockSpec((E, Tt), tok), pl.BlockSpec((E, 1), lambda i: (0, 0))],
        out_specs=[pl.BlockSpec((TOP_K, Tt), tok), pl.BlockSpec((TOP_K, Tt), tok),
                   pl.BlockSpec((TOP_K, Tt), tok), pl.BlockSpec((E, LANES), lambda i: (0, 0))],
        out_shape=[jax.ShapeDtypeStruct((TOP_K, N), I32), jax.ShapeDtypeStruct((TOP_K, N), F32),
                   jax.ShapeDtypeStruct((TOP_K, N), I32), jax.ShapeDtypeStruct((E, LANES), I32)],
        scratch_shapes=[pltpu.VMEM((Tt, Tt), BF16), pltpu.VMEM((E, LANES), F32)],
        compiler_params=_params(("arbitrary",), 32),
        name="route_topk",
    )(logits_t, router_bias.reshape(E, 1))


def _dest_kernel(pstart_ref, idx_ref, rank_ref, dest_ref):
    idx = idx_ref[...]
    base = jnp.zeros(idx.shape, I32)
    for e in range(pstart_ref.shape[0]):
        base = jnp.where(idx == e, pstart_ref[e], base)
    dest_ref[...] = base + rank_ref[...]


def _dest(pstart, idx, rank):
    return pl.pallas_call(
        _dest_kernel,
        in_specs=[pl.BlockSpec(memory_space=pltpu.SMEM), pl.BlockSpec(memory_space=pltpu.VMEM),
                  pl.BlockSpec(memory_space=pltpu.VMEM)],
        out_specs=pl.BlockSpec(memory_space=pltpu.VMEM),
        out_shape=jax.ShapeDtypeStruct(idx.shape, I32),
        name="route_dest",
    )(pstart, idx, rank)


def _dispatch_kernel(dest_hbm, h_ref, xs_hbm, dest_smem, idx_sem, row_sem, *, Td):
    i = pl.program_id(0)
    n = Td * TOP_K
    cp = pltpu.make_async_copy(dest_hbm.at[pl.ds(i * n, n)], dest_smem, idx_sem)
    cp.start()
    cp.wait()

    def row_copy(t, k):
        return pltpu.make_async_copy(h_ref.at[pl.ds(t, 1)],
                                     xs_hbm.at[pl.ds(dest_smem[t * TOP_K + k], 1)], row_sem)

    def issue(t, carry):
        for k in range(TOP_K):
            row_copy(t, k).start(priority=k % 2)
        return carry

    lax.fori_loop(0, Td, issue, 0)

    def drain(t, carry):
        for k in range(TOP_K):
            row_copy(t, k).wait()
        return carry

    lax.fori_loop(0, Td, drain, 0)


def _dispatch(dest_flat, h2, P, Td=256):
    N, D = h2.shape
    return pl.pallas_call(
        functools.partial(_dispatch_kernel, Td=Td),
        grid=(N // Td,),
        in_specs=[pl.BlockSpec(memory_space=pl.ANY), pl.BlockSpec((Td, D), lambda i: (i, 0))],
        out_specs=pl.BlockSpec(memory_space=pl.ANY),
        out_shape=jax.ShapeDtypeStruct((P, D), h2.dtype),
        scratch_shapes=[pltpu.SMEM((Td * TOP_K,), I32), pltpu.SemaphoreType.DMA,
                        pltpu.SemaphoreType.DMA],
        compiler_params=_params(("arbitrary",), 32),
        name="moe_dispatch",
    )(dest_flat, h2)


def _moe_kernel(be_ref, nv_ref, nb_ref, xs_ref, wg_ref, wu_ref, wd_ref, ys_ref, wg_s, wu_s, wd_s):
    b = pl.program_id(0)
    active = b < nb_ref[0]
    new_expert = (b == 0) | (be_ref[b] != be_ref[jnp.maximum(b - 1, 0)])

    @pl.when(active & new_expert)
    def _():
        wg_s[...] = wg_ref[0].astype(BF16)
        wu_s[...] = wu_ref[0].astype(BF16)
        wd_s[...] = wd_ref[0].astype(BF16)

    @pl.when(active)
    def _():
        rows = lax.broadcasted_iota(I32, xs_ref.shape, 0)
        hi, lo = _unpack_pairs(jnp.where(rows < nv_ref[b], xs_ref[...], jnp.uint32(0)))
        x = jnp.concatenate([hi.astype(BF16), lo.astype(BF16)], axis=1)
        g = jnp.dot(x, wg_s[...], preferred_element_type=F32)
        u = jnp.dot(x, wu_s[...], preferred_element_type=F32)
        a = (g * jax.nn.sigmoid(g) * u).astype(BF16)
        ys_ref[...] = _pack_pairs(jnp.dot(a, wd_s[...], preferred_element_type=F32))


def _moe(block_expert, nvalid, nblocks, xs, wg, wu, wd):
    P, Dp = xs.shape
    _, D, F = wg.shape
    n_blocks = P // MOE_BLOCK
    blk = lambda b, be, nv, nb: (jnp.minimum(b, nb[0] - 1), 0)
    return pl.pallas_call(
        _moe_kernel,
        grid_spec=pltpu.PrefetchScalarGridSpec(
            num_scalar_prefetch=3,
            grid=(n_blocks,),
            in_specs=[pl.BlockSpec((MOE_BLOCK, Dp), blk),
                      pl.BlockSpec((1, D, F), lambda b, be, nv, nb: (be[b], 0, 0)),
                      pl.BlockSpec((1, D, F), lambda b, be, nv, nb: (be[b], 0, 0)),
                      pl.BlockSpec((1, F, D), lambda b, be, nv, nb: (be[b], 0, 0))],
            out_specs=pl.BlockSpec((MOE_BLOCK, Dp), blk),
            scratch_shapes=[pltpu.VMEM((D, F), BF16), pltpu.VMEM((D, F), BF16),
                            pltpu.VMEM((F, D), BF16)]),
        out_shape=jax.ShapeDtypeStruct((P, Dp), U32),
        compiler_params=_params(("arbitrary",), 52),
        name="moe_experts",
    )(block_expert, nvalid, nblocks, xs, wg, wu, wd)


def _combine_kernel(dest_hbm, ys_hbm, gate_ref, h_ref, x_ref, g2_ref, sg_ref, su_ref, sd_ref,
                    fg_ref, o_ref, dest_smem, ybuf, idx_sem, row_sem, *, Tc, final_norm):
    i = pl.program_id(0)
    n = Tc * TOP_K
    cp = pltpu.make_async_copy(dest_hbm.at[pl.ds(i * n, n)], dest_smem, idx_sem)
    cp.start()
    cp.wait()

    def row_copy(t, k):
        return pltpu.make_async_copy(ys_hbm.at[pl.ds(dest_smem[t * TOP_K + k], 1)],
                                     ybuf.at[k, pl.ds(t, 1)], row_sem)

    def issue(t, carry):
        for k in range(TOP_K):
            row_copy(t, k).start(priority=k % 2)
        return carry

    lax.fori_loop(0, Tc, issue, 0)

    hh, hl = _unpack_pairs(h_ref[...])
    h = jnp.concatenate([hh.astype(BF16), hl.astype(BF16)], axis=1)
    g = jnp.dot(h, sg_ref[...], preferred_element_type=F32)
    u = jnp.dot(h, su_ref[...], preferred_element_type=F32)
    a = (g * jax.nn.sigmoid(g) * u).astype(BF16)
    f = jnp.dot(a, sd_ref[...], preferred_element_type=F32)

    def drain(t, carry):
        for k in range(TOP_K):
            row_copy(t, k).wait()
        return carry

    lax.fori_loop(0, Tc, drain, 0)

    gate = gate_ref[...]
    half = f.shape[1] // 2
    f_hi, f_lo = f[:, :half], f[:, half:]
    for k in range(TOP_K):
        y_hi, y_lo = _unpack_pairs(ybuf[k])
        f_hi = f_hi + gate[:, k:k + 1] * y_hi
        f_lo = f_lo + gate[:, k:k + 1] * y_lo
    f = jnp.concatenate([f_hi, f_lo], axis=1)
    x2 = x_ref[...] + g2_ref[0] * f
    if final_norm:
        x2 = _rms(x2, NORM_EPS) * fg_ref[...]
    o_ref[...] = x2


def _combine(dest_flat, ys, gate_t, h2, x1, g2, sg, su, sd, final_g, S, final_norm, Tc=256):
    N, D = x1.shape
    F = sg.shape[1]
    tpb = S // Tc
    const = dict(pipeline_mode=pl.Buffered(1))
    row = lambda i: (i, 0)
    return pl.pallas_call(
        functools.partial(_combine_kernel, Tc=Tc, final_norm=final_norm),
        grid=(N // Tc,),
        in_specs=[pl.BlockSpec(memory_space=pl.ANY), pl.BlockSpec(memory_space=pl.ANY),
                  pl.BlockSpec((Tc, TOP_K), row),
                  pl.BlockSpec((Tc, D // 2), row), pl.BlockSpec((Tc, D), row),
                  pl.BlockSpec((1, 1, D), lambda i: (i // tpb, 0, 0)),
                  pl.BlockSpec((D, F), lambda i: (0, 0), **const),
                  pl.BlockSpec((D, F), lambda i: (0, 0), **const),
                  pl.BlockSpec((F, D), lambda i: (0, 0), **const),
                  pl.BlockSpec((1, D), lambda i: (0, 0))],
        out_specs=pl.BlockSpec((Tc, D), row),
        out_shape=jax.ShapeDtypeStruct((N, D), F32),
        scratch_shapes=[pltpu.SMEM((Tc * TOP_K,), I32), pltpu.VMEM((TOP_K, Tc, D // 2), U32),
                        pltpu.SemaphoreType.DMA, pltpu.SemaphoreType.DMA],
        compiler_params=_params(("arbitrary",), 52),
        name="moe_combine",
    )(dest_flat, ys, gate_t, h2, x1, g2, sg, su, sd, final_g.reshape(1, D))


def _moe_ffn(h2, logits_t, x1, g2, router_bias, wg, wu, wd, e_off, sg, su, sd, final_g, S,
             final_norm):
    N = h2.shape[0]
    E = logits_t.shape[0]
    idx, gate, rank, cnt = _route(logits_t, router_bias)
    counts = cnt[:, 0]
    padded = (counts + MOE_BLOCK - 1) // MOE_BLOCK * MOE_BLOCK
    pend = jnp.cumsum(padded)
    pstart = (pend - padded).astype(I32)
    n_blocks = -(-(N * TOP_K + E * (MOE_BLOCK - 1)) // MOE_BLOCK)
    bstart = jnp.arange(n_blocks, dtype=I32) * MOE_BLOCK
    block_expert = jnp.minimum(jnp.sum(pend[None, :] <= bstart[:, None], axis=1), E - 1).astype(I32)
    nvalid = jnp.clip(pstart[block_expert] + counts[block_expert] - bstart, 0, MOE_BLOCK).astype(I32)
    nblocks = (pend[-1:] // MOE_BLOCK).astype(I32)

    dest = _dest(pstart, idx, rank)
    dest_flat = dest.T.reshape(N * TOP_K)
    xs = _dispatch(dest_flat, h2, n_blocks * MOE_BLOCK)
    ys = _moe(block_expert + e_off, nvalid, nblocks, xs, wg, wu, wd)
    return _combine(dest_flat, ys, gate.T, h2, x1, g2, sg, su, sd, final_g, S, final_norm)


def kernel(x, c, ada_w, ada_b, norm_mix_g, norm_ffn_g, w_in, lambda_q1, lambda_k1, lambda_q2,
           lambda_k2, subln_g, rpb, w_out, router_w, router_bias, exp_w_gate, exp_w_up,
           exp_w_down, sh_w_gate, sh_w_up, sh_w_down, final_g):
    B, S, D = x.shape
    L = ada_w.shape[0]
    N = B * S
    n_heads = w_in.shape[2] // (3 * HEAD_DIM)
    heads_a = n_heads // HEADS_A_FRAC
    heads_b = n_heads // HEADS_B_FRAC
    heads_c = n_heads - heads_a - heads_b
    slopes = _alibi_slopes(heads_a + heads_b)
    slopes_a = jnp.asarray(slopes[:heads_a], F32)
    slopes_b = jnp.asarray(slopes[heads_a:], F32)

    E, F = exp_w_gate.shape[1], exp_w_gate.shape[3]
    wg_all = exp_w_gate.reshape(L * E, D, F)
    wu_all = exp_w_up.reshape(L * E, D, F)
    wd_all = exp_w_down.reshape(L * E, F, D)

    mod = _ada(c, ada_w, ada_b).reshape(L, B, 6, 1, D)
    xf = x.reshape(N, D)
    for l in range(L):
        sh1, sc1, g1, sh2, sc2, g2 = (mod[l, :, j] for j in range(6))
        qkv = _qkv(xf, norm_mix_g[l], sc1, sh1, w_in[l].astype(BF16), S)
        oa = _attn_a(qkv, slopes_a, B, S, n_heads, heads_a)
        lambda_init = 0.8 - 0.6 * math.exp(-0.3 * l)
        lam_vecs = jnp.stack([lambda_q1[l], lambda_k1[l], lambda_q2[l], lambda_k2[l]]).astype(F32)
        ob = _attn_b(qkv, slopes_b, lam_vecs, subln_g[l], B, S, n_heads, heads_a, heads_b, lambda_init)
        oc = _attn_c(qkv, rpb[l].reshape(-1), B, S, n_heads, heads_a + heads_b, heads_c)
        x1, h2, logits_t = _out_proj(oa, ob, oc, w_out[l].astype(BF16), xf, g1, norm_ffn_g[l],
                                     sc2, sh2, router_w[l].T, S)
        xf = _moe_ffn(h2, logits_t, x1, g2, router_bias[l], wg_all, wu_all, wd_all, l * E,
                      sh_w_gate[l].astype(BF16), sh_w_up[l].astype(BF16),
                      sh_w_down[l].astype(BF16), final_g, S, l == L - 1)
    return xf.reshape(B, S, D)
```

```python
import functools
import math

import jax
import jax.numpy as jnp
from jax import lax
from jax.experimental import pallas as pl
from jax.experimental.pallas import tpu as pltpu

F32 = jnp.float32
BF16 = jnp.bfloat16
I32 = jnp.int32
U32 = jnp.uint32
HIGHEST = lax.Precision.HIGHEST
_NT = (((1,), (1,)), ((), ()))

HEAD_DIM = 128
HEADS_A_FRAC, HEADS_B_FRAC = 2, 4
DILATED_PATTERNS = ((128, 1), (512, 4), (2048, 16))
GRID_W = 64
NA_ROWS = 8
NA_COLS = 16
N_GROUPS = 8
TOPK_GROUPS = 4
TOP_K = 8
ROUTED_SCALE = 2.5
NORM_EPS = 1e-6
SUBLN_EPS = 1e-5
NEG_INF = -1e30
LOG2E = math.log2(math.e)

LANES = 128
VMEM_BYTES_V7X = 64 << 20

MOE_BLOCK = 512


def _params(semantics, vmem_mib):
    return pltpu.CompilerParams(dimension_semantics=semantics,
                                vmem_limit_bytes=min(vmem_mib << 20, VMEM_BYTES_V7X - (4 << 20)))


def _alibi_slopes(n):
    def pow2(m):
        start = 2.0 ** (-8.0 / m)
        return [start ** (i + 1) for i in range(m)]
    p = 2 ** int(math.floor(math.log2(n)))
    return pow2(p) + pow2(2 * p)[0::2][: n - p]


def _rms(x, eps):
    return x * lax.rsqrt(jnp.mean(x * x, axis=-1, keepdims=True) + eps)


def _pack_pairs(x):
    half = x.shape[1] // 2
    hi = lax.bitcast_convert_type(x[:, :half].astype(BF16).astype(F32), U32)
    lo = lax.bitcast_convert_type(x[:, half:].astype(BF16).astype(F32), U32)
    return hi | (lo >> 16)


def _unpack_pairs(w):
    hi = lax.bitcast_convert_type(w & jnp.uint32(0xFFFF0000), F32)
    lo = lax.bitcast_convert_type(w << 16, F32)
    return hi, lo


def _ada_kernel(c_ref, w_ref, b_ref, o_ref):
    c = c_ref[...]
    sc = c * jax.nn.sigmoid(c)
    o_ref[0] = jnp.dot(sc, w_ref[0], preferred_element_type=F32, precision=HIGHEST) + b_ref[0]


def _ada(c, ada_w, ada_b):
    B, D = c.shape
    L, _, D6 = ada_w.shape
    rows = 8
    cp = jnp.zeros((rows, D), F32).at[:B].set(c)
    tn = 1024
    out = pl.pallas_call(
        _ada_kernel,
        grid=(L, D6 // tn),
        in_specs=[pl.BlockSpec((rows, D), lambda l, j: (0, 0)),
                  pl.BlockSpec((1, D, tn), lambda l, j: (l, 0, j)),
                  pl.BlockSpec((1, 1, tn), lambda l, j: (l, 0, j))],
        out_specs=pl.BlockSpec((1, rows, tn), lambda l, j: (l, 0, j)),
        out_shape=jax.ShapeDtypeStruct((L, rows, D6), F32),
        compiler_params=_params(("arbitrary", "arbitrary"), 32),
        name="ada_mod",
    )(cp, ada_w, ada_b.reshape(L, 1, D6))
    return out[:, :B]


def _qkv_kernel(x_ref, g_ref, sc_ref, sh_ref, w_ref, o_ref, h_scr, *, heads_per_step):
    @pl.when(pl.program_id(1) == 0)
    def _():
        y = _rms(x_ref[...], NORM_EPS) * g_ref[...]
        h_scr[...] = (y * (1.0 + sc_ref[0]) + sh_ref[0]).astype(BF16)

    res = jnp.dot(h_scr[...], w_ref[...], preferred_element_type=F32)
    for hh in range(heads_per_step):
        o_ref[hh] = res[:, hh * HEAD_DIM:(hh + 1) * HEAD_DIM].astype(BF16)


def _qkv(x2d, g, sc, sh, w_bf, S, tm=512, tn=1536):
    N, D = x2d.shape
    W3 = w_bf.shape[1]
    hps = tn // HEAD_DIM
    tpb = S // tm
    return pl.pallas_call(
        functools.partial(_qkv_kernel, heads_per_step=hps),
        grid=(N // tm, W3 // tn),
        in_specs=[pl.BlockSpec((tm, D), lambda i, j: (i, 0)),
                  pl.BlockSpec((1, D), lambda i, j: (0, 0)),
                  pl.BlockSpec((1, 1, D), lambda i, j: (i // tpb, 0, 0)),
                  pl.BlockSpec((1, 1, D), lambda i, j: (i // tpb, 0, 0)),
                  pl.BlockSpec((D, tn), lambda i, j: (0, j))],
        out_specs=pl.BlockSpec((hps, tm, HEAD_DIM), lambda i, j: (j, i, 0)),
        out_shape=jax.ShapeDtypeStruct((W3 // HEAD_DIM, N, HEAD_DIM), BF16),
        scratch_shapes=[pltpu.VMEM((tm, D), BF16)],
        compiler_params=_params(("arbitrary", "arbitrary"), 48),
        name="qkv_proj",
    )(x2d, g.reshape(1, D), sc, sh, w_bf)


def _col_to_row(col, eye):
    return jnp.sum(jnp.where(eye, col, 0.0), axis=0, keepdims=True)


def _attn_a_kernel(slopes_ref, q_ref, k_ref, v_ref, o_ref, lse_ref, nat, cq, ck, cv, onat, *,
                   L, dil, half, Tq):
    h = pl.program_id(1)
    wk = 2 * Tq
    n_tiles = L // Tq
    c_dist = LOG2E * slopes_ref[h] * dil
    a = lax.broadcasted_iota(I32, (Tq, wk), 0)
    u = lax.broadcasted_iota(I32, (Tq, wk), 1)
    eye = lax.broadcasted_iota(I32, (Tq, Tq), 0) == lax.broadcasted_iota(I32, (Tq, Tq), 1)

    def table(off):
        d = u - a - off
        ad = jnp.maximum(d, -d)
        return jnp.where(ad <= half, -c_dist * ad.astype(F32), NEG_INF)

    tabs = {off: table(off) for off in sorted({0, half, wk - Tq})}

    for src, dst, scale in ((q_ref, cq, LOG2E * HEAD_DIM ** -0.5), (k_ref, ck, None), (v_ref, cv, None)):
        x = src[0].astype(F32)
        nat[...] = x if scale is None else x * scale
        for r in range(dil):
            dst[r * L:(r + 1) * L, :] = nat[pl.ds(r, L, stride=dil), :].astype(BF16)

    for r in range(dil):
        for j in range(n_tiles):
            start = min(max(j * Tq - half, 0), L - wk)
            q = cq[r * L + j * Tq:r * L + (j + 1) * Tq, :]
            kw = ck[r * L + start:r * L + start + wk, :]
            vw = cv[r * L + start:r * L + start + wk, :]
            s = lax.dot_general(q, kw, _NT, preferred_element_type=F32) + tabs[j * Tq - start]
            m = jnp.max(s, axis=-1, keepdims=True)
            p = jnp.exp2(s - m)
            l = jnp.sum(p, axis=-1, keepdims=True)
            o = jnp.dot(p.astype(BF16), vw, preferred_element_type=F32) / l
            onat[pl.ds(r + j * Tq * dil, Tq, stride=dil), :] = o
            lse_ref[0, 0, r:r + 1, j * Tq:(j + 1) * Tq] = _col_to_row(m + jnp.log2(l), eye)
    o_ref[0] = onat[...].astype(BF16)


def _attn_a_pattern(qkv, slopes, B, S, n_heads, heads_a, window, dil, Tq=128):
    N = B * S
    L = S // dil
    half = window // (2 * dil)
    assert L % Tq == 0 and L >= 2 * Tq and 2 * half <= Tq
    blk = (1, S, HEAD_DIM)
    o, lse = pl.pallas_call(
        functools.partial(_attn_a_kernel, L=L, dil=dil, half=half, Tq=Tq),
        grid=(B, heads_a),
        in_specs=[pl.BlockSpec(memory_space=pltpu.SMEM),
                  pl.BlockSpec(blk, lambda b, h: (h, b, 0)),
                  pl.BlockSpec(blk, lambda b, h: (n_heads + h, b, 0)),
                  pl.BlockSpec(blk, lambda b, h: (2 * n_heads + h, b, 0))],
        out_specs=[pl.BlockSpec(blk, lambda b, h: (h, b, 0)),
                   pl.BlockSpec((1, 1, dil, L), lambda b, h: (b, h, 0, 0))],
        out_shape=[jax.ShapeDtypeStruct((heads_a, N, HEAD_DIM), BF16),
                   jax.ShapeDtypeStruct((B, heads_a, dil, L), F32)],
        scratch_shapes=[pltpu.VMEM((S, HEAD_DIM), F32)] + [pltpu.VMEM((S, HEAD_DIM), BF16)] * 3
        + [pltpu.VMEM((S, HEAD_DIM), F32)],
        compiler_params=_params(("arbitrary", "arbitrary"), 32),
        name=f"attn_dilated_d{dil}",
    )(slopes, qkv, qkv, qkv)
    lse_nat = lse.transpose(1, 0, 3, 2).reshape(heads_a, N)
    return o, lse_nat


def _mix_a_kernel(*refs, n_pat, tm):
    o_refs, lse_refs, out_ref = refs[:n_pat], refs[n_pat:2 * n_pat], refs[2 * n_pat]
    heads = lse_refs[0].shape[0]
    lses = [r[...] for r in lse_refs]
    mx = functools.reduce(jnp.maximum, lses)
    es = [jnp.exp2(x - mx) for x in lses]
    den = functools.reduce(lambda x, y: x + y, es)
    ws = [e / den for e in es]
    pad = jnp.zeros((LANES - n_pat * heads, LANES), F32)
    for c in range(tm // LANES):
        cols = slice(c * LANES, (c + 1) * LANES)
        w_rows = jnp.concatenate([w[:, cols] for w in ws] + [pad], axis=0)
        w_cols = w_rows.T
        for hh in range(heads):
            acc = jnp.zeros((LANES, HEAD_DIM), F32)
            for p in range(n_pat):
                wc = w_cols[:, p * heads + hh:p * heads + hh + 1]
                acc = acc + wc * o_refs[p][hh, cols, :].astype(F32)
            out_ref[cols, hh * HEAD_DIM:(hh + 1) * HEAD_DIM] = acc.astype(BF16)


def _mix_a(os, lses, tm=512):
    heads, N, _ = os[0].shape
    n_pat = len(os)
    assert n_pat * heads <= LANES
    return pl.pallas_call(
        functools.partial(_mix_a_kernel, n_pat=n_pat, tm=tm),
        grid=(N // tm,),
        in_specs=[pl.BlockSpec((heads, tm, HEAD_DIM), lambda i: (0, i, 0))] * n_pat
        + [pl.BlockSpec((heads, tm), lambda i: (0, i))] * n_pat,
        out_specs=pl.BlockSpec((tm, heads * HEAD_DIM), lambda i: (i, 0)),
        out_shape=jax.ShapeDtypeStruct((N, heads * HEAD_DIM), BF16),
        compiler_params=_params(("arbitrary",), 32),
        name="attn_dilated_mix",
    )(*os, *lses)


def _attn_a(qkv, slopes, B, S, n_heads, heads_a):
    parts = [_attn_a_pattern(qkv, slopes, B, S, n_heads, heads_a, w, d) for w, d in DILATED_PATTERNS]
    return _mix_a([p[0] for p in parts], [p[1] for p in parts])


def _attn_b_kernel(slopes_ref, lam_ref, q_ref, k_ref, v_ref, g_ref, o_ref, tab_ref, kmax_ref,
                   m_scr, l_scr, acc_scr, *, S, Tq, Tk, lambda_init):
    h = pl.program_id(1)
    i = pl.program_id(2)
    dh = HEAD_DIM // 2
    n_chunks = S // Tk
    c_dist = LOG2E * slopes_ref[h]

    def sub_norm(x):
        sq = x * x
        return jnp.sqrt(jnp.maximum(jnp.sum(sq[:, :dh], axis=-1, keepdims=True),
                                    jnp.sum(sq[:, dh:], axis=-1, keepdims=True)))

    @pl.when(i == 0)
    def _():
        shape = (Tq, 2 * S - Tq)
        a = lax.broadcasted_iota(I32, shape, 0)
        u = lax.broadcasted_iota(I32, shape, 1)
        d = a - u + (S - Tq)
        tab_ref[...] = -c_dist * jnp.maximum(d, -d).astype(F32)
        kn = sub_norm(k_ref[0].astype(F32))
        for j in range(n_chunks):
            kmax_ref[j] = jnp.max(kn[j * Tk:(j + 1) * Tk])

    lv = lam_ref[...]
    lam = (jnp.exp(jnp.sum(lv[0:1] * lv[1:2], axis=-1, keepdims=True))
           - jnp.exp(jnp.sum(lv[2:3] * lv[3:4], axis=-1, keepdims=True)) + lambda_init)

    t0 = i * Tq
    qf = q_ref[0].astype(F32) * (LOG2E * dh ** -0.5)
    q = qf.astype(BF16)
    qmax = jnp.max(sub_norm(q.astype(F32)))

    def chunk(kstart, first):
        kc = k_ref[0, pl.ds(kstart, Tk), :]
        vc = v_ref[0, pl.ds(kstart, Tk), :]
        bias = tab_ref[:, pl.ds(pl.multiple_of(S - Tq - t0 + kstart, LANES), Tk)]
        s = jnp.concatenate(
            [lax.dot_general(q[:, c * dh:(c + 1) * dh], kc[:, c * dh:(c + 1) * dh], _NT,
                             preferred_element_type=F32) + bias for c in range(2)], axis=0)
        mc = jnp.max(s, axis=-1, keepdims=True)
        if first:
            m_new = mc
        else:
            m_old = m_scr[...]
            m_new = jnp.maximum(m_old, mc)
            alpha = jnp.exp2(m_old - m_new)
        p = jnp.exp2(s - m_new)
        ls = jnp.sum(p, axis=-1, keepdims=True)
        pv = jnp.dot(p.astype(BF16), vc, preferred_element_type=F32)
        m_scr[...] = m_new
        l_scr[...] = ls if first else alpha * l_scr[...] + ls
        acc_scr[...] = pv if first else alpha * acc_scr[...] + pv
        return m_new

    jd = t0 // Tk
    m_diag = chunk(pl.multiple_of(jd * Tk, Tk), True)
    floor = jnp.min(m_diag) - 160.0
    for j in range(n_chunks):
        gap = jnp.maximum(jnp.maximum(j * Tk - (t0 + Tq - 1), t0 - ((j + 1) * Tk - 1)), 0)
        bound = 1.001 * qmax * kmax_ref[j] + 0.01 - c_dist * gap.astype(F32)

        @pl.when((j != jd) & (bound > floor))
        def _():
            chunk(j * Tk, False)

    acc = acc_scr[...]
    l = l_scr[...]
    o = acc[:Tq] / l[:Tq] - acc[Tq:] * (lam / l[Tq:])
    o = _rms(o, SUBLN_EPS) * g_ref[...] * (1.0 - lambda_init)
    o_ref[...] = o.astype(BF16)


def _attn_b(qkv, slopes, lam_vecs, subln_g, B, S, n_heads, head0, heads_b, lambda_init, Tq=256,
            Tk=1024):
    N = B * S
    nq = S // Tq
    Tk = min(Tk, S)
    return pl.pallas_call(
        functools.partial(_attn_b_kernel, S=S, Tq=Tq, Tk=Tk, lambda_init=lambda_init),
        grid=(B, heads_b, nq),
        in_specs=[pl.BlockSpec(memory_space=pltpu.SMEM),
                  pl.BlockSpec((4, HEAD_DIM // 2), lambda b, h, i: (0, 0)),
                  pl.BlockSpec((1, Tq, HEAD_DIM), lambda b, h, i: (head0 + h, b * nq + i, 0)),
                  pl.BlockSpec((1, S, HEAD_DIM), lambda b, h, i: (n_heads + head0 + h, b, 0)),
                  pl.BlockSpec((1, S, HEAD_DIM), lambda b, h, i: (2 * n_heads + head0 + h, b, 0)),
                  pl.BlockSpec((1, HEAD_DIM), lambda b, h, i: (0, 0))],
        out_specs=pl.BlockSpec((Tq, HEAD_DIM), lambda b, h, i: (b * nq + i, h)),
        out_shape=jax.ShapeDtypeStruct((N, heads_b * HEAD_DIM), BF16),
        scratch_shapes=[pltpu.VMEM((Tq, 2 * S - Tq), F32), pltpu.SMEM((S // Tk,), F32),
                        pltpu.VMEM((2 * Tq, 1), F32), pltpu.VMEM((2 * Tq, 1), F32),
                        pltpu.VMEM((2 * Tq, HEAD_DIM), F32)],
        compiler_params=_params(("arbitrary", "arbitrary", "arbitrary"), 48),
        name="attn_diff",
    )(slopes, lam_vecs, qkv, qkv, qkv, subln_g.reshape(1, HEAD_DIM))


def _attn_c_kernel(rpb_ref, q_ref, k_ref, v_ref, o_ref, tab_ref, *, R):
    h = pl.program_id(1)
    W = GRID_W
    kr = min(NA_ROWS, R)
    n_dr = 2 * NA_ROWS - 1
    n_dc = 2 * NA_COLS - 1

    c_io = lax.broadcasted_iota(I32, (W, 2 * W), 0)
    l_io = lax.broadcasted_iota(I32, (W, 2 * W), 1)
    cp = l_io & (W - 1)
    dcm = cp - c_io + (NA_COLS - 1)
    cstart = jnp.clip(c_io - NA_COLS // 2, 0, W - NA_COLS)
    ok = (cp >= cstart) & (cp < cstart + NA_COLS)
    blocks = []
    for dr in range(n_dr):
        blk = jnp.zeros((W, 2 * W), F32)
        for dc in range(n_dc):
            blk = jnp.where(dcm == dc, rpb_ref[(h * n_dr + dr) * n_dc + dc], blk)
        blocks.append(jnp.where(ok, blk, NEG_INF))
    for o in range(NA_ROWS):
        for jj in range(kr // 2):
            tab_ref[o, :, jj * 2 * W:(jj + 1) * 2 * W] = jnp.where(
                l_io < W, blocks[o + 2 * jj], blocks[o + 2 * jj + 1])

    def row(r, carry):
        rs = jnp.clip(r - kr // 2, 0, R - kr)
        q = q_ref[0, pl.ds(pl.multiple_of(r * W, W), W), :]
        kw = k_ref[0, pl.ds(pl.multiple_of(rs * W, W), kr * W), :]
        vw = v_ref[0, pl.ds(pl.multiple_of(rs * W, W), kr * W), :]
        s = lax.dot_general(q, kw, _NT, preferred_element_type=F32) * (HEAD_DIM ** -0.5)
        s = s + tab_ref[rs - r + (NA_ROWS - 1)]
        m = jnp.max(s, axis=-1, keepdims=True)
        p = jnp.exp(s - m)
        l = jnp.sum(p, axis=-1, keepdims=True)
        o = jnp.dot(p.astype(BF16), vw, preferred_element_type=F32) / l
        o_ref[pl.ds(pl.multiple_of(r * W, W), W), :] = o.astype(BF16)
        return carry

    lax.fori_loop(0, R, row, 0, unroll=8 if R % 8 == 0 else 1)


def _attn_c(qkv, rpb_flat, B, S, n_heads, head0, heads_c):
    N = B * S
    R = S // GRID_W
    kr = min(NA_ROWS, R)
    assert kr == NA_ROWS and kr % 2 == 0
    return pl.pallas_call(
        functools.partial(_attn_c_kernel, R=R),
        grid=(B, heads_c),
        in_specs=[pl.BlockSpec(memory_space=pltpu.SMEM),
                  pl.BlockSpec((1, S, HEAD_DIM), lambda b, h: (head0 + h, b, 0)),
                  pl.BlockSpec((1, S, HEAD_DIM), lambda b, h: (n_heads + head0 + h, b, 0)),
                  pl.BlockSpec((1, S, HEAD_DIM), lambda b, h: (2 * n_heads + head0 + h, b, 0))],
        out_specs=pl.BlockSpec((S, HEAD_DIM), lambda b, h: (b, h)),
        out_shape=jax.ShapeDtypeStruct((N, heads_c * HEAD_DIM), BF16),
        scratch_shapes=[pltpu.VMEM((NA_ROWS, GRID_W, kr * GRID_W), F32)],
        compiler_params=_params(("arbitrary", "arbitrary"), 32),
        name="attn_nbr",
    )(rpb_flat, qkv, qkv, qkv)


def _out_kernel(oa_ref, ob_ref, oc_ref, wa_ref, wb_ref, wc_ref, x_ref, g1_ref, ng_ref, sc_ref,
                sh_ref, rw_ref, x1_ref, h2_ref, lg_ref):
    acc = jnp.dot(oa_ref[...], wa_ref[...], preferred_element_type=F32)
    acc = acc + jnp.dot(ob_ref[...], wb_ref[...], preferred_element_type=F32)
    acc = acc + jnp.dot(oc_ref[...], wc_ref[...], preferred_element_type=F32)
    x1 = x_ref[...] + g1_ref[0] * acc
    x1_ref[...] = x1
    h2 = (_rms(x1, NORM_EPS) * ng_ref[...]) * (1.0 + sc_ref[0]) + sh_ref[0]
    h2_ref[...] = _pack_pairs(h2)
    lg_ref[...] = lax.dot_general(rw_ref[...], h2, _NT, preferred_element_type=F32,
                                  precision=HIGHEST)


def _out_proj(oa, ob, oc, w_bf, x2d, g1, ng, sc, sh, rw_t, S, tm=512):
    N, D = x2d.shape
    wa, wb, wc = oa.shape[1], ob.shape[1], oc.shape[1]
    E = rw_t.shape[0]
    tpb = S // tm
    const = dict(pipeline_mode=pl.Buffered(1))
    row = lambda i: (i, 0)
    per_b = lambda i: (i // tpb, 0, 0)
    return pl.pallas_call(
        _out_kernel,
        grid=(N // tm,),
        in_specs=[pl.BlockSpec((tm, wa), row), pl.BlockSpec((tm, wb), row), pl.BlockSpec((tm, wc), row),
                  pl.BlockSpec((wa, D), lambda i: (0, 0), **const),
                  pl.BlockSpec((wb, D), lambda i: (wa // wb, 0), **const),
                  pl.BlockSpec((wc, D), lambda i: ((wa + wb) // wc, 0), **const),
                  pl.BlockSpec((tm, D), row),
                  pl.BlockSpec((1, 1, D), per_b),
                  pl.BlockSpec((1, D), lambda i: (0, 0)),
                  pl.BlockSpec((1, 1, D), per_b),
                  pl.BlockSpec((1, 1, D), per_b),
                  pl.BlockSpec((E, D), lambda i: (0, 0), **const)],
        out_specs=[pl.BlockSpec((tm, D), row), pl.BlockSpec((tm, D // 2), row),
                   pl.BlockSpec((E, tm), lambda i: (0, i))],
        out_shape=[jax.ShapeDtypeStruct((N, D), F32), jax.ShapeDtypeStruct((N, D // 2), U32),
                   jax.ShapeDtypeStruct((E, N), F32)],
        compiler_params=_params(("arbitrary",), 52),
        name="out_proj",
    )(oa, ob, oc, w_bf, w_bf, w_bf, x2d, g1, ng.reshape(1, D), sc, sh, rw_t)


def _first_argmax(vals, iota, big):
    mx = jnp.max(vals, axis=0, keepdims=True)
    idx = jnp.min(jnp.where(vals == mx, iota, big), axis=0, keepdims=True)
    return mx, idx


def _route_kernel(lg_ref, rb_ref, idx_ref, gate_ref, rank_ref, cnt_ref, tri_ref, carry_ref, *, Tt):
    i = pl.program_id(0)
    E = lg_ref.shape[0]
    gs = E // N_GROUPS

    @pl.when(i == 0)
    def _():
        r = lax.broadcasted_iota(I32, (Tt, Tt), 0)
        c = lax.broadcasted_iota(I32, (Tt, Tt), 1)
        tri_ref[...] = jnp.where(r < c, 1.0, 0.0).astype(BF16)
        carry_ref[...] = jnp.zeros_like(carry_ref)

    scores = jax.nn.sigmoid(lg_ref[...])
    sel = scores + rb_ref[...]
    e_io = lax.broadcasted_iota(I32, (E, Tt), 0).astype(F32)
    s_io = lax.broadcasted_iota(I32, (gs, Tt), 0).astype(F32)
    g_io = lax.broadcasted_iota(I32, (N_GROUPS, Tt), 0).astype(F32)

    grp = []
    for g in range(N_GROUPS):
        blk = sel[g * gs:(g + 1) * gs]
        m1, a1 = _first_argmax(blk, s_io, float(gs))
        m2 = jnp.max(jnp.where(s_io == a1, -jnp.inf, blk), axis=0, keepdims=True)
        grp.append(m1 + m2)
    grp = jnp.concatenate(grp, axis=0)

    gsel = jnp.zeros((N_GROUPS, Tt), F32)
    for _ in range(TOPK_GROUPS):
        _, gi = _first_argmax(grp, g_io, float(N_GROUPS))
        hit = g_io == gi
        gsel = jnp.where(hit, 1.0, gsel)
        grp = jnp.where(hit, -jnp.inf, grp)

    esel = jnp.concatenate(
        [jnp.broadcast_to(gsel[g:g + 1], (gs, Tt)) for g in range(N_GROUPS)], axis=0)
    cur = jnp.where(esel > 0.0, sel, NEG_INF)

    idxs, gates = [], []
    onehot = jnp.zeros((E, Tt), F32)
    for _ in range(TOP_K):
        _, ei = _first_argmax(cur, e_io, float(E))
        hit = e_io == ei
        idxs.append(ei)
        gates.append(jnp.sum(jnp.where(hit, scores, 0.0), axis=0, keepdims=True))
        onehot = jnp.where(hit, 1.0, onehot)
        cur = jnp.where(hit, -jnp.inf, cur)
    gate = jnp.concatenate(gates, axis=0)
    gate = gate / jnp.sum(gate, axis=0, keepdims=True) * ROUTED_SCALE
    idx_ref[...] = jnp.concatenate(idxs, axis=0).astype(I32)
    gate_ref[...] = gate

    oh = onehot.astype(BF16)
    before = jnp.dot(oh, tri_ref[...], preferred_element_type=F32)
    tile_cnt = jnp.dot(oh, jnp.ones((Tt, LANES), BF16), preferred_element_type=F32)
    carry = carry_ref[...]
    rank_mat = before + jnp.concatenate([carry] * (Tt // LANES), axis=1)
    ranks = [jnp.sum(jnp.where(e_io == ei, rank_mat, 0.0), axis=0, keepdims=True) for ei in idxs]
    rank_ref[...] = jnp.concatenate(ranks, axis=0).astype(I32)
    carry_ref[...] = carry + tile_cnt
    cnt_ref[...] = (carry + tile_cnt).astype(I32)


def _route(logits_t, router_bias, Tt=512):
    E, N = logits_t.shape
    tok = lambda i: (0, i)
    return pl.pallas_call(
        functools.partial(_route_kernel, Tt=Tt),
        grid=(N // Tt,),
        in_specs=[pl.BlockSpec((E, Tt), tok), pl.BlockSpec((E, 1), lambda i: (0, 0))],
        out_specs=[pl.BlockSpec((TOP_K, Tt), tok), pl.BlockSpec((TOP_K, Tt), tok),
                   pl.BlockSpec((TOP_K, Tt), tok), pl.BlockSpec((E, LANES), lambda i: (0, 0))],
        out_shape=[jax.ShapeDtypeStruct((TOP_K, N), I32), jax.ShapeDtypeStruct((TOP_K, N), F32),
                   jax.ShapeDtypeStruct((TOP_K, N), I32), jax.ShapeDtypeStruct((E, LANES), I32)],
        scratch_shapes=[pltpu.VMEM((Tt, Tt), BF16), pltpu.VMEM((E, LANES), F32)],
        compiler_params=_params(("arbitrary",), 32),
        name="route_topk",
    )(logits_t, router_bias.reshape(E, 1))


def _dest_kernel(pstart_ref, idx_ref, rank_ref, dest_ref):
    idx = idx_ref[...]
    base = jnp.zeros(idx.shape, I32)
    for e in range(pstart_ref.shape[0]):
        base = jnp.where(idx == e, pstart_ref[e], base)
    dest_ref[...] = base + rank_ref[...]


def _dest(pstart, idx, rank):
    return pl.pallas_call(
        _dest_kernel,
        in_specs=[pl.BlockSpec(memory_space=pltpu.SMEM), pl.BlockSpec(memory_space=pltpu.VMEM),
                  pl.BlockSpec(memory_space=pltpu.VMEM)],
        out_specs=pl.BlockSpec(memory_space=pltpu.VMEM),
        out_shape=jax.ShapeDtypeStruct(idx.shape, I32),
        name="route_dest",
    )(pstart, idx, rank)


def _dispatch_kernel(dest_hbm, h_ref, xs_hbm, dest_smem, idx_sem, row_sem, *, Td):
    i = pl.program_id(0)
    n = Td * TOP_K
    cp = pltpu.make_async_copy(dest_hbm.at[pl.ds(i * n, n)], dest_smem, idx_sem)
    cp.start()
    cp.wait()

    def row_copy(t, k):
        return pltpu.make_async_copy(h_ref.at[pl.ds(t, 1)],
                                     xs_hbm.at[pl.ds(dest_smem[t * TOP_K + k], 1)], row_sem)

    def issue(t, carry):
        for k in range(TOP_K):
            row_copy(t, k).start(priority=k % 2)
        return carry

    lax.fori_loop(0, Td, issue, 0)

    def drain(t, carry):
        for k in range(TOP_K):
            row_copy(t, k).wait()
        return carry

    lax.fori_loop(0, Td, drain, 0)


def _dispatch(dest_flat, h2, P, Td=256):
    N, D = h2.shape
    return pl.pallas_call(
        functools.partial(_dispatch_kernel, Td=Td),
        grid=(N // Td,),
        in_specs=[pl.BlockSpec(memory_space=pl.ANY), pl.BlockSpec((Td, D), lambda i: (i, 0))],
        out_specs=pl.BlockSpec(memory_space=pl.ANY),
        out_shape=jax.ShapeDtypeStruct((P, D), h2.dtype),
        scratch_shapes=[pltpu.SMEM((Td * TOP_K,), I32), pltpu.SemaphoreType.DMA,
                        pltpu.SemaphoreType.DMA],
        compiler_params=_params(("arbitrary",), 32),
        name="moe_dispatch",
    )(dest_flat, h2)


def _moe_kernel(be_ref, nv_ref, nb_ref, xs_ref, wg_ref, wu_ref, wd_ref, ys_ref, wg_s, wu_s, wd_s):
    b = pl.program_id(0)
    active = b < nb_ref[0]
    new_expert = (b == 0) | (be_ref[b] != be_ref[jnp.maximum(b - 1, 0)])

    @pl.when(active & new_expert)
    def _():
        wg_s[...] = wg_ref[0].astype(BF16)
        wu_s[...] = wu_ref[0].astype(BF16)
        wd_s[...] = wd_ref[0].astype(BF16)

    @pl.when(active)
    def _():
        rows = lax.broadcasted_iota(I32, xs_ref.shape, 0)
        hi, lo = _unpack_pairs(jnp.where(rows < nv_ref[b], xs_ref[...], jnp.uint32(0)))
        x = jnp.concatenate([hi.astype(BF16), lo.astype(BF16)], axis=1)
        g = jnp.dot(x, wg_s[...], preferred_element_type=F32)
        u = jnp.dot(x, wu_s[...], preferred_element_type=F32)
        a = (g * jax.nn.sigmoid(g) * u).astype(BF16)
        ys_ref[...] = _pack_pairs(jnp.dot(a, wd_s[...], preferred_element_type=F32))


def _moe(block_expert, nvalid, nblocks, xs, wg, wu, wd):
    P, Dp = xs.shape
    _, D, F = wg.shape
    n_blocks = P // MOE_BLOCK
    blk = lambda b, be, nv, nb: (jnp.minimum(b, nb[0] - 1), 0)
    return pl.pallas_call(
        _moe_kernel,
        grid_spec=pltpu.PrefetchScalarGridSpec(
            num_scalar_prefetch=3,
            grid=(n_blocks,),
            in_specs=[pl.BlockSpec((MOE_BLOCK, Dp), blk),
                      pl.BlockSpec((1, D, F), lambda b, be, nv, nb: (be[b], 0, 0)),
                      pl.BlockSpec((1, D, F), lambda b, be, nv, nb: (be[b], 0, 0)),
                      pl.BlockSpec((1, F, D), lambda b, be, nv, nb: (be[b], 0, 0))],
            out_specs=pl.BlockSpec((MOE_BLOCK, Dp), blk),
            scratch_shapes=[pltpu.VMEM((D, F), BF16), pltpu.VMEM((D, F), BF16),
                            pltpu.VMEM((F, D), BF16)]),
        out_shape=jax.ShapeDtypeStruct((P, Dp), U32),
        compiler_params=_params(("arbitrary",), 52),
        name="moe_experts",
    )(block_expert, nvalid, nblocks, xs, wg, wu, wd)


def _combine_kernel(dest_hbm, ys_hbm, gate_ref, h_ref, x_ref, g2_ref, sg_ref, su_ref, sd_ref,
                    fg_ref, o_ref, dest_smem, ybuf, idx_sem, row_sem, *, Tc, final_norm):
    i = pl.program_id(0)
    n = Tc * TOP_K
    cp = pltpu.make_async_copy(dest_hbm.at[pl.ds(i * n, n)], dest_smem, idx_sem)
    cp.start()
    cp.wait()

    def row_copy(t, k):
        return pltpu.make_async_copy(ys_hbm.at[pl.ds(dest_smem[t * TOP_K + k], 1)],
                                     ybuf.at[k, pl.ds(t, 1)], row_sem)

    def issue(t, carry):
        for k in range(TOP_K):
            row_copy(t, k).start(priority=k % 2)
        return carry

    lax.fori_loop(0, Tc, issue, 0)

    hh, hl = _unpack_pairs(h_ref[...])
    h = jnp.concatenate([hh.astype(BF16), hl.astype(BF16)], axis=1)
    g = jnp.dot(h, sg_ref[...], preferred_element_type=F32)
    u = jnp.dot(h, su_ref[...], preferred_element_type=F32)
    a = (g * jax.nn.sigmoid(g) * u).astype(BF16)
    f = jnp.dot(a, sd_ref[...], preferred_element_type=F32)

    def drain(t, carry):
        for k in range(TOP_K):
            row_copy(t, k).wait()
        return carry

    lax.fori_loop(0, Tc, drain, 0)

    gate = gate_ref[...]
    half = f.shape[1] // 2
    f_hi, f_lo = f[:, :half], f[:, half:]
    for k in range(TOP_K):
        y_hi, y_lo = _unpack_pairs(ybuf[k])
        f_hi = f_hi + gate[:, k:k + 1] * y_hi
        f_lo = f_lo + gate[:, k:k + 1] * y_lo
    f = jnp.concatenate([f_hi, f_lo], axis=1)
    x2 = x_ref[...] + g2_ref[0] * f
    if final_norm:
        x2 = _rms(x2, NORM_EPS) * fg_ref[...]
    o_ref[...] = x2


def _combine(dest_flat, ys, gate_t, h2, x1, g2, sg, su, sd, final_g, S, final_norm, Tc=256):
    N, D = x1.shape
    F = sg.shape[1]
    tpb = S // Tc
    const = dict(pipeline_mode=pl.Buffered(1))
    row = lambda i: (i, 0)
    return pl.pallas_call(
        functools.partial(_combine_kernel, Tc=Tc, final_norm=final_norm),
        grid=(N // Tc,),
        in_specs=[pl.BlockSpec(memory_space=pl.ANY), pl.BlockSpec(memory_space=pl.ANY),
                  pl.BlockSpec((Tc, TOP_K), row),
                  pl.BlockSpec((Tc, D // 2), row), pl.BlockSpec((Tc, D), row),
                  pl.BlockSpec((1, 1, D), lambda i: (i // tpb, 0, 0)),
                  pl.BlockSpec((D, F), lambda i: (0, 0), **const),
                  pl.BlockSpec((D, F), lambda i: (0, 0), **const),
                  pl.BlockSpec((F, D), lambda i: (0, 0), **const),
                  pl.BlockSpec((1, D), lambda i: (0, 0))],
        out_specs=pl.BlockSpec((Tc, D), row),
        out_shape=jax.ShapeDtypeStruct((N, D), F32),
        scratch_shapes=[pltpu.SMEM((Tc * TOP_K,), I32), pltpu.VMEM((TOP_K, Tc, D // 2), U32),
                        pltpu.SemaphoreType.DMA, pltpu.SemaphoreType.DMA],
        compiler_params=_params(("arbitrary",), 52),
        name="moe_combine",
    )(dest_flat, ys, gate_t, h2, x1, g2, sg, su, sd, final_g.reshape(1, D))


def _moe_ffn(h2, logits_t, x1, g2, router_bias, wg, wu, wd, e_off, sg, su, sd, final_g, S,
             final_norm):
    N = h2.shape[0]
    E = logits_t.shape[0]
    idx, gate, rank, cnt = _route(logits_t, router_bias)
    counts = cnt[:, 0]
    padded = (counts + MOE_BLOCK - 1) // MOE_BLOCK * MOE_BLOCK
    pend = jnp.cumsum(padded)
    pstart = (pend - padded).astype(I32)
    n_blocks = -(-(N * TOP_K + E * (MOE_BLOCK - 1)) // MOE_BLOCK)
    bstart = jnp.arange(n_blocks, dtype=I32) * MOE_BLOCK
    block_expert = jnp.minimum(jnp.sum(pend[None, :] <= bstart[:, None], axis=1), E - 1).astype(I32)
    nvalid = jnp.clip(pstart[block_expert] + counts[block_expert] - bstart, 0, MOE_BLOCK).astype(I32)
    nblocks = (pend[-1:] // MOE_BLOCK).astype(I32)

    dest = _dest(pstart, idx, rank)
    dest_flat = dest.T.reshape(N * TOP_K)
    xs = _dispatch(dest_flat, h2, n_blocks * MOE_BLOCK)
    ys = _moe(block_expert + e_off, nvalid, nblocks, xs, wg, wu, wd)
    return _combine(dest_flat, ys, gate.T, h2, x1, g2, sg, su, sd, final_g, S, final_norm)


def kernel(x, c, ada_w, ada_b, norm_mix_g, norm_ffn_g, w_in, lambda_q1, lambda_k1, lambda_q2,
           lambda_k2, subln_g, rpb, w_out, router_w, router_bias, exp_w_gate, exp_w_up,
           exp_w_down, sh_w_gate, sh_w_up, sh_w_down, final_g):
    B, S, D = x.shape
    L = ada_w.shape[0]
    N = B * S
    n_heads = w_in.shape[2] // (3 * HEAD_DIM)
    heads_a = n_heads // HEADS_A_FRAC
    heads_b = n_heads // HEADS_B_FRAC
    heads_c = n_heads - heads_a - heads_b
    slopes = _alibi_slopes(heads_a + heads_b)
    slopes_a = jnp.asarray(slopes[:heads_a], F32)
    slopes_b = jnp.asarray(slopes[heads_a:], F32)

    E, F = exp_w_gate.shape[1], exp_w_gate.shape[3]
    wg_all = exp_w_gate.reshape(L * E, D, F)
    wu_all = exp_w_up.reshape(L * E, D, F)
    wd_all = exp_w_down.reshape(L * E, F, D)

    mod = _ada(c, ada_w, ada_b).reshape(L, B, 6, 1, D)
    xf = x.reshape(N, D)
    for l in range(L):
        sh1, sc1, g1, sh2, sc2, g2 = (mod[l, :, j] for j in range(6))
        qkv = _qkv(xf, norm_mix_g[l], sc1, sh1, w_in[l].astype(BF16), S)
        oa = _attn_a(qkv, slopes_a, B, S, n_heads, heads_a)
        lambda_init = 0.8 - 0.6 * math.exp(-0.3 * l)
        lam_vecs = jnp.stack([lambda_q1[l], lambda_k1[l], lambda_q2[l], lambda_k2[l]]).astype(F32)
        ob = _attn_b(qkv, slopes_b, lam_vecs, subln_g[l], B, S, n_heads, heads_a, heads_b, lambda_init)
        oc = _attn_c(qkv, rpb[l].reshape(-1), B, S, n_heads, heads_a + heads_b, heads_c)
        x1, h2, logits_t = _out_proj(oa, ob, oc, w_out[l].astype(BF16), xf, g1, norm_ffn_g[l],
                                     sc2, sh2, router_w[l].T, S)
        xf = _moe_ffn(h2, logits_t, x1, g2, router_bias[l], wg_all, wu_all, wd_all, l * E,
                      sh_w_gate[l].astype(BF16), sh_w_up[l].astype(BF16),
                      sh_w_down[l].astype(BF16), final_g, S, l == L - 1)
    return xf.reshape(B, S, D)
```

```python
import functools
import math

import jax
import jax.numpy as jnp
from jax import lax
from jax.experimental import pallas as pl
from jax.experimental.pallas import tpu as pltpu

F32 = jnp.float32
BF16 = jnp.bfloat16
I32 = jnp.int32
U32 = jnp.uint32
HIGHEST = lax.Precision.HIGHEST
_NT = (((1,), (1,)), ((), ()))

HEAD_DIM = 128
HEADS_A_FRAC, HEADS_B_FRAC = 2, 4
DILATED_PATTERNS = ((128, 1), (512, 4), (2048, 16))
GRID_W = 64
NA_ROWS = 8
NA_COLS = 16
N_GROUPS = 8
TOPK_GROUPS = 4
TOP_K = 8
ROUTED_SCALE = 2.5
NORM_EPS = 1e-6
SUBLN_EPS = 1e-5
NEG_INF = -1e30
LOG2E = math.log2(math.e)

LANES = 128
VMEM_BYTES_V7X = 64 << 20

MOE_BLOCK = 512


def _params(semantics, vmem_mib):
    return pltpu.CompilerParams(dimension_semantics=semantics,
                                vmem_limit_bytes=min(vmem_mib << 20, VMEM_BYTES_V7X - (4 << 20)))


def _alibi_slopes(n):
    def pow2(m):
        start = 2.0 ** (-8.0 / m)
        return [start ** (i + 1) for i in range(m)]
    p = 2 ** int(math.floor(math.log2(n)))
    return pow2(p) + pow2(2 * p)[0::2][: n - p]


def _rms(x, eps):
    return x * lax.rsqrt(jnp.mean(x * x, axis=-1, keepdims=True) + eps)


def _pack_pairs(x):
    half = x.shape[1] // 2
    hi = lax.bitcast_convert_type(x[:, :half].astype(BF16).astype(F32), U32)
    lo = lax.bitcast_convert_type(x[:, half:].astype(BF16).astype(F32), U32)
    return hi | (lo >> 16)


def _unpack_pairs(w):
    hi = lax.bitcast_convert_type(w & jnp.uint32(0xFFFF0000), F32)
    lo = lax.bitcast_convert_type(w << 16, F32)
    return hi, lo


def _store_row_slabs(ref, packed):
    T, W = packed.shape
    for c in range(W // LANES):
        ref[pl.ds(c, T, stride=W // LANES), :] = packed[:, c * LANES:(c + 1) * LANES]


def _load_row_slab(ref, c, T, G):
    return ref[pl.ds(c, T, stride=G), :]


def _ada_kernel(c_ref, w_ref, b_ref, o_ref):
    c = c_ref[...]
    sc = c * jax.nn.sigmoid(c)
    o_ref[0] = jnp.dot(sc, w_ref[0], preferred_element_type=F32, precision=HIGHEST) + b_ref[0]


def _ada(c, ada_w, ada_b):
    B, D = c.shape
    L, _, D6 = ada_w.shape
    rows = 8
    cp = jnp.zeros((rows, D), F32).at[:B].set(c)
    tn = 1024
    out = pl.pallas_call(
        _ada_kernel,
        grid=(L, D6 // tn),
        in_specs=[pl.BlockSpec((rows, D), lambda l, j: (0, 0)),
                  pl.BlockSpec((1, D, tn), lambda l, j: (l, 0, j)),
                  pl.BlockSpec((1, 1, tn), lambda l, j: (l, 0, j))],
        out_specs=pl.BlockSpec((1, rows, tn), lambda l, j: (l, 0, j)),
        out_shape=jax.ShapeDtypeStruct((L, rows, D6), F32),
        compiler_params=_params(("arbitrary", "arbitrary"), 32),
        name="ada_mod",
    )(cp, ada_w, ada_b.reshape(L, 1, D6))
    return out[:, :B]


def _qkv_kernel(x_ref, g_ref, sc_ref, sh_ref, w_ref, o_ref, h_scr, *, heads_per_step):
    @pl.when(pl.program_id(1) == 0)
    def _():
        y = _rms(x_ref[...], NORM_EPS) * g_ref[...]
        h_scr[...] = (y * (1.0 + sc_ref[0]) + sh_ref[0]).astype(BF16)

    res = jnp.dot(h_scr[...], w_ref[...], preferred_element_type=F32)
    for hh in range(heads_per_step):
        o_ref[hh] = res[:, hh * HEAD_DIM:(hh + 1) * HEAD_DIM].astype(BF16)


def _qkv(x2d, g, sc, sh, w_bf, S, tm=512, tn=1536):
    N, D = x2d.shape
    W3 = w_bf.shape[1]
    hps = tn // HEAD_DIM
    tpb = S // tm
    return pl.pallas_call(
        functools.partial(_qkv_kernel, heads_per_step=hps),
        grid=(N // tm, W3 // tn),
        in_specs=[pl.BlockSpec((tm, D), lambda i, j: (i, 0)),
                  pl.BlockSpec((1, D), lambda i, j: (0, 0)),
                  pl.BlockSpec((1, 1, D), lambda i, j: (i // tpb, 0, 0)),
                  pl.BlockSpec((1, 1, D), lambda i, j: (i // tpb, 0, 0)),
                  pl.BlockSpec((D, tn), lambda i, j: (0, j))],
        out_specs=pl.BlockSpec((hps, tm, HEAD_DIM), lambda i, j: (j, i, 0)),
        out_shape=jax.ShapeDtypeStruct((W3 // HEAD_DIM, N, HEAD_DIM), BF16),
        scratch_shapes=[pltpu.VMEM((tm, D), BF16)],
        compiler_params=_params(("arbitrary", "arbitrary"), 48),
        name="qkv_proj",
    )(x2d, g.reshape(1, D), sc, sh, w_bf)


def _col_to_row(col, eye):
    return jnp.sum(jnp.where(eye, col, 0.0), axis=0, keepdims=True)


def _attn_a_kernel(slopes_ref, q_ref, k_ref, v_ref, o_ref, lse_ref, nat, cq, ck, cv, onat, *,
                   L, dil, half, Tq):
    h = pl.program_id(1)
    wk = 2 * Tq
    n_tiles = L // Tq
    c_dist = LOG2E * slopes_ref[h] * dil
    a = lax.broadcasted_iota(I32, (Tq, wk), 0)
    u = lax.broadcasted_iota(I32, (Tq, wk), 1)
    eye = lax.broadcasted_iota(I32, (Tq, Tq), 0) == lax.broadcasted_iota(I32, (Tq, Tq), 1)

    def table(off):
        d = u - a - off
        ad = jnp.maximum(d, -d)
        return jnp.where(ad <= half, -c_dist * ad.astype(F32), NEG_INF)

    tabs = {off: table(off) for off in sorted({0, half, wk - Tq})}

    for src, dst, scale in ((q_ref, cq, LOG2E * HEAD_DIM ** -0.5), (k_ref, ck, None), (v_ref, cv, None)):
        x = src[0].astype(F32)
        nat[...] = x if scale is None else x * scale
        for r in range(dil):
            dst[r * L:(r + 1) * L, :] = nat[pl.ds(r, L, stride=dil), :].astype(BF16)

    for r in range(dil):
        for j in range(n_tiles):
            start = min(max(j * Tq - half, 0), L - wk)
            q = cq[r * L + j * Tq:r * L + (j + 1) * Tq, :]
            kw = ck[r * L + start:r * L + start + wk, :]
            vw = cv[r * L + start:r * L + start + wk, :]
            s = lax.dot_general(q, kw, _NT, preferred_element_type=F32) + tabs[j * Tq - start]
            m = jnp.max(s, axis=-1, keepdims=True)
            p = jnp.exp2(s - m)
            l = jnp.sum(p, axis=-1, keepdims=True)
            o = jnp.dot(p.astype(BF16), vw, preferred_element_type=F32) / l
            onat[pl.ds(r + j * Tq * dil, Tq, stride=dil), :] = o
            lse_ref[0, 0, r:r + 1, j * Tq:(j + 1) * Tq] = _col_to_row(m + jnp.log2(l), eye)
    o_ref[0] = onat[...].astype(BF16)


def _attn_a_pattern(qkv, slopes, B, S, n_heads, heads_a, window, dil, Tq=128):
    N = B * S
    L = S // dil
    half = window // (2 * dil)
    assert L % Tq == 0 and L >= 2 * Tq and 2 * half <= Tq
    blk = (1, S, HEAD_DIM)
    o, lse = pl.pallas_call(
        functools.partial(_attn_a_kernel, L=L, dil=dil, half=half, Tq=Tq),
        grid=(B, heads_a),
        in_specs=[pl.BlockSpec(memory_space=pltpu.SMEM),
                  pl.BlockSpec(blk, lambda b, h: (h, b, 0)),
                  pl.BlockSpec(blk, lambda b, h: (n_heads + h, b, 0)),
                  pl.BlockSpec(blk, lambda b, h: (2 * n_heads + h, b, 0))],
        out_specs=[pl.BlockSpec(blk, lambda b, h: (h, b, 0)),
                   pl.BlockSpec((1, 1, dil, L), lambda b, h: (b, h, 0, 0))],
        out_shape=[jax.ShapeDtypeStruct((heads_a, N, HEAD_DIM), BF16),
                   jax.ShapeDtypeStruct((B, heads_a, dil, L), F32)],
        scratch_shapes=[pltpu.VMEM((S, HEAD_DIM), F32)] + [pltpu.VMEM((S, HEAD_DIM), BF16)] * 3
        + [pltpu.VMEM((S, HEAD_DIM), F32)],
        compiler_params=_params(("arbitrary", "arbitrary"), 32),
        name=f"attn_dilated_d{dil}",
    )(slopes, qkv, qkv, qkv)
    lse_nat = lse.transpose(1, 0, 3, 2).reshape(heads_a, N)
    return o, lse_nat


def _mix_a_kernel(*refs, n_pat, tm):
    o_refs, lse_refs, out_ref = refs[:n_pat], refs[n_pat:2 * n_pat], refs[2 * n_pat]
    heads = lse_refs[0].shape[0]
    lses = [r[...] for r in lse_refs]
    mx = functools.reduce(jnp.maximum, lses)
    es = [jnp.exp2(x - mx) for x in lses]
    den = functools.reduce(lambda x, y: x + y, es)
    ws = [e / den for e in es]
    pad = jnp.zeros((LANES - n_pat * heads, LANES), F32)
    for c in range(tm // LANES):
        cols = slice(c * LANES, (c + 1) * LANES)
        w_rows = jnp.concatenate([w[:, cols] for w in ws] + [pad], axis=0)
        w_cols = w_rows.T
        for hh in range(heads):
            acc = jnp.zeros((LANES, HEAD_DIM), F32)
            for p in range(n_pat):
                wc = w_cols[:, p * heads + hh:p * heads + hh + 1]
                acc = acc + wc * o_refs[p][hh, cols, :].astype(F32)
            out_ref[cols, hh * HEAD_DIM:(hh + 1) * HEAD_DIM] = acc.astype(BF16)


def _mix_a(os, lses, tm=512):
    heads, N, _ = os[0].shape
    n_pat = len(os)
    assert n_pat * heads <= LANES
    return pl.pallas_call(
        functools.partial(_mix_a_kernel, n_pat=n_pat, tm=tm),
        grid=(N // tm,),
        in_specs=[pl.BlockSpec((heads, tm, HEAD_DIM), lambda i: (0, i, 0))] * n_pat
        + [pl.BlockSpec((heads, tm), lambda i: (0, i))] * n_pat,
        out_specs=pl.BlockSpec((tm, heads * HEAD_DIM), lambda i: (i, 0)),
        out_shape=jax.ShapeDtypeStruct((N, heads * HEAD_DIM), BF16),
        compiler_params=_params(("arbitrary",), 32),
        name="attn_dilated_mix",
    )(*os, *lses)


def _attn_a(qkv, slopes, B, S, n_heads, heads_a):
    parts = [_attn_a_pattern(qkv, slopes, B, S, n_heads, heads_a, w, d) for w, d in DILATED_PATTERNS]
    return _mix_a([p[0] for p in parts], [p[1] for p in parts])


def _attn_b_kernel(slopes_ref, lam_ref, q_ref, k_ref, v_ref, g_ref, o_ref, tab_ref, kmax_ref,
                   m_scr, l_scr, acc_scr, *, S, Tq, Tk, lambda_init):
    h = pl.program_id(1)
    i = pl.program_id(2)
    dh = HEAD_DIM // 2
    n_chunks = S // Tk
    c_dist = LOG2E * slopes_ref[h]

    def sub_norm(x):
        sq = x * x
        return jnp.sqrt(jnp.maximum(jnp.sum(sq[:, :dh], axis=-1, keepdims=True),
                                    jnp.sum(sq[:, dh:], axis=-1, keepdims=True)))

    @pl.when(i == 0)
    def _():
        shape = (Tq, 2 * S - Tq)
        a = lax.broadcasted_iota(I32, shape, 0)
        u = lax.broadcasted_iota(I32, shape, 1)
        d = a - u + (S - Tq)
        tab_ref[...] = -c_dist * jnp.maximum(d, -d).astype(F32)
        kn = sub_norm(k_ref[0].astype(F32))
        for j in range(n_chunks):
            kmax_ref[j] = jnp.max(kn[j * Tk:(j + 1) * Tk])

    lv = lam_ref[...]
    lam = (jnp.exp(jnp.sum(lv[0:1] * lv[1:2], axis=-1, keepdims=True))
           - jnp.exp(jnp.sum(lv[2:3] * lv[3:4], axis=-1, keepdims=True)) + lambda_init)

    t0 = i * Tq
    qf = q_ref[0].astype(F32) * (LOG2E * dh ** -0.5)
    q = qf.astype(BF16)
    qmax = jnp.max(sub_norm(q.astype(F32)))

    def chunk(kstart, first):
        kc = k_ref[0, pl.ds(kstart, Tk), :]
        vc = v_ref[0, pl.ds(kstart, Tk), :]
        bias = tab_ref[:, pl.ds(pl.multiple_of(S - Tq - t0 + kstart, LANES), Tk)]
        s = jnp.concatenate(
            [lax.dot_general(q[:, c * dh:(c + 1) * dh], kc[:, c * dh:(c + 1) * dh], _NT,
                             preferred_element_type=F32) + bias for c in range(2)], axis=0)
        mc = jnp.max(s, axis=-1, keepdims=True)
        if first:
            m_new = mc
        else:
            m_old = m_scr[...]
            m_new = jnp.maximum(m_old, mc)
            alpha = jnp.exp2(m_old - m_new)
        p = jnp.exp2(s - m_new)
        ls = jnp.sum(p, axis=-1, keepdims=True)
        pv = jnp.dot(p.astype(BF16), vc, preferred_element_type=F32)
        m_scr[...] = m_new
        l_scr[...] = ls if first else alpha * l_scr[...] + ls
        acc_scr[...] = pv if first else alpha * acc_scr[...] + pv
        return m_new

    jd = t0 // Tk
    m_diag = chunk(pl.multiple_of(jd * Tk, Tk), True)
    floor = jnp.min(m_diag) - 160.0
    for j in range(n_chunks):
        gap = jnp.maximum(jnp.maximum(j * Tk - (t0 + Tq - 1), t0 - ((j + 1) * Tk - 1)), 0)
        bound = 1.001 * qmax * kmax_ref[j] + 0.01 - c_dist * gap.astype(F32)

        @pl.when((j != jd) & (bound > floor))
        def _():
            chunk(j * Tk, False)

    acc = acc_scr[...]
    l = l_scr[...]
    o = acc[:Tq] / l[:Tq] - acc[Tq:] * (lam / l[Tq:])
    o = _rms(o, SUBLN_EPS) * g_ref[...] * (1.0 - lambda_init)
    o_ref[...] = o.astype(BF16)


def _attn_b(qkv, slopes, lam_vecs, subln_g, B, S, n_heads, head0, heads_b, lambda_init, Tq=256,
            Tk=1024):
    N = B * S
    nq = S // Tq
    Tk = min(Tk, S)
    return pl.pallas_call(
        functools.partial(_attn_b_kernel, S=S, Tq=Tq, Tk=Tk, lambda_init=lambda_init),
        grid=(B, heads_b, nq),
        in_specs=[pl.BlockSpec(memory_space=pltpu.SMEM),
                  pl.BlockSpec((4, HEAD_DIM // 2), lambda b, h, i: (0, 0)),
                  pl.BlockSpec((1, Tq, HEAD_DIM), lambda b, h, i: (head0 + h, b * nq + i, 0)),
                  pl.BlockSpec((1, S, HEAD_DIM), lambda b, h, i: (n_heads + head0 + h, b, 0)),
                  pl.BlockSpec((1, S, HEAD_DIM), lambda b, h, i: (2 * n_heads + head0 + h, b, 0)),
                  pl.BlockSpec((1, HEAD_DIM), lambda b, h, i: (0, 0))],
        out_specs=pl.BlockSpec((Tq, HEAD_DIM), lambda b, h, i: (b * nq + i, h)),
        out_shape=jax.ShapeDtypeStruct((N, heads_b * HEAD_DIM), BF16),
        scratch_shapes=[pltpu.VMEM((Tq, 2 * S - Tq), F32), pltpu.SMEM((S // Tk,), F32),
                        pltpu.VMEM((2 * Tq, 1), F32), pltpu.VMEM((2 * Tq, 1), F32),
                        pltpu.VMEM((2 * Tq, HEAD_DIM), F32)],
        compiler_params=_params(("arbitrary", "arbitrary", "arbitrary"), 48),
        name="attn_diff",
    )(slopes, lam_vecs, qkv, qkv, qkv, subln_g.reshape(1, HEAD_DIM))


def _attn_c_kernel(rpb_ref, q_ref, k_ref, v_ref, o_ref, tab_ref, *, R):
    h = pl.program_id(1)
    W = GRID_W
    kr = min(NA_ROWS, R)
    n_dr = 2 * NA_ROWS - 1
    n_dc = 2 * NA_COLS - 1

    c_io = lax.broadcasted_iota(I32, (W, 2 * W), 0)
    l_io = lax.broadcasted_iota(I32, (W, 2 * W), 1)
    cp = l_io & (W - 1)
    dcm = cp - c_io + (NA_COLS - 1)
    cstart = jnp.clip(c_io - NA_COLS // 2, 0, W - NA_COLS)
    ok = (cp >= cstart) & (cp < cstart + NA_COLS)
    blocks = []
    for dr in range(n_dr):
        blk = jnp.zeros((W, 2 * W), F32)
        for dc in range(n_dc):
            blk = jnp.where(dcm == dc, rpb_ref[(h * n_dr + dr) * n_dc + dc], blk)
        blocks.append(jnp.where(ok, blk, NEG_INF))
    for o in range(NA_ROWS):
        for jj in range(kr // 2):
            tab_ref[o, :, jj * 2 * W:(jj + 1) * 2 * W] = jnp.where(
                l_io < W, blocks[o + 2 * jj], blocks[o + 2 * jj + 1])

    def row(r, carry):
        rs = jnp.clip(r - kr // 2, 0, R - kr)
        q = q_ref[0, pl.ds(pl.multiple_of(r * W, W), W), :]
        kw = k_ref[0, pl.ds(pl.multiple_of(rs * W, W), kr * W), :]
        vw = v_ref[0, pl.ds(pl.multiple_of(rs * W, W), kr * W), :]
        s = lax.dot_general(q, kw, _NT, preferred_element_type=F32) * (HEAD_DIM ** -0.5)
        s = s + tab_ref[rs - r + (NA_ROWS - 1)]
        m = jnp.max(s, axis=-1, keepdims=True)
        p = jnp.exp(s - m)
        l = jnp.sum(p, axis=-1, keepdims=True)
        o = jnp.dot(p.astype(BF16), vw, preferred_element_type=F32) / l
        o_ref[pl.ds(pl.multiple_of(r * W, W), W), :] = o.astype(BF16)
        return carry

    lax.fori_loop(0, R, row, 0, unroll=8 if R % 8 == 0 else 1)


def _attn_c(qkv, rpb_flat, B, S, n_heads, head0, heads_c):
    N = B * S
    R = S // GRID_W
    kr = min(NA_ROWS, R)
    assert kr == NA_ROWS and kr % 2 == 0
    return pl.pallas_call(
        functools.partial(_attn_c_kernel, R=R),
        grid=(B, heads_c),
        in_specs=[pl.BlockSpec(memory_space=pltpu.SMEM),
                  pl.BlockSpec((1, S, HEAD_DIM), lambda b, h: (head0 + h, b, 0)),
                  pl.BlockSpec((1, S, HEAD_DIM), lambda b, h: (n_heads + head0 + h, b, 0)),
                  pl.BlockSpec((1, S, HEAD_DIM), lambda b, h: (2 * n_heads + head0 + h, b, 0))],
        out_specs=pl.BlockSpec((S, HEAD_DIM), lambda b, h: (b, h)),
        out_shape=jax.ShapeDtypeStruct((N, heads_c * HEAD_DIM), BF16),
        scratch_shapes=[pltpu.VMEM((NA_ROWS, GRID_W, kr * GRID_W), F32)],
        compiler_params=_params(("arbitrary", "arbitrary"), 32),
        name="attn_nbr",
    )(rpb_flat, qkv, qkv, qkv)


def _out_kernel(oa_ref, ob_ref, oc_ref, wa_ref, wb_ref, wc_ref, x_ref, g1_ref, ng_ref, sc_ref,
                sh_ref, rw_ref, x1_ref, h2_ref, lg_ref):
    acc = jnp.dot(oa_ref[...], wa_ref[...], preferred_element_type=F32)
    acc = acc + jnp.dot(ob_ref[...], wb_ref[...], preferred_element_type=F32)
    acc = acc + jnp.dot(oc_ref[...], wc_ref[...], preferred_element_type=F32)
    x1 = x_ref[...] + g1_ref[0] * acc
    x1_ref[...] = x1
    h2 = (_rms(x1, NORM_EPS) * ng_ref[...]) * (1.0 + sc_ref[0]) + sh_ref[0]
    _store_row_slabs(h2_ref, _pack_pairs(h2))
    lg_ref[...] = lax.dot_general(rw_ref[...], h2, _NT, preferred_element_type=F32,
                                  precision=HIGHEST)


def _out_proj(oa, ob, oc, w_bf, x2d, g1, ng, sc, sh, rw_t, S, tm=512):
    N, D = x2d.shape
    wa, wb, wc = oa.shape[1], ob.shape[1], oc.shape[1]
    E = rw_t.shape[0]
    tpb = S // tm
    const = dict(pipeline_mode=pl.Buffered(1))
    row = lambda i: (i, 0)
    per_b = lambda i: (i // tpb, 0, 0)
    return pl.pallas_call(
        _out_kernel,
        grid=(N // tm,),
        in_specs=[pl.BlockSpec((tm, wa), row), pl.BlockSpec((tm, wb), row), pl.BlockSpec((tm, wc), row),
                  pl.BlockSpec((wa, D), lambda i: (0, 0), **const),
                  pl.BlockSpec((wb, D), lambda i: (wa // wb, 0), **const),
                  pl.BlockSpec((wc, D), lambda i: ((wa + wb) // wc, 0), **const),
                  pl.BlockSpec((tm, D), row),
                  pl.BlockSpec((1, 1, D), per_b),
                  pl.BlockSpec((1, D), lambda i: (0, 0)),
                  pl.BlockSpec((1, 1, D), per_b),
                  pl.BlockSpec((1, 1, D), per_b),
                  pl.BlockSpec((E, D), lambda i: (0, 0), **const)],
        out_specs=[pl.BlockSpec((tm, D), row), pl.BlockSpec((tm * (D // 2 // LANES), LANES), row),
                   pl.BlockSpec((E, tm), lambda i: (0, i))],
        out_shape=[jax.ShapeDtypeStruct((N, D), F32),
                   jax.ShapeDtypeStruct((N * (D // 2 // LANES), LANES), U32),
                   jax.ShapeDtypeStruct((E, N), F32)],
        compiler_params=_params(("arbitrary",), 52),
        name="out_proj",
    )(oa, ob, oc, w_bf, w_bf, w_bf, x2d, g1, ng.reshape(1, D), sc, sh, rw_t)


def _first_argmax(vals, iota, big):
    mx = jnp.max(vals, axis=0, keepdims=True)
    idx = jnp.min(jnp.where(vals == mx, iota, big), axis=0, keepdims=True)
    return mx, idx


def _route_kernel(lg_ref, rb_ref, idx_ref, gate_ref, rank_ref, cnt_ref, tri_ref, carry_ref, *, Tt):
    i = pl.program_id(0)
    E = lg_ref.shape[0]
    gs = E // N_GROUPS

    @pl.when(i == 0)
    def _():
        r = lax.broadcasted_iota(I32, (Tt, Tt), 0)
        c = lax.broadcasted_iota(I32, (Tt, Tt), 1)
        tri_ref[...] = jnp.where(r < c, 1.0, 0.0).astype(BF16)
        carry_ref[...] = jnp.zeros_like(carry_ref)

    scores = jax.nn.sigmoid(lg_ref[...])
    sel = scores + rb_ref[...]
    e_io = lax.broadcasted_iota(I32, (E, Tt), 0).astype(F32)
    s_io = lax.broadcasted_iota(I32, (gs, Tt), 0).astype(F32)
    g_io = lax.broadcasted_iota(I32, (N_GROUPS, Tt), 0).astype(F32)

    grp = []
    for g in range(N_GROUPS):
        blk = sel[g * gs:(g + 1) * gs]
        m1, a1 = _first_argmax(blk, s_io, float(gs))
        m2 = jnp.max(jnp.where(s_io == a1, -jnp.inf, blk), axis=0, keepdims=True)
        grp.append(m1 + m2)
    grp = jnp.concatenate(grp, axis=0)

    gsel = jnp.zeros((N_GROUPS, Tt), F32)
    for _ in range(TOPK_GROUPS):
        _, gi = _first_argmax(grp, g_io, float(N_GROUPS))
        hit = g_io == gi
        gsel = jnp.where(hit, 1.0, gsel)
        grp = jnp.where(hit, -jnp.inf, grp)

    esel = jnp.concatenate(
        [jnp.broadcast_to(gsel[g:g + 1], (gs, Tt)) for g in range(N_GROUPS)], axis=0)
    cur = jnp.where(esel > 0.0, sel, NEG_INF)

    idxs, gates = [], []
    onehot = jnp.zeros((E, Tt), F32)
    for _ in range(TOP_K):
        _, ei = _first_argmax(cur, e_io, float(E))
        hit = e_io == ei
        idxs.append(ei)
        gates.append(jnp.sum(jnp.where(hit, scores, 0.0), axis=0, keepdims=True))
        onehot = jnp.where(hit, 1.0, onehot)
        cur = jnp.where(hit, -jnp.inf, cur)
    gate = jnp.concatenate(gates, axis=0)
    gate = gate / jnp.sum(gate, axis=0, keepdims=True) * ROUTED_SCALE
    idx_ref[...] = jnp.concatenate(idxs, axis=0).astype(I32)
    gate_ref[...] = gate

    oh = onehot.astype(BF16)
    before = jnp.dot(oh, tri_ref[...], preferred_element_type=F32)
    tile_cnt = jnp.dot(oh, jnp.ones((Tt, LANES), BF16), preferred_element_type=F32)
    carry = carry_ref[...]
    rank_mat = before + jnp.concatenate([carry] * (Tt // LANES), axis=1)
    ranks = [jnp.sum(jnp.where(e_io == ei, rank_mat, 0.0), axis=0, keepdims=True) for ei in idxs]
    rank_ref[...] = jnp.concatenate(ranks, axis=0).astype(I32)
    carry_ref[...] = carry + tile_cnt
    cnt_ref[...] = (carry + tile_cnt).astype(I32)


def _route(logits_t, router_bias, Tt=512):
    E, N = logits_t.shape
    tok = lambda i: (0, i)
    return pl.pallas_call(
        functools.partial(_route_kernel, Tt=Tt),
        grid=(N // Tt,),
        in_specs=[pl.BlockSpec((E, Tt), tok), pl.BlockSpec((E, 1), lambda i: (0, 0))],
        out_specs=[pl.BlockSpec((TOP_K, Tt), tok), pl.BlockSpec((TOP_K, Tt), tok),
                   pl.BlockSpec((TOP_K, Tt), tok), pl.BlockSpec((E, LANES), lambda i: (0, 0))],
        out_shape=[jax.ShapeDtypeStruct((TOP_K, N), I32), jax.ShapeDtypeStruct((TOP_K, N), F32),
                   jax.ShapeDtypeStruct((TOP_K, N), I32), jax.ShapeDtypeStruct((E, LANES), I32)],
        scratch_shapes=[pltpu.VMEM((Tt, Tt), BF16), pltpu.VMEM((E, LANES), F32)],
        compiler_params=_params(("arbitrary",), 32),
        name="route_topk",
    )(logits_t, router_bias.reshape(E, 1))


def _dest_kernel(pstart_ref, idx_ref, rank_ref, dest_ref, *, G):
    idx = idx_ref[...]
    base = jnp.zeros(idx.shape, I32)
    for e in range(pstart_ref.shape[0]):
        base = jnp.where(idx == e, pstart_ref[e], base)
    dest_ref[...] = (base + rank_ref[...]) * G


def _dest(pstart, idx, rank, G):
    return pl.pallas_call(
        functools.partial(_dest_kernel, G=G),
        in_specs=[pl.BlockSpec(memory_space=pltpu.SMEM), pl.BlockSpec(memory_space=pltpu.VMEM),
                  pl.BlockSpec(memory_space=pltpu.VMEM)],
        out_specs=pl.BlockSpec(memory_space=pltpu.VMEM),
        out_shape=jax.ShapeDtypeStruct(idx.shape, I32),
        name="route_dest",
    )(pstart, idx, rank)


def _dispatch_kernel(dest_hbm, h_ref, xs_hbm, dest_smem, idx_sem, row_sem, *, Td, G):
    i = pl.program_id(0)
    n = Td * TOP_K
    cp = pltpu.make_async_copy(dest_hbm.at[pl.ds(i * n, n)], dest_smem, idx_sem)
    cp.start()
    cp.wait()

    def row_copy(t, k):
        src = h_ref.at[pl.ds(pl.multiple_of(t * G, G), G)]
        dst = xs_hbm.at[pl.ds(pl.multiple_of(dest_smem[t * TOP_K + k], G), G)]
        return pltpu.make_async_copy(src, dst, row_sem)

    def issue(t, carry):
        for k in range(TOP_K):
            row_copy(t, k).start(priority=k % 2)
        return carry

    lax.fori_loop(0, Td, issue, 0)

    def drain(t, carry):
        for k in range(TOP_K):
            row_copy(t, k).wait()
        return carry

    lax.fori_loop(0, Td, drain, 0)


def _dispatch(dest_flat, h2, P, G, Td=256):
    N = h2.shape[0] // G
    return pl.pallas_call(
        functools.partial(_dispatch_kernel, Td=Td, G=G),
        grid=(N // Td,),
        in_specs=[pl.BlockSpec(memory_space=pl.ANY), pl.BlockSpec((Td * G, LANES), lambda i: (i, 0))],
        out_specs=pl.BlockSpec(memory_space=pl.ANY),
        out_shape=jax.ShapeDtypeStruct((P * G, LANES), h2.dtype),
        scratch_shapes=[pltpu.SMEM((Td * TOP_K,), I32), pltpu.SemaphoreType.DMA,
                        pltpu.SemaphoreType.DMA],
        compiler_params=_params(("arbitrary",), 32),
        name="moe_dispatch",
    )(dest_flat, h2)


def _moe_kernel(be_ref, nv_ref, nb_ref, xs_ref, wg_ref, wu_ref, wd_ref, ys_ref, wg_s, wu_s, wd_s):
    b = pl.program_id(0)
    active = b < nb_ref[0]
    new_expert = (b == 0) | (be_ref[b] != be_ref[jnp.maximum(b - 1, 0)])

    @pl.when(active & new_expert)
    def _():
        wg_s[...] = wg_ref[0].astype(BF16)
        wu_s[...] = wu_ref[0].astype(BF16)
        wd_s[...] = wd_ref[0].astype(BF16)

    @pl.when(active)
    def _():
        T = MOE_BLOCK
        G = xs_ref.shape[0] // T
        valid = lax.broadcasted_iota(I32, (T, LANES), 0) < nv_ref[b]
        his, los = [], []
        for c in range(G):
            w = jnp.where(valid, _load_row_slab(xs_ref, c, T, G), jnp.uint32(0))
            hi, lo = _unpack_pairs(w)
            his.append(hi.astype(BF16))
            los.append(lo.astype(BF16))
        x = jnp.concatenate(his + los, axis=1)
        g = jnp.dot(x, wg_s[...], preferred_element_type=F32)
        u = jnp.dot(x, wu_s[...], preferred_element_type=F32)
        a = (g * jax.nn.sigmoid(g) * u).astype(BF16)
        _store_row_slabs(ys_ref, _pack_pairs(jnp.dot(a, wd_s[...], preferred_element_type=F32)))


def _moe(block_expert, nvalid, nblocks, xs, wg, wu, wd):
    _, D, F = wg.shape
    G = D // 2 // LANES
    P = xs.shape[0] // G
    n_blocks = P // MOE_BLOCK
    blk = lambda b, be, nv, nb: (jnp.minimum(b, nb[0] - 1), 0)
    return pl.pallas_call(
        _moe_kernel,
        grid_spec=pltpu.PrefetchScalarGridSpec(
            num_scalar_prefetch=3,
            grid=(n_blocks,),
            in_specs=[pl.BlockSpec((MOE_BLOCK * G, LANES), blk),
                      pl.BlockSpec((1, D, F), lambda b, be, nv, nb: (be[b], 0, 0)),
                      pl.BlockSpec((1, D, F), lambda b, be, nv, nb: (be[b], 0, 0)),
                      pl.BlockSpec((1, F, D), lambda b, be, nv, nb: (be[b], 0, 0))],
            out_specs=pl.BlockSpec((MOE_BLOCK * G, LANES), blk),
            scratch_shapes=[pltpu.VMEM((D, F), BF16), pltpu.VMEM((D, F), BF16),
                            pltpu.VMEM((F, D), BF16)]),
        out_shape=jax.ShapeDtypeStruct((P * G, LANES), U32),
        compiler_params=_params(("arbitrary",), 52),
        name="moe_experts",
    )(block_expert, nvalid, nblocks, xs, wg, wu, wd)


def _combine_kernel(dest_hbm, ys_hbm, gate_ref, h_ref, x_ref, g2_ref, sg_ref, su_ref, sd_ref,
                    fg_ref, o_ref, dest_smem, ybuf, idx_sem, row_sem, *, Tc, final_norm):
    i = pl.program_id(0)
    n = Tc * TOP_K
    cp = pltpu.make_async_copy(dest_hbm.at[pl.ds(i * n, n)], dest_smem, idx_sem)
    cp.start()
    cp.wait()

    G = h_ref.shape[0] // Tc

    def row_copy(t, k):
        src = ys_hbm.at[pl.ds(pl.multiple_of(dest_smem[t * TOP_K + k], G), G)]
        return pltpu.make_async_copy(src, ybuf.at[k, pl.ds(pl.multiple_of(t * G, G), G)], row_sem)

    def issue(t, carry):
        for k in range(TOP_K):
            row_copy(t, k).start(priority=k % 2)
        return carry

    lax.fori_loop(0, Tc, issue, 0)

    halves = [_unpack_pairs(_load_row_slab(h_ref, c, Tc, G)) for c in range(G)]
    h = jnp.concatenate([p[0].astype(BF16) for p in halves] + [p[1].astype(BF16) for p in halves],
                        axis=1)
    g = jnp.dot(h, sg_ref[...], preferred_element_type=F32)
    u = jnp.dot(h, su_ref[...], preferred_element_type=F32)
    a = (g * jax.nn.sigmoid(g) * u).astype(BF16)
    f = jnp.dot(a, sd_ref[...], preferred_element_type=F32)

    def drain(t, carry):
        for k in range(TOP_K):
            row_copy(t, k).wait()
        return carry

    lax.fori_loop(0, Tc, drain, 0)

    gate = gate_ref[...]
    half = f.shape[1] // 2
    his, los = [], []
    for c in range(G):
        cols = slice(c * LANES, (c + 1) * LANES)
        f_hi, f_lo = f[:, cols], f[:, half + c * LANES:half + (c + 1) * LANES]
        for k in range(TOP_K):
            y_hi, y_lo = _unpack_pairs(_load_row_slab(ybuf.at[k], c, Tc, G))
            f_hi = f_hi + gate[:, k:k + 1] * y_hi
            f_lo = f_lo + gate[:, k:k + 1] * y_lo
        his.append(f_hi)
        los.append(f_lo)
    f = jnp.concatenate(his + los, axis=1)
    x2 = x_ref[...] + g2_ref[0] * f
    if final_norm:
        x2 = _rms(x2, NORM_EPS) * fg_ref[...]
    o_ref[...] = x2


def _combine(dest_flat, ys, gate_t, h2, x1, g2, sg, su, sd, final_g, S, final_norm, Tc=256):
    N, D = x1.shape
    F = sg.shape[1]
    G = D // 2 // LANES
    tpb = S // Tc
    const = dict(pipeline_mode=pl.Buffered(1))
    row = lambda i: (i, 0)
    return pl.pallas_call(
        functools.partial(_combine_kernel, Tc=Tc, final_norm=final_norm),
        grid=(N // Tc,),
        in_specs=[pl.BlockSpec(memory_space=pl.ANY), pl.BlockSpec(memory_space=pl.ANY),
                  pl.BlockSpec((Tc, TOP_K), row),
                  pl.BlockSpec((Tc * G, LANES), row), pl.BlockSpec((Tc, D), row),
                  pl.BlockSpec((1, 1, D), lambda i: (i // tpb, 0, 0)),
                  pl.BlockSpec((D, F), lambda i: (0, 0), **const),
                  pl.BlockSpec((D, F), lambda i: (0, 0), **const),
                  pl.BlockSpec((F, D), lambda i: (0, 0), **const),
                  pl.BlockSpec((1, D), lambda i: (0, 0))],
        out_specs=pl.BlockSpec((Tc, D), row),
        out_shape=jax.ShapeDtypeStruct((N, D), F32),
        scratch_shapes=[pltpu.SMEM((Tc * TOP_K,), I32), pltpu.VMEM((TOP_K, Tc * G, LANES), U32),
                        pltpu.SemaphoreType.DMA, pltpu.SemaphoreType.DMA],
        compiler_params=_params(("arbitrary",), 52),
        name="moe_combine",
    )(dest_flat, ys, gate_t, h2, x1, g2, sg, su, sd, final_g.reshape(1, D))


def _moe_ffn(h2, logits_t, x1, g2, router_bias, wg, wu, wd, e_off, sg, su, sd, final_g, S,
             final_norm):
    N, D = x1.shape
    G = D // 2 // LANES
    E = logits_t.shape[0]
    idx, gate, rank, cnt = _route(logits_t, router_bias)
    counts = cnt[:, 0]
    padded = (counts + MOE_BLOCK - 1) // MOE_BLOCK * MOE_BLOCK
    pend = jnp.cumsum(padded)
    pstart = (pend - padded).astype(I32)
    n_blocks = -(-(N * TOP_K + E * (MOE_BLOCK - 1)) // MOE_BLOCK)
    bstart = jnp.arange(n_blocks, dtype=I32) * MOE_BLOCK
    block_expert = jnp.minimum(jnp.sum(pend[None, :] <= bstart[:, None], axis=1), E - 1).astype(I32)
    nvalid = jnp.clip(pstart[block_expert] + counts[block_expert] - bstart, 0, MOE_BLOCK).astype(I32)
    nblocks = (pend[-1:] // MOE_BLOCK).astype(I32)

    dest = _dest(pstart, idx, rank, G)
    dest_flat = dest.T.reshape(N * TOP_K)
    xs = _dispatch(dest_flat, h2, n_blocks * MOE_BLOCK, G)
    ys = _moe(block_expert + e_off, nvalid, nblocks, xs, wg, wu, wd)
    return _combine(dest_flat, ys, gate.T, h2, x1, g2, sg, su, sd, final_g, S, final_norm)


def kernel(x, c, ada_w, ada_b, norm_mix_g, norm_ffn_g, w_in, lambda_q1, lambda_k1, lambda_q2,
           lambda_k2, subln_g, rpb, w_out, router_w, router_bias, exp_w_gate, exp_w_up,
           exp_w_down, sh_w_gate, sh_w_up, sh_w_down, final_g):
    B, S, D = x.shape
    L = ada_w.shape[0]
    N = B * S
    n_heads = w_in.shape[2] // (3 * HEAD_DIM)
    heads_a = n_heads // HEADS_A_FRAC
    heads_b = n_heads // HEADS_B_FRAC
    heads_c = n_heads - heads_a - heads_b
    slopes = _alibi_slopes(heads_a + heads_b)
    slopes_a = jnp.asarray(slopes[:heads_a], F32)
    slopes_b = jnp.asarray(slopes[heads_a:], F32)

    E, F = exp_w_gate.shape[1], exp_w_gate.shape[3]
    wg_all = exp_w_gate.reshape(L * E, D, F)
    wu_all = exp_w_up.reshape(L * E, D, F)
    wd_all = exp_w_down.reshape(L * E, F, D)

    mod = _ada(c, ada_w, ada_b).reshape(L, B, 6, 1, D)
    xf = x.reshape(N, D)
    for l in range(L):
        sh1, sc1, g1, sh2, sc2, g2 = (mod[l, :, j] for j in range(6))
        qkv = _qkv(xf, norm_mix_g[l], sc1, sh1, w_in[l].astype(BF16), S)
        oa = _attn_a(qkv, slopes_a, B, S, n_heads, heads_a)
        lambda_init = 0.8 - 0.6 * math.exp(-0.3 * l)
        lam_vecs = jnp.stack([lambda_q1[l], lambda_k1[l], lambda_q2[l], lambda_k2[l]]).astype(F32)
        ob = _attn_b(qkv, slopes_b, lam_vecs, subln_g[l], B, S, n_heads, heads_a, heads_b, lambda_init)
        oc = _attn_c(qkv, rpb[l].reshape(-1), B, S, n_heads, heads_a + heads_b, heads_c)
        x1, h2, logits_t = _out_proj(oa, ob, oc, w_out[l].astype(BF16), xf, g1, norm_ffn_g[l],
                                     sc2, sh2, router_w[l].T, S)
        xf = _moe_ffn(h2, logits_t, x1, g2, router_bias[l], wg_all, wu_all, wd_all, l * E,
                      sh_w_gate[l].astype(BF16), sh_w_up[l].astype(BF16),
                      sh_w_down[l].astype(BF16), final_g, S, l == L - 1)
    return xf.reshape(B, S, D)
```

```python
import functools
import math

import jax
import jax.numpy as jnp
from jax import lax
from jax.experimental import pallas as pl
from jax.experimental.pallas import tpu as pltpu

F32 = jnp.float32
BF16 = jnp.bfloat16
I32 = jnp.int32
U32 = jnp.uint32
HIGHEST = lax.Precision.HIGHEST
_NT = (((1,), (1,)), ((), ()))

HEAD_DIM = 128
HEADS_A_FRAC, HEADS_B_FRAC = 2, 4
DILATED_PATTERNS = ((128, 1), (512, 4), (2048, 16))
GRID_W = 64
NA_ROWS = 8
NA_COLS = 16
N_GROUPS = 8
TOPK_GROUPS = 4
TOP_K = 8
ROUTED_SCALE = 2.5
NORM_EPS = 1e-6
SUBLN_EPS = 1e-5
NEG_INF = -1e30
LOG2E = math.log2(math.e)

LANES = 128
VMEM_BYTES_V7X = 64 << 20

MOE_BLOCK = 512


def _params(semantics, vmem_mib):
    return pltpu.CompilerParams(dimension_semantics=semantics,
                                vmem_limit_bytes=min(vmem_mib << 20, VMEM_BYTES_V7X - (4 << 20)))


def _alibi_slopes(n):
    def pow2(m):
        start = 2.0 ** (-8.0 / m)
        return [start ** (i + 1) for i in range(m)]
    p = 2 ** int(math.floor(math.log2(n)))
    return pow2(p) + pow2(2 * p)[0::2][: n - p]


def _rms(x, eps):
    return x * lax.rsqrt(jnp.mean(x * x, axis=-1, keepdims=True) + eps)


def _pack_pairs(x):
    half = x.shape[1] // 2
    hi = lax.bitcast_convert_type(x[:, :half].astype(BF16).astype(F32), U32)
    lo = lax.bitcast_convert_type(x[:, half:].astype(BF16).astype(F32), U32)
    return hi | (lo >> 16)


def _unpack_pairs(w):
    hi = lax.bitcast_convert_type(w & jnp.uint32(0xFFFF0000), F32)
    lo = lax.bitcast_convert_type(w << 16, F32)
    return hi, lo


def _store_row_slabs(ref, packed):
    T, W = packed.shape
    for c in range(W // LANES):
        ref[pl.ds(c, T, stride=W // LANES), :] = packed[:, c * LANES:(c + 1) * LANES]


def _load_row_slab(ref, c, T, G):
    return ref[pl.ds(c, T, stride=G), :]


def _ada_kernel(c_ref, w_ref, b_ref, o_ref):
    c = c_ref[...]
    sc = c * jax.nn.sigmoid(c)
    o_ref[0] = jnp.dot(sc, w_ref[0], preferred_element_type=F32, precision=HIGHEST) + b_ref[0]


def _ada(c, ada_w, ada_b):
    B, D = c.shape
    L, _, D6 = ada_w.shape
    rows = 8
    cp = jnp.zeros((rows, D), F32).at[:B].set(c)
    tn = 1024
    out = pl.pallas_call(
        _ada_kernel,
        grid=(L, D6 // tn),
        in_specs=[pl.BlockSpec((rows, D), lambda l, j: (0, 0)),
                  pl.BlockSpec((1, D, tn), lambda l, j: (l, 0, j)),
                  pl.BlockSpec((1, 1, tn), lambda l, j: (l, 0, j))],
        out_specs=pl.BlockSpec((1, rows, tn), lambda l, j: (l, 0, j)),
        out_shape=jax.ShapeDtypeStruct((L, rows, D6), F32),
        compiler_params=_params(("arbitrary", "arbitrary"), 32),
        name="ada_mod",
    )(cp, ada_w, ada_b.reshape(L, 1, D6))
    return out[:, :B]


def _qkv_kernel(x_ref, g_ref, sc_ref, sh_ref, w_ref, o_ref, h_scr, *, heads_per_step):
    @pl.when(pl.program_id(1) == 0)
    def _():
        y = _rms(x_ref[...], NORM_EPS) * g_ref[...]
        h_scr[...] = (y * (1.0 + sc_ref[0]) + sh_ref[0]).astype(BF16)

    res = jnp.dot(h_scr[...], w_ref[...], preferred_element_type=F32)
    for hh in range(heads_per_step):
        o_ref[hh] = res[:, hh * HEAD_DIM:(hh + 1) * HEAD_DIM].astype(BF16)


def _qkv(x2d, g, sc, sh, w_bf, S, tm=512, tn=1536):
    N, D = x2d.shape
    W3 = w_bf.shape[1]
    hps = tn // HEAD_DIM
    tpb = S // tm
    return pl.pallas_call(
        functools.partial(_qkv_kernel, heads_per_step=hps),
        grid=(N // tm, W3 // tn),
        in_specs=[pl.BlockSpec((tm, D), lambda i, j: (i, 0)),
                  pl.BlockSpec((1, D), lambda i, j: (0, 0)),
                  pl.BlockSpec((1, 1, D), lambda i, j: (i // tpb, 0, 0)),
                  pl.BlockSpec((1, 1, D), lambda i, j: (i // tpb, 0, 0)),
                  pl.BlockSpec((D, tn), lambda i, j: (0, j))],
        out_specs=pl.BlockSpec((hps, tm, HEAD_DIM), lambda i, j: (j, i, 0)),
        out_shape=jax.ShapeDtypeStruct((W3 // HEAD_DIM, N, HEAD_DIM), BF16),
        scratch_shapes=[pltpu.VMEM((tm, D), BF16)],
        compiler_params=_params(("arbitrary", "arbitrary"), 48),
        name="qkv_proj",
    )(x2d, g.reshape(1, D), sc, sh, w_bf)


def _col_to_row(col, eye):
    return jnp.sum(jnp.where(eye, col, 0.0), axis=0, keepdims=True)


def _attn_a_kernel(slopes_ref, q_ref, k_ref, v_ref, o_ref, lse_ref, nat, cq, ck, cv, onat, *,
                   L, dil, half, Tq):
    h = pl.program_id(1)
    wk = 2 * Tq
    n_tiles = L // Tq
    c_dist = LOG2E * slopes_ref[h] * dil
    a = lax.broadcasted_iota(I32, (Tq, wk), 0)
    u = lax.broadcasted_iota(I32, (Tq, wk), 1)
    eye = lax.broadcasted_iota(I32, (Tq, Tq), 0) == lax.broadcasted_iota(I32, (Tq, Tq), 1)

    def table(off):
        d = u - a - off
        ad = jnp.maximum(d, -d)
        return jnp.where(ad <= half, -c_dist * ad.astype(F32), NEG_INF)

    tabs = {off: table(off) for off in sorted({0, half, wk - Tq})}

    for src, dst, scale in ((q_ref, cq, LOG2E * HEAD_DIM ** -0.5), (k_ref, ck, None), (v_ref, cv, None)):
        x = src[0].astype(F32)
        nat[...] = x if scale is None else x * scale
        for r in range(dil):
            dst[r * L:(r + 1) * L, :] = nat[pl.ds(r, L, stride=dil), :].astype(BF16)

    for r in range(dil):
        for j in range(n_tiles):
            start = min(max(j * Tq - half, 0), L - wk)
            q = cq[r * L + j * Tq:r * L + (j + 1) * Tq, :]
            kw = ck[r * L + start:r * L + start + wk, :]
            vw = cv[r * L + start:r * L + start + wk, :]
            s = lax.dot_general(q, kw, _NT, preferred_element_type=F32) + tabs[j * Tq - start]
            m = jnp.max(s, axis=-1, keepdims=True)
            p = jnp.exp2(s - m)
            l = jnp.sum(p, axis=-1, keepdims=True)
            o = jnp.dot(p.astype(BF16), vw, preferred_element_type=F32) / l
            onat[pl.ds(r + j * Tq * dil, Tq, stride=dil), :] = o
            lse_ref[0, 0, r:r + 1, j * Tq:(j + 1) * Tq] = _col_to_row(m + jnp.log2(l), eye)
    o_ref[0] = onat[...].astype(BF16)


def _attn_a_pattern(qkv, slopes, B, S, n_heads, heads_a, window, dil, Tq=128):
    N = B * S
    L = S // dil
    half = window // (2 * dil)
    assert L % Tq == 0 and L >= 2 * Tq and 2 * half <= Tq
    blk = (1, S, HEAD_DIM)
    o, lse = pl.pallas_call(
        functools.partial(_attn_a_kernel, L=L, dil=dil, half=half, Tq=Tq),
        grid=(B, heads_a),
        in_specs=[pl.BlockSpec(memory_space=pltpu.SMEM),
                  pl.BlockSpec(blk, lambda b, h: (h, b, 0)),
                  pl.BlockSpec(blk, lambda b, h: (n_heads + h, b, 0)),
                  pl.BlockSpec(blk, lambda b, h: (2 * n_heads + h, b, 0))],
        out_specs=[pl.BlockSpec(blk, lambda b, h: (h, b, 0)),
                   pl.BlockSpec((1, 1, dil, L), lambda b, h: (b, h, 0, 0))],
        out_shape=[jax.ShapeDtypeStruct((heads_a, N, HEAD_DIM), BF16),
                   jax.ShapeDtypeStruct((B, heads_a, dil, L), F32)],
        scratch_shapes=[pltpu.VMEM((S, HEAD_DIM), F32)] + [pltpu.VMEM((S, HEAD_DIM), BF16)] * 3
        + [pltpu.VMEM((S, HEAD_DIM), F32)],
        compiler_params=_params(("arbitrary", "arbitrary"), 32),
        name=f"attn_dilated_d{dil}",
    )(slopes, qkv, qkv, qkv)
    lse_nat = lse.transpose(1, 0, 3, 2).reshape(heads_a, N)
    return o, lse_nat


def _mix_a_kernel(*refs, n_pat, tm):
    o_refs, lse_refs, out_ref = refs[:n_pat], refs[n_pat:2 * n_pat], refs[2 * n_pat]
    heads = lse_refs[0].shape[0]
    lses = [r[...] for r in lse_refs]
    mx = functools.reduce(jnp.maximum, lses)
    es = [jnp.exp2(x - mx) for x in lses]
    den = functools.reduce(lambda x, y: x + y, es)
    ws = [e / den for e in es]
    pad = jnp.zeros((LANES - n_pat * heads, LANES), F32)
    for c in range(tm // LANES):
        cols = slice(c * LANES, (c + 1) * LANES)
        w_rows = jnp.concatenate([w[:, cols] for w in ws] + [pad], axis=0)
        w_cols = w_rows.T
        for hh in range(heads):
            acc = jnp.zeros((LANES, HEAD_DIM), F32)
            for p in range(n_pat):
                wc = w_cols[:, p * heads + hh:p * heads + hh + 1]
                acc = acc + wc * o_refs[p][hh, cols, :].astype(F32)
            out_ref[cols, hh * HEAD_DIM:(hh + 1) * HEAD_DIM] = acc.astype(BF16)


def _mix_a(os, lses, tm=512):
    heads, N, _ = os[0].shape
    n_pat = len(os)
    assert n_pat * heads <= LANES
    return pl.pallas_call(
        functools.partial(_mix_a_kernel, n_pat=n_pat, tm=tm),
        grid=(N // tm,),
        in_specs=[pl.BlockSpec((heads, tm, HEAD_DIM), lambda i: (0, i, 0))] * n_pat
        + [pl.BlockSpec((heads, tm), lambda i: (0, i))] * n_pat,
        out_specs=pl.BlockSpec((tm, heads * HEAD_DIM), lambda i: (i, 0)),
        out_shape=jax.ShapeDtypeStruct((N, heads * HEAD_DIM), BF16),
        compiler_params=_params(("arbitrary",), 32),
        name="attn_dilated_mix",
    )(*os, *lses)


def _attn_a(qkv, slopes, B, S, n_heads, heads_a):
    parts = [_attn_a_pattern(qkv, slopes, B, S, n_heads, heads_a, w, d) for w, d in DILATED_PATTERNS]
    return _mix_a([p[0] for p in parts], [p[1] for p in parts])


def _attn_b_kernel(slopes_ref, lam_ref, q_ref, k_ref, v_ref, g_ref, o_ref, tab_ref, kmax_ref,
                   m_scr, l_scr, acc_scr, *, S, Tq, Tk, lambda_init):
    h = pl.program_id(1)
    i = pl.program_id(2)
    dh = HEAD_DIM // 2
    n_chunks = S // Tk
    c_dist = LOG2E * slopes_ref[h]

    def sub_norm(x):
        sq = x * x
        return jnp.sqrt(jnp.maximum(jnp.sum(sq[:, :dh], axis=-1, keepdims=True),
                                    jnp.sum(sq[:, dh:], axis=-1, keepdims=True)))

    @pl.when(i == 0)
    def _():
        shape = (Tq, 2 * S - Tq)
        a = lax.broadcasted_iota(I32, shape, 0)
        u = lax.broadcasted_iota(I32, shape, 1)
        d = a - u + (S - Tq)
        tab_ref[...] = -c_dist * jnp.maximum(d, -d).astype(F32)
        kn = sub_norm(k_ref[0].astype(F32))
        for j in range(n_chunks):
            kmax_ref[j] = jnp.max(kn[j * Tk:(j + 1) * Tk])

    lv = lam_ref[...]
    lam = (jnp.exp(jnp.sum(lv[0:1] * lv[1:2], axis=-1, keepdims=True))
           - jnp.exp(jnp.sum(lv[2:3] * lv[3:4], axis=-1, keepdims=True)) + lambda_init)

    t0 = i * Tq
    qf = q_ref[0].astype(F32) * (LOG2E * dh ** -0.5)
    q = qf.astype(BF16)
    qmax = jnp.max(sub_norm(q.astype(F32)))

    def chunk(kstart, first):
        kc = k_ref[0, pl.ds(kstart, Tk), :]
        vc = v_ref[0, pl.ds(kstart, Tk), :]
        bias = tab_ref[:, pl.ds(pl.multiple_of(S - Tq - t0 + kstart, LANES), Tk)]
        s = jnp.concatenate(
            [lax.dot_general(q[:, c * dh:(c + 1) * dh], kc[:, c * dh:(c + 1) * dh], _NT,
                             preferred_element_type=F32) + bias for c in range(2)], axis=0)
        mc = jnp.max(s, axis=-1, keepdims=True)
        if first:
            m_new = mc
        else:
            m_old = m_scr[...]
            m_new = jnp.maximum(m_old, mc)
            alpha = jnp.exp2(m_old - m_new)
        p = jnp.exp2(s - m_new)
        ls = jnp.sum(p, axis=-1, keepdims=True)
        pv = jnp.dot(p.astype(BF16), vc, preferred_element_type=F32)
        m_scr[...] = m_new
        l_scr[...] = ls if first else alpha * l_scr[...] + ls
        acc_scr[...] = pv if first else alpha * acc_scr[...] + pv
        return m_new

    jd = t0 // Tk
    m_diag = chunk(pl.multiple_of(jd * Tk, Tk), True)
    floor = jnp.min(m_diag) - 160.0
    for j in range(n_chunks):
        gap = jnp.maximum(jnp.maximum(j * Tk - (t0 + Tq - 1), t0 - ((j + 1) * Tk - 1)), 0)
        bound = 1.001 * qmax * kmax_ref[j] + 0.01 - c_dist * gap.astype(F32)

        @pl.when((j != jd) & (bound > floor))
        def _():
            chunk(j * Tk, False)

    acc = acc_scr[...]
    l = l_scr[...]
    o = acc[:Tq] / l[:Tq] - acc[Tq:] * (lam / l[Tq:])
    o = _rms(o, SUBLN_EPS) * g_ref[...] * (1.0 - lambda_init)
    o_ref[...] = o.astype(BF16)


def _attn_b(qkv, slopes, lam_vecs, subln_g, B, S, n_heads, head0, heads_b, lambda_init, Tq=256,
            Tk=1024):
    N = B * S
    nq = S // Tq
    Tk = min(Tk, S)
    return pl.pallas_call(
        functools.partial(_attn_b_kernel, S=S, Tq=Tq, Tk=Tk, lambda_init=lambda_init),
        grid=(B, heads_b, nq),
        in_specs=[pl.BlockSpec(memory_space=pltpu.SMEM),
                  pl.BlockSpec((4, HEAD_DIM // 2), lambda b, h, i: (0, 0)),
                  pl.BlockSpec((1, Tq, HEAD_DIM), lambda b, h, i: (head0 + h, b * nq + i, 0)),
                  pl.BlockSpec((1, S, HEAD_DIM), lambda b, h, i: (n_heads + head0 + h, b, 0)),
                  pl.BlockSpec((1, S, HEAD_DIM), lambda b, h, i: (2 * n_heads + head0 + h, b, 0)),
                  pl.BlockSpec((1, HEAD_DIM), lambda b, h, i: (0, 0))],
        out_specs=pl.BlockSpec((Tq, HEAD_DIM), lambda b, h, i: (b * nq + i, h)),
        out_shape=jax.ShapeDtypeStruct((N, heads_b * HEAD_DIM), BF16),
        scratch_shapes=[pltpu.VMEM((Tq, 2 * S - Tq), F32), pltpu.SMEM((S // Tk,), F32),
                        pltpu.VMEM((2 * Tq, 1), F32), pltpu.VMEM((2 * Tq, 1), F32),
                        pltpu.VMEM((2 * Tq, HEAD_DIM), F32)],
        compiler_params=_params(("arbitrary", "arbitrary", "arbitrary"), 48),
        name="attn_diff",
    )(slopes, lam_vecs, qkv, qkv, qkv, subln_g.reshape(1, HEAD_DIM))


def _attn_c_kernel(rpb_ref, q_ref, k_ref, v_ref, o_ref, tab_ref, *, R):
    h = pl.program_id(1)
    W = GRID_W
    kr = min(NA_ROWS, R)
    n_dr = 2 * NA_ROWS - 1
    n_dc = 2 * NA_COLS - 1

    c_io = lax.broadcasted_iota(I32, (W, 2 * W), 0)
    l_io = lax.broadcasted_iota(I32, (W, 2 * W), 1)
    cp = l_io & (W - 1)
    dcm = cp - c_io + (NA_COLS - 1)
    cstart = jnp.clip(c_io - NA_COLS // 2, 0, W - NA_COLS)
    ok = (cp >= cstart) & (cp < cstart + NA_COLS)
    blocks = []
    for dr in range(n_dr):
        blk = jnp.zeros((W, 2 * W), F32)
        for dc in range(n_dc):
            blk = jnp.where(dcm == dc, rpb_ref[(h * n_dr + dr) * n_dc + dc], blk)
        blocks.append(jnp.where(ok, blk, NEG_INF))
    for o in range(NA_ROWS):
        for jj in range(kr // 2):
            tab_ref[o, :, jj * 2 * W:(jj + 1) * 2 * W] = jnp.where(
                l_io < W, blocks[o + 2 * jj], blocks[o + 2 * jj + 1])

    def row(r, carry):
        rs = jnp.clip(r - kr // 2, 0, R - kr)
        q = q_ref[0, pl.ds(pl.multiple_of(r * W, W), W), :]
        kw = k_ref[0, pl.ds(pl.multiple_of(rs * W, W), kr * W), :]
        vw = v_ref[0, pl.ds(pl.multiple_of(rs * W, W), kr * W), :]
        s = lax.dot_general(q, kw, _NT, preferred_element_type=F32) * (HEAD_DIM ** -0.5)
        s = s + tab_ref[rs - r + (NA_ROWS - 1)]
        m = jnp.max(s, axis=-1, keepdims=True)
        p = jnp.exp(s - m)
        l = jnp.sum(p, axis=-1, keepdims=True)
        o = jnp.dot(p.astype(BF16), vw, preferred_element_type=F32) / l
        o_ref[pl.ds(pl.multiple_of(r * W, W), W), :] = o.astype(BF16)
        return carry

    lax.fori_loop(0, R, row, 0, unroll=16 if R % 16 == 0 else 1)


def _attn_c(qkv, rpb_flat, B, S, n_heads, head0, heads_c):
    N = B * S
    R = S // GRID_W
    kr = min(NA_ROWS, R)
    assert kr == NA_ROWS and kr % 2 == 0
    return pl.pallas_call(
        functools.partial(_attn_c_kernel, R=R),
        grid=(B, heads_c),
        in_specs=[pl.BlockSpec(memory_space=pltpu.SMEM),
                  pl.BlockSpec((1, S, HEAD_DIM), lambda b, h: (head0 + h, b, 0)),
                  pl.BlockSpec((1, S, HEAD_DIM), lambda b, h: (n_heads + head0 + h, b, 0)),
                  pl.BlockSpec((1, S, HEAD_DIM), lambda b, h: (2 * n_heads + head0 + h, b, 0))],
        out_specs=pl.BlockSpec((S, HEAD_DIM), lambda b, h: (b, h)),
        out_shape=jax.ShapeDtypeStruct((N, heads_c * HEAD_DIM), BF16),
        scratch_shapes=[pltpu.VMEM((NA_ROWS, GRID_W, kr * GRID_W), F32)],
        compiler_params=_params(("arbitrary", "arbitrary"), 32),
        name="attn_nbr",
    )(rpb_flat, qkv, qkv, qkv)


def _out_kernel(oa_ref, ob_ref, oc_ref, wa_ref, wb_ref, wc_ref, x_ref, g1_ref, ng_ref, sc_ref,
                sh_ref, rw_ref, x1_ref, h2_ref, lg_ref):
    acc = jnp.dot(oa_ref[...], wa_ref[...], preferred_element_type=F32)
    acc = acc + jnp.dot(ob_ref[...], wb_ref[...], preferred_element_type=F32)
    acc = acc + jnp.dot(oc_ref[...], wc_ref[...], preferred_element_type=F32)
    x1 = x_ref[...] + g1_ref[0] * acc
    x1_ref[...] = x1
    h2 = (_rms(x1, NORM_EPS) * ng_ref[...]) * (1.0 + sc_ref[0]) + sh_ref[0]
    _store_row_slabs(h2_ref, _pack_pairs(h2))
    lg_ref[...] = lax.dot_general(rw_ref[...], h2, _NT, preferred_element_type=F32,
                                  precision=HIGHEST)


def _out_proj(oa, ob, oc, w_bf, x2d, g1, ng, sc, sh, rw_t, S, tm=512):
    N, D = x2d.shape
    wa, wb, wc = oa.shape[1], ob.shape[1], oc.shape[1]
    E = rw_t.shape[0]
    tpb = S // tm
    const = dict(pipeline_mode=pl.Buffered(1))
    row = lambda i: (i, 0)
    per_b = lambda i: (i // tpb, 0, 0)
    return pl.pallas_call(
        _out_kernel,
        grid=(N // tm,),
        in_specs=[pl.BlockSpec((tm, wa), row), pl.BlockSpec((tm, wb), row), pl.BlockSpec((tm, wc), row),
                  pl.BlockSpec((wa, D), lambda i: (0, 0), **const),
                  pl.BlockSpec((wb, D), lambda i: (wa // wb, 0), **const),
                  pl.BlockSpec((wc, D), lambda i: ((wa + wb) // wc, 0), **const),
                  pl.BlockSpec((tm, D), row),
                  pl.BlockSpec((1, 1, D), per_b),
                  pl.BlockSpec((1, D), lambda i: (0, 0)),
                  pl.BlockSpec((1, 1, D), per_b),
                  pl.BlockSpec((1, 1, D), per_b),
                  pl.BlockSpec((E, D), lambda i: (0, 0), **const)],
        out_specs=[pl.BlockSpec((tm, D), row), pl.BlockSpec((tm * (D // 2 // LANES), LANES), row),
                   pl.BlockSpec((E, tm), lambda i: (0, i))],
        out_shape=[jax.ShapeDtypeStruct((N, D), F32),
                   jax.ShapeDtypeStruct((N * (D // 2 // LANES), LANES), U32),
                   jax.ShapeDtypeStruct((E, N), F32)],
        compiler_params=_params(("arbitrary",), 52),
        name="out_proj",
    )(oa, ob, oc, w_bf, w_bf, w_bf, x2d, g1, ng.reshape(1, D), sc, sh, rw_t)


def _first_argmax(vals, iota, big):
    mx = jnp.max(vals, axis=0, keepdims=True)
    idx = jnp.min(jnp.where(vals == mx, iota, big), axis=0, keepdims=True)
    return mx, idx


def _route_kernel(lg_ref, rb_ref, idx_ref, gate_ref, rank_ref, cnt_ref, tri_ref, carry_ref, *, Tt):
    i = pl.program_id(0)
    E = lg_ref.shape[0]
    gs = E // N_GROUPS

    @pl.when(i == 0)
    def _():
        r = lax.broadcasted_iota(I32, (Tt, Tt), 0)
        c = lax.broadcasted_iota(I32, (Tt, Tt), 1)
        tri_ref[...] = jnp.where(r < c, 1.0, 0.0).astype(BF16)
        carry_ref[...] = jnp.zeros_like(carry_ref)

    scores = jax.nn.sigmoid(lg_ref[...])
    sel = scores + rb_ref[...]
    e_io = lax.broadcasted_iota(I32, (E, Tt), 0).astype(F32)
    s_io = lax.broadcasted_iota(I32, (gs, Tt), 0).astype(F32)
    g_io = lax.broadcasted_iota(I32, (N_GROUPS, Tt), 0).astype(F32)

    grp = []
    for g in range(N_GROUPS):
        blk = sel[g * gs:(g + 1) * gs]
        m1, a1 = _first_argmax(blk, s_io, float(gs))
        m2 = jnp.max(jnp.where(s_io == a1, -jnp.inf, blk), axis=0, keepdims=True)
        grp.append(m1 + m2)
    grp = jnp.concatenate(grp, axis=0)

    gsel = jnp.zeros((N_GROUPS, Tt), F32)
    for _ in range(TOPK_GROUPS):
        _, gi = _first_argmax(grp, g_io, float(N_GROUPS))
        hit = g_io == gi
        gsel = jnp.where(hit, 1.0, gsel)
        grp = jnp.where(hit, -jnp.inf, grp)

    esel = jnp.concatenate(
        [jnp.broadcast_to(gsel[g:g + 1], (gs, Tt)) for g in range(N_GROUPS)], axis=0)
    cur = jnp.where(esel > 0.0, sel, NEG_INF)

    idxs, gates = [], []
    onehot = jnp.zeros((E, Tt), F32)
    for _ in range(TOP_K):
        _, ei = _first_argmax(cur, e_io, float(E))
        hit = e_io == ei
        idxs.append(ei)
        gates.append(jnp.sum(jnp.where(hit, scores, 0.0), axis=0, keepdims=True))
        onehot = jnp.where(hit, 1.0, onehot)
        cur = jnp.where(hit, -jnp.inf, cur)
    gate = jnp.concatenate(gates, axis=0)
    gate = gate / jnp.sum(gate, axis=0, keepdims=True) * ROUTED_SCALE
    idx_ref[...] = jnp.concatenate(idxs, axis=0).astype(I32)
    gate_ref[...] = gate

    oh = onehot.astype(BF16)
    before = jnp.dot(oh, tri_ref[...], preferred_element_type=F32)
    tile_cnt = jnp.dot(oh, jnp.ones((Tt, LANES), BF16), preferred_element_type=F32)
    carry = carry_ref[...]
    rank_mat = before + jnp.concatenate([carry] * (Tt // LANES), axis=1)
    ranks = [jnp.sum(jnp.where(e_io == ei, rank_mat, 0.0), axis=0, keepdims=True) for ei in idxs]
    rank_ref[...] = jnp.concatenate(ranks, axis=0).astype(I32)
    carry_ref[...] = carry + tile_cnt
    cnt_ref[...] = (carry + tile_cnt).astype(I32)


def _route(logits_t, router_bias, Tt=512):
    E, N = logits_t.shape
    tok = lambda i: (0, i)
    return pl.pallas_call(
        functools.partial(_route_kernel, Tt=Tt),
        grid=(N // Tt,),
        in_specs=[pl.BlockSpec((E, Tt), tok), pl.BlockSpec((E, 1), lambda i: (0, 0))],
        out_specs=[pl.BlockSpec((TOP_K, Tt), tok), pl.BlockSpec((TOP_K, Tt), tok),
                   pl.BlockSpec((TOP_K, Tt), tok), pl.BlockSpec((E, LANES), lambda i: (0, 0))],
        out_shape=[jax.ShapeDtypeStruct((TOP_K, N), I32), jax.ShapeDtypeStruct((TOP_K, N), F32),
                   jax.ShapeDtypeStruct((TOP_K, N), I32), jax.ShapeDtypeStruct((E, LANES), I32)],
        scratch_shapes=[pltpu.VMEM((Tt, Tt), BF16), pltpu.VMEM((E, LANES), F32)],
        compiler_params=_params(("arbitrary",), 32),
        name="route_topk",
    )(logits_t, router_bias.reshape(E, 1))


def _dest_kernel(pstart_ref, idx_ref, rank_ref, dest_ref, *, G):
    idx = idx_ref[...]
    base = jnp.zeros(idx.shape, I32)
    for e in range(pstart_ref.shape[0]):
        base = jnp.where(idx == e, pstart_ref[e], base)
    dest_ref[...] = (base + rank_ref[...]) * G


def _dest(pstart, idx, rank, G):
    return pl.pallas_call(
        functools.partial(_dest_kernel, G=G),
        in_specs=[pl.BlockSpec(memory_space=pltpu.SMEM), pl.BlockSpec(memory_space=pltpu.VMEM),
                  pl.BlockSpec(memory_space=pltpu.VMEM)],
        out_specs=pl.BlockSpec(memory_space=pltpu.VMEM),
        out_shape=jax.ShapeDtypeStruct(idx.shape, I32),
        name="route_dest",
    )(pstart, idx, rank)


def _dispatch_kernel(dest_hbm, h_ref, sg_ref, su_ref, sd_ref, xs_hbm, fs_ref, dest_smem, idx_sem,
                     row_sem, *, Td, G):
    i = pl.program_id(0)
    n = Td * TOP_K
    cp = pltpu.make_async_copy(dest_hbm.at[pl.ds(i * n, n)], dest_smem, idx_sem)
    cp.start()
    cp.wait()

    def row_copy(t, k):
        src = h_ref.at[pl.ds(pl.multiple_of(t * G, G), G)]
        dst = xs_hbm.at[pl.ds(pl.multiple_of(dest_smem[t * TOP_K + k], G), G)]
        return pltpu.make_async_copy(src, dst, row_sem)

    def issue(t, carry):
        for k in range(TOP_K):
            row_copy(t, k).start(priority=k % 2)
        return carry

    lax.fori_loop(0, Td, issue, 0)

    halves = [_unpack_pairs(_load_row_slab(h_ref, c, Td, G)) for c in range(G)]
    h = jnp.concatenate([p[0].astype(BF16) for p in halves] + [p[1].astype(BF16) for p in halves],
                        axis=1)
    g = jnp.dot(h, sg_ref[...], preferred_element_type=F32)
    u = jnp.dot(h, su_ref[...], preferred_element_type=F32)
    a = (g * jax.nn.sigmoid(g) * u).astype(BF16)
    fs_ref[...] = jnp.dot(a, sd_ref[...], preferred_element_type=F32).astype(BF16)

    def drain(t, carry):
        for k in range(TOP_K):
            row_copy(t, k).wait()
        return carry

    lax.fori_loop(0, Td, drain, 0)


def _dispatch(dest_flat, h2, sg, su, sd, P, G, Td=256):
    N = h2.shape[0] // G
    D, F = sg.shape
    const = dict(pipeline_mode=pl.Buffered(1))
    return pl.pallas_call(
        functools.partial(_dispatch_kernel, Td=Td, G=G),
        grid=(N // Td,),
        in_specs=[pl.BlockSpec(memory_space=pl.ANY), pl.BlockSpec((Td * G, LANES), lambda i: (i, 0)),
                  pl.BlockSpec((D, F), lambda i: (0, 0), **const),
                  pl.BlockSpec((D, F), lambda i: (0, 0), **const),
                  pl.BlockSpec((F, D), lambda i: (0, 0), **const)],
        out_specs=[pl.BlockSpec(memory_space=pl.ANY), pl.BlockSpec((Td, D), lambda i: (i, 0))],
        out_shape=[jax.ShapeDtypeStruct((P * G, LANES), h2.dtype),
                   jax.ShapeDtypeStruct((N, D), BF16)],
        scratch_shapes=[pltpu.SMEM((Td * TOP_K,), I32), pltpu.SemaphoreType.DMA,
                        pltpu.SemaphoreType.DMA],
        compiler_params=_params(("arbitrary",), 40),
        name="moe_dispatch",
    )(dest_flat, h2, sg, su, sd)


def _moe_kernel(be_ref, nv_ref, nb_ref, xs_ref, wg_ref, wu_ref, wd_ref, ys_ref, wg_s, wu_s, wd_s):
    b = pl.program_id(0)
    active = b < nb_ref[0]
    new_expert = (b == 0) | (be_ref[b] != be_ref[jnp.maximum(b - 1, 0)])

    @pl.when(active & new_expert)
    def _():
        wg_s[...] = wg_ref[0].astype(BF16)
        wu_s[...] = wu_ref[0].astype(BF16)
        wd_s[...] = wd_ref[0].astype(BF16)

    G = xs_ref.shape[0] // MOE_BLOCK
    T = MOE_BLOCK // 2

    def half_block(r0):
        xs_h = xs_ref.at[pl.ds(r0 * G, T * G)]
        valid = lax.broadcasted_iota(I32, (T, LANES), 0) + r0 < nv_ref[b]
        his, los = [], []
        for c in range(G):
            w = jnp.where(valid, _load_row_slab(xs_h, c, T, G), jnp.uint32(0))
            hi, lo = _unpack_pairs(w)
            his.append(hi.astype(BF16))
            los.append(lo.astype(BF16))
        x = jnp.concatenate(his + los, axis=1)
        g = jnp.dot(x, wg_s[...], preferred_element_type=F32)
        u = jnp.dot(x, wu_s[...], preferred_element_type=F32)
        a = (g * jax.nn.sigmoid(g) * u).astype(BF16)
        _store_row_slabs(ys_ref.at[pl.ds(r0 * G, T * G)],
                         _pack_pairs(jnp.dot(a, wd_s[...], preferred_element_type=F32)))

    @pl.when(active)
    def _():
        half_block(0)

    @pl.when(active & (nv_ref[b] > T))
    def _():
        half_block(T)


def _moe(block_expert, nvalid, nblocks, xs, wg, wu, wd):
    _, D, F = wg.shape
    G = D // 2 // LANES
    P = xs.shape[0] // G
    n_blocks = P // MOE_BLOCK
    blk = lambda b, be, nv, nb: (jnp.minimum(b, nb[0] - 1), 0)
    return pl.pallas_call(
        _moe_kernel,
        grid_spec=pltpu.PrefetchScalarGridSpec(
            num_scalar_prefetch=3,
            grid=(n_blocks,),
            in_specs=[pl.BlockSpec((MOE_BLOCK * G, LANES), blk),
                      pl.BlockSpec((1, D, F), lambda b, be, nv, nb: (be[b], 0, 0)),
                      pl.BlockSpec((1, D, F), lambda b, be, nv, nb: (be[b], 0, 0)),
                      pl.BlockSpec((1, F, D), lambda b, be, nv, nb: (be[b], 0, 0))],
            out_specs=pl.BlockSpec((MOE_BLOCK * G, LANES), blk),
            scratch_shapes=[pltpu.VMEM((D, F), BF16), pltpu.VMEM((D, F), BF16),
                            pltpu.VMEM((F, D), BF16)]),
        out_shape=jax.ShapeDtypeStruct((P * G, LANES), U32),
        compiler_params=_params(("arbitrary",), 52),
        name="moe_experts",
    )(block_expert, nvalid, nblocks, xs, wg, wu, wd)


def _combine_kernel(dest_hbm, ys_hbm, gate_ref, fs_ref, x_ref, g2_ref, fg_ref, o_ref, dest0, dest1,
                    ybuf0, ybuf1, idx_sem, row_sem, *, Tc, G, final_norm):
    j = pl.program_id(0)
    n = Tc * TOP_K
    slots = ((dest0, ybuf0), (dest1, ybuf1))

    def row_copy(slot, t, k):
        dest, ybuf = slots[slot]
        src = ys_hbm.at[pl.ds(pl.multiple_of(dest[t * TOP_K + k], G), G)]
        dst = ybuf.at[k, pl.ds(pl.multiple_of(t * G, G), G)]
        return pltpu.make_async_copy(src, dst, row_sem.at[slot])

    def request(tile, slot):
        cp = pltpu.make_async_copy(dest_hbm.at[pl.ds(tile * n, n)], slots[slot][0],
                                   idx_sem.at[slot])
        cp.start()
        cp.wait()

        def issue(t, carry):
            for k in range(TOP_K):
                row_copy(slot, t, k).start(priority=k % 2)
            return carry

        lax.fori_loop(0, Tc, issue, 0)

    def reduce(slot):
        def drain(t, carry):
            for k in range(TOP_K):
                row_copy(slot, t, k).wait()
            return carry

        lax.fori_loop(0, Tc, drain, 0)
        rows = slice(slot * Tc, (slot + 1) * Tc)
        gate = gate_ref[rows, :]
        f = fs_ref[rows, :].astype(F32)
        half = f.shape[1] // 2
        his, los = [], []
        for c in range(G):
            f_hi = f[:, c * LANES:(c + 1) * LANES]
            f_lo = f[:, half + c * LANES:half + (c + 1) * LANES]
            for k in range(TOP_K):
                y_hi, y_lo = _unpack_pairs(_load_row_slab(slots[slot][1].at[k], c, Tc, G))
                f_hi = f_hi + gate[:, k:k + 1] * y_hi
                f_lo = f_lo + gate[:, k:k + 1] * y_lo
            his.append(f_hi)
            los.append(f_lo)
        x2 = x_ref[rows, :] + g2_ref[0] * jnp.concatenate(his + los, axis=1)
        if final_norm:
            x2 = _rms(x2, NORM_EPS) * fg_ref[...]
        o_ref[rows, :] = x2

    @pl.when(j == 0)
    def _():
        request(0, 0)

    request(2 * j + 1, 1)
    reduce(0)

    @pl.when(j + 1 < pl.num_programs(0))
    def _():
        request(2 * j + 2, 0)

    reduce(1)


def _combine(dest_flat, ys, gate_t, fs, x1, g2, final_g, S, final_norm, Tc=256):
    N, D = x1.shape
    G = D // 2 // LANES
    tb = 2 * Tc
    tpb = S // tb
    row = lambda j: (j, 0)
    return pl.pallas_call(
        functools.partial(_combine_kernel, Tc=Tc, G=G, final_norm=final_norm),
        grid=(N // tb,),
        in_specs=[pl.BlockSpec(memory_space=pl.ANY), pl.BlockSpec(memory_space=pl.ANY),
                  pl.BlockSpec((tb, TOP_K), row),
                  pl.BlockSpec((tb, D), row), pl.BlockSpec((tb, D), row),
                  pl.BlockSpec((1, 1, D), lambda j: (j // tpb, 0, 0)),
                  pl.BlockSpec((1, D), lambda j: (0, 0))],
        out_specs=pl.BlockSpec((tb, D), row),
        out_shape=jax.ShapeDtypeStruct((N, D), F32),
        scratch_shapes=[pltpu.SMEM((Tc * TOP_K,), I32), pltpu.SMEM((Tc * TOP_K,), I32),
                        pltpu.VMEM((TOP_K, Tc * G, LANES), U32),
                        pltpu.VMEM((TOP_K, Tc * G, LANES), U32),
                        pltpu.SemaphoreType.DMA((2,)), pltpu.SemaphoreType.DMA((2,))],
        compiler_params=_params(("arbitrary",), 48),
        name="moe_combine",
    )(dest_flat, ys, gate_t, fs, x1, g2, final_g.reshape(1, D))


def _moe_ffn(h2, logits_t, x1, g2, router_bias, wg, wu, wd, e_off, sg, su, sd, final_g, S,
             final_norm):
    N, D = x1.shape
    G = D // 2 // LANES
    E = logits_t.shape[0]
    idx, gate, rank, cnt = _route(logits_t, router_bias)
    counts = cnt[:, 0]
    padded = (counts + MOE_BLOCK - 1) // MOE_BLOCK * MOE_BLOCK
    pend = jnp.cumsum(padded)
    pstart = (pend - padded).astype(I32)
    n_blocks = -(-(N * TOP_K + E * (MOE_BLOCK - 1)) // MOE_BLOCK)
    bstart = jnp.arange(n_blocks, dtype=I32) * MOE_BLOCK
    block_expert = jnp.minimum(jnp.sum(pend[None, :] <= bstart[:, None], axis=1), E - 1).astype(I32)
    nvalid = jnp.clip(pstart[block_expert] + counts[block_expert] - bstart, 0, MOE_BLOCK).astype(I32)
    nblocks = (pend[-1:] // MOE_BLOCK).astype(I32)

    dest = _dest(pstart, idx, rank, G)
    dest_flat = dest.T.reshape(N * TOP_K)
    xs, fs = _dispatch(dest_flat, h2, sg, su, sd, n_blocks * MOE_BLOCK, G)
    ys = _moe(block_expert + e_off, nvalid, nblocks, xs, wg, wu, wd)
    return _combine(dest_flat, ys, gate.T, fs, x1, g2, final_g, S, final_norm)


def kernel(x, c, ada_w, ada_b, norm_mix_g, norm_ffn_g, w_in, lambda_q1, lambda_k1, lambda_q2,
           lambda_k2, subln_g, rpb, w_out, router_w, router_bias, exp_w_gate, exp_w_up,
           exp_w_down, sh_w_gate, sh_w_up, sh_w_down, final_g):
    B, S, D = x.shape
    L = ada_w.shape[0]
    N = B * S
    n_heads = w_in.shape[2] // (3 * HEAD_DIM)
    heads_a = n_heads // HEADS_A_FRAC
    heads_b = n_heads // HEADS_B_FRAC
    heads_c = n_heads - heads_a - heads_b
    slopes = _alibi_slopes(heads_a + heads_b)
    slopes_a = jnp.asarray(slopes[:heads_a], F32)
    slopes_b = jnp.asarray(slopes[heads_a:], F32)

    E, F = exp_w_gate.shape[1], exp_w_gate.shape[3]
    wg_all = exp_w_gate.reshape(L * E, D, F)
    wu_all = exp_w_up.reshape(L * E, D, F)
    wd_all = exp_w_down.reshape(L * E, F, D)

    mod = _ada(c, ada_w, ada_b).reshape(L, B, 6, 1, D)
    xf = x.reshape(N, D)
    for l in range(L):
        sh1, sc1, g1, sh2, sc2, g2 = (mod[l, :, j] for j in range(6))
        qkv = _qkv(xf, norm_mix_g[l], sc1, sh1, w_in[l].astype(BF16), S)
        oa = _attn_a(qkv, slopes_a, B, S, n_heads, heads_a)
        lambda_init = 0.8 - 0.6 * math.exp(-0.3 * l)
        lam_vecs = jnp.stack([lambda_q1[l], lambda_k1[l], lambda_q2[l], lambda_k2[l]]).astype(F32)
        ob = _attn_b(qkv, slopes_b, lam_vecs, subln_g[l], B, S, n_heads, heads_a, heads_b, lambda_init)
        oc = _attn_c(qkv, rpb[l].reshape(-1), B, S, n_heads, heads_a + heads_b, heads_c)
        x1, h2, logits_t = _out_proj(oa, ob, oc, w_out[l].astype(BF16), xf, g1, norm_ffn_g[l],
                                     sc2, sh2, router_w[l].T, S)
        xf = _moe_ffn(h2, logits_t, x1, g2, router_bias[l], wg_all, wu_all, wd_all, l * E,
                      sh_w_gate[l].astype(BF16), sh_w_up[l].astype(BF16),
                      sh_w_down[l].astype(BF16), final_g, S, l == L - 1)
    return xf.reshape(B, S, D)
```

```python
import functools
import math

import jax
import jax.numpy as jnp
from jax import lax
from jax.experimental import pallas as pl
from jax.experimental.pallas import tpu as pltpu

F32 = jnp.float32
BF16 = jnp.bfloat16
I32 = jnp.int32
U32 = jnp.uint32
HIGHEST = lax.Precision.HIGHEST
_NT = (((1,), (1,)), ((), ()))

HEAD_DIM = 128
HEADS_A_FRAC, HEADS_B_FRAC = 2, 4
DILATED_PATTERNS = ((128, 1), (512, 4), (2048, 16))
GRID_W = 64
NA_ROWS = 8
NA_COLS = 16
N_GROUPS = 8
TOPK_GROUPS = 4
TOP_K = 8
ROUTED_SCALE = 2.5
NORM_EPS = 1e-6
SUBLN_EPS = 1e-5
NEG_INF = -1e30
LOG2E = math.log2(math.e)

LANES = 128
VMEM_BYTES_V7X = 64 << 20

MOE_BLOCK = 512


def _params(semantics, vmem_mib):
    return pltpu.CompilerParams(dimension_semantics=semantics,
                                vmem_limit_bytes=min(vmem_mib << 20, VMEM_BYTES_V7X - (4 << 20)))


def _alibi_slopes(n):
    def pow2(m):
        start = 2.0 ** (-8.0 / m)
        return [start ** (i + 1) for i in range(m)]
    p = 2 ** int(math.floor(math.log2(n)))
    return pow2(p) + pow2(2 * p)[0::2][: n - p]


def _rms(x, eps):
    return x * lax.rsqrt(jnp.mean(x * x, axis=-1, keepdims=True) + eps)


def _pack_pairs(x):
    half = x.shape[1] // 2
    hi = lax.bitcast_convert_type(x[:, :half].astype(BF16).astype(F32), U32)
    lo = lax.bitcast_convert_type(x[:, half:].astype(BF16).astype(F32), U32)
    return hi | (lo >> 16)


def _unpack_pairs(w):
    hi = lax.bitcast_convert_type(w & jnp.uint32(0xFFFF0000), F32)
    lo = lax.bitcast_convert_type(w << 16, F32)
    return hi, lo


def _store_row_slabs(ref, packed):
    T, W = packed.shape
    for c in range(W // LANES):
        ref[pl.ds(c, T, stride=W // LANES), :] = packed[:, c * LANES:(c + 1) * LANES]


def _load_row_slab(ref, c, T, G):
    return ref[pl.ds(c, T, stride=G), :]


def _ada_kernel(c_ref, w_ref, b_ref, o_ref):
    c = c_ref[...]
    sc = c * jax.nn.sigmoid(c)
    o_ref[0] = jnp.dot(sc, w_ref[0], preferred_element_type=F32, precision=HIGHEST) + b_ref[0]


def _ada(c, ada_w, ada_b):
    B, D = c.shape
    L, _, D6 = ada_w.shape
    rows = 8
    cp = jnp.zeros((rows, D), F32).at[:B].set(c)
    tn = 1024
    out = pl.pallas_call(
        _ada_kernel,
        grid=(L, D6 // tn),
        in_specs=[pl.BlockSpec((rows, D), lambda l, j: (0, 0)),
                  pl.BlockSpec((1, D, tn), lambda l, j: (l, 0, j)),
                  pl.BlockSpec((1, 1, tn), lambda l, j: (l, 0, j))],
        out_specs=pl.BlockSpec((1, rows, tn), lambda l, j: (l, 0, j)),
        out_shape=jax.ShapeDtypeStruct((L, rows, D6), F32),
        compiler_params=_params(("arbitrary", "arbitrary"), 32),
        name="ada_mod",
    )(cp, ada_w, ada_b.reshape(L, 1, D6))
    return out[:, :B]


def _qkv_kernel(x_ref, g_ref, sc_ref, sh_ref, w_ref, o_ref, h_scr, *, heads_per_step):
    @pl.when(pl.program_id(1) == 0)
    def _():
        y = _rms(x_ref[...], NORM_EPS) * g_ref[...]
        h_scr[...] = (y * (1.0 + sc_ref[0]) + sh_ref[0]).astype(BF16)

    res = jnp.dot(h_scr[...], w_ref[...], preferred_element_type=F32)
    for hh in range(heads_per_step):
        o_ref[hh] = res[:, hh * HEAD_DIM:(hh + 1) * HEAD_DIM].astype(BF16)


def _qkv(x2d, g, sc, sh, w_bf, S, tm=512, tn=1536):
    N, D = x2d.shape
    W3 = w_bf.shape[1]
    hps = tn // HEAD_DIM
    tpb = S // tm
    return pl.pallas_call(
        functools.partial(_qkv_kernel, heads_per_step=hps),
        grid=(N // tm, W3 // tn),
        in_specs=[pl.BlockSpec((tm, D), lambda i, j: (i, 0)),
                  pl.BlockSpec((1, D), lambda i, j: (0, 0)),
                  pl.BlockSpec((1, 1, D), lambda i, j: (i // tpb, 0, 0)),
                  pl.BlockSpec((1, 1, D), lambda i, j: (i // tpb, 0, 0)),
                  pl.BlockSpec((D, tn), lambda i, j: (0, j))],
        out_specs=pl.BlockSpec((hps, tm, HEAD_DIM), lambda i, j: (j, i, 0)),
        out_shape=jax.ShapeDtypeStruct((W3 // HEAD_DIM, N, HEAD_DIM), BF16),
        scratch_shapes=[pltpu.VMEM((tm, D), BF16)],
        compiler_params=_params(("arbitrary", "arbitrary"), 48),
        name="qkv_proj",
    )(x2d, g.reshape(1, D), sc, sh, w_bf)


def _col_to_row(col, eye):
    return jnp.sum(jnp.where(eye, col, 0.0), axis=0, keepdims=True)


def _attn_a_kernel(slopes_ref, q_ref, k_ref, v_ref, o_ref, lse_ref, nat, cq, ck, cv, onat, *,
                   L, dil, half, Tq):
    h = pl.program_id(1)
    wk = 2 * Tq
    n_tiles = L // Tq
    c_dist = LOG2E * slopes_ref[h] * dil
    a = lax.broadcasted_iota(I32, (Tq, wk), 0)
    u = lax.broadcasted_iota(I32, (Tq, wk), 1)
    eye = lax.broadcasted_iota(I32, (Tq, Tq), 0) == lax.broadcasted_iota(I32, (Tq, Tq), 1)

    def table(off):
        d = u - a - off
        ad = jnp.maximum(d, -d)
        return jnp.where(ad <= half, -c_dist * ad.astype(F32), NEG_INF)

    tabs = {off: table(off) for off in sorted({0, half, wk - Tq})}

    for src, dst, scale in ((q_ref, cq, LOG2E * HEAD_DIM ** -0.5), (k_ref, ck, None), (v_ref, cv, None)):
        x = src[0].astype(F32)
        nat[...] = x if scale is None else x * scale
        for r in range(dil):
            dst[r * L:(r + 1) * L, :] = nat[pl.ds(r, L, stride=dil), :].astype(BF16)

    for r in range(dil):
        for j in range(n_tiles):
            start = min(max(j * Tq - half, 0), L - wk)
            q = cq[r * L + j * Tq:r * L + (j + 1) * Tq, :]
            kw = ck[r * L + start:r * L + start + wk, :]
            vw = cv[r * L + start:r * L + start + wk, :]
            s = lax.dot_general(q, kw, _NT, preferred_element_type=F32) + tabs[j * Tq - start]
            m = jnp.max(s, axis=-1, keepdims=True)
            p = jnp.exp2(s - m)
            l = jnp.sum(p, axis=-1, keepdims=True)
            o = jnp.dot(p.astype(BF16), vw, preferred_element_type=F32) / l
            onat[pl.ds(r + j * Tq * dil, Tq, stride=dil), :] = o
            lse_ref[0, 0, r:r + 1, j * Tq:(j + 1) * Tq] = _col_to_row(m + jnp.log2(l), eye)
    o_ref[0] = onat[...].astype(BF16)


def _attn_a_pattern(qkv, slopes, B, S, n_heads, heads_a, window, dil, Tq=128):
    N = B * S
    L = S // dil
    half = window // (2 * dil)
    assert L % Tq == 0 and L >= 2 * Tq and 2 * half <= Tq
    blk = (1, S, HEAD_DIM)
    o, lse = pl.pallas_call(
        functools.partial(_attn_a_kernel, L=L, dil=dil, half=half, Tq=Tq),
        grid=(B, heads_a),
        in_specs=[pl.BlockSpec(memory_space=pltpu.SMEM),
                  pl.BlockSpec(blk, lambda b, h: (h, b, 0)),
                  pl.BlockSpec(blk, lambda b, h: (n_heads + h, b, 0)),
                  pl.BlockSpec(blk, lambda b, h: (2 * n_heads + h, b, 0))],
        out_specs=[pl.BlockSpec(blk, lambda b, h: (h, b, 0)),
                   pl.BlockSpec((1, 1, dil, L), lambda b, h: (b, h, 0, 0))],
        out_shape=[jax.ShapeDtypeStruct((heads_a, N, HEAD_DIM), BF16),
                   jax.ShapeDtypeStruct((B, heads_a, dil, L), F32)],
        scratch_shapes=[pltpu.VMEM((S, HEAD_DIM), F32)] + [pltpu.VMEM((S, HEAD_DIM), BF16)] * 3
        + [pltpu.VMEM((S, HEAD_DIM), F32)],
        compiler_params=_params(("arbitrary", "arbitrary"), 32),
        name=f"attn_dilated_d{dil}",
    )(slopes, qkv, qkv, qkv)
    lse_nat = lse.transpose(1, 0, 3, 2).reshape(heads_a, N)
    return o, lse_nat


def _mix_a_kernel(*refs, n_pat, tm):
    o_refs, lse_refs, out_ref = refs[:n_pat], refs[n_pat:2 * n_pat], refs[2 * n_pat]
    heads = lse_refs[0].shape[0]
    lses = [r[...] for r in lse_refs]
    mx = functools.reduce(jnp.maximum, lses)
    es = [jnp.exp2(x - mx) for x in lses]
    den = functools.reduce(lambda x, y: x + y, es)
    ws = [e / den for e in es]
    pad = jnp.zeros((LANES - n_pat * heads, LANES), F32)
    for c in range(tm // LANES):
        cols = slice(c * LANES, (c + 1) * LANES)
        w_rows = jnp.concatenate([w[:, cols] for w in ws] + [pad], axis=0)
        w_cols = w_rows.T
        for hh in range(heads):
            acc = jnp.zeros((LANES, HEAD_DIM), F32)
            for p in range(n_pat):
                wc = w_cols[:, p * heads + hh:p * heads + hh + 1]
                acc = acc + wc * o_refs[p][hh, cols, :].astype(F32)
            out_ref[cols, hh * HEAD_DIM:(hh + 1) * HEAD_DIM] = acc.astype(BF16)


def _mix_a(os, lses, tm=512):
    heads, N, _ = os[0].shape
    n_pat = len(os)
    assert n_pat * heads <= LANES
    return pl.pallas_call(
        functools.partial(_mix_a_kernel, n_pat=n_pat, tm=tm),
        grid=(N // tm,),
        in_specs=[pl.BlockSpec((heads, tm, HEAD_DIM), lambda i: (0, i, 0))] * n_pat
        + [pl.BlockSpec((heads, tm), lambda i: (0, i))] * n_pat,
        out_specs=pl.BlockSpec((tm, heads * HEAD_DIM), lambda i: (i, 0)),
        out_shape=jax.ShapeDtypeStruct((N, heads * HEAD_DIM), BF16),
        compiler_params=_params(("arbitrary",), 32),
        name="attn_dilated_mix",
    )(*os, *lses)


def _attn_a(qkv, slopes, B, S, n_heads, heads_a):
    parts = [_attn_a_pattern(qkv, slopes, B, S, n_heads, heads_a, w, d) for w, d in DILATED_PATTERNS]
    return _mix_a([p[0] for p in parts], [p[1] for p in parts])


def _attn_b_kernel(slopes_ref, lam_ref, q_ref, k_ref, v_ref, g_ref, o_ref, tab_ref, kmax_ref,
                   m_scr, l_scr, acc_scr, *, S, Tq, Tk, lambda_init):
    h = pl.program_id(1)
    i = pl.program_id(2)
    dh = HEAD_DIM // 2
    n_chunks = S // Tk
    c_dist = LOG2E * slopes_ref[h]

    def sub_norm(x):
        sq = x * x
        return jnp.sqrt(jnp.maximum(jnp.sum(sq[:, :dh], axis=-1, keepdims=True),
                                    jnp.sum(sq[:, dh:], axis=-1, keepdims=True)))

    @pl.when(i == 0)
    def _():
        shape = (Tq, 2 * S - Tq)
        a = lax.broadcasted_iota(I32, shape, 0)
        u = lax.broadcasted_iota(I32, shape, 1)
        d = a - u + (S - Tq)
        tab_ref[...] = -c_dist * jnp.maximum(d, -d).astype(F32)
        kn = sub_norm(k_ref[0].astype(F32))
        for j in range(n_chunks):
            kmax_ref[j] = jnp.max(kn[j * Tk:(j + 1) * Tk])

    lv = lam_ref[...]
    lam = (jnp.exp(jnp.sum(lv[0:1] * lv[1:2], axis=-1, keepdims=True))
           - jnp.exp(jnp.sum(lv[2:3] * lv[3:4], axis=-1, keepdims=True)) + lambda_init)

    t0 = i * Tq
    qf = q_ref[0].astype(F32) * (LOG2E * dh ** -0.5)
    q = qf.astype(BF16)
    qmax = jnp.max(sub_norm(q.astype(F32)))

    def chunk(kstart, first):
        kc = k_ref[0, pl.ds(kstart, Tk), :]
        vc = v_ref[0, pl.ds(kstart, Tk), :]
        bias = tab_ref[:, pl.ds(pl.multiple_of(S - Tq - t0 + kstart, LANES), Tk)]
        s = jnp.concatenate(
            [lax.dot_general(q[:, c * dh:(c + 1) * dh], kc[:, c * dh:(c + 1) * dh], _NT,
                             preferred_element_type=F32) + bias for c in range(2)], axis=0)
        mc = jnp.max(s, axis=-1, keepdims=True)
        if first:
            m_new = mc
        else:
            m_old = m_scr[...]
            m_new = jnp.maximum(m_old, mc)
            alpha = jnp.exp2(m_old - m_new)
        p = jnp.exp2(s - m_new)
        ls = jnp.sum(p, axis=-1, keepdims=True)
        pv = jnp.dot(p.astype(BF16), vc, preferred_element_type=F32)
        m_scr[...] = m_new
        l_scr[...] = ls if first else alpha * l_scr[...] + ls
        acc_scr[...] = pv if first else alpha * acc_scr[...] + pv
        return m_new

    jd = t0 // Tk
    m_diag = chunk(pl.multiple_of(jd * Tk, Tk), True)
    floor = jnp.min(m_diag) - 160.0
    for j in range(n_chunks):
        gap = jnp.maximum(jnp.maximum(j * Tk - (t0 + Tq - 1), t0 - ((j + 1) * Tk - 1)), 0)
        bound = 1.001 * qmax * kmax_ref[j] + 0.01 - c_dist * gap.astype(F32)

        @pl.when((j != jd) & (bound > floor))
        def _():
            chunk(j * Tk, False)

    acc = acc_scr[...]
    l = l_scr[...]
    o = acc[:Tq] / l[:Tq] - acc[Tq:] * (lam / l[Tq:])
    o = _rms(o, SUBLN_EPS) * g_ref[...] * (1.0 - lambda_init)
    o_ref[...] = o.astype(BF16)


def _attn_b(qkv, slopes, lam_vecs, subln_g, B, S, n_heads, head0, heads_b, lambda_init, Tq=256,
            Tk=1024):
    N = B * S
    nq = S // Tq
    Tk = min(Tk, S)
    return pl.pallas_call(
        functools.partial(_attn_b_kernel, S=S, Tq=Tq, Tk=Tk, lambda_init=lambda_init),
        grid=(B, heads_b, nq),
        in_specs=[pl.BlockSpec(memory_space=pltpu.SMEM),
                  pl.BlockSpec((4, HEAD_DIM // 2), lambda b, h, i: (0, 0)),
                  pl.BlockSpec((1, Tq, HEAD_DIM), lambda b, h, i: (head0 + h, b * nq + i, 0)),
                  pl.BlockSpec((1, S, HEAD_DIM), lambda b, h, i: (n_heads + head0 + h, b, 0)),
                  pl.BlockSpec((1, S, HEAD_DIM), lambda b, h, i: (2 * n_heads + head0 + h, b, 0)),
                  pl.BlockSpec((1, HEAD_DIM), lambda b, h, i: (0, 0))],
        out_specs=pl.BlockSpec((Tq, HEAD_DIM), lambda b, h, i: (b * nq + i, h)),
        out_shape=jax.ShapeDtypeStruct((N, heads_b * HEAD_DIM), BF16),
        scratch_shapes=[pltpu.VMEM((Tq, 2 * S - Tq), F32), pltpu.SMEM((S // Tk,), F32),
                        pltpu.VMEM((2 * Tq, 1), F32), pltpu.VMEM((2 * Tq, 1), F32),
                        pltpu.VMEM((2 * Tq, HEAD_DIM), F32)],
        compiler_params=_params(("arbitrary", "arbitrary", "arbitrary"), 48),
        name="attn_diff",
    )(slopes, lam_vecs, qkv, qkv, qkv, subln_g.reshape(1, HEAD_DIM))


def _attn_c_kernel(rpb_ref, q_ref, k_ref, v_ref, o_ref, tab_ref, *, R):
    h = pl.program_id(1)
    W = GRID_W
    kr = min(NA_ROWS, R)
    n_dr = 2 * NA_ROWS - 1
    n_dc = 2 * NA_COLS - 1

    c_io = lax.broadcasted_iota(I32, (W, 2 * W), 0)
    l_io = lax.broadcasted_iota(I32, (W, 2 * W), 1)
    cp = l_io & (W - 1)
    dcm = cp - c_io + (NA_COLS - 1)
    cstart = jnp.clip(c_io - NA_COLS // 2, 0, W - NA_COLS)
    ok = (cp >= cstart) & (cp < cstart + NA_COLS)
    blocks = []
    for dr in range(n_dr):
        blk = jnp.zeros((W, 2 * W), F32)
        for dc in range(n_dc):
            blk = jnp.where(dcm == dc, rpb_ref[(h * n_dr + dr) * n_dc + dc], blk)
        blocks.append(jnp.where(ok, blk, NEG_INF))
    for o in range(NA_ROWS):
        for jj in range(kr // 2):
            tab_ref[o, :, jj * 2 * W:(jj + 1) * 2 * W] = jnp.where(
                l_io < W, blocks[o + 2 * jj], blocks[o + 2 * jj + 1])

    def row(r, carry):
        rs = jnp.clip(r - kr // 2, 0, R - kr)
        q = q_ref[0, pl.ds(pl.multiple_of(r * W, W), W), :]
        kw = k_ref[0, pl.ds(pl.multiple_of(rs * W, W), kr * W), :]
        vw = v_ref[0, pl.ds(pl.multiple_of(rs * W, W), kr * W), :]
        s = lax.dot_general(q, kw, _NT, preferred_element_type=F32) * (HEAD_DIM ** -0.5)
        s = s + tab_ref[rs - r + (NA_ROWS - 1)]
        m = jnp.max(s, axis=-1, keepdims=True)
        p = jnp.exp(s - m)
        l = jnp.sum(p, axis=-1, keepdims=True)
        o = jnp.dot(p.astype(BF16), vw, preferred_element_type=F32) / l
        o_ref[pl.ds(pl.multiple_of(r * W, W), W), :] = o.astype(BF16)
        return carry

    lax.fori_loop(0, R, row, 0, unroll=16 if R % 16 == 0 else 1)


def _attn_c(qkv, rpb_flat, B, S, n_heads, head0, heads_c):
    N = B * S
    R = S // GRID_W
    kr = min(NA_ROWS, R)
    assert kr == NA_ROWS and kr % 2 == 0
    return pl.pallas_call(
        functools.partial(_attn_c_kernel, R=R),
        grid=(B, heads_c),
        in_specs=[pl.BlockSpec(memory_space=pltpu.SMEM),
                  pl.BlockSpec((1, S, HEAD_DIM), lambda b, h: (head0 + h, b, 0)),
                  pl.BlockSpec((1, S, HEAD_DIM), lambda b, h: (n_heads + head0 + h, b, 0)),
                  pl.BlockSpec((1, S, HEAD_DIM), lambda b, h: (2 * n_heads + head0 + h, b, 0))],
        out_specs=pl.BlockSpec((S, HEAD_DIM), lambda b, h: (b, h)),
        out_shape=jax.ShapeDtypeStruct((N, heads_c * HEAD_DIM), BF16),
        scratch_shapes=[pltpu.VMEM((NA_ROWS, GRID_W, kr * GRID_W), F32)],
        compiler_params=_params(("arbitrary", "arbitrary"), 32),
        name="attn_nbr",
    )(rpb_flat, qkv, qkv, qkv)


def _out_kernel(oa_ref, ob_ref, oc_ref, wa_ref, wb_ref, wc_ref, x_ref, g1_ref, ng_ref, sc_ref,
                sh_ref, rw_ref, x1_ref, h2_ref, lg_ref):
    acc = jnp.dot(oa_ref[...], wa_ref[...], preferred_element_type=F32)
    acc = acc + jnp.dot(ob_ref[...], wb_ref[...], preferred_element_type=F32)
    acc = acc + jnp.dot(oc_ref[...], wc_ref[...], preferred_element_type=F32)
    x1 = x_ref[...] + g1_ref[0] * acc
    x1_ref[...] = x1
    h2 = (_rms(x1, NORM_EPS) * ng_ref[...]) * (1.0 + sc_ref[0]) + sh_ref[0]
    _store_row_slabs(h2_ref, _pack_pairs(h2))
    lg_ref[...] = lax.dot_general(rw_ref[...], h2, _NT, preferred_element_type=F32,
                                  precision=HIGHEST)


def _out_proj(oa, ob, oc, w_bf, x2d, g1, ng, sc, sh, rw_t, S, tm=512):
    N, D = x2d.shape
    wa, wb, wc = oa.shape[1], ob.shape[1], oc.shape[1]
    E = rw_t.shape[0]
    tpb = S // tm
    const = dict(pipeline_mode=pl.Buffered(1))
    row = lambda i: (i, 0)
    per_b = lambda i: (i // tpb, 0, 0)
    return pl.pallas_call(
        _out_kernel,
        grid=(N // tm,),
        in_specs=[pl.BlockSpec((tm, wa), row), pl.BlockSpec((tm, wb), row), pl.BlockSpec((tm, wc), row),
                  pl.BlockSpec((wa, D), lambda i: (0, 0), **const),
                  pl.BlockSpec((wb, D), lambda i: (wa // wb, 0), **const),
                  pl.BlockSpec((wc, D), lambda i: ((wa + wb) // wc, 0), **const),
                  pl.BlockSpec((tm, D), row),
                  pl.BlockSpec((1, 1, D), per_b),
                  pl.BlockSpec((1, D), lambda i: (0, 0)),
                  pl.BlockSpec((1, 1, D), per_b),
                  pl.BlockSpec((1, 1, D), per_b),
                  pl.BlockSpec((E, D), lambda i: (0, 0), **const)],
        out_specs=[pl.BlockSpec((tm, D), row), pl.BlockSpec((tm * (D // 2 // LANES), LANES), row),
                   pl.BlockSpec((E, tm), lambda i: (0, i))],
        out_shape=[jax.ShapeDtypeStruct((N, D), F32),
                   jax.ShapeDtypeStruct((N * (D // 2 // LANES), LANES), U32),
                   jax.ShapeDtypeStruct((E, N), F32)],
        compiler_params=_params(("arbitrary",), 52),
        name="out_proj",
    )(oa, ob, oc, w_bf, w_bf, w_bf, x2d, g1, ng.reshape(1, D), sc, sh, rw_t)


def _first_argmax(vals, iota, big):
    mx = jnp.max(vals, axis=0, keepdims=True)
    idx = jnp.min(jnp.where(vals == mx, iota, big), axis=0, keepdims=True)
    return mx, idx


def _route_kernel(lg_ref, rb_ref, idx_ref, gate_ref, rank_ref, cnt_ref, tri_ref, carry_ref, *, Tt):
    i = pl.program_id(0)
    E = lg_ref.shape[0]
    gs = E // N_GROUPS

    @pl.when(i == 0)
    def _():
        r = lax.broadcasted_iota(I32, (Tt, Tt), 0)
        c = lax.broadcasted_iota(I32, (Tt, Tt), 1)
        tri_ref[...] = jnp.where(r < c, 1.0, 0.0).astype(BF16)
        carry_ref[...] = jnp.zeros_like(carry_ref)

    scores = jax.nn.sigmoid(lg_ref[...])
    sel = scores + rb_ref[...]
    e_io = lax.broadcasted_iota(I32, (E, Tt), 0).astype(F32)
    s_io = lax.broadcasted_iota(I32, (gs, Tt), 0).astype(F32)
    g_io = lax.broadcasted_iota(I32, (N_GROUPS, Tt), 0).astype(F32)

    grp = []
    for g in range(N_GROUPS):
        blk = sel[g * gs:(g + 1) * gs]
        m1, a1 = _first_argmax(blk, s_io, float(gs))
        m2 = jnp.max(jnp.where(s_io == a1, -jnp.inf, blk), axis=0, keepdims=True)
        grp.append(m1 + m2)
    grp = jnp.concatenate(grp, axis=0)

    gsel = jnp.zeros((N_GROUPS, Tt), F32)
    for _ in range(TOPK_GROUPS):
        _, gi = _first_argmax(grp, g_io, float(N_GROUPS))
        hit = g_io == gi
        gsel = jnp.where(hit, 1.0, gsel)
        grp = jnp.where(hit, -jnp.inf, grp)

    esel = jnp.concatenate(
        [jnp.broadcast_to(gsel[g:g + 1], (gs, Tt)) for g in range(N_GROUPS)], axis=0)
    cur = jnp.where(esel > 0.0, sel, NEG_INF)

    idxs, gates = [], []
    onehot = jnp.zeros((E, Tt), F32)
    for _ in range(TOP_K):
        _, ei = _first_argmax(cur, e_io, float(E))
        hit = e_io == ei
        idxs.append(ei)
        gates.append(jnp.sum(jnp.where(hit, scores, 0.0), axis=0, keepdims=True))
        onehot = jnp.where(hit, 1.0, onehot)
        cur = jnp.where(hit, -jnp.inf, cur)
    gate = jnp.concatenate(gates, axis=0)
    gate = gate / jnp.sum(gate, axis=0, keepdims=True) * ROUTED_SCALE
    idx_ref[...] = jnp.concatenate(idxs, axis=0).astype(I32)
    gate_ref[...] = gate

    oh = onehot.astype(BF16)
    before = jnp.dot(oh, tri_ref[...], preferred_element_type=F32)
    tile_cnt = jnp.dot(oh, jnp.ones((Tt, LANES), BF16), preferred_element_type=F32)
    carry = carry_ref[...]
    rank_mat = before + jnp.concatenate([carry] * (Tt // LANES), axis=1)
    ranks = [jnp.sum(jnp.where(e_io == ei, rank_mat, 0.0), axis=0, keepdims=True) for ei in idxs]
    rank_ref[...] = jnp.concatenate(ranks, axis=0).astype(I32)
    carry_ref[...] = carry + tile_cnt
    cnt_ref[...] = (carry + tile_cnt).astype(I32)


def _route(logits_t, router_bias, Tt=512):
    E, N = logits_t.shape
    tok = lambda i: (0, i)
    return pl.pallas_call(
        functools.partial(_route_kernel, Tt=Tt),
        grid=(N // Tt,),
        in_specs=[pl.BlockSpec((E, Tt), tok), pl.BlockSpec((E, 1), lambda i: (0, 0))],
        out_specs=[pl.BlockSpec((TOP_K, Tt), tok), pl.BlockSpec((TOP_K, Tt), tok),
                   pl.BlockSpec((TOP_K, Tt), tok), pl.BlockSpec((E, LANES), lambda i: (0, 0))],
        out_shape=[jax.ShapeDtypeStruct((TOP_K, N), I32), jax.ShapeDtypeStruct((TOP_K, N), F32),
                   jax.ShapeDtypeStruct((TOP_K, N), I32), jax.ShapeDtypeStruct((E, LANES), I32)],
        scratch_shapes=[pltpu.VMEM((Tt, Tt), BF16), pltpu.VMEM((E, LANES), F32)],
        compiler_params=_params(("arbitrary",), 32),
        name="route_topk",
    )(logits_t, router_bias.reshape(E, 1))


def _dest_kernel(pstart_ref, idx_ref, rank_ref, dest_ref, *, G):
    idx = idx_ref[...]
    base = jnp.zeros(idx.shape, I32)
    for e in range(pstart_ref.shape[0]):
        base = jnp.where(idx == e, pstart_ref[e], base)
    dest_ref[...] = (base + rank_ref[...]) * G


def _dest(pstart, idx, rank, G):
    return pl.pallas_call(
        functools.partial(_dest_kernel, G=G),
        in_specs=[pl.BlockSpec(memory_space=pltpu.SMEM), pl.BlockSpec(memory_space=pltpu.VMEM),
                  pl.BlockSpec(memory_space=pltpu.VMEM)],
        out_specs=pl.BlockSpec(memory_space=pltpu.VMEM),
        out_shape=jax.ShapeDtypeStruct(idx.shape, I32),
        name="route_dest",
    )(pstart, idx, rank)


def _dispatch_kernel(dest_hbm, h_ref, xs_hbm, dest_smem, idx_sem, row_sem, *, Td, G):
    i = pl.program_id(0)
    n = Td * TOP_K
    cp = pltpu.make_async_copy(dest_hbm.at[pl.ds(i * n, n)], dest_smem, idx_sem)
    cp.start()
    cp.wait()

    def row_copy(t, k):
        src = h_ref.at[pl.ds(pl.multiple_of(t * G, G), G)]
        dst = xs_hbm.at[pl.ds(pl.multiple_of(dest_smem[t * TOP_K + k], G), G)]
        return pltpu.make_async_copy(src, dst, row_sem)

    def issue(t, carry):
        for k in range(TOP_K):
            row_copy(t, k).start(priority=k % 2)
        return carry

    lax.fori_loop(0, Td, issue, 0)

    def drain(t, carry):
        for k in range(TOP_K):
            row_copy(t, k).wait()
        return carry

    lax.fori_loop(0, Td, drain, 0)


def _dispatch(dest_flat, h2, P, G, Td=256):
    N = h2.shape[0] // G
    return pl.pallas_call(
        functools.partial(_dispatch_kernel, Td=Td, G=G),
        grid=(N // Td,),
        in_specs=[pl.BlockSpec(memory_space=pl.ANY), pl.BlockSpec((Td * G, LANES), lambda i: (i, 0))],
        out_specs=pl.BlockSpec(memory_space=pl.ANY),
        out_shape=jax.ShapeDtypeStruct((P * G, LANES), h2.dtype),
        scratch_shapes=[pltpu.SMEM((Td * TOP_K,), I32), pltpu.SemaphoreType.DMA,
                        pltpu.SemaphoreType.DMA],
        compiler_params=_params(("arbitrary",), 32),
        name="moe_dispatch",
    )(dest_flat, h2)


def _moe_kernel(be_ref, nv_ref, nb_ref, xs_ref, wg_ref, wu_ref, wd_ref, ys_ref, wg_s, wu_s, wd_s):
    b = pl.program_id(0)
    active = b < nb_ref[0]
    new_expert = (b == 0) | (be_ref[b] != be_ref[jnp.maximum(b - 1, 0)])

    @pl.when(active & new_expert)
    def _():
        wg_s[...] = wg_ref[0].astype(BF16)
        wu_s[...] = wu_ref[0].astype(BF16)
        wd_s[...] = wd_ref[0].astype(BF16)

    @pl.when(active)
    def _():
        T = MOE_BLOCK
        G = xs_ref.shape[0] // T
        valid = lax.broadcasted_iota(I32, (T, LANES), 0) < nv_ref[b]
        his, los = [], []
        for c in range(G):
            w = jnp.where(valid, _load_row_slab(xs_ref, c, T, G), jnp.uint32(0))
            hi, lo = _unpack_pairs(w)
            his.append(hi.astype(BF16))
            los.append(lo.astype(BF16))
        x = jnp.concatenate(his + los, axis=1)
        g = jnp.dot(x, wg_s[...], preferred_element_type=F32)
        u = jnp.dot(x, wu_s[...], preferred_element_type=F32)
        a = (g * jax.nn.sigmoid(g) * u).astype(BF16)
        _store_row_slabs(ys_ref, _pack_pairs(jnp.dot(a, wd_s[...], preferred_element_type=F32)))


def _moe(block_expert, nvalid, nblocks, xs, wg, wu, wd):
    _, D, F = wg.shape
    G = D // 2 // LANES
    P = xs.shape[0] // G
    n_blocks = P // MOE_BLOCK
    blk = lambda b, be, nv, nb: (jnp.minimum(b, nb[0] - 1), 0)
    return pl.pallas_call(
        _moe_kernel,
        grid_spec=pltpu.PrefetchScalarGridSpec(
            num_scalar_prefetch=3,
            grid=(n_blocks,),
            in_specs=[pl.BlockSpec((MOE_BLOCK * G, LANES), blk),
                      pl.BlockSpec((1, D, F), lambda b, be, nv, nb: (be[b], 0, 0)),
                      pl.BlockSpec((1, D, F), lambda b, be, nv, nb: (be[b], 0, 0)),
                      pl.BlockSpec((1, F, D), lambda b, be, nv, nb: (be[b], 0, 0))],
            out_specs=pl.BlockSpec((MOE_BLOCK * G, LANES), blk),
            scratch_shapes=[pltpu.VMEM((D, F), BF16), pltpu.VMEM((D, F), BF16),
                            pltpu.VMEM((F, D), BF16)]),
        out_shape=jax.ShapeDtypeStruct((P * G, LANES), U32),
        compiler_params=_params(("arbitrary",), 52),
        name="moe_experts",
    )(block_expert, nvalid, nblocks, xs, wg, wu, wd)


def _combine_kernel(dest_hbm, ys_hbm, gate_ref, h_ref, x_ref, g2_ref, sg_ref, su_ref, sd_ref, fg_ref,
                    o_ref, dest0, dest1, ybuf0, ybuf1, idx_sem, row_sem, *, Tc, G, final_norm):
    j = pl.program_id(0)
    n = Tc * TOP_K
    slots = ((dest0, ybuf0), (dest1, ybuf1))

    def row_copy(slot, t, k):
        dest, ybuf = slots[slot]
        src = ys_hbm.at[pl.ds(pl.multiple_of(dest[t * TOP_K + k], G), G)]
        dst = ybuf.at[k, pl.ds(pl.multiple_of(t * G, G), G)]
        return pltpu.make_async_copy(src, dst, row_sem.at[slot])

    def request(tile, slot):
        cp = pltpu.make_async_copy(dest_hbm.at[pl.ds(tile * n, n)], slots[slot][0],
                                   idx_sem.at[slot])
        cp.start()
        cp.wait()

        def issue(t, carry):
            for k in range(TOP_K):
                row_copy(slot, t, k).start(priority=k % 2)
            return carry

        lax.fori_loop(0, Tc, issue, 0)

    def reduce(slot):
        rows = slice(slot * Tc, (slot + 1) * Tc)
        h_tile = h_ref.at[pl.ds(slot * Tc * G, Tc * G)]
        halves = [_unpack_pairs(_load_row_slab(h_tile, c, Tc, G)) for c in range(G)]
        h = jnp.concatenate([p[0].astype(BF16) for p in halves]
                            + [p[1].astype(BF16) for p in halves], axis=1)
        g = jnp.dot(h, sg_ref[...], preferred_element_type=F32)
        u = jnp.dot(h, su_ref[...], preferred_element_type=F32)
        a = (g * jax.nn.sigmoid(g) * u).astype(BF16)
        f = jnp.dot(a, sd_ref[...], preferred_element_type=F32)

        def drain(t, carry):
            for k in range(TOP_K):
                row_copy(slot, t, k).wait()
            return carry

        lax.fori_loop(0, Tc, drain, 0)
        gate = gate_ref[rows, :]
        half = f.shape[1] // 2
        his, los = [], []
        for c in range(G):
            f_hi = f[:, c * LANES:(c + 1) * LANES]
            f_lo = f[:, half + c * LANES:half + (c + 1) * LANES]
            for k in range(TOP_K):
                y_hi, y_lo = _unpack_pairs(_load_row_slab(slots[slot][1].at[k], c, Tc, G))
                f_hi = f_hi + gate[:, k:k + 1] * y_hi
                f_lo = f_lo + gate[:, k:k + 1] * y_lo
            his.append(f_hi)
            los.append(f_lo)
        x2 = x_ref[rows, :] + g2_ref[0] * jnp.concatenate(his + los, axis=1)
        if final_norm:
            x2 = _rms(x2, NORM_EPS) * fg_ref[...]
        o_ref[rows, :] = x2

    @pl.when(j == 0)
    def _():
        request(0, 0)

    request(2 * j + 1, 1)
    reduce(0)

    @pl.when(j + 1 < pl.num_programs(0))
    def _():
        request(2 * j + 2, 0)

    reduce(1)


def _combine(dest_flat, ys, gate_t, h2, x1, g2, sg, su, sd, final_g, S, final_norm, Tc=256):
    N, D = x1.shape
    F = sg.shape[1]
    G = D // 2 // LANES
    tb = 2 * Tc
    tpb = S // tb
    row = lambda j: (j, 0)
    const = dict(pipeline_mode=pl.Buffered(1))
    return pl.pallas_call(
        functools.partial(_combine_kernel, Tc=Tc, G=G, final_norm=final_norm),
        grid=(N // tb,),
        in_specs=[pl.BlockSpec(memory_space=pl.ANY), pl.BlockSpec(memory_space=pl.ANY),
                  pl.BlockSpec((tb, TOP_K), row),
                  pl.BlockSpec((tb * G, LANES), row), pl.BlockSpec((tb, D), row),
                  pl.BlockSpec((1, 1, D), lambda j: (j // tpb, 0, 0)),
                  pl.BlockSpec((D, F), lambda j: (0, 0), **const),
                  pl.BlockSpec((D, F), lambda j: (0, 0), **const),
                  pl.BlockSpec((F, D), lambda j: (0, 0), **const),
                  pl.BlockSpec((1, D), lambda j: (0, 0))],
        out_specs=pl.BlockSpec((tb, D), row),
        out_shape=jax.ShapeDtypeStruct((N, D), F32),
        scratch_shapes=[pltpu.SMEM((Tc * TOP_K,), I32), pltpu.SMEM((Tc * TOP_K,), I32),
                        pltpu.VMEM((TOP_K, Tc * G, LANES), U32),
                        pltpu.VMEM((TOP_K, Tc * G, LANES), U32),
                        pltpu.SemaphoreType.DMA((2,)), pltpu.SemaphoreType.DMA((2,))],
        compiler_params=_params(("arbitrary",), 52),
        name="moe_combine",
    )(dest_flat, ys, gate_t, h2, x1, g2, sg, su, sd, final_g.reshape(1, D))


def _moe_ffn(h2, logits_t, x1, g2, router_bias, wg, wu, wd, e_off, sg, su, sd, final_g, S,
             final_norm):
    N, D = x1.shape
    G = D // 2 // LANES
    E = logits_t.shape[0]
    idx, gate, rank, cnt = _route(logits_t, router_bias)
    counts = cnt[:, 0]
    padded = (counts + MOE_BLOCK - 1) // MOE_BLOCK * MOE_BLOCK
    pend = jnp.cumsum(padded)
    pstart = (pend - padded).astype(I32)
    n_blocks = -(-(N * TOP_K + E * (MOE_BLOCK - 1)) // MOE_BLOCK)
    bstart = jnp.arange(n_blocks, dtype=I32) * MOE_BLOCK
    block_expert = jnp.minimum(jnp.sum(pend[None, :] <= bstart[:, None], axis=1), E - 1).astype(I32)
    nvalid = jnp.clip(pstart[block_expert] + counts[block_expert] - bstart, 0, MOE_BLOCK).astype(I32)
    nblocks = (pend[-1:] // MOE_BLOCK).astype(I32)

    dest = _dest(pstart, idx, rank, G)
    dest_flat = dest.T.reshape(N * TOP_K)
    xs = _dispatch(dest_flat, h2, n_blocks * MOE_BLOCK, G)
    ys = _moe(block_expert + e_off, nvalid, nblocks, xs, wg, wu, wd)
    return _combine(dest_flat, ys, gate.T, h2, x1, g2, sg, su, sd, final_g, S, final_norm)


def kernel(x, c, ada_w, ada_b, norm_mix_g, norm_ffn_g, w_in, lambda_q1, lambda_k1, lambda_q2,
           lambda_k2, subln_g, rpb, w_out, router_w, router_bias, exp_w_gate, exp_w_up,
           exp_w_down, sh_w_gate, sh_w_up, sh_w_down, final_g):
    B, S, D = x.shape
    L = ada_w.shape[0]
    N = B * S
    n_heads = w_in.shape[2] // (3 * HEAD_DIM)
    heads_a = n_heads // HEADS_A_FRAC
    heads_b = n_heads // HEADS_B_FRAC
    heads_c = n_heads - heads_a - heads_b
    slopes = _alibi_slopes(heads_a + heads_b)
    slopes_a = jnp.asarray(slopes[:heads_a], F32)
    slopes_b = jnp.asarray(slopes[heads_a:], F32)

    E, F = exp_w_gate.shape[1], exp_w_gate.shape[3]
    wg_all = exp_w_gate.reshape(L * E, D, F)
    wu_all = exp_w_up.reshape(L * E, D, F)
    wd_all = exp_w_down.reshape(L * E, F, D)

    mod = _ada(c, ada_w, ada_b).reshape(L, B, 6, 1, D)
    xf = x.reshape(N, D)
    for l in range(L):
        sh1, sc1, g1, sh2, sc2, g2 = (mod[l, :, j] for j in range(6))
        qkv = _qkv(xf, norm_mix_g[l], sc1, sh1, w_in[l].astype(BF16), S)
        oa = _attn_a(qkv, slopes_a, B, S, n_heads, heads_a)
        lambda_init = 0.8 - 0.6 * math.exp(-0.3 * l)
        lam_vecs = jnp.stack([lambda_q1[l], lambda_k1[l], lambda_q2[l], lambda_k2[l]]).astype(F32)
        ob = _attn_b(qkv, slopes_b, lam_vecs, subln_g[l], B, S, n_heads, heads_a, heads_b, lambda_init)
        oc = _attn_c(qkv, rpb[l].reshape(-1), B, S, n_heads, heads_a + heads_b, heads_c)
        x1, h2, logits_t = _out_proj(oa, ob, oc, w_out[l].astype(BF16), xf, g1, norm_ffn_g[l],
                                     sc2, sh2, router_w[l].T, S)
        xf = _moe_ffn(h2, logits_t, x1, g2, router_bias[l], wg_all, wu_all, wd_all, l * E,
                      sh_w_gate[l].astype(BF16), sh_w_up[l].astype(BF16),
                      sh_w_down[l].astype(BF16), final_g, S, l == L - 1)
    return xf.reshape(B, S, D)
```

```python
import functools
import math

import jax
import jax.numpy as jnp
from jax import lax
from jax.experimental import pallas as pl
from jax.experimental.pallas import tpu as pltpu

F32 = jnp.float32
BF16 = jnp.bfloat16
I32 = jnp.int32
U32 = jnp.uint32
HIGHEST = lax.Precision.HIGHEST
_NT = (((1,), (1,)), ((), ()))

HEAD_DIM = 128
HEADS_A_FRAC, HEADS_B_FRAC = 2, 4
DILATED_PATTERNS = ((128, 1), (512, 4), (2048, 16))
GRID_W = 64
NA_ROWS = 8
NA_COLS = 16
N_GROUPS = 8
TOPK_GROUPS = 4
TOP_K = 8
ROUTED_SCALE = 2.5
NORM_EPS = 1e-6
SUBLN_EPS = 1e-5
NEG_INF = -1e30
LOG2E = math.log2(math.e)

LANES = 128
VMEM_BYTES_V7X = 64 << 20

MOE_BLOCK = 512


def _params(semantics, vmem_mib):
    return pltpu.CompilerParams(dimension_semantics=semantics,
                                vmem_limit_bytes=min(vmem_mib << 20, VMEM_BYTES_V7X - (4 << 20)))


def _alibi_slopes(n):
    def pow2(m):
        start = 2.0 ** (-8.0 / m)
        return [start ** (i + 1) for i in range(m)]
    p = 2 ** int(math.floor(math.log2(n)))
    return pow2(p) + pow2(2 * p)[0::2][: n - p]


def _rms(x, eps):
    return x * lax.rsqrt(jnp.mean(x * x, axis=-1, keepdims=True) + eps)


def _pack_pairs(x):
    half = x.shape[1] // 2
    hi = lax.bitcast_convert_type(x[:, :half].astype(BF16).astype(F32), U32)
    lo = lax.bitcast_convert_type(x[:, half:].astype(BF16).astype(F32), U32)
    return hi | (lo >> 16)


def _unpack_pairs(w):
    hi = lax.bitcast_convert_type(w & jnp.uint32(0xFFFF0000), F32)
    lo = lax.bitcast_convert_type(w << 16, F32)
    return hi, lo


def _store_row_slabs(ref, packed):
    T, W = packed.shape
    for c in range(W // LANES):
        ref[pl.ds(c, T, stride=W // LANES), :] = packed[:, c * LANES:(c + 1) * LANES]


def _load_row_slab(ref, c, T, G):
    return ref[pl.ds(c, T, stride=G), :]


def _ada_kernel(c_ref, w_ref, b_ref, o_ref):
    c = c_ref[...]
    sc = c * jax.nn.sigmoid(c)
    o_ref[0] = jnp.dot(sc, w_ref[0], preferred_element_type=F32, precision=HIGHEST) + b_ref[0]


def _ada(c, ada_w, ada_b):
    B, D = c.shape
    L, _, D6 = ada_w.shape
    rows = 8
    cp = jnp.zeros((rows, D), F32).at[:B].set(c)
    tn = 1024
    out = pl.pallas_call(
        _ada_kernel,
        grid=(L, D6 // tn),
        in_specs=[pl.BlockSpec((rows, D), lambda l, j: (0, 0)),
                  pl.BlockSpec((1, D, tn), lambda l, j: (l, 0, j)),
                  pl.BlockSpec((1, 1, tn), lambda l, j: (l, 0, j))],
        out_specs=pl.BlockSpec((1, rows, tn), lambda l, j: (l, 0, j)),
        out_shape=jax.ShapeDtypeStruct((L, rows, D6), F32),
        compiler_params=_params(("arbitrary", "arbitrary"), 32),
        name="ada_mod",
    )(cp, ada_w, ada_b.reshape(L, 1, D6))
    return out[:, :B]


def _qkv_kernel(x_ref, g_ref, sc_ref, sh_ref, w_ref, o_ref, h_scr, *, heads_per_step):
    @pl.when(pl.program_id(1) == 0)
    def _():
        y = _rms(x_ref[...], NORM_EPS) * g_ref[...]
        h_scr[...] = (y * (1.0 + sc_ref[0]) + sh_ref[0]).astype(BF16)

    res = jnp.dot(h_scr[...], w_ref[...], preferred_element_type=F32)
    for hh in range(heads_per_step):
        o_ref[hh] = res[:, hh * HEAD_DIM:(hh + 1) * HEAD_DIM].astype(BF16)


def _qkv(x2d, g, sc, sh, w_bf, S, tm=512, tn=2048):
    N, D = x2d.shape
    W3 = w_bf.shape[1]
    hps = tn // HEAD_DIM
    tpb = S // tm
    return pl.pallas_call(
        functools.partial(_qkv_kernel, heads_per_step=hps),
        grid=(N // tm, W3 // tn),
        in_specs=[pl.BlockSpec((tm, D), lambda i, j: (i, 0)),
                  pl.BlockSpec((1, D), lambda i, j: (0, 0)),
                  pl.BlockSpec((1, 1, D), lambda i, j: (i // tpb, 0, 0)),
                  pl.BlockSpec((1, 1, D), lambda i, j: (i // tpb, 0, 0)),
                  pl.BlockSpec((D, tn), lambda i, j: (0, j))],
        out_specs=pl.BlockSpec((hps, tm, HEAD_DIM), lambda i, j: (j, i, 0)),
        out_shape=jax.ShapeDtypeStruct((W3 // HEAD_DIM, N, HEAD_DIM), BF16),
        scratch_shapes=[pltpu.VMEM((tm, D), BF16)],
        compiler_params=_params(("arbitrary", "arbitrary"), 48),
        name="qkv_proj",
    )(x2d, g.reshape(1, D), sc, sh, w_bf)


def _col_to_row(col, eye):
    return jnp.sum(jnp.where(eye, col, 0.0), axis=0, keepdims=True)


def _attn_a_kernel(slopes_ref, q_ref, k_ref, v_ref, o_ref, lse_ref, nat, cq, ck, cv, onat, *,
                   L, dil, half, Tq):
    h = pl.program_id(1)
    wk = 2 * Tq
    n_tiles = L // Tq
    c_dist = LOG2E * slopes_ref[h] * dil
    a = lax.broadcasted_iota(I32, (Tq, wk), 0)
    u = lax.broadcasted_iota(I32, (Tq, wk), 1)
    eye = lax.broadcasted_iota(I32, (Tq, Tq), 0) == lax.broadcasted_iota(I32, (Tq, Tq), 1)

    def table(off):
        d = u - a - off
        ad = jnp.maximum(d, -d)
        return jnp.where(ad <= half, -c_dist * ad.astype(F32), NEG_INF)

    tabs = {off: table(off) for off in sorted({0, half, wk - Tq})}

    q_scale = LOG2E * HEAD_DIM ** -0.5
    if dil > 1:
        for src, dst, scale in ((q_ref, cq, q_scale), (k_ref, ck, None), (v_ref, cv, None)):
            x = src[0].astype(F32)
            nat[...] = x if scale is None else x * scale
            for r in range(dil):
                dst[r * L:(r + 1) * L, :] = nat[pl.ds(r, L, stride=dil), :].astype(BF16)
        kc, vc = ck, cv
    else:
        kc, vc = k_ref.at[0], v_ref.at[0]

    for r in range(dil):
        for j in range(n_tiles):
            start = min(max(j * Tq - half, 0), L - wk)
            rows = slice(r * L + j * Tq, r * L + (j + 1) * Tq)
            q = cq[rows, :] if dil > 1 else (q_ref[0, rows, :].astype(F32) * q_scale).astype(BF16)
            kw = kc[r * L + start:r * L + start + wk, :]
            vw = vc[r * L + start:r * L + start + wk, :]
            s = lax.dot_general(q, kw, _NT, preferred_element_type=F32) + tabs[j * Tq - start]
            m = jnp.max(s, axis=-1, keepdims=True)
            p = jnp.exp2(s - m)
            l = jnp.sum(p, axis=-1, keepdims=True)
            o = jnp.dot(p.astype(BF16), vw, preferred_element_type=F32) / l
            if dil > 1:
                onat[pl.ds(r + j * Tq * dil, Tq, stride=dil), :] = o
            else:
                o_ref[0, rows, :] = o.astype(BF16)
            lse_ref[0, 0, r:r + 1, j * Tq:(j + 1) * Tq] = _col_to_row(m + jnp.log2(l), eye)
    if dil > 1:
        o_ref[0] = onat[...].astype(BF16)


def _attn_a_pattern(qkv, slopes, B, S, n_heads, heads_a, window, dil, Tq=128):
    N = B * S
    L = S // dil
    half = window // (2 * dil)
    assert L % Tq == 0 and L >= 2 * Tq and 2 * half <= Tq
    blk = (1, S, HEAD_DIM)
    o, lse = pl.pallas_call(
        functools.partial(_attn_a_kernel, L=L, dil=dil, half=half, Tq=Tq),
        grid=(B, heads_a),
        in_specs=[pl.BlockSpec(memory_space=pltpu.SMEM),
                  pl.BlockSpec(blk, lambda b, h: (h, b, 0)),
                  pl.BlockSpec(blk, lambda b, h: (n_heads + h, b, 0)),
                  pl.BlockSpec(blk, lambda b, h: (2 * n_heads + h, b, 0))],
        out_specs=[pl.BlockSpec(blk, lambda b, h: (h, b, 0)),
                   pl.BlockSpec((1, 1, dil, L), lambda b, h: (b, h, 0, 0))],
        out_shape=[jax.ShapeDtypeStruct((heads_a, N, HEAD_DIM), BF16),
                   jax.ShapeDtypeStruct((B, heads_a, dil, L), F32)],
        scratch_shapes=[pltpu.VMEM((S, HEAD_DIM), F32)] + [pltpu.VMEM((S, HEAD_DIM), BF16)] * 3
        + [pltpu.VMEM((S, HEAD_DIM), F32)],
        compiler_params=_params(("arbitrary", "arbitrary"), 32),
        name=f"attn_dilated_d{dil}",
    )(slopes, qkv, qkv, qkv)
    lse_nat = lse.transpose(1, 0, 3, 2).reshape(heads_a, N)
    return o, lse_nat


def _mix_a_kernel(*refs, n_pat, tm):
    o_refs, lse_refs, out_ref = refs[:n_pat], refs[n_pat:2 * n_pat], refs[2 * n_pat]
    heads = lse_refs[0].shape[0]
    lses = [r[...] for r in lse_refs]
    mx = functools.reduce(jnp.maximum, lses)
    es = [jnp.exp2(x - mx) for x in lses]
    den = functools.reduce(lambda x, y: x + y, es)
    ws = [e / den for e in es]
    pad = jnp.zeros((LANES - n_pat * heads, LANES), F32)
    for c in range(tm // LANES):
        cols = slice(c * LANES, (c + 1) * LANES)
        w_rows = jnp.concatenate([w[:, cols] for w in ws] + [pad], axis=0)
        w_cols = w_rows.T
        for hh in range(heads):
            acc = jnp.zeros((LANES, HEAD_DIM), F32)
            for p in range(n_pat):
                wc = w_cols[:, p * heads + hh:p * heads + hh + 1]
                acc = acc + wc * o_refs[p][hh, cols, :].astype(F32)
            out_ref[cols, hh * HEAD_DIM:(hh + 1) * HEAD_DIM] = acc.astype(BF16)


def _mix_a(os, lses, tm=512):
    heads, N, _ = os[0].shape
    n_pat = len(os)
    assert n_pat * heads <= LANES
    return pl.pallas_call(
        functools.partial(_mix_a_kernel, n_pat=n_pat, tm=tm),
        grid=(N // tm,),
        in_specs=[pl.BlockSpec((heads, tm, HEAD_DIM), lambda i: (0, i, 0))] * n_pat
        + [pl.BlockSpec((heads, tm), lambda i: (0, i))] * n_pat,
        out_specs=pl.BlockSpec((tm, heads * HEAD_DIM), lambda i: (i, 0)),
        out_shape=jax.ShapeDtypeStruct((N, heads * HEAD_DIM), BF16),
        compiler_params=_params(("arbitrary",), 32),
        name="attn_dilated_mix",
    )(*os, *lses)


def _attn_a(qkv, slopes, B, S, n_heads, heads_a):
    parts = [_attn_a_pattern(qkv, slopes, B, S, n_heads, heads_a, w, d) for w, d in DILATED_PATTERNS]
    return _mix_a([p[0] for p in parts], [p[1] for p in parts])


def _attn_b_kernel(slopes_ref, lam_ref, q_ref, k_ref, v_ref, g_ref, o_ref, tab_ref, kmax_ref,
                   m_scr, l_scr, acc_scr, *, S, Tq, Tk, lambda_init):
    h = pl.program_id(1)
    i = pl.program_id(2)
    dh = HEAD_DIM // 2
    n_chunks = S // Tk
    c_dist = LOG2E * slopes_ref[h]

    def sub_norm(x):
        sq = x * x
        return jnp.sqrt(jnp.maximum(jnp.sum(sq[:, :dh], axis=-1, keepdims=True),
                                    jnp.sum(sq[:, dh:], axis=-1, keepdims=True)))

    @pl.when(i == 0)
    def _():
        shape = (Tq, 2 * S - Tq)
        a = lax.broadcasted_iota(I32, shape, 0)
        u = lax.broadcasted_iota(I32, shape, 1)
        d = a - u + (S - Tq)
        tab_ref[...] = -c_dist * jnp.maximum(d, -d).astype(F32)
        kn = sub_norm(k_ref[0].astype(F32))
        for j in range(n_chunks):
            kmax_ref[j] = jnp.max(kn[j * Tk:(j + 1) * Tk])

    lv = lam_ref[...]
    lam = (jnp.exp(jnp.sum(lv[0:1] * lv[1:2], axis=-1, keepdims=True))
           - jnp.exp(jnp.sum(lv[2:3] * lv[3:4], axis=-1, keepdims=True)) + lambda_init)

    t0 = i * Tq
    qf = q_ref[0].astype(F32) * (LOG2E * dh ** -0.5)
    q = qf.astype(BF16)
    qmax = jnp.max(sub_norm(q.astype(F32)))

    def chunk(kstart, first):
        kc = k_ref[0, pl.ds(kstart, Tk), :]
        vc = v_ref[0, pl.ds(kstart, Tk), :]
        bias = tab_ref[:, pl.ds(pl.multiple_of(S - Tq - t0 + kstart, LANES), Tk)]
        s = jnp.concatenate(
            [lax.dot_general(q[:, c * dh:(c + 1) * dh], kc[:, c * dh:(c + 1) * dh], _NT,
                             preferred_element_type=F32) + bias for c in range(2)], axis=0)
        mc = jnp.max(s, axis=-1, keepdims=True)
        if first:
            m_new = mc
        else:
            m_old = m_scr[...]
            m_new = jnp.maximum(m_old, mc)
            alpha = jnp.exp2(m_old - m_new)
        p = jnp.exp2(s - m_new)
        ls = jnp.sum(p, axis=-1, keepdims=True)
        pv = jnp.dot(p.astype(BF16), vc, preferred_element_type=F32)
        m_scr[...] = m_new
        l_scr[...] = ls if first else alpha * l_scr[...] + ls
        acc_scr[...] = pv if first else alpha * acc_scr[...] + pv
        return m_new

    jd = t0 // Tk
    m_diag = chunk(pl.multiple_of(jd * Tk, Tk), True)
    floor = jnp.min(m_diag) - 160.0
    for j in range(n_chunks):
        gap = jnp.maximum(jnp.maximum(j * Tk - (t0 + Tq - 1), t0 - ((j + 1) * Tk - 1)), 0)
        bound = 1.001 * qmax * kmax_ref[j] + 0.01 - c_dist * gap.astype(F32)

        @pl.when((j != jd) & (bound > floor))
        def _():
            chunk(j * Tk, False)

    acc = acc_scr[...]
    l = l_scr[...]
    o = acc[:Tq] / l[:Tq] - acc[Tq:] * (lam / l[Tq:])
    o = _rms(o, SUBLN_EPS) * g_ref[...] * (1.0 - lambda_init)
    o_ref[...] = o.astype(BF16)


def _attn_b(qkv, slopes, lam_vecs, subln_g, B, S, n_heads, head0, heads_b, lambda_init, Tq=256,
            Tk=1024):
    N = B * S
    nq = S // Tq
    Tk = min(Tk, S)
    return pl.pallas_call(
        functools.partial(_attn_b_kernel, S=S, Tq=Tq, Tk=Tk, lambda_init=lambda_init),
        grid=(B, heads_b, nq),
        in_specs=[pl.BlockSpec(memory_space=pltpu.SMEM),
                  pl.BlockSpec((4, HEAD_DIM // 2), lambda b, h, i: (0, 0)),
                  pl.BlockSpec((1, Tq, HEAD_DIM), lambda b, h, i: (head0 + h, b * nq + i, 0)),
                  pl.BlockSpec((1, S, HEAD_DIM), lambda b, h, i: (n_heads + head0 + h, b, 0)),
                  pl.BlockSpec((1, S, HEAD_DIM), lambda b, h, i: (2 * n_heads + head0 + h, b, 0)),
                  pl.BlockSpec((1, HEAD_DIM), lambda b, h, i: (0, 0))],
        out_specs=pl.BlockSpec((Tq, HEAD_DIM), lambda b, h, i: (b * nq + i, h)),
        out_shape=jax.ShapeDtypeStruct((N, heads_b * HEAD_DIM), BF16),
        scratch_shapes=[pltpu.VMEM((Tq, 2 * S - Tq), F32), pltpu.SMEM((S // Tk,), F32),
                        pltpu.VMEM((2 * Tq, 1), F32), pltpu.VMEM((2 * Tq, 1), F32),
                        pltpu.VMEM((2 * Tq, HEAD_DIM), F32)],
        compiler_params=_params(("arbitrary", "arbitrary", "arbitrary"), 48),
        name="attn_diff",
    )(slopes, lam_vecs, qkv, qkv, qkv, subln_g.reshape(1, HEAD_DIM))


def _attn_c_kernel(rpb_ref, q_ref, k_ref, v_ref, o_ref, tab_ref, *, R):
    h = pl.program_id(1)
    W = GRID_W
    kr = min(NA_ROWS, R)
    n_dr = 2 * NA_ROWS - 1
    n_dc = 2 * NA_COLS - 1

    c_io = lax.broadcasted_iota(I32, (W, 2 * W), 0)
    l_io = lax.broadcasted_iota(I32, (W, 2 * W), 1)
    cp = l_io & (W - 1)
    dcm = cp - c_io + (NA_COLS - 1)
    cstart = jnp.clip(c_io - NA_COLS // 2, 0, W - NA_COLS)
    ok = (cp >= cstart) & (cp < cstart + NA_COLS)
    blocks = []
    for dr in range(n_dr):
        blk = jnp.zeros((W, 2 * W), F32)
        for dc in range(n_dc):
            blk = jnp.where(dcm == dc, rpb_ref[(h * n_dr + dr) * n_dc + dc], blk)
        blocks.append(jnp.where(ok, blk, NEG_INF))
    for o in range(NA_ROWS):
        for jj in range(kr // 2):
            tab_ref[o, :, jj * 2 * W:(jj + 1) * 2 * W] = jnp.where(
                l_io < W, blocks[o + 2 * jj], blocks[o + 2 * jj + 1])

    def row(r, carry):
        rs = jnp.clip(r - kr // 2, 0, R - kr)
        q = q_ref[0, pl.ds(pl.multiple_of(r * W, W), W), :]
        kw = k_ref[0, pl.ds(pl.multiple_of(rs * W, W), kr * W), :]
        vw = v_ref[0, pl.ds(pl.multiple_of(rs * W, W), kr * W), :]
        s = lax.dot_general(q, kw, _NT, preferred_element_type=F32) * (HEAD_DIM ** -0.5)
        s = s + tab_ref[rs - r + (NA_ROWS - 1)]
        m = jnp.max(s, axis=-1, keepdims=True)
        p = jnp.exp(s - m)
        l = jnp.sum(p, axis=-1, keepdims=True)
        o = jnp.dot(p.astype(BF16), vw, preferred_element_type=F32) / l
        o_ref[pl.ds(pl.multiple_of(r * W, W), W), :] = o.astype(BF16)
        return carry

    lax.fori_loop(0, R, row, 0, unroll=16 if R % 16 == 0 else 1)


def _attn_c(qkv, rpb_flat, B, S, n_heads, head0, heads_c):
    N = B * S
    R = S // GRID_W
    kr = min(NA_ROWS, R)
    assert kr == NA_ROWS and kr % 2 == 0
    return pl.pallas_call(
        functools.partial(_attn_c_kernel, R=R),
        grid=(B, heads_c),
        in_specs=[pl.BlockSpec(memory_space=pltpu.SMEM),
                  pl.BlockSpec((1, S, HEAD_DIM), lambda b, h: (head0 + h, b, 0)),
                  pl.BlockSpec((1, S, HEAD_DIM), lambda b, h: (n_heads + head0 + h, b, 0)),
                  pl.BlockSpec((1, S, HEAD_DIM), lambda b, h: (2 * n_heads + head0 + h, b, 0))],
        out_specs=pl.BlockSpec((S, HEAD_DIM), lambda b, h: (b, h)),
        out_shape=jax.ShapeDtypeStruct((N, heads_c * HEAD_DIM), BF16),
        scratch_shapes=[pltpu.VMEM((NA_ROWS, GRID_W, kr * GRID_W), F32)],
        compiler_params=_params(("arbitrary", "arbitrary"), 32),
        name="attn_nbr",
    )(rpb_flat, qkv, qkv, qkv)


def _out_kernel(oa_ref, ob_ref, oc_ref, wa_ref, wb_ref, wc_ref, x_ref, g1_ref, ng_ref, sc_ref,
                sh_ref, rw_ref, x1_ref, h2_ref, lg_ref):
    acc = jnp.dot(oa_ref[...], wa_ref[...], preferred_element_type=F32)
    acc = acc + jnp.dot(ob_ref[...], wb_ref[...], preferred_element_type=F32)
    acc = acc + jnp.dot(oc_ref[...], wc_ref[...], preferred_element_type=F32)
    x1 = x_ref[...] + g1_ref[0] * acc
    x1_ref[...] = x1
    h2 = (_rms(x1, NORM_EPS) * ng_ref[...]) * (1.0 + sc_ref[0]) + sh_ref[0]
    _store_row_slabs(h2_ref, _pack_pairs(h2))
    lg_ref[...] = lax.dot_general(rw_ref[...], h2, _NT, preferred_element_type=F32,
                                  precision=HIGHEST)


def _out_proj(oa, ob, oc, w_bf, x2d, g1, ng, sc, sh, rw_t, S, tm=512):
    N, D = x2d.shape
    wa, wb, wc = oa.shape[1], ob.shape[1], oc.shape[1]
    E = rw_t.shape[0]
    tpb = S // tm
    const = dict(pipeline_mode=pl.Buffered(1))
    row = lambda i: (i, 0)
    per_b = lambda i: (i // tpb, 0, 0)
    return pl.pallas_call(
        _out_kernel,
        grid=(N // tm,),
        in_specs=[pl.BlockSpec((tm, wa), row), pl.BlockSpec((tm, wb), row), pl.BlockSpec((tm, wc), row),
                  pl.BlockSpec((wa, D), lambda i: (0, 0), **const),
                  pl.BlockSpec((wb, D), lambda i: (wa // wb, 0), **const),
                  pl.BlockSpec((wc, D), lambda i: ((wa + wb) // wc, 0), **const),
                  pl.BlockSpec((tm, D), row),
                  pl.BlockSpec((1, 1, D), per_b),
                  pl.BlockSpec((1, D), lambda i: (0, 0)),
                  pl.BlockSpec((1, 1, D), per_b),
                  pl.BlockSpec((1, 1, D), per_b),
                  pl.BlockSpec((E, D), lambda i: (0, 0), **const)],
        out_specs=[pl.BlockSpec((tm, D), row), pl.BlockSpec((tm * (D // 2 // LANES), LANES), row),
                   pl.BlockSpec((E, tm), lambda i: (0, i))],
        out_shape=[jax.ShapeDtypeStruct((N, D), F32),
                   jax.ShapeDtypeStruct((N * (D // 2 // LANES), LANES), U32),
                   jax.ShapeDtypeStruct((E, N), F32)],
        compiler_params=_params(("arbitrary",), 52),
        name="out_proj",
    )(oa, ob, oc, w_bf, w_bf, w_bf, x2d, g1, ng.reshape(1, D), sc, sh, rw_t)


def _first_argmax(vals, iota, big):
    mx = jnp.max(vals, axis=0, keepdims=True)
    idx = jnp.min(jnp.where(vals == mx, iota, big), axis=0, keepdims=True)
    return mx, idx


def _route_kernel(lg_ref, rb_ref, idx_ref, gate_ref, rank_ref, cnt_ref, tri_ref, carry_ref, *, Tt):
    i = pl.program_id(0)
    E = lg_ref.shape[0]
    gs = E // N_GROUPS

    @pl.when(i == 0)
    def _():
        r = lax.broadcasted_iota(I32, (Tt, Tt), 0)
        c = lax.broadcasted_iota(I32, (Tt, Tt), 1)
        tri_ref[...] = jnp.where(r < c, 1.0, 0.0).astype(BF16)
        carry_ref[...] = jnp.zeros_like(carry_ref)

    scores = jax.nn.sigmoid(lg_ref[...])
    sel = scores + rb_ref[...]
    e_io = lax.broadcasted_iota(I32, (E, Tt), 0).astype(F32)
    s_io = lax.broadcasted_iota(I32, (gs, Tt), 0).astype(F32)
    g_io = lax.broadcasted_iota(I32, (N_GROUPS, Tt), 0).astype(F32)

    grp = []
    for g in range(N_GROUPS):
        blk = sel[g * gs:(g + 1) * gs]
        m1, a1 = _first_argmax(blk, s_io, float(gs))
        m2 = jnp.max(jnp.where(s_io == a1, -jnp.inf, blk), axis=0, keepdims=True)
        grp.append(m1 + m2)
    grp = jnp.concatenate(grp, axis=0)

    gsel = jnp.zeros((N_GROUPS, Tt), F32)
    for _ in range(TOPK_GROUPS):
        _, gi = _first_argmax(grp, g_io, float(N_GROUPS))
        hit = g_io == gi
        gsel = jnp.where(hit, 1.0, gsel)
        grp = jnp.where(hit, -jnp.inf, grp)

    esel = jnp.concatenate(
        [jnp.broadcast_to(gsel[g:g + 1], (gs, Tt)) for g in range(N_GROUPS)], axis=0)
    cur = jnp.where(esel > 0.0, sel, NEG_INF)

    idxs, gates = [], []
    onehot = jnp.zeros((E, Tt), F32)
    for _ in range(TOP_K):
        _, ei = _first_argmax(cur, e_io, float(E))
        hit = e_io == ei
        idxs.append(ei)
        gates.append(jnp.sum(jnp.where(hit, scores, 0.0), axis=0, keepdims=True))
        onehot = jnp.where(hit, 1.0, onehot)
        cur = jnp.where(hit, -jnp.inf, cur)
    gate = jnp.concatenate(gates, axis=0)
    gate = gate / jnp.sum(gate, axis=0, keepdims=True) * ROUTED_SCALE
    idx_ref[...] = jnp.concatenate(idxs, axis=0).astype(I32)
    gate_ref[...] = gate

    oh = onehot.astype(BF16)
    before = jnp.dot(oh, tri_ref[...], preferred_element_type=F32)
    tile_cnt = jnp.dot(oh, jnp.ones((Tt, LANES), BF16), preferred_element_type=F32)
    carry = carry_ref[...]
    rank_mat = before + jnp.concatenate([carry] * (Tt // LANES), axis=1)
    ranks = [jnp.sum(jnp.where(e_io == ei, rank_mat, 0.0), axis=0, keepdims=True) for ei in idxs]
    rank_ref[...] = jnp.concatenate(ranks, axis=0).astype(I32)
    carry_ref[...] = carry + tile_cnt
    cnt_ref[...] = (carry + tile_cnt).astype(I32)


def _route(logits_t, router_bias, Tt=512):
    E, N = logits_t.shape
    tok = lambda i: (0, i)
    return pl.pallas_call(
        functools.partial(_route_kernel, Tt=Tt),
        grid=(N // Tt,),
        in_specs=[pl.BlockSpec((E, Tt), tok), pl.BlockSpec((E, 1), lambda i: (0, 0))],
        out_specs=[pl.BlockSpec((TOP_K, Tt), tok), pl.BlockSpec((TOP_K, Tt), tok),
                   pl.BlockSpec((TOP_K, Tt), tok), pl.BlockSpec((E, LANES), lambda i: (0, 0))],
        out_shape=[jax.ShapeDtypeStruct((TOP_K, N), I32), jax.ShapeDtypeStruct((TOP_K, N), F32),
                   jax.ShapeDtypeStruct((TOP_K, N), I32), jax.ShapeDtypeStruct((E, LANES), I32)],
        scratch_shapes=[pltpu.VMEM((Tt, Tt), BF16), pltpu.VMEM((E, LANES), F32)],
        compiler_params=_params(("arbitrary",), 32),
        name="route_topk",
    )(logits_t, router_bias.reshape(E, 1))


def _dest_kernel(pstart_ref, idx_ref, rank_ref, dest_ref, *, G):
    idx = idx_ref[...]
    base = jnp.zeros(idx.shape, I32)
    for e in range(pstart_ref.shape[0]):
        base = jnp.where(idx == e, pstart_ref[e], base)
    dest_ref[...] = (base + rank_ref[...]) * G


def _dest(pstart, idx, rank, G):
    return pl.pallas_call(
        functools.partial(_dest_kernel, G=G),
        in_specs=[pl.BlockSpec(memory_space=pltpu.SMEM), pl.BlockSpec(memory_space=pltpu.VMEM),
                  pl.BlockSpec(memory_space=pltpu.VMEM)],
        out_specs=pl.BlockSpec(memory_space=pltpu.VMEM),
        out_shape=jax.ShapeDtypeStruct(idx.shape, I32),
        name="route_dest",
    )(pstart, idx, rank)


def _dispatch_kernel(dest_hbm, h_ref, xs_hbm, dest_smem, idx_sem, row_sem, *, Td, G):
    i = pl.program_id(0)
    n = Td * TOP_K
    cp = pltpu.make_async_copy(dest_hbm.at[pl.ds(i * n, n)], dest_smem, idx_sem)
    cp.start()
    cp.wait()

    def row_copy(t, k):
        src = h_ref.at[pl.ds(pl.multiple_of(t * G, G), G)]
        dst = xs_hbm.at[pl.ds(pl.multiple_of(dest_smem[t * TOP_K + k], G), G)]
        return pltpu.make_async_copy(src, dst, row_sem)

    def issue(t, carry):
        for k in range(TOP_K):
            row_copy(t, k).start(priority=k % 2)
        return carry

    lax.fori_loop(0, Td, issue, 0)

    def drain(t, carry):
        for k in range(TOP_K):
            row_copy(t, k).wait()
        return carry

    lax.fori_loop(0, Td, drain, 0)


def _dispatch(dest_flat, h2, P, G, Td=512):
    N = h2.shape[0] // G
    return pl.pallas_call(
        functools.partial(_dispatch_kernel, Td=Td, G=G),
        grid=(N // Td,),
        in_specs=[pl.BlockSpec(memory_space=pl.ANY), pl.BlockSpec((Td * G, LANES), lambda i: (i, 0))],
        out_specs=pl.BlockSpec(memory_space=pl.ANY),
        out_shape=jax.ShapeDtypeStruct((P * G, LANES), h2.dtype),
        scratch_shapes=[pltpu.SMEM((Td * TOP_K,), I32), pltpu.SemaphoreType.DMA,
                        pltpu.SemaphoreType.DMA],
        compiler_params=_params(("arbitrary",), 32),
        name="moe_dispatch",
    )(dest_flat, h2)


def _moe_kernel(be_ref, nv_ref, nb_ref, xs_ref, wg_ref, wu_ref, wd_ref, ys_ref, wg_s, wu_s, wd_s):
    b = pl.program_id(0)
    active = b < nb_ref[0]
    new_expert = (b == 0) | (be_ref[b] != be_ref[jnp.maximum(b - 1, 0)])

    @pl.when(active & new_expert)
    def _():
        wg_s[...] = wg_ref[0].astype(BF16)
        wu_s[...] = wu_ref[0].astype(BF16)
        wd_s[...] = wd_ref[0].astype(BF16)

    @pl.when(active)
    def _():
        T = MOE_BLOCK
        G = xs_ref.shape[0] // T
        valid = lax.broadcasted_iota(I32, (T, LANES), 0) < nv_ref[b]
        his, los = [], []
        for c in range(G):
            w = jnp.where(valid, _load_row_slab(xs_ref, c, T, G), jnp.uint32(0))
            hi, lo = _unpack_pairs(w)
            his.append(hi.astype(BF16))
            los.append(lo.astype(BF16))
        x = jnp.concatenate(his + los, axis=1)
        g = jnp.dot(x, wg_s[...], preferred_element_type=F32)
        u = jnp.dot(x, wu_s[...], preferred_element_type=F32)
        a = (g * jax.nn.sigmoid(g) * u).astype(BF16)
        _store_row_slabs(ys_ref, _pack_pairs(jnp.dot(a, wd_s[...], preferred_element_type=F32)))


def _moe(block_expert, nvalid, nblocks, xs, wg, wu, wd):
    _, D, F = wg.shape
    G = D // 2 // LANES
    P = xs.shape[0] // G
    n_blocks = P // MOE_BLOCK
    blk = lambda b, be, nv, nb: (jnp.minimum(b, nb[0] - 1), 0)
    return pl.pallas_call(
        _moe_kernel,
        grid_spec=pltpu.PrefetchScalarGridSpec(
            num_scalar_prefetch=3,
            grid=(n_blocks,),
            in_specs=[pl.BlockSpec((MOE_BLOCK * G, LANES), blk),
                      pl.BlockSpec((1, D, F), lambda b, be, nv, nb: (be[b], 0, 0)),
                      pl.BlockSpec((1, D, F), lambda b, be, nv, nb: (be[b], 0, 0)),
                      pl.BlockSpec((1, F, D), lambda b, be, nv, nb: (be[b], 0, 0))],
            out_specs=pl.BlockSpec((MOE_BLOCK * G, LANES), blk),
            scratch_shapes=[pltpu.VMEM((D, F), BF16), pltpu.VMEM((D, F), BF16),
                            pltpu.VMEM((F, D), BF16)]),
        out_shape=jax.ShapeDtypeStruct((P * G, LANES), U32),
        compiler_params=_params(("arbitrary",), 52),
        name="moe_experts",
    )(block_expert, nvalid, nblocks, xs, wg, wu, wd)


def _combine_kernel(dest_hbm, ys_hbm, gate_ref, h_ref, x_ref, g2_ref, sg_ref, su_ref, sd_ref, fg_ref,
                    o_ref, dest0, dest1, ybuf0, ybuf1, idx_sem, row_sem, *, Tc, G, final_norm):
    j = pl.program_id(0)
    n = Tc * TOP_K
    slots = ((dest0, ybuf0), (dest1, ybuf1))

    def row_copy(slot, t, k):
        dest, ybuf = slots[slot]
        src = ys_hbm.at[pl.ds(pl.multiple_of(dest[t * TOP_K + k], G), G)]
        dst = ybuf.at[k, pl.ds(pl.multiple_of(t * G, G), G)]
        return pltpu.make_async_copy(src, dst, row_sem.at[slot])

    def request(tile, slot):
        cp = pltpu.make_async_copy(dest_hbm.at[pl.ds(tile * n, n)], slots[slot][0],
                                   idx_sem.at[slot])
        cp.start()
        cp.wait()

        def issue(t, carry):
            for k in range(TOP_K):
                row_copy(slot, t, k).start(priority=k % 2)
            return carry

        lax.fori_loop(0, Tc, issue, 0)

    def reduce(slot):
        rows = slice(slot * Tc, (slot + 1) * Tc)
        h_tile = h_ref.at[pl.ds(slot * Tc * G, Tc * G)]
        halves = [_unpack_pairs(_load_row_slab(h_tile, c, Tc, G)) for c in range(G)]
        h = jnp.concatenate([p[0].astype(BF16) for p in halves]
                            + [p[1].astype(BF16) for p in halves], axis=1)
        g = jnp.dot(h, sg_ref[...], preferred_element_type=F32)
        u = jnp.dot(h, su_ref[...], preferred_element_type=F32)
        a = (g * jax.nn.sigmoid(g) * u).astype(BF16)
        f = jnp.dot(a, sd_ref[...], preferred_element_type=F32)

        def drain(t, carry):
            for k in range(TOP_K):
                row_copy(slot, t, k).wait()
            return carry

        lax.fori_loop(0, Tc, drain, 0)
        gate = gate_ref[rows, :]
        half = f.shape[1] // 2
        his, los = [], []
        for c in range(G):
            f_hi = f[:, c * LANES:(c + 1) * LANES]
            f_lo = f[:, half + c * LANES:half + (c + 1) * LANES]
            for k in range(TOP_K):
                y_hi, y_lo = _unpack_pairs(_load_row_slab(slots[slot][1].at[k], c, Tc, G))
                f_hi = f_hi + gate[:, k:k + 1] * y_hi
                f_lo = f_lo + gate[:, k:k + 1] * y_lo
            his.append(f_hi)
            los.append(f_lo)
        x2 = x_ref[rows, :] + g2_ref[0] * jnp.concatenate(his + los, axis=1)
        if final_norm:
            x2 = _rms(x2, NORM_EPS) * fg_ref[...]
        o_ref[rows, :] = x2

    @pl.when(j == 0)
    def _():
        request(0, 0)

    request(2 * j + 1, 1)
    reduce(0)

    @pl.when(j + 1 < pl.num_programs(0))
    def _():
        request(2 * j + 2, 0)

    reduce(1)


def _combine(dest_flat, ys, gate_t, h2, x1, g2, sg, su, sd, final_g, S, final_norm, Tc=256):
    N, D = x1.shape
    F = sg.shape[1]
    G = D // 2 // LANES
    tb = 2 * Tc
    tpb = S // tb
    row = lambda j: (j, 0)
    const = dict(pipeline_mode=pl.Buffered(1))
    return pl.pallas_call(
        functools.partial(_combine_kernel, Tc=Tc, G=G, final_norm=final_norm),
        grid=(N // tb,),
        in_specs=[pl.BlockSpec(memory_space=pl.ANY), pl.BlockSpec(memory_space=pl.ANY),
                  pl.BlockSpec((tb, TOP_K), row),
                  pl.BlockSpec((tb * G, LANES), row), pl.BlockSpec((tb, D), row),
                  pl.BlockSpec((1, 1, D), lambda j: (j // tpb, 0, 0)),
                  pl.BlockSpec((D, F), lambda j: (0, 0), **const),
                  pl.BlockSpec((D, F), lambda j: (0, 0), **const),
                  pl.BlockSpec((F, D), lambda j: (0, 0), **const),
                  pl.BlockSpec((1, D), lambda j: (0, 0))],
        out_specs=pl.BlockSpec((tb, D), row),
        out_shape=jax.ShapeDtypeStruct((N, D), F32),
        scratch_shapes=[pltpu.SMEM((Tc * TOP_K,), I32), pltpu.SMEM((Tc * TOP_K,), I32),
                        pltpu.VMEM((TOP_K, Tc * G, LANES), U32),
                        pltpu.VMEM((TOP_K, Tc * G, LANES), U32),
                        pltpu.SemaphoreType.DMA((2,)), pltpu.SemaphoreType.DMA((2,))],
        compiler_params=_params(("arbitrary",), 52),
        name="moe_combine",
    )(dest_flat, ys, gate_t, h2, x1, g2, sg, su, sd, final_g.reshape(1, D))


def _moe_ffn(h2, logits_t, x1, g2, router_bias, wg, wu, wd, e_off, sg, su, sd, final_g, S,
             final_norm):
    N, D = x1.shape
    G = D // 2 // LANES
    E = logits_t.shape[0]
    idx, gate, rank, cnt = _route(logits_t, router_bias)
    counts = cnt[:, 0]
    padded = (counts + MOE_BLOCK - 1) // MOE_BLOCK * MOE_BLOCK
    pend = jnp.cumsum(padded)
    pstart = (pend - padded).astype(I32)
    n_blocks = -(-(N * TOP_K + E * (MOE_BLOCK - 1)) // MOE_BLOCK)
    bstart = jnp.arange(n_blocks, dtype=I32) * MOE_BLOCK
    block_expert = jnp.minimum(jnp.sum(pend[None, :] <= bstart[:, None], axis=1), E - 1).astype(I32)
    nvalid = jnp.clip(pstart[block_expert] + counts[block_expert] - bstart, 0, MOE_BLOCK).astype(I32)
    nblocks = (pend[-1:] // MOE_BLOCK).astype(I32)

    dest = _dest(pstart, idx, rank, G)
    dest_flat = dest.T.reshape(N * TOP_K)
    xs = _dispatch(dest_flat, h2, n_blocks * MOE_BLOCK, G)
    ys = _moe(block_expert + e_off, nvalid, nblocks, xs, wg, wu, wd)
    return _combine(dest_flat, ys, gate.T, h2, x1, g2, sg, su, sd, final_g, S, final_norm)


def kernel(x, c, ada_w, ada_b, norm_mix_g, norm_ffn_g, w_in, lambda_q1, lambda_k1, lambda_q2,
           lambda_k2, subln_g, rpb, w_out, router_w, router_bias, exp_w_gate, exp_w_up,
           exp_w_down, sh_w_gate, sh_w_up, sh_w_down, final_g):
    B, S, D = x.shape
    L = ada_w.shape[0]
    N = B * S
    n_heads = w_in.shape[2] // (3 * HEAD_DIM)
    heads_a = n_heads // HEADS_A_FRAC
    heads_b = n_heads // HEADS_B_FRAC
    heads_c = n_heads - heads_a - heads_b
    slopes = _alibi_slopes(heads_a + heads_b)
    slopes_a = jnp.asarray(slopes[:heads_a], F32)
    slopes_b = jnp.asarray(slopes[heads_a:], F32)

    E, F = exp_w_gate.shape[1], exp_w_gate.shape[3]
    wg_all = exp_w_gate.reshape(L * E, D, F)
    wu_all = exp_w_up.reshape(L * E, D, F)
    wd_all = exp_w_down.reshape(L * E, F, D)

    mod = _ada(c, ada_w, ada_b).reshape(L, B, 6, 1, D)
    xf = x.reshape(N, D)
    for l in range(L):
        sh1, sc1, g1, sh2, sc2, g2 = (mod[l, :, j] for j in range(6))
        qkv = _qkv(xf, norm_mix_g[l], sc1, sh1, w_in[l].astype(BF16), S)
        oa = _attn_a(qkv, slopes_a, B, S, n_heads, heads_a)
        lambda_init = 0.8 - 0.6 * math.exp(-0.3 * l)
        lam_vecs = jnp.stack([lambda_q1[l], lambda_k1[l], lambda_q2[l], lambda_k2[l]]).astype(F32)
        ob = _attn_b(qkv, slopes_b, lam_vecs, subln_g[l], B, S, n_heads, heads_a, heads_b, lambda_init)
        oc = _attn_c(qkv, rpb[l].reshape(-1), B, S, n_heads, heads_a + heads_b, heads_c)
        x1, h2, logits_t = _out_proj(oa, ob, oc, w_out[l].astype(BF16), xf, g1, norm_ffn_g[l],
                                     sc2, sh2, router_w[l].T, S)
        xf = _moe_ffn(h2, logits_t, x1, g2, router_bias[l], wg_all, wu_all, wd_all, l * E,
                      sh_w_gate[l].astype(BF16), sh_w_up[l].astype(BF16),
                      sh_w_down[l].astype(BF16), final_g, S, l == L - 1)
    return xf.reshape(B, S, D)
```

```python
import functools
import math

import jax
import jax.numpy as jnp
from jax import lax
from jax.experimental import pallas as pl
from jax.experimental.pallas import tpu as pltpu

F32 = jnp.float32
BF16 = jnp.bfloat16
I32 = jnp.int32
U32 = jnp.uint32
HIGHEST = lax.Precision.HIGHEST
_NT = (((1,), (1,)), ((), ()))

HEAD_DIM = 128
HEADS_A_FRAC, HEADS_B_FRAC = 2, 4
DILATED_PATTERNS = ((128, 1), (512, 4), (2048, 16))
GRID_W = 64
NA_ROWS = 8
NA_COLS = 16
N_GROUPS = 8
TOPK_GROUPS = 4
TOP_K = 8
ROUTED_SCALE = 2.5
NORM_EPS = 1e-6
SUBLN_EPS = 1e-5
NEG_INF = -1e30
LOG2E = math.log2(math.e)

LANES = 128
VMEM_BYTES_V7X = 64 << 20

MOE_BLOCK = 256


def _params(semantics, vmem_mib):
    return pltpu.CompilerParams(dimension_semantics=semantics,
                                vmem_limit_bytes=min(vmem_mib << 20, VMEM_BYTES_V7X - (4 << 20)))


def _alibi_slopes(n):
    def pow2(m):
        start = 2.0 ** (-8.0 / m)
        return [start ** (i + 1) for i in range(m)]
    p = 2 ** int(math.floor(math.log2(n)))
    return pow2(p) + pow2(2 * p)[0::2][: n - p]


def _rms(x, eps):
    return x * lax.rsqrt(jnp.mean(x * x, axis=-1, keepdims=True) + eps)


def _pack_pairs(x):
    half = x.shape[1] // 2
    hi = lax.bitcast_convert_type(x[:, :half].astype(BF16).astype(F32), U32)
    lo = lax.bitcast_convert_type(x[:, half:].astype(BF16).astype(F32), U32)
    return hi | (lo >> 16)


def _unpack_pairs(w):
    hi = lax.bitcast_convert_type(w & jnp.uint32(0xFFFF0000), F32)
    lo = lax.bitcast_convert_type(w << 16, F32)
    return hi, lo


def _store_row_slabs(ref, packed):
    T, W = packed.shape
    for c in range(W // LANES):
        ref[pl.ds(c, T, stride=W // LANES), :] = packed[:, c * LANES:(c + 1) * LANES]


def _load_row_slab(ref, c, T, G):
    return ref[pl.ds(c, T, stride=G), :]


def _ada_kernel(c_ref, w_ref, b_ref, o_ref):
    c = c_ref[...]
    sc = c * jax.nn.sigmoid(c)
    o_ref[0] = jnp.dot(sc, w_ref[0], preferred_element_type=F32, precision=HIGHEST) + b_ref[0]


def _ada(c, ada_w, ada_b):
    B, D = c.shape
    L, _, D6 = ada_w.shape
    rows = 8
    cp = jnp.zeros((rows, D), F32).at[:B].set(c)
    tn = 1024
    out = pl.pallas_call(
        _ada_kernel,
        grid=(L, D6 // tn),
        in_specs=[pl.BlockSpec((rows, D), lambda l, j: (0, 0)),
                  pl.BlockSpec((1, D, tn), lambda l, j: (l, 0, j)),
                  pl.BlockSpec((1, 1, tn), lambda l, j: (l, 0, j))],
        out_specs=pl.BlockSpec((1, rows, tn), lambda l, j: (l, 0, j)),
        out_shape=jax.ShapeDtypeStruct((L, rows, D6), F32),
        compiler_params=_params(("arbitrary", "arbitrary"), 32),
        name="ada_mod",
    )(cp, ada_w, ada_b.reshape(L, 1, D6))
    return out[:, :B]


def _qkv_kernel(x_ref, g_ref, sc_ref, sh_ref, w_ref, o_ref, h_scr, *, heads_per_step):
    @pl.when(pl.program_id(1) == 0)
    def _():
        y = _rms(x_ref[...], NORM_EPS) * g_ref[...]
        h_scr[...] = (y * (1.0 + sc_ref[0]) + sh_ref[0]).astype(BF16)

    res = jnp.dot(h_scr[...], w_ref[...], preferred_element_type=F32)
    for hh in range(heads_per_step):
        o_ref[hh] = res[:, hh * HEAD_DIM:(hh + 1) * HEAD_DIM].astype(BF16)


def _qkv(x2d, g, sc, sh, w_bf, S, tm=512, tn=2048):
    N, D = x2d.shape
    W3 = w_bf.shape[1]
    hps = tn // HEAD_DIM
    tpb = S // tm
    return pl.pallas_call(
        functools.partial(_qkv_kernel, heads_per_step=hps),
        grid=(N // tm, W3 // tn),
        in_specs=[pl.BlockSpec((tm, D), lambda i, j: (i, 0)),
                  pl.BlockSpec((1, D), lambda i, j: (0, 0)),
                  pl.BlockSpec((1, 1, D), lambda i, j: (i // tpb, 0, 0)),
                  pl.BlockSpec((1, 1, D), lambda i, j: (i // tpb, 0, 0)),
                  pl.BlockSpec((D, tn), lambda i, j: (0, j))],
        out_specs=pl.BlockSpec((hps, tm, HEAD_DIM), lambda i, j: (j, i, 0)),
        out_shape=jax.ShapeDtypeStruct((W3 // HEAD_DIM, N, HEAD_DIM), BF16),
        scratch_shapes=[pltpu.VMEM((tm, D), BF16)],
        compiler_params=_params(("arbitrary", "arbitrary"), 48),
        name="qkv_proj",
    )(x2d, g.reshape(1, D), sc, sh, w_bf)


def _col_to_row(col, eye):
    return jnp.sum(jnp.where(eye, col, 0.0), axis=0, keepdims=True)


def _attn_a_kernel(slopes_ref, q_ref, k_ref, v_ref, o_ref, lse_ref, nat, cq, ck, cv, onat, *,
                   L, dil, half, Tq):
    h = pl.program_id(1)
    wk = 2 * Tq
    n_tiles = L // Tq
    c_dist = LOG2E * slopes_ref[h] * dil
    a = lax.broadcasted_iota(I32, (Tq, wk), 0)
    u = lax.broadcasted_iota(I32, (Tq, wk), 1)
    eye = lax.broadcasted_iota(I32, (Tq, Tq), 0) == lax.broadcasted_iota(I32, (Tq, Tq), 1)

    def table(off):
        d = u - a - off
        ad = jnp.maximum(d, -d)
        return jnp.where(ad <= half, -c_dist * ad.astype(F32), NEG_INF)

    tabs = {off: table(off) for off in sorted({0, half, wk - Tq})}

    q_scale = LOG2E * HEAD_DIM ** -0.5
    if dil > 1:
        for src, dst, scale in ((q_ref, cq, q_scale), (k_ref, ck, None), (v_ref, cv, None)):
            x = src[0].astype(F32)
            nat[...] = x if scale is None else x * scale
            for r in range(dil):
                dst[r * L:(r + 1) * L, :] = nat[pl.ds(r, L, stride=dil), :].astype(BF16)
        kc, vc = ck, cv
    else:
        kc, vc = k_ref.at[0], v_ref.at[0]

    for r in range(dil):
        for j in range(n_tiles):
            start = min(max(j * Tq - half, 0), L - wk)
            rows = slice(r * L + j * Tq, r * L + (j + 1) * Tq)
            q = cq[rows, :] if dil > 1 else (q_ref[0, rows, :].astype(F32) * q_scale).astype(BF16)
            kw = kc[r * L + start:r * L + start + wk, :]
            vw = vc[r * L + start:r * L + start + wk, :]
            s = lax.dot_general(q, kw, _NT, preferred_element_type=F32) + tabs[j * Tq - start]
            m = jnp.max(s, axis=-1, keepdims=True)
            p = jnp.exp2(s - m)
            l = jnp.sum(p, axis=-1, keepdims=True)
            o = jnp.dot(p.astype(BF16), vw, preferred_element_type=F32) / l
            if dil > 1:
                onat[pl.ds(r + j * Tq * dil, Tq, stride=dil), :] = o
            else:
                o_ref[0, rows, :] = o.astype(BF16)
            lse_ref[0, 0, r:r + 1, j * Tq:(j + 1) * Tq] = _col_to_row(m + jnp.log2(l), eye)
    if dil > 1:
        o_ref[0] = onat[...].astype(BF16)


def _attn_a_pattern(qkv, slopes, B, S, n_heads, heads_a, window, dil, Tq=128):
    N = B * S
    L = S // dil
    half = window // (2 * dil)
    assert L % Tq == 0 and L >= 2 * Tq and 2 * half <= Tq
    blk = (1, S, HEAD_DIM)
    o, lse = pl.pallas_call(
        functools.partial(_attn_a_kernel, L=L, dil=dil, half=half, Tq=Tq),
        grid=(B, heads_a),
        in_specs=[pl.BlockSpec(memory_space=pltpu.SMEM),
                  pl.BlockSpec(blk, lambda b, h: (h, b, 0)),
                  pl.BlockSpec(blk, lambda b, h: (n_heads + h, b, 0)),
                  pl.BlockSpec(blk, lambda b, h: (2 * n_heads + h, b, 0))],
        out_specs=[pl.BlockSpec(blk, lambda b, h: (h, b, 0)),
                   pl.BlockSpec((1, 1, dil, L), lambda b, h: (b, h, 0, 0))],
        out_shape=[jax.ShapeDtypeStruct((heads_a, N, HEAD_DIM), BF16),
                   jax.ShapeDtypeStruct((B, heads_a, dil, L), F32)],
        scratch_shapes=[pltpu.VMEM((S, HEAD_DIM), F32)] + [pltpu.VMEM((S, HEAD_DIM), BF16)] * 3
        + [pltpu.VMEM((S, HEAD_DIM), F32)],
        compiler_params=_params(("arbitrary", "arbitrary"), 32),
        name=f"attn_dilated_d{dil}",
    )(slopes, qkv, qkv, qkv)
    lse_nat = lse.transpose(1, 0, 3, 2).reshape(heads_a, N)
    return o, lse_nat


def _mix_a_kernel(*refs, n_pat, tm):
    o_refs, lse_refs, out_ref = refs[:n_pat], refs[n_pat:2 * n_pat], refs[2 * n_pat]
    heads = lse_refs[0].shape[0]
    lses = [r[...] for r in lse_refs]
    mx = functools.reduce(jnp.maximum, lses)
    es = [jnp.exp2(x - mx) for x in lses]
    den = functools.reduce(lambda x, y: x + y, es)
    ws = [e / den for e in es]
    pad = jnp.zeros((LANES - n_pat * heads, LANES), F32)
    for c in range(tm // LANES):
        cols = slice(c * LANES, (c + 1) * LANES)
        w_rows = jnp.concatenate([w[:, cols] for w in ws] + [pad], axis=0)
        w_cols = w_rows.T
        for hh in range(heads):
            acc = jnp.zeros((LANES, HEAD_DIM), F32)
            for p in range(n_pat):
                wc = w_cols[:, p * heads + hh:p * heads + hh + 1]
                acc = acc + wc * o_refs[p][hh, cols, :].astype(F32)
            out_ref[cols, hh * HEAD_DIM:(hh + 1) * HEAD_DIM] = acc.astype(BF16)


def _mix_a(os, lses, tm=512):
    heads, N, _ = os[0].shape
    n_pat = len(os)
    assert n_pat * heads <= LANES
    return pl.pallas_call(
        functools.partial(_mix_a_kernel, n_pat=n_pat, tm=tm),
        grid=(N // tm,),
        in_specs=[pl.BlockSpec((heads, tm, HEAD_DIM), lambda i: (0, i, 0))] * n_pat
        + [pl.BlockSpec((heads, tm), lambda i: (0, i))] * n_pat,
        out_specs=pl.BlockSpec((tm, heads * HEAD_DIM), lambda i: (i, 0)),
        out_shape=jax.ShapeDtypeStruct((N, heads * HEAD_DIM), BF16),
        compiler_params=_params(("arbitrary",), 32),
        name="attn_dilated_mix",
    )(*os, *lses)


def _attn_a(qkv, slopes, B, S, n_heads, heads_a):
    parts = [_attn_a_pattern(qkv, slopes, B, S, n_heads, heads_a, w, d) for w, d in DILATED_PATTERNS]
    return _mix_a([p[0] for p in parts], [p[1] for p in parts])


def _attn_b_kernel(slopes_ref, lam_ref, q_ref, k_ref, v_ref, g_ref, o_ref, tab_ref, kmax_ref,
                   m_scr, l_scr, acc_scr, *, S, Tq, Tk, lambda_init):
    h = pl.program_id(1)
    i = pl.program_id(2)
    dh = HEAD_DIM // 2
    n_chunks = S // Tk
    c_dist = LOG2E * slopes_ref[h]

    def sub_norm(x):
        sq = x * x
        return jnp.sqrt(jnp.maximum(jnp.sum(sq[:, :dh], axis=-1, keepdims=True),
                                    jnp.sum(sq[:, dh:], axis=-1, keepdims=True)))

    @pl.when(i == 0)
    def _():
        shape = (Tq, 2 * S - Tq)
        a = lax.broadcasted_iota(I32, shape, 0)
        u = lax.broadcasted_iota(I32, shape, 1)
        d = a - u + (S - Tq)
        tab_ref[...] = -c_dist * jnp.maximum(d, -d).astype(F32)
        kn = sub_norm(k_ref[0].astype(F32))
        for j in range(n_chunks):
            kmax_ref[j] = jnp.max(kn[j * Tk:(j + 1) * Tk])

    lv = lam_ref[...]
    lam = (jnp.exp(jnp.sum(lv[0:1] * lv[1:2], axis=-1, keepdims=True))
           - jnp.exp(jnp.sum(lv[2:3] * lv[3:4], axis=-1, keepdims=True)) + lambda_init)

    t0 = i * Tq
    qf = q_ref[0].astype(F32) * (LOG2E * dh ** -0.5)
    q = qf.astype(BF16)
    qmax = jnp.max(sub_norm(q.astype(F32)))

    def chunk(kstart, first):
        kc = k_ref[0, pl.ds(kstart, Tk), :]
        vc = v_ref[0, pl.ds(kstart, Tk), :]
        bias = tab_ref[:, pl.ds(pl.multiple_of(S - Tq - t0 + kstart, LANES), Tk)]
        s = jnp.concatenate(
            [lax.dot_general(q[:, c * dh:(c + 1) * dh], kc[:, c * dh:(c + 1) * dh], _NT,
                             preferred_element_type=F32) + bias for c in range(2)], axis=0)
        mc = jnp.max(s, axis=-1, keepdims=True)
        if first:
            m_new = mc
        else:
            m_old = m_scr[...]
            m_new = jnp.maximum(m_old, mc)
            alpha = jnp.exp2(m_old - m_new)
        p = jnp.exp2(s - m_new)
        ls = jnp.sum(p, axis=-1, keepdims=True)
        pv = jnp.dot(p.astype(BF16), vc, preferred_element_type=F32)
        m_scr[...] = m_new
        l_scr[...] = ls if first else alpha * l_scr[...] + ls
        acc_scr[...] = pv if first else alpha * acc_scr[...] + pv
        return m_new

    jd = t0 // Tk
    m_diag = chunk(pl.multiple_of(jd * Tk, Tk), True)
    floor = jnp.min(m_diag) - 160.0
    for j in range(n_chunks):
        gap = jnp.maximum(jnp.maximum(j * Tk - (t0 + Tq - 1), t0 - ((j + 1) * Tk - 1)), 0)
        bound = 1.001 * qmax * kmax_ref[j] + 0.01 - c_dist * gap.astype(F32)

        @pl.when((j != jd) & (bound > floor))
        def _():
            chunk(j * Tk, False)

    acc = acc_scr[...]
    l = l_scr[...]
    o = acc[:Tq] / l[:Tq] - acc[Tq:] * (lam / l[Tq:])
    o = _rms(o, SUBLN_EPS) * g_ref[...] * (1.0 - lambda_init)
    o_ref[...] = o.astype(BF16)


def _attn_b(qkv, slopes, lam_vecs, subln_g, B, S, n_heads, head0, heads_b, lambda_init, Tq=256,
            Tk=1024):
    N = B * S
    nq = S // Tq
    Tk = min(Tk, S)
    return pl.pallas_call(
        functools.partial(_attn_b_kernel, S=S, Tq=Tq, Tk=Tk, lambda_init=lambda_init),
        grid=(B, heads_b, nq),
        in_specs=[pl.BlockSpec(memory_space=pltpu.SMEM),
                  pl.BlockSpec((4, HEAD_DIM // 2), lambda b, h, i: (0, 0)),
                  pl.BlockSpec((1, Tq, HEAD_DIM), lambda b, h, i: (head0 + h, b * nq + i, 0)),
                  pl.BlockSpec((1, S, HEAD_DIM), lambda b, h, i: (n_heads + head0 + h, b, 0)),
                  pl.BlockSpec((1, S, HEAD_DIM), lambda b, h, i: (2 * n_heads + head0 + h, b, 0)),
                  pl.BlockSpec((1, HEAD_DIM), lambda b, h, i: (0, 0))],
        out_specs=pl.BlockSpec((Tq, HEAD_DIM), lambda b, h, i: (b * nq + i, h)),
        out_shape=jax.ShapeDtypeStruct((N, heads_b * HEAD_DIM), BF16),
        scratch_shapes=[pltpu.VMEM((Tq, 2 * S - Tq), F32), pltpu.SMEM((S // Tk,), F32),
                        pltpu.VMEM((2 * Tq, 1), F32), pltpu.VMEM((2 * Tq, 1), F32),
                        pltpu.VMEM((2 * Tq, HEAD_DIM), F32)],
        compiler_params=_params(("arbitrary", "arbitrary", "arbitrary"), 48),
        name="attn_diff",
    )(slopes, lam_vecs, qkv, qkv, qkv, subln_g.reshape(1, HEAD_DIM))


def _attn_c_kernel(rpb_ref, q_ref, k_ref, v_ref, o_ref, tab_ref, *, R):
    h = pl.program_id(1)
    W = GRID_W
    kr = min(NA_ROWS, R)
    n_dr = 2 * NA_ROWS - 1
    n_dc = 2 * NA_COLS - 1

    c_io = lax.broadcasted_iota(I32, (W, 2 * W), 0)
    l_io = lax.broadcasted_iota(I32, (W, 2 * W), 1)
    cp = l_io & (W - 1)
    dcm = cp - c_io + (NA_COLS - 1)
    cstart = jnp.clip(c_io - NA_COLS // 2, 0, W - NA_COLS)
    ok = (cp >= cstart) & (cp < cstart + NA_COLS)
    blocks = []
    for dr in range(n_dr):
        blk = jnp.zeros((W, 2 * W), F32)
        for dc in range(n_dc):
            blk = jnp.where(dcm == dc, rpb_ref[(h * n_dr + dr) * n_dc + dc], blk)
        blocks.append(jnp.where(ok, blk, NEG_INF))
    for o in range(NA_ROWS):
        for jj in range(kr // 2):
            tab_ref[o, :, jj * 2 * W:(jj + 1) * 2 * W] = jnp.where(
                l_io < W, blocks[o + 2 * jj], blocks[o + 2 * jj + 1])

    def row(r, carry):
        rs = jnp.clip(r - kr // 2, 0, R - kr)
        q = q_ref[0, pl.ds(pl.multiple_of(r * W, W), W), :]
        kw = k_ref[0, pl.ds(pl.multiple_of(rs * W, W), kr * W), :]
        vw = v_ref[0, pl.ds(pl.multiple_of(rs * W, W), kr * W), :]
        s = lax.dot_general(q, kw, _NT, preferred_element_type=F32) * (HEAD_DIM ** -0.5)
        s = s + tab_ref[rs - r + (NA_ROWS - 1)]
        m = jnp.max(s, axis=-1, keepdims=True)
        p = jnp.exp(s - m)
        l = jnp.sum(p, axis=-1, keepdims=True)
        o = jnp.dot(p.astype(BF16), vw, preferred_element_type=F32) / l
        o_ref[pl.ds(pl.multiple_of(r * W, W), W), :] = o.astype(BF16)
        return carry

    lax.fori_loop(0, R, row, 0, unroll=16 if R % 16 == 0 else 1)


def _attn_c(qkv, rpb_flat, B, S, n_heads, head0, heads_c):
    N = B * S
    R = S // GRID_W
    kr = min(NA_ROWS, R)
    assert kr == NA_ROWS and kr % 2 == 0
    return pl.pallas_call(
        functools.partial(_attn_c_kernel, R=R),
        grid=(B, heads_c),
        in_specs=[pl.BlockSpec(memory_space=pltpu.SMEM),
                  pl.BlockSpec((1, S, HEAD_DIM), lambda b, h: (head0 + h, b, 0)),
                  pl.BlockSpec((1, S, HEAD_DIM), lambda b, h: (n_heads + head0 + h, b, 0)),
                  pl.BlockSpec((1, S, HEAD_DIM), lambda b, h: (2 * n_heads + head0 + h, b, 0))],
        out_specs=pl.BlockSpec((S, HEAD_DIM), lambda b, h: (b, h)),
        out_shape=jax.ShapeDtypeStruct((N, heads_c * HEAD_DIM), BF16),
        scratch_shapes=[pltpu.VMEM((NA_ROWS, GRID_W, kr * GRID_W), F32)],
        compiler_params=_params(("arbitrary", "arbitrary"), 32),
        name="attn_nbr",
    )(rpb_flat, qkv, qkv, qkv)


def _out_kernel(oa_ref, ob_ref, oc_ref, wa_ref, wb_ref, wc_ref, x_ref, g1_ref, ng_ref, sc_ref,
                sh_ref, rw_ref, x1_ref, h2_ref, lg_ref):
    acc = jnp.dot(oa_ref[...], wa_ref[...], preferred_element_type=F32)
    acc = acc + jnp.dot(ob_ref[...], wb_ref[...], preferred_element_type=F32)
    acc = acc + jnp.dot(oc_ref[...], wc_ref[...], preferred_element_type=F32)
    x1 = x_ref[...] + g1_ref[0] * acc
    x1_ref[...] = x1
    h2 = (_rms(x1, NORM_EPS) * ng_ref[...]) * (1.0 + sc_ref[0]) + sh_ref[0]
    _store_row_slabs(h2_ref, _pack_pairs(h2))
    lg_ref[...] = lax.dot_general(rw_ref[...], h2, _NT, preferred_element_type=F32,
                                  precision=HIGHEST)


def _out_proj(oa, ob, oc, w_bf, x2d, g1, ng, sc, sh, rw_t, S, tm=512):
    N, D = x2d.shape
    wa, wb, wc = oa.shape[1], ob.shape[1], oc.shape[1]
    E = rw_t.shape[0]
    tpb = S // tm
    const = dict(pipeline_mode=pl.Buffered(1))
    row = lambda i: (i, 0)
    per_b = lambda i: (i // tpb, 0, 0)
    return pl.pallas_call(
        _out_kernel,
        grid=(N // tm,),
        in_specs=[pl.BlockSpec((tm, wa), row), pl.BlockSpec((tm, wb), row), pl.BlockSpec((tm, wc), row),
                  pl.BlockSpec((wa, D), lambda i: (0, 0), **const),
                  pl.BlockSpec((wb, D), lambda i: (wa // wb, 0), **const),
                  pl.BlockSpec((wc, D), lambda i: ((wa + wb) // wc, 0), **const),
                  pl.BlockSpec((tm, D), row),
                  pl.BlockSpec((1, 1, D), per_b),
                  pl.BlockSpec((1, D), lambda i: (0, 0)),
                  pl.BlockSpec((1, 1, D), per_b),
                  pl.BlockSpec((1, 1, D), per_b),
                  pl.BlockSpec((E, D), lambda i: (0, 0), **const)],
        out_specs=[pl.BlockSpec((tm, D), row), pl.BlockSpec((tm * (D // 2 // LANES), LANES), row),
                   pl.BlockSpec((E, tm), lambda i: (0, i))],
        out_shape=[jax.ShapeDtypeStruct((N, D), F32),
                   jax.ShapeDtypeStruct((N * (D // 2 // LANES), LANES), U32),
                   jax.ShapeDtypeStruct((E, N), F32)],
        compiler_params=_params(("arbitrary",), 52),
        name="out_proj",
    )(oa, ob, oc, w_bf, w_bf, w_bf, x2d, g1, ng.reshape(1, D), sc, sh, rw_t)


def _first_argmax(vals, iota, big):
    mx = jnp.max(vals, axis=0, keepdims=True)
    idx = jnp.min(jnp.where(vals == mx, iota, big), axis=0, keepdims=True)
    return mx, idx


def _route_kernel(lg_ref, rb_ref, idx_ref, gate_ref, rank_ref, cnt_ref, tri_ref, carry_ref, *, Tt):
    i = pl.program_id(0)
    E = lg_ref.shape[0]
    gs = E // N_GROUPS

    @pl.when(i == 0)
    def _():
        r = lax.broadcasted_iota(I32, (Tt, Tt), 0)
        c = lax.broadcasted_iota(I32, (Tt, Tt), 1)
        tri_ref[...] = jnp.where(r < c, 1.0, 0.0).astype(BF16)
        carry_ref[...] = jnp.zeros_like(carry_ref)

    scores = jax.nn.sigmoid(lg_ref[...])
    sel = scores + rb_ref[...]
    e_io = lax.broadcasted_iota(I32, (E, Tt), 0).astype(F32)
    s_io = lax.broadcasted_iota(I32, (gs, Tt), 0).astype(F32)
    g_io = lax.broadcasted_iota(I32, (N_GROUPS, Tt), 0).astype(F32)

    grp = []
    for g in range(N_GROUPS):
        blk = sel[g * gs:(g + 1) * gs]
        m1, a1 = _first_argmax(blk, s_io, float(gs))
        m2 = jnp.max(jnp.where(s_io == a1, -jnp.inf, blk), axis=0, keepdims=True)
        grp.append(m1 + m2)
    grp = jnp.concatenate(grp, axis=0)

    gsel = jnp.zeros((N_GROUPS, Tt), F32)
    for _ in range(TOPK_GROUPS):
        _, gi = _first_argmax(grp, g_io, float(N_GROUPS))
        hit = g_io == gi
        gsel = jnp.where(hit, 1.0, gsel)
        grp = jnp.where(hit, -jnp.inf, grp)

    esel = jnp.concatenate(
        [jnp.broadcast_to(gsel[g:g + 1], (gs, Tt)) for g in range(N_GROUPS)], axis=0)
    cur = jnp.where(esel > 0.0, sel, NEG_INF)

    idxs, gates = [], []
    onehot = jnp.zeros((E, Tt), F32)
    for _ in range(TOP_K):
        _, ei = _first_argmax(cur, e_io, float(E))
        hit = e_io == ei
        idxs.append(ei)
        gates.append(jnp.sum(jnp.where(hit, scores, 0.0), axis=0, keepdims=True))
        onehot = jnp.where(hit, 1.0, onehot)
        cur = jnp.where(hit, -jnp.inf, cur)
    gate = jnp.concatenate(gates, axis=0)
    gate = gate / jnp.sum(gate, axis=0, keepdims=True) * ROUTED_SCALE
    idx_ref[...] = jnp.concatenate(idxs, axis=0).astype(I32)
    gate_ref[...] = gate

    oh = onehot.astype(BF16)
    before = jnp.dot(oh, tri_ref[...], preferred_element_type=F32)
    tile_cnt = jnp.dot(oh, jnp.ones((Tt, LANES), BF16), preferred_element_type=F32)
    carry = carry_ref[...]
    rank_mat = before + jnp.concatenate([carry] * (Tt // LANES), axis=1)
    ranks = [jnp.sum(jnp.where(e_io == ei, rank_mat, 0.0), axis=0, keepdims=True) for ei in idxs]
    rank_ref[...] = jnp.concatenate(ranks, axis=0).astype(I32)
    carry_ref[...] = carry + tile_cnt
    cnt_ref[...] = (carry + tile_cnt).astype(I32)


def _route(logits_t, router_bias, Tt=512):
    E, N = logits_t.shape
    tok = lambda i: (0, i)
    return pl.pallas_call(
        functools.partial(_route_kernel, Tt=Tt),
        grid=(N // Tt,),
        in_specs=[pl.BlockSpec((E, Tt), tok), pl.BlockSpec((E, 1), lambda i: (0, 0))],
        out_specs=[pl.BlockSpec((TOP_K, Tt), tok), pl.BlockSpec((TOP_K, Tt), tok),
                   pl.BlockSpec((TOP_K, Tt), tok), pl.BlockSpec((E, LANES), lambda i: (0, 0))],
        out_shape=[jax.ShapeDtypeStruct((TOP_K, N), I32), jax.ShapeDtypeStruct((TOP_K, N), F32),
                   jax.ShapeDtypeStruct((TOP_K, N), I32), jax.ShapeDtypeStruct((E, LANES), I32)],
        scratch_shapes=[pltpu.VMEM((Tt, Tt), BF16), pltpu.VMEM((E, LANES), F32)],
        compiler_params=_params(("arbitrary",), 32),
        name="route_topk",
    )(logits_t, router_bias.reshape(E, 1))


def _dest_kernel(pstart_ref, idx_ref, rank_ref, dest_ref, *, G):
    idx = idx_ref[...]
    base = jnp.zeros(idx.shape, I32)
    for e in range(pstart_ref.shape[0]):
        base = jnp.where(idx == e, pstart_ref[e], base)
    dest_ref[...] = (base + rank_ref[...]) * G


def _dest(pstart, idx, rank, G):
    return pl.pallas_call(
        functools.partial(_dest_kernel, G=G),
        in_specs=[pl.BlockSpec(memory_space=pltpu.SMEM), pl.BlockSpec(memory_space=pltpu.VMEM),
                  pl.BlockSpec(memory_space=pltpu.VMEM)],
        out_specs=pl.BlockSpec(memory_space=pltpu.VMEM),
        out_shape=jax.ShapeDtypeStruct(idx.shape, I32),
        name="route_dest",
    )(pstart, idx, rank)


def _dispatch_kernel(dest_hbm, h_ref, xs_hbm, dest_smem, idx_sem, row_sem, *, Td, G):
    i = pl.program_id(0)
    n = Td * TOP_K
    cp = pltpu.make_async_copy(dest_hbm.at[pl.ds(i * n, n)], dest_smem, idx_sem)
    cp.start()
    cp.wait()

    def row_copy(t, k):
        src = h_ref.at[pl.ds(pl.multiple_of(t * G, G), G)]
        dst = xs_hbm.at[pl.ds(pl.multiple_of(dest_smem[t * TOP_K + k], G), G)]
        return pltpu.make_async_copy(src, dst, row_sem)

    def issue(t, carry):
        for k in range(TOP_K):
            row_copy(t, k).start(priority=k % 2)
        return carry

    lax.fori_loop(0, Td, issue, 0)

    def drain(t, carry):
        for k in range(TOP_K):
            row_copy(t, k).wait()
        return carry

    lax.fori_loop(0, Td, drain, 0)


def _dispatch(dest_flat, h2, P, G, Td=512):
    N = h2.shape[0] // G
    return pl.pallas_call(
        functools.partial(_dispatch_kernel, Td=Td, G=G),
        grid=(N // Td,),
        in_specs=[pl.BlockSpec(memory_space=pl.ANY), pl.BlockSpec((Td * G, LANES), lambda i: (i, 0))],
        out_specs=pl.BlockSpec(memory_space=pl.ANY),
        out_shape=jax.ShapeDtypeStruct((P * G, LANES), h2.dtype),
        scratch_shapes=[pltpu.SMEM((Td * TOP_K,), I32), pltpu.SemaphoreType.DMA,
                        pltpu.SemaphoreType.DMA],
        compiler_params=_params(("arbitrary",), 32),
        name="moe_dispatch",
    )(dest_flat, h2)


def _moe_kernel(eo_ref, ps_ref, cnt_ref, xs_hbm, wg_ref, wu_ref, wd_ref, ys_hbm, wg_s, wu_s, wd_s,
                xbuf, ybuf, sem_in, sem_out, *, G):
    e = pl.program_id(0)
    T = MOE_BLOCK
    cnt = cnt_ref[e]
    row0 = ps_ref[e]
    nb = lax.shift_right_logical(cnt + (T - 1), T.bit_length() - 1)

    def block_rows(b):
        return pl.ds(pl.multiple_of((row0 + b * T) * G, T * G), T * G)

    def load(b, slot):
        return pltpu.make_async_copy(xs_hbm.at[block_rows(b)], xbuf.at[slot], sem_in.at[slot])

    def store(b, slot):
        return pltpu.make_async_copy(ybuf.at[slot], ys_hbm.at[block_rows(b)], sem_out.at[slot])

    @pl.when(nb > 0)
    def _():
        load(0, 0).start()
        wg_s[...] = wg_ref[0].astype(BF16)
        wu_s[...] = wu_ref[0].astype(BF16)
        wd_s[...] = wd_ref[0].astype(BF16)

    def body(b, carry):
        slot = b & 1
        load(b, slot).wait()

        @pl.when(b + 1 < nb)
        def _():
            load(b + 1, 1 - slot).start()

        @pl.when(b >= 2)
        def _():
            store(b - 2, slot).wait()

        xb = xbuf.at[slot]
        valid = lax.broadcasted_iota(I32, (T, LANES), 0) < cnt - b * T
        his, los = [], []
        for c in range(G):
            w = jnp.where(valid, _load_row_slab(xb, c, T, G), jnp.uint32(0))
            hi, lo = _unpack_pairs(w)
            his.append(hi.astype(BF16))
            los.append(lo.astype(BF16))
        x = jnp.concatenate(his + los, axis=1)
        g = jnp.dot(x, wg_s[...], preferred_element_type=F32)
        u = jnp.dot(x, wu_s[...], preferred_element_type=F32)
        a = (g * jax.nn.sigmoid(g) * u).astype(BF16)
        _store_row_slabs(ybuf.at[slot], _pack_pairs(jnp.dot(a, wd_s[...], preferred_element_type=F32)))
        store(b, slot).start()
        return carry

    lax.fori_loop(0, nb, body, 0)

    @pl.when(nb >= 2)
    def _():
        store(nb - 2, nb & 1).wait()

    @pl.when(nb >= 1)
    def _():
        store(nb - 1, (nb - 1) & 1).wait()


def _moe(e_off, pstart, counts, xs, wg, wu, wd):
    _, D, F = wg.shape
    E = counts.shape[0]
    G = D // 2 // LANES
    return pl.pallas_call(
        functools.partial(_moe_kernel, G=G),
        grid_spec=pltpu.PrefetchScalarGridSpec(
            num_scalar_prefetch=3,
            grid=(E,),
            in_specs=[pl.BlockSpec(memory_space=pl.ANY),
                      pl.BlockSpec((1, D, F), lambda e, eo, ps, cn: (eo[0] + e, 0, 0)),
                      pl.BlockSpec((1, D, F), lambda e, eo, ps, cn: (eo[0] + e, 0, 0)),
                      pl.BlockSpec((1, F, D), lambda e, eo, ps, cn: (eo[0] + e, 0, 0))],
            out_specs=pl.BlockSpec(memory_space=pl.ANY),
            scratch_shapes=[pltpu.VMEM((D, F), BF16), pltpu.VMEM((D, F), BF16),
                            pltpu.VMEM((F, D), BF16),
                            pltpu.VMEM((2, MOE_BLOCK * G, LANES), U32),
                            pltpu.VMEM((2, MOE_BLOCK * G, LANES), U32),
                            pltpu.SemaphoreType.DMA((2,)), pltpu.SemaphoreType.DMA((2,))]),
        out_shape=jax.ShapeDtypeStruct(xs.shape, U32),
        compiler_params=_params(("arbitrary",), 52),
        name="moe_experts",
    )(e_off, pstart, counts, xs, wg, wu, wd)


def _combine_kernel(dest_hbm, ys_hbm, gate_ref, h_ref, x_ref, g2_ref, sg_ref, su_ref, sd_ref, fg_ref,
                    o_ref, dest0, dest1, ybuf0, ybuf1, idx_sem, row_sem, *, Tc, G, final_norm):
    j = pl.program_id(0)
    n = Tc * TOP_K
    slots = ((dest0, ybuf0), (dest1, ybuf1))

    def row_copy(slot, t, k):
        dest, ybuf = slots[slot]
        src = ys_hbm.at[pl.ds(pl.multiple_of(dest[t * TOP_K + k], G), G)]
        dst = ybuf.at[k, pl.ds(pl.multiple_of(t * G, G), G)]
        return pltpu.make_async_copy(src, dst, row_sem.at[slot])

    def request(tile, slot):
        cp = pltpu.make_async_copy(dest_hbm.at[pl.ds(tile * n, n)], slots[slot][0],
                                   idx_sem.at[slot])
        cp.start()
        cp.wait()

        def issue(t, carry):
            for k in range(TOP_K):
                row_copy(slot, t, k).start(priority=k % 2)
            return carry

        lax.fori_loop(0, Tc, issue, 0)

    def reduce(slot):
        rows = slice(slot * Tc, (slot + 1) * Tc)
        h_tile = h_ref.at[pl.ds(slot * Tc * G, Tc * G)]
        halves = [_unpack_pairs(_load_row_slab(h_tile, c, Tc, G)) for c in range(G)]
        h = jnp.concatenate([p[0].astype(BF16) for p in halves]
                            + [p[1].astype(BF16) for p in halves], axis=1)
        g = jnp.dot(h, sg_ref[...], preferred_element_type=F32)
        u = jnp.dot(h, su_ref[...], preferred_element_type=F32)
        a = (g * jax.nn.sigmoid(g) * u).astype(BF16)
        f = jnp.dot(a, sd_ref[...], preferred_element_type=F32)

        def drain(t, carry):
            for k in range(TOP_K):
                row_copy(slot, t, k).wait()
            return carry

        lax.fori_loop(0, Tc, drain, 0)
        gate = gate_ref[rows, :]
        half = f.shape[1] // 2
        his, los = [], []
        for c in range(G):
            f_hi = f[:, c * LANES:(c + 1) * LANES]
            f_lo = f[:, half + c * LANES:half + (c + 1) * LANES]
            for k in range(TOP_K):
                y_hi, y_lo = _unpack_pairs(_load_row_slab(slots[slot][1].at[k], c, Tc, G))
                f_hi = f_hi + gate[:, k:k + 1] * y_hi
                f_lo = f_lo + gate[:, k:k + 1] * y_lo
            his.append(f_hi)
            los.append(f_lo)
        x2 = x_ref[rows, :] + g2_ref[0] * jnp.concatenate(his + los, axis=1)
        if final_norm:
            x2 = _rms(x2, NORM_EPS) * fg_ref[...]
        o_ref[rows, :] = x2

    @pl.when(j == 0)
    def _():
        request(0, 0)

    request(2 * j + 1, 1)
    reduce(0)

    @pl.when(j + 1 < pl.num_programs(0))
    def _():
        request(2 * j + 2, 0)

    reduce(1)


def _combine(dest_flat, ys, gate_t, h2, x1, g2, sg, su, sd, final_g, S, final_norm, Tc=256):
    N, D = x1.shape
    F = sg.shape[1]
    G = D // 2 // LANES
    tb = 2 * Tc
    tpb = S // tb
    row = lambda j: (j, 0)
    const = dict(pipeline_mode=pl.Buffered(1))
    return pl.pallas_call(
        functools.partial(_combine_kernel, Tc=Tc, G=G, final_norm=final_norm),
        grid=(N // tb,),
        in_specs=[pl.BlockSpec(memory_space=pl.ANY), pl.BlockSpec(memory_space=pl.ANY),
                  pl.BlockSpec((tb, TOP_K), row),
                  pl.BlockSpec((tb * G, LANES), row), pl.BlockSpec((tb, D), row),
                  pl.BlockSpec((1, 1, D), lambda j: (j // tpb, 0, 0)),
                  pl.BlockSpec((D, F), lambda j: (0, 0), **const),
                  pl.BlockSpec((D, F), lambda j: (0, 0), **const),
                  pl.BlockSpec((F, D), lambda j: (0, 0), **const),
                  pl.BlockSpec((1, D), lambda j: (0, 0))],
        out_specs=pl.BlockSpec((tb, D), row),
        out_shape=jax.ShapeDtypeStruct((N, D), F32),
        scratch_shapes=[pltpu.SMEM((Tc * TOP_K,), I32), pltpu.SMEM((Tc * TOP_K,), I32),
                        pltpu.VMEM((TOP_K, Tc * G, LANES), U32),
                        pltpu.VMEM((TOP_K, Tc * G, LANES), U32),
                        pltpu.SemaphoreType.DMA((2,)), pltpu.SemaphoreType.DMA((2,))],
        compiler_params=_params(("arbitrary",), 52),
        name="moe_combine",
    )(dest_flat, ys, gate_t, h2, x1, g2, sg, su, sd, final_g.reshape(1, D))


def _moe_ffn(h2, logits_t, x1, g2, router_bias, wg, wu, wd, e_off, sg, su, sd, final_g, S,
             final_norm):
    N, D = x1.shape
    G = D // 2 // LANES
    E = logits_t.shape[0]
    idx, gate, rank, cnt = _route(logits_t, router_bias)
    counts = cnt[:, 0]
    padded = (counts + MOE_BLOCK - 1) // MOE_BLOCK * MOE_BLOCK
    pstart = (jnp.cumsum(padded) - padded).astype(I32)
    n_blocks = -(-(N * TOP_K + E * (MOE_BLOCK - 1)) // MOE_BLOCK)

    dest = _dest(pstart, idx, rank, G)
    dest_flat = dest.T.reshape(N * TOP_K)
    xs = _dispatch(dest_flat, h2, n_blocks * MOE_BLOCK, G)
    ys = _moe(jnp.full((1,), e_off, I32), pstart, counts.astype(I32), xs, wg, wu, wd)
    return _combine(dest_flat, ys, gate.T, h2, x1, g2, sg, su, sd, final_g, S, final_norm)


def kernel(x, c, ada_w, ada_b, norm_mix_g, norm_ffn_g, w_in, lambda_q1, lambda_k1, lambda_q2,
           lambda_k2, subln_g, rpb, w_out, router_w, router_bias, exp_w_gate, exp_w_up,
           exp_w_down, sh_w_gate, sh_w_up, sh_w_down, final_g):
    B, S, D = x.shape
    L = ada_w.shape[0]
    N = B * S
    n_heads = w_in.shape[2] // (3 * HEAD_DIM)
    heads_a = n_heads // HEADS_A_FRAC
    heads_b = n_heads // HEADS_B_FRAC
    heads_c = n_heads - heads_a - heads_b
    slopes = _alibi_slopes(heads_a + heads_b)
    slopes_a = jnp.asarray(slopes[:heads_a], F32)
    slopes_b = jnp.asarray(slopes[heads_a:], F32)

    E, F = exp_w_gate.shape[1], exp_w_gate.shape[3]
    wg_all = exp_w_gate.reshape(L * E, D, F)
    wu_all = exp_w_up.reshape(L * E, D, F)
    wd_all = exp_w_down.reshape(L * E, F, D)

    mod = _ada(c, ada_w, ada_b).reshape(L, B, 6, 1, D)
    xf = x.reshape(N, D)
    for l in range(L):
        sh1, sc1, g1, sh2, sc2, g2 = (mod[l, :, j] for j in range(6))
        qkv = _qkv(xf, norm_mix_g[l], sc1, sh1, w_in[l].astype(BF16), S)
        oa = _attn_a(qkv, slopes_a, B, S, n_heads, heads_a)
        lambda_init = 0.8 - 0.6 * math.exp(-0.3 * l)
        lam_vecs = jnp.stack([lambda_q1[l], lambda_k1[l], lambda_q2[l], lambda_k2[l]]).astype(F32)
        ob = _attn_b(qkv, slopes_b, lam_vecs, subln_g[l], B, S, n_heads, heads_a, heads_b, lambda_init)
        oc = _attn_c(qkv, rpb[l].reshape(-1), B, S, n_heads, heads_a + heads_b, heads_c)
        x1, h2, logits_t = _out_proj(oa, ob, oc, w_out[l].astype(BF16), xf, g1, norm_ffn_g[l],
                                     sc2, sh2, router_w[l].T, S)
        xf = _moe_ffn(h2, logits_t, x1, g2, router_bias[l], wg_all, wu_all, wd_all, l * E,
                      sh_w_gate[l].astype(BF16), sh_w_up[l].astype(BF16),
                      sh_w_down[l].astype(BF16), final_g, S, l == L - 1)
    return xf.reshape(B, S, D)
```

```python
import functools
import math

import jax
import jax.numpy as jnp
from jax import lax
from jax.experimental import pallas as pl
from jax.experimental.pallas import tpu as pltpu

F32 = jnp.float32
BF16 = jnp.bfloat16
I32 = jnp.int32
U32 = jnp.uint32
HIGHEST = lax.Precision.HIGHEST
_NT = (((1,), (1,)), ((), ()))

HEAD_DIM = 128
HEADS_A_FRAC, HEADS_B_FRAC = 2, 4
DILATED_PATTERNS = ((128, 1), (512, 4), (2048, 16))
GRID_W = 64
NA_ROWS = 8
NA_COLS = 16
N_GROUPS = 8
TOPK_GROUPS = 4
TOP_K = 8
ROUTED_SCALE = 2.5
NORM_EPS = 1e-6
SUBLN_EPS = 1e-5
NEG_INF = -1e30
LOG2E = math.log2(math.e)

LANES = 128
VMEM_BYTES_V7X = 64 << 20

MOE_BLOCK = 256


def _params(semantics, vmem_mib):
    return pltpu.CompilerParams(dimension_semantics=semantics,
                                vmem_limit_bytes=min(vmem_mib << 20, VMEM_BYTES_V7X - (4 << 20)))


def _alibi_slopes(n):
    def pow2(m):
        start = 2.0 ** (-8.0 / m)
        return [start ** (i + 1) for i in range(m)]
    p = 2 ** int(math.floor(math.log2(n)))
    return pow2(p) + pow2(2 * p)[0::2][: n - p]


def _rms(x, eps):
    return x * lax.rsqrt(jnp.mean(x * x, axis=-1, keepdims=True) + eps)


def _pack_pairs(x):
    half = x.shape[1] // 2
    hi = lax.bitcast_convert_type(x[:, :half].astype(BF16).astype(F32), U32)
    lo = lax.bitcast_convert_type(x[:, half:].astype(BF16).astype(F32), U32)
    return hi | (lo >> 16)


def _unpack_pairs(w):
    hi = lax.bitcast_convert_type(w & jnp.uint32(0xFFFF0000), F32)
    lo = lax.bitcast_convert_type(w << 16, F32)
    return hi, lo


def _store_row_slabs(ref, packed):
    T, W = packed.shape
    for c in range(W // LANES):
        ref[pl.ds(c, T, stride=W // LANES), :] = packed[:, c * LANES:(c + 1) * LANES]


def _load_row_slab(ref, c, T, G):
    return ref[pl.ds(c, T, stride=G), :]


def _ada_kernel(c_ref, w_ref, b_ref, o_ref):
    c = c_ref[...]
    sc = c * jax.nn.sigmoid(c)
    o_ref[0] = jnp.dot(sc, w_ref[0], preferred_element_type=F32, precision=HIGHEST) + b_ref[0]


def _ada(c, ada_w, ada_b):
    B, D = c.shape
    L, _, D6 = ada_w.shape
    rows = 8
    cp = jnp.zeros((rows, D), F32).at[:B].set(c)
    tn = 1024
    out = pl.pallas_call(
        _ada_kernel,
        grid=(L, D6 // tn),
        in_specs=[pl.BlockSpec((rows, D), lambda l, j: (0, 0)),
                  pl.BlockSpec((1, D, tn), lambda l, j: (l, 0, j)),
                  pl.BlockSpec((1, 1, tn), lambda l, j: (l, 0, j))],
        out_specs=pl.BlockSpec((1, rows, tn), lambda l, j: (l, 0, j)),
        out_shape=jax.ShapeDtypeStruct((L, rows, D6), F32),
        compiler_params=_params(("arbitrary", "arbitrary"), 32),
        name="ada_mod",
    )(cp, ada_w, ada_b.reshape(L, 1, D6))
    return out[:, :B]


def _qkv_kernel(x_ref, g_ref, sc_ref, sh_ref, w_ref, o_ref, h_scr, *, heads_per_step):
    @pl.when(pl.program_id(1) == 0)
    def _():
        y = _rms(x_ref[...], NORM_EPS) * g_ref[...]
        h_scr[...] = (y * (1.0 + sc_ref[0]) + sh_ref[0]).astype(BF16)

    res = jnp.dot(h_scr[...], w_ref[...], preferred_element_type=F32)
    for hh in range(heads_per_step):
        o_ref[hh] = res[:, hh * HEAD_DIM:(hh + 1) * HEAD_DIM].astype(BF16)


def _qkv(x2d, g, sc, sh, w_bf, S, tm=512, tn=2048):
    N, D = x2d.shape
    W3 = w_bf.shape[1]
    hps = tn // HEAD_DIM
    tpb = S // tm
    return pl.pallas_call(
        functools.partial(_qkv_kernel, heads_per_step=hps),
        grid=(N // tm, W3 // tn),
        in_specs=[pl.BlockSpec((tm, D), lambda i, j: (i, 0)),
                  pl.BlockSpec((1, D), lambda i, j: (0, 0)),
                  pl.BlockSpec((1, 1, D), lambda i, j: (i // tpb, 0, 0)),
                  pl.BlockSpec((1, 1, D), lambda i, j: (i // tpb, 0, 0)),
                  pl.BlockSpec((D, tn), lambda i, j: (0, j))],
        out_specs=pl.BlockSpec((hps, tm, HEAD_DIM), lambda i, j: (j, i, 0)),
        out_shape=jax.ShapeDtypeStruct((W3 // HEAD_DIM, N, HEAD_DIM), BF16),
        scratch_shapes=[pltpu.VMEM((tm, D), BF16)],
        compiler_params=_params(("arbitrary", "arbitrary"), 48),
        name="qkv_proj",
    )(x2d, g.reshape(1, D), sc, sh, w_bf)


def _col_to_row(col, eye):
    return jnp.sum(jnp.where(eye, col, 0.0), axis=0, keepdims=True)


def _attn_a_kernel(slopes_ref, q_ref, k_ref, v_ref, o_ref, lse_ref, nat, cq, ck, cv, onat, *,
                   L, dil, half, Tq):
    h = pl.program_id(1)
    wk = 2 * Tq
    n_tiles = L // Tq
    c_dist = LOG2E * slopes_ref[h] * dil
    a = lax.broadcasted_iota(I32, (Tq, wk), 0)
    u = lax.broadcasted_iota(I32, (Tq, wk), 1)
    eye = lax.broadcasted_iota(I32, (Tq, Tq), 0) == lax.broadcasted_iota(I32, (Tq, Tq), 1)

    def table(off):
        d = u - a - off
        ad = jnp.maximum(d, -d)
        return jnp.where(ad <= half, -c_dist * ad.astype(F32), NEG_INF)

    tabs = {off: table(off) for off in sorted({0, half, wk - Tq})}

    q_scale = LOG2E * HEAD_DIM ** -0.5
    if dil > 1:
        for src, dst, scale in ((q_ref, cq, q_scale), (k_ref, ck, None), (v_ref, cv, None)):
            x = src[0].astype(F32)
            nat[...] = x if scale is None else x * scale
            for r in range(dil):
                dst[r * L:(r + 1) * L, :] = nat[pl.ds(r, L, stride=dil), :].astype(BF16)
        kc, vc = ck, cv
    else:
        kc, vc = k_ref.at[0], v_ref.at[0]

    for r in range(dil):
        for j in range(n_tiles):
            start = min(max(j * Tq - half, 0), L - wk)
            rows = slice(r * L + j * Tq, r * L + (j + 1) * Tq)
            q = cq[rows, :] if dil > 1 else (q_ref[0, rows, :].astype(F32) * q_scale).astype(BF16)
            kw = kc[r * L + start:r * L + start + wk, :]
            vw = vc[r * L + start:r * L + start + wk, :]
            s = lax.dot_general(q, kw, _NT, preferred_element_type=F32) + tabs[j * Tq - start]
            m = jnp.max(s, axis=-1, keepdims=True)
            p = jnp.exp2(s - m)
            l = jnp.sum(p, axis=-1, keepdims=True)
            o = jnp.dot(p.astype(BF16), vw, preferred_element_type=F32) / l
            if dil > 1:
                onat[pl.ds(r + j * Tq * dil, Tq, stride=dil), :] = o
            else:
                o_ref[0, rows, :] = o.astype(BF16)
            lse_ref[0, 0, r:r + 1, j * Tq:(j + 1) * Tq] = _col_to_row(m + jnp.log2(l), eye)
    if dil > 1:
        o_ref[0] = onat[...].astype(BF16)


def _attn_a_pattern(qkv, slopes, B, S, n_heads, heads_a, window, dil, Tq=128):
    N = B * S
    L = S // dil
    half = window // (2 * dil)
    assert L % Tq == 0 and L >= 2 * Tq and 2 * half <= Tq
    blk = (1, S, HEAD_DIM)
    o, lse = pl.pallas_call(
        functools.partial(_attn_a_kernel, L=L, dil=dil, half=half, Tq=Tq),
        grid=(B, heads_a),
        in_specs=[pl.BlockSpec(memory_space=pltpu.SMEM),
                  pl.BlockSpec(blk, lambda b, h: (h, b, 0)),
                  pl.BlockSpec(blk, lambda b, h: (n_heads + h, b, 0)),
                  pl.BlockSpec(blk, lambda b, h: (2 * n_heads + h, b, 0))],
        out_specs=[pl.BlockSpec(blk, lambda b, h: (h, b, 0)),
                   pl.BlockSpec((1, 1, dil, L), lambda b, h: (b, h, 0, 0))],
        out_shape=[jax.ShapeDtypeStruct((heads_a, N, HEAD_DIM), BF16),
                   jax.ShapeDtypeStruct((B, heads_a, dil, L), F32)],
        scratch_shapes=[pltpu.VMEM((S, HEAD_DIM), F32)] + [pltpu.VMEM((S, HEAD_DIM), BF16)] * 3
        + [pltpu.VMEM((S, HEAD_DIM), F32)],
        compiler_params=_params(("arbitrary", "arbitrary"), 32),
        name=f"attn_dilated_d{dil}",
    )(slopes, qkv, qkv, qkv)
    lse_nat = lse.transpose(1, 0, 3, 2).reshape(heads_a, N)
    return o, lse_nat


def _mix_a_kernel(*refs, n_pat, tm):
    o_refs, lse_refs, out_ref = refs[:n_pat], refs[n_pat:2 * n_pat], refs[2 * n_pat]
    heads = lse_refs[0].shape[0]
    lses = [r[...] for r in lse_refs]
    mx = functools.reduce(jnp.maximum, lses)
    es = [jnp.exp2(x - mx) for x in lses]
    den = functools.reduce(lambda x, y: x + y, es)
    ws = [e / den for e in es]
    pad = jnp.zeros((LANES - n_pat * heads, LANES), F32)
    for c in range(tm // LANES):
        cols = slice(c * LANES, (c + 1) * LANES)
        w_rows = jnp.concatenate([w[:, cols] for w in ws] + [pad], axis=0)
        w_cols = w_rows.T
        for hh in range(heads):
            acc = jnp.zeros((LANES, HEAD_DIM), F32)
            for p in range(n_pat):
                wc = w_cols[:, p * heads + hh:p * heads + hh + 1]
                acc = acc + wc * o_refs[p][hh, cols, :].astype(F32)
            out_ref[cols, hh * HEAD_DIM:(hh + 1) * HEAD_DIM] = acc.astype(BF16)


def _mix_a(os, lses, tm=512):
    heads, N, _ = os[0].shape
    n_pat = len(os)
    assert n_pat * heads <= LANES
    return pl.pallas_call(
        functools.partial(_mix_a_kernel, n_pat=n_pat, tm=tm),
        grid=(N // tm,),
        in_specs=[pl.BlockSpec((heads, tm, HEAD_DIM), lambda i: (0, i, 0))] * n_pat
        + [pl.BlockSpec((heads, tm), lambda i: (0, i))] * n_pat,
        out_specs=pl.BlockSpec((tm, heads * HEAD_DIM), lambda i: (i, 0)),
        out_shape=jax.ShapeDtypeStruct((N, heads * HEAD_DIM), BF16),
        compiler_params=_params(("arbitrary",), 32),
        name="attn_dilated_mix",
    )(*os, *lses)


def _attn_a(qkv, slopes, B, S, n_heads, heads_a):
    parts = [_attn_a_pattern(qkv, slopes, B, S, n_heads, heads_a, w, d) for w, d in DILATED_PATTERNS]
    return _mix_a([p[0] for p in parts], [p[1] for p in parts])


def _attn_b_kernel(slopes_ref, lam_ref, q_ref, k_ref, v_ref, g_ref, o_ref, tab_ref, kmax_ref,
                   m_scr, l_scr, acc_scr, *, S, Tq, Tk, lambda_init):
    h = pl.program_id(1)
    i = pl.program_id(2)
    dh = HEAD_DIM // 2
    n_chunks = S // Tk
    c_dist = LOG2E * slopes_ref[h]

    def sub_norm(x):
        sq = x * x
        return jnp.sqrt(jnp.maximum(jnp.sum(sq[:, :dh], axis=-1, keepdims=True),
                                    jnp.sum(sq[:, dh:], axis=-1, keepdims=True)))

    @pl.when(i == 0)
    def _():
        shape = (Tq, 2 * S - Tq)
        a = lax.broadcasted_iota(I32, shape, 0)
        u = lax.broadcasted_iota(I32, shape, 1)
        d = a - u + (S - Tq)
        tab_ref[...] = -c_dist * jnp.maximum(d, -d).astype(F32)
        kn = sub_norm(k_ref[0].astype(F32))
        for j in range(n_chunks):
            kmax_ref[j] = jnp.max(kn[j * Tk:(j + 1) * Tk])

    lv = lam_ref[...]
    lam = (jnp.exp(jnp.sum(lv[0:1] * lv[1:2], axis=-1, keepdims=True))
           - jnp.exp(jnp.sum(lv[2:3] * lv[3:4], axis=-1, keepdims=True)) + lambda_init)

    t0 = i * Tq
    qf = q_ref[0].astype(F32) * (LOG2E * dh ** -0.5)
    q = qf.astype(BF16)
    qmax = jnp.max(sub_norm(q.astype(F32)))

    def chunk(kstart, first):
        kc = k_ref[0, pl.ds(kstart, Tk), :]
        vc = v_ref[0, pl.ds(kstart, Tk), :]
        bias = tab_ref[:, pl.ds(pl.multiple_of(S - Tq - t0 + kstart, LANES), Tk)]
        s = jnp.concatenate(
            [lax.dot_general(q[:, c * dh:(c + 1) * dh], kc[:, c * dh:(c + 1) * dh], _NT,
                             preferred_element_type=F32) + bias for c in range(2)], axis=0)
        mc = jnp.max(s, axis=-1, keepdims=True)
        if first:
            m_new = mc
        else:
            m_old = m_scr[...]
            m_new = jnp.maximum(m_old, mc)
            alpha = jnp.exp2(m_old - m_new)
        p = jnp.exp2(s - m_new)
        ls = jnp.sum(p, axis=-1, keepdims=True)
        pv = jnp.dot(p.astype(BF16), vc, preferred_element_type=F32)
        m_scr[...] = m_new
        l_scr[...] = ls if first else alpha * l_scr[...] + ls
        acc_scr[...] = pv if first else alpha * acc_scr[...] + pv
        return m_new

    jd = t0 // Tk
    m_diag = chunk(pl.multiple_of(jd * Tk, Tk), True)
    floor = jnp.min(m_diag) - 160.0
    for j in range(n_chunks):
        gap = jnp.maximum(jnp.maximum(j * Tk - (t0 + Tq - 1), t0 - ((j + 1) * Tk - 1)), 0)
        bound = 1.001 * qmax * kmax_ref[j] + 0.01 - c_dist * gap.astype(F32)

        @pl.when((j != jd) & (bound > floor))
        def _():
            chunk(j * Tk, False)

    acc = acc_scr[...]
    l = l_scr[...]
    o = acc[:Tq] / l[:Tq] - acc[Tq:] * (lam / l[Tq:])
    o = _rms(o, SUBLN_EPS) * g_ref[...] * (1.0 - lambda_init)
    o_ref[...] = o.astype(BF16)


def _attn_b(qkv, slopes, lam_vecs, subln_g, B, S, n_heads, head0, heads_b, lambda_init, Tq=256,
            Tk=1024):
    N = B * S
    nq = S // Tq
    Tk = min(Tk, S)
    return pl.pallas_call(
        functools.partial(_attn_b_kernel, S=S, Tq=Tq, Tk=Tk, lambda_init=lambda_init),
        grid=(B, heads_b, nq),
        in_specs=[pl.BlockSpec(memory_space=pltpu.SMEM),
                  pl.BlockSpec((4, HEAD_DIM // 2), lambda b, h, i: (0, 0)),
                  pl.BlockSpec((1, Tq, HEAD_DIM), lambda b, h, i: (head0 + h, b * nq + i, 0)),
                  pl.BlockSpec((1, S, HEAD_DIM), lambda b, h, i: (n_heads + head0 + h, b, 0)),
                  pl.BlockSpec((1, S, HEAD_DIM), lambda b, h, i: (2 * n_heads + head0 + h, b, 0)),
                  pl.BlockSpec((1, HEAD_DIM), lambda b, h, i: (0, 0))],
        out_specs=pl.BlockSpec((Tq, HEAD_DIM), lambda b, h, i: (b * nq + i, h)),
        out_shape=jax.ShapeDtypeStruct((N, heads_b * HEAD_DIM), BF16),
        scratch_shapes=[pltpu.VMEM((Tq, 2 * S - Tq), F32), pltpu.SMEM((S // Tk,), F32),
                        pltpu.VMEM((2 * Tq, 1), F32), pltpu.VMEM((2 * Tq, 1), F32),
                        pltpu.VMEM((2 * Tq, HEAD_DIM), F32)],
        compiler_params=_params(("arbitrary", "arbitrary", "arbitrary"), 48),
        name="attn_diff",
    )(slopes, lam_vecs, qkv, qkv, qkv, subln_g.reshape(1, HEAD_DIM))


def _attn_c_kernel(rpb_ref, q_ref, k_ref, v_ref, o_ref, tab_ref, *, R):
    h = pl.program_id(1)
    W = GRID_W
    kr = min(NA_ROWS, R)
    n_dr = 2 * NA_ROWS - 1
    n_dc = 2 * NA_COLS - 1

    c_io = lax.broadcasted_iota(I32, (W, 2 * W), 0)
    l_io = lax.broadcasted_iota(I32, (W, 2 * W), 1)
    cp = l_io & (W - 1)
    dcm = cp - c_io + (NA_COLS - 1)
    cstart = jnp.clip(c_io - NA_COLS // 2, 0, W - NA_COLS)
    ok = (cp >= cstart) & (cp < cstart + NA_COLS)
    blocks = []
    for dr in range(n_dr):
        blk = jnp.zeros((W, 2 * W), F32)
        for dc in range(n_dc):
            blk = jnp.where(dcm == dc, rpb_ref[(h * n_dr + dr) * n_dc + dc], blk)
        blocks.append(jnp.where(ok, blk, NEG_INF))
    for o in range(NA_ROWS):
        for jj in range(kr // 2):
            tab_ref[o, :, jj * 2 * W:(jj + 1) * 2 * W] = jnp.where(
                l_io < W, blocks[o + 2 * jj], blocks[o + 2 * jj + 1])

    def row(r, carry):
        rs = jnp.clip(r - kr // 2, 0, R - kr)
        q = q_ref[0, pl.ds(pl.multiple_of(r * W, W), W), :]
        kw = k_ref[0, pl.ds(pl.multiple_of(rs * W, W), kr * W), :]
        vw = v_ref[0, pl.ds(pl.multiple_of(rs * W, W), kr * W), :]
        s = lax.dot_general(q, kw, _NT, preferred_element_type=F32) * (HEAD_DIM ** -0.5)
        s = s + tab_ref[rs - r + (NA_ROWS - 1)]
        m = jnp.max(s, axis=-1, keepdims=True)
        p = jnp.exp(s - m)
        l = jnp.sum(p, axis=-1, keepdims=True)
        o = jnp.dot(p.astype(BF16), vw, preferred_element_type=F32) / l
        o_ref[pl.ds(pl.multiple_of(r * W, W), W), :] = o.astype(BF16)
        return carry

    lax.fori_loop(0, R, row, 0, unroll=16 if R % 16 == 0 else 1)


def _attn_c(qkv, rpb_flat, B, S, n_heads, head0, heads_c):
    N = B * S
    R = S // GRID_W
    kr = min(NA_ROWS, R)
    assert kr == NA_ROWS and kr % 2 == 0
    return pl.pallas_call(
        functools.partial(_attn_c_kernel, R=R),
        grid=(B, heads_c),
        in_specs=[pl.BlockSpec(memory_space=pltpu.SMEM),
                  pl.BlockSpec((1, S, HEAD_DIM), lambda b, h: (head0 + h, b, 0)),
                  pl.BlockSpec((1, S, HEAD_DIM), lambda b, h: (n_heads + head0 + h, b, 0)),
                  pl.BlockSpec((1, S, HEAD_DIM), lambda b, h: (2 * n_heads + head0 + h, b, 0))],
        out_specs=pl.BlockSpec((S, HEAD_DIM), lambda b, h: (b, h)),
        out_shape=jax.ShapeDtypeStruct((N, heads_c * HEAD_DIM), BF16),
        scratch_shapes=[pltpu.VMEM((NA_ROWS, GRID_W, kr * GRID_W), F32)],
        compiler_params=_params(("arbitrary", "arbitrary"), 32),
        name="attn_nbr",
    )(rpb_flat, qkv, qkv, qkv)


def _out_kernel(oa_ref, ob_ref, oc_ref, wa_ref, wb_ref, wc_ref, x_ref, g1_ref, ng_ref, sc_ref,
                sh_ref, rw_ref, x1_ref, h2_ref, lg_ref):
    acc = jnp.dot(oa_ref[...], wa_ref[...], preferred_element_type=F32)
    acc = acc + jnp.dot(ob_ref[...], wb_ref[...], preferred_element_type=F32)
    acc = acc + jnp.dot(oc_ref[...], wc_ref[...], preferred_element_type=F32)
    x1 = x_ref[...] + g1_ref[0] * acc
    x1_ref[...] = x1
    h2 = (_rms(x1, NORM_EPS) * ng_ref[...]) * (1.0 + sc_ref[0]) + sh_ref[0]
    _store_row_slabs(h2_ref, _pack_pairs(h2))
    lg_ref[...] = lax.dot_general(rw_ref[...], h2, _NT, preferred_element_type=F32,
                                  precision=HIGHEST)


def _out_proj(oa, ob, oc, w_bf, x2d, g1, ng, sc, sh, rw_t, S, tm=512):
    N, D = x2d.shape
    wa, wb, wc = oa.shape[1], ob.shape[1], oc.shape[1]
    E = rw_t.shape[0]
    tpb = S // tm
    const = dict(pipeline_mode=pl.Buffered(1))
    row = lambda i: (i, 0)
    per_b = lambda i: (i // tpb, 0, 0)
    return pl.pallas_call(
        _out_kernel,
        grid=(N // tm,),
        in_specs=[pl.BlockSpec((tm, wa), row), pl.BlockSpec((tm, wb), row), pl.BlockSpec((tm, wc), row),
                  pl.BlockSpec((wa, D), lambda i: (0, 0), **const),
                  pl.BlockSpec((wb, D), lambda i: (wa // wb, 0), **const),
                  pl.BlockSpec((wc, D), lambda i: ((wa + wb) // wc, 0), **const),
                  pl.BlockSpec((tm, D), row),
                  pl.BlockSpec((1, 1, D), per_b),
                  pl.BlockSpec((1, D), lambda i: (0, 0)),
                  pl.BlockSpec((1, 1, D), per_b),
                  pl.BlockSpec((1, 1, D), per_b),
                  pl.BlockSpec((E, D), lambda i: (0, 0), **const)],
        out_specs=[pl.BlockSpec((tm, D), row), pl.BlockSpec((tm * (D // 2 // LANES), LANES), row),
                   pl.BlockSpec((E, tm), lambda i: (0, i))],
        out_shape=[jax.ShapeDtypeStruct((N, D), F32),
                   jax.ShapeDtypeStruct((N * (D // 2 // LANES), LANES), U32),
                   jax.ShapeDtypeStruct((E, N), F32)],
        compiler_params=_params(("arbitrary",), 52),
        name="out_proj",
    )(oa, ob, oc, w_bf, w_bf, w_bf, x2d, g1, ng.reshape(1, D), sc, sh, rw_t)


def _first_argmax(vals, iota, big):
    mx = jnp.max(vals, axis=0, keepdims=True)
    idx = jnp.min(jnp.where(vals == mx, iota, big), axis=0, keepdims=True)
    return mx, idx


def _route_kernel(lg_ref, rb_ref, idx_ref, gate_ref, rank_ref, cnt_ref, tri_ref, carry_ref, *, Tt):
    i = pl.program_id(0)
    E = lg_ref.shape[0]
    gs = E // N_GROUPS

    @pl.when(i == 0)
    def _():
        r = lax.broadcasted_iota(I32, (Tt, Tt), 0)
        c = lax.broadcasted_iota(I32, (Tt, Tt), 1)
        tri_ref[...] = jnp.where(r < c, 1.0, 0.0).astype(BF16)
        carry_ref[...] = jnp.zeros_like(carry_ref)

    scores = jax.nn.sigmoid(lg_ref[...])
    sel = scores + rb_ref[...]
    e_io = lax.broadcasted_iota(I32, (E, Tt), 0).astype(F32)
    s_io = lax.broadcasted_iota(I32, (gs, Tt), 0).astype(F32)
    g_io = lax.broadcasted_iota(I32, (N_GROUPS, Tt), 0).astype(F32)

    grp = []
    for g in range(N_GROUPS):
        blk = sel[g * gs:(g + 1) * gs]
        m1, a1 = _first_argmax(blk, s_io, float(gs))
        m2 = jnp.max(jnp.where(s_io == a1, -jnp.inf, blk), axis=0, keepdims=True)
        grp.append(m1 + m2)
    grp = jnp.concatenate(grp, axis=0)

    gsel = jnp.zeros((N_GROUPS, Tt), F32)
    for _ in range(TOPK_GROUPS):
        _, gi = _first_argmax(grp, g_io, float(N_GROUPS))
        hit = g_io == gi
        gsel = jnp.where(hit, 1.0, gsel)
        grp = jnp.where(hit, -jnp.inf, grp)

    esel = jnp.concatenate(
        [jnp.broadcast_to(gsel[g:g + 1], (gs, Tt)) for g in range(N_GROUPS)], axis=0)
    cur = jnp.where(esel > 0.0, sel, NEG_INF)

    idxs, gates = [], []
    onehot = jnp.zeros((E, Tt), F32)
    for _ in range(TOP_K):
        _, ei = _first_argmax(cur, e_io, float(E))
        hit = e_io == ei
        idxs.append(ei)
        gates.append(jnp.sum(jnp.where(hit, scores, 0.0), axis=0, keepdims=True))
        onehot = jnp.where(hit, 1.0, onehot)
        cur = jnp.where(hit, -jnp.inf, cur)
    gate = jnp.concatenate(gates, axis=0)
    gate = gate / jnp.sum(gate, axis=0, keepdims=True) * ROUTED_SCALE
    idx_ref[...] = jnp.concatenate(idxs, axis=0).astype(I32)
    gate_ref[...] = gate

    oh = onehot.astype(BF16)
    before = jnp.dot(oh, tri_ref[...], preferred_element_type=F32)
    tile_cnt = jnp.dot(oh, jnp.ones((Tt, LANES), BF16), preferred_element_type=F32)
    carry = carry_ref[...]
    rank_mat = before + jnp.concatenate([carry] * (Tt // LANES), axis=1)
    ranks = [jnp.sum(jnp.where(e_io == ei, rank_mat, 0.0), axis=0, keepdims=True) for ei in idxs]
    rank_ref[...] = jnp.concatenate(ranks, axis=0).astype(I32)
    carry_ref[...] = carry + tile_cnt
    cnt_ref[...] = (carry + tile_cnt).astype(I32)


def _route(logits_t, router_bias, Tt=512):
    E, N = logits_t.shape
    tok = lambda i: (0, i)
    return pl.pallas_call(
        functools.partial(_route_kernel, Tt=Tt),
        grid=(N // Tt,),
        in_specs=[pl.BlockSpec((E, Tt), tok), pl.BlockSpec((E, 1), lambda i: (0, 0))],
        out_specs=[pl.BlockSpec((TOP_K, Tt), tok), pl.BlockSpec((TOP_K, Tt), tok),
                   pl.BlockSpec((TOP_K, Tt), tok), pl.BlockSpec((E, LANES), lambda i: (0, 0))],
        out_shape=[jax.ShapeDtypeStruct((TOP_K, N), I32), jax.ShapeDtypeStruct((TOP_K, N), F32),
                   jax.ShapeDtypeStruct((TOP_K, N), I32), jax.ShapeDtypeStruct((E, LANES), I32)],
        scratch_shapes=[pltpu.VMEM((Tt, Tt), BF16), pltpu.VMEM((E, LANES), F32)],
        compiler_params=_params(("arbitrary",), 32),
        name="route_topk",
    )(logits_t, router_bias.reshape(E, 1))


def _dest_kernel(pstart_ref, idx_ref, rank_ref, dest_ref, *, G):
    idx = idx_ref[...]
    base = jnp.zeros(idx.shape, I32)
    for e in range(pstart_ref.shape[0]):
        base = jnp.where(idx == e, pstart_ref[e], base)
    dest_ref[...] = (base + rank_ref[...]) * G


def _dest(pstart, idx, rank, G):
    return pl.pallas_call(
        functools.partial(_dest_kernel, G=G),
        in_specs=[pl.BlockSpec(memory_space=pltpu.SMEM), pl.BlockSpec(memory_space=pltpu.VMEM),
                  pl.BlockSpec(memory_space=pltpu.VMEM)],
        out_specs=pl.BlockSpec(memory_space=pltpu.VMEM),
        out_shape=jax.ShapeDtypeStruct(idx.shape, I32),
        name="route_dest",
    )(pstart, idx, rank)


def _dispatch_kernel(dest_hbm, h_ref, xs_hbm, dest_smem, idx_sem, row_sem, *, Td, G):
    i = pl.program_id(0)
    n = Td * TOP_K
    cp = pltpu.make_async_copy(dest_hbm.at[pl.ds(i * n, n)], dest_smem, idx_sem)
    cp.start()
    cp.wait()

    def row_copy(t, k):
        src = h_ref.at[pl.ds(pl.multiple_of(t * G, G), G)]
        dst = xs_hbm.at[pl.ds(pl.multiple_of(dest_smem[t * TOP_K + k], G), G)]
        return pltpu.make_async_copy(src, dst, row_sem)

    def issue(t, carry):
        for k in range(TOP_K):
            row_copy(t, k).start(priority=k % 2)
        return carry

    lax.fori_loop(0, Td, issue, 0)

    def drain(t, carry):
        for k in range(TOP_K):
            row_copy(t, k).wait()
        return carry

    lax.fori_loop(0, Td, drain, 0)


def _dispatch(dest_flat, h2, P, G, Td=512):
    N = h2.shape[0] // G
    return pl.pallas_call(
        functools.partial(_dispatch_kernel, Td=Td, G=G),
        grid=(N // Td,),
        in_specs=[pl.BlockSpec(memory_space=pl.ANY), pl.BlockSpec((Td * G, LANES), lambda i: (i, 0))],
        out_specs=pl.BlockSpec(memory_space=pl.ANY),
        out_shape=jax.ShapeDtypeStruct((P * G, LANES), h2.dtype),
        scratch_shapes=[pltpu.SMEM((Td * TOP_K,), I32), pltpu.SemaphoreType.DMA,
                        pltpu.SemaphoreType.DMA],
        compiler_params=_params(("arbitrary",), 32),
        name="moe_dispatch",
    )(dest_flat, h2)


def _moe_kernel(eo_ref, ps_ref, cnt_ref, xs_hbm, wg_ref, wu_ref, wd_ref, ys_hbm, wg_s, wu_s, wd_s,
                xbuf, ybuf, sem_in, sem_out, *, G):
    e = pl.program_id(0)
    T = MOE_BLOCK
    cnt = cnt_ref[e]
    row0 = ps_ref[e]
    nb = lax.shift_right_logical(cnt + (T - 1), T.bit_length() - 1)

    def block_rows(b):
        return pl.ds(pl.multiple_of((row0 + b * T) * G, T * G), T * G)

    def load(b, slot):
        return pltpu.make_async_copy(xs_hbm.at[block_rows(b)], xbuf.at[slot], sem_in.at[slot])

    def store(b, slot):
        return pltpu.make_async_copy(ybuf.at[slot], ys_hbm.at[block_rows(b)], sem_out.at[slot])

    @pl.when(nb > 0)
    def _():
        load(0, 0).start(priority=1)
        wg_s[...] = wg_ref[0].astype(BF16)
        wu_s[...] = wu_ref[0].astype(BF16)
        wd_s[...] = wd_ref[0].astype(BF16)

    def body(b, carry):
        slot = b & 1
        load(b, slot).wait()

        @pl.when(b + 1 < nb)
        def _():
            load(b + 1, 1 - slot).start(priority=1)

        @pl.when(b >= 2)
        def _():
            store(b - 2, slot).wait()

        xb = xbuf.at[slot]
        valid = lax.broadcasted_iota(I32, (T, LANES), 0) < cnt - b * T
        his, los = [], []
        for c in range(G):
            w = jnp.where(valid, _load_row_slab(xb, c, T, G), jnp.uint32(0))
            hi, lo = _unpack_pairs(w)
            his.append(hi.astype(BF16))
            los.append(lo.astype(BF16))
        x = jnp.concatenate(his + los, axis=1)
        g = jnp.dot(x, wg_s[...], preferred_element_type=F32)
        u = jnp.dot(x, wu_s[...], preferred_element_type=F32)
        a = (g * jax.nn.sigmoid(g) * u).astype(BF16)
        _store_row_slabs(ybuf.at[slot], _pack_pairs(jnp.dot(a, wd_s[...], preferred_element_type=F32)))
        store(b, slot).start(priority=1)
        return carry

    lax.fori_loop(0, nb, body, 0)

    @pl.when(nb >= 2)
    def _():
        store(nb - 2, nb & 1).wait()

    @pl.when(nb >= 1)
    def _():
        store(nb - 1, (nb - 1) & 1).wait()


def _moe(e_off, pstart, counts, xs, wg, wu, wd):
    _, D, F = wg.shape
    E = counts.shape[0]
    G = D // 2 // LANES
    return pl.pallas_call(
        functools.partial(_moe_kernel, G=G),
        grid_spec=pltpu.PrefetchScalarGridSpec(
            num_scalar_prefetch=3,
            grid=(E,),
            in_specs=[pl.BlockSpec(memory_space=pl.ANY),
                      pl.BlockSpec((1, D, F), lambda e, eo, ps, cn: (eo[0] + e, 0, 0)),
                      pl.BlockSpec((1, D, F), lambda e, eo, ps, cn: (eo[0] + e, 0, 0)),
                      pl.BlockSpec((1, F, D), lambda e, eo, ps, cn: (eo[0] + e, 0, 0))],
            out_specs=pl.BlockSpec(memory_space=pl.ANY),
            scratch_shapes=[pltpu.VMEM((D, F), BF16), pltpu.VMEM((D, F), BF16),
                            pltpu.VMEM((F, D), BF16),
                            pltpu.VMEM((2, MOE_BLOCK * G, LANES), U32),
                            pltpu.VMEM((2, MOE_BLOCK * G, LANES), U32),
                            pltpu.SemaphoreType.DMA((2,)), pltpu.SemaphoreType.DMA((2,))]),
        out_shape=jax.ShapeDtypeStruct(xs.shape, U32),
        compiler_params=_params(("arbitrary",), 52),
        name="moe_experts",
    )(e_off, pstart, counts, xs, wg, wu, wd)


def _combine_kernel(dest_hbm, ys_hbm, gate_ref, h_ref, x_ref, g2_ref, sg_ref, su_ref, sd_ref, fg_ref,
                    o_ref, dest0, dest1, ybuf0, ybuf1, idx_sem, row_sem, *, Tc, G, final_norm):
    j = pl.program_id(0)
    n = Tc * TOP_K
    slots = ((dest0, ybuf0), (dest1, ybuf1))

    def row_copy(slot, t, k):
        dest, ybuf = slots[slot]
        src = ys_hbm.at[pl.ds(pl.multiple_of(dest[t * TOP_K + k], G), G)]
        dst = ybuf.at[k, pl.ds(pl.multiple_of(t * G, G), G)]
        return pltpu.make_async_copy(src, dst, row_sem.at[slot])

    def request(tile, slot):
        cp = pltpu.make_async_copy(dest_hbm.at[pl.ds(tile * n, n)], slots[slot][0],
                                   idx_sem.at[slot])
        cp.start()
        cp.wait()

        def issue(t, carry):
            for k in range(TOP_K):
                row_copy(slot, t, k).start(priority=k % 2)
            return carry

        lax.fori_loop(0, Tc, issue, 0)

    def reduce(slot):
        rows = slice(slot * Tc, (slot + 1) * Tc)
        h_tile = h_ref.at[pl.ds(slot * Tc * G, Tc * G)]
        halves = [_unpack_pairs(_load_row_slab(h_tile, c, Tc, G)) for c in range(G)]
        h = jnp.concatenate([p[0].astype(BF16) for p in halves]
                            + [p[1].astype(BF16) for p in halves], axis=1)
        g = jnp.dot(h, sg_ref[...], preferred_element_type=F32)
        u = jnp.dot(h, su_ref[...], preferred_element_type=F32)
        a = (g * jax.nn.sigmoid(g) * u).astype(BF16)
        f = jnp.dot(a, sd_ref[...], preferred_element_type=F32)

        def drain(t, carry):
            for k in range(TOP_K):
                row_copy(slot, t, k).wait()
            return carry

        lax.fori_loop(0, Tc, drain, 0)
        gate = gate_ref[rows, :]
        half = f.shape[1] // 2
        his, los = [], []
        for c in range(G):
            f_hi = f[:, c * LANES:(c + 1) * LANES]
            f_lo = f[:, half + c * LANES:half + (c + 1) * LANES]
            for k in range(TOP_K):
                y_hi, y_lo = _unpack_pairs(_load_row_slab(slots[slot][1].at[k], c, Tc, G))
                f_hi = f_hi + gate[:, k:k + 1] * y_hi
                f_lo = f_lo + gate[:, k:k + 1] * y_lo
            his.append(f_hi)
            los.append(f_lo)
        x2 = x_ref[rows, :] + g2_ref[0] * jnp.concatenate(his + los, axis=1)
        if final_norm:
            x2 = _rms(x2, NORM_EPS) * fg_ref[...]
        o_ref[rows, :] = x2

    @pl.when(j == 0)
    def _():
        request(0, 0)

    request(2 * j + 1, 1)
    reduce(0)

    @pl.when(j + 1 < pl.num_programs(0))
    def _():
        request(2 * j + 2, 0)

    reduce(1)


def _combine(dest_flat, ys, gate_t, h2, x1, g2, sg, su, sd, final_g, S, final_norm, Tc=256):
    N, D = x1.shape
    F = sg.shape[1]
    G = D // 2 // LANES
    tb = 2 * Tc
    tpb = S // tb
    row = lambda j: (j, 0)
    const = dict(pipeline_mode=pl.Buffered(1))
    return pl.pallas_call(
        functools.partial(_combine_kernel, Tc=Tc, G=G, final_norm=final_norm),
        grid=(N // tb,),
        in_specs=[pl.BlockSpec(memory_space=pl.ANY), pl.BlockSpec(memory_space=pl.ANY),
                  pl.BlockSpec((tb, TOP_K), row),
                  pl.BlockSpec((tb * G, LANES), row), pl.BlockSpec((tb, D), row),
                  pl.BlockSpec((1, 1, D), lambda j: (j // tpb, 0, 0)),
                  pl.BlockSpec((D, F), lambda j: (0, 0), **const),
                  pl.BlockSpec((D, F), lambda j: (0, 0), **const),
                  pl.BlockSpec((F, D), lambda j: (0, 0), **const),
                  pl.BlockSpec((1, D), lambda j: (0, 0))],
        out_specs=pl.BlockSpec((tb, D), row),
        out_shape=jax.ShapeDtypeStruct((N, D), F32),
        scratch_shapes=[pltpu.SMEM((Tc * TOP_K,), I32), pltpu.SMEM((Tc * TOP_K,), I32),
                        pltpu.VMEM((TOP_K, Tc * G, LANES), U32),
                        pltpu.VMEM((TOP_K, Tc * G, LANES), U32),
                        pltpu.SemaphoreType.DMA((2,)), pltpu.SemaphoreType.DMA((2,))],
        compiler_params=_params(("arbitrary",), 52),
        name="moe_combine",
    )(dest_flat, ys, gate_t, h2, x1, g2, sg, su, sd, final_g.reshape(1, D))


def _moe_ffn(h2, logits_t, x1, g2, router_bias, wg, wu, wd, e_off, sg, su, sd, final_g, S,
             final_norm):
    N, D = x1.shape
    G = D // 2 // LANES
    E = logits_t.shape[0]
    idx, gate, rank, cnt = _route(logits_t, router_bias)
    counts = cnt[:, 0]
    padded = (counts + MOE_BLOCK - 1) // MOE_BLOCK * MOE_BLOCK
    pstart = (jnp.cumsum(padded) - padded).astype(I32)
    n_blocks = -(-(N * TOP_K + E * (MOE_BLOCK - 1)) // MOE_BLOCK)

    dest = _dest(pstart, idx, rank, G)
    dest_flat = dest.T.reshape(N * TOP_K)
    xs = _dispatch(dest_flat, h2, n_blocks * MOE_BLOCK, G)
    ys = _moe(jnp.full((1,), e_off, I32), pstart, counts.astype(I32), xs, wg, wu, wd)
    return _combine(dest_flat, ys, gate.T, h2, x1, g2, sg, su, sd, final_g, S, final_norm)


def kernel(x, c, ada_w, ada_b, norm_mix_g, norm_ffn_g, w_in, lambda_q1, lambda_k1, lambda_q2,
           lambda_k2, subln_g, rpb, w_out, router_w, router_bias, exp_w_gate, exp_w_up,
           exp_w_down, sh_w_gate, sh_w_up, sh_w_down, final_g):
    B, S, D = x.shape
    L = ada_w.shape[0]
    N = B * S
    n_heads = w_in.shape[2] // (3 * HEAD_DIM)
    heads_a = n_heads // HEADS_A_FRAC
    heads_b = n_heads // HEADS_B_FRAC
    heads_c = n_heads - heads_a - heads_b
    slopes = _alibi_slopes(heads_a + heads_b)
    slopes_a = jnp.asarray(slopes[:heads_a], F32)
    slopes_b = jnp.asarray(slopes[heads_a:], F32)

    E, F = exp_w_gate.shape[1], exp_w_gate.shape[3]
    wg_all = exp_w_gate.reshape(L * E, D, F)
    wu_all = exp_w_up.reshape(L * E, D, F)
    wd_all = exp_w_down.reshape(L * E, F, D)

    mod = _ada(c, ada_w, ada_b).reshape(L, B, 6, 1, D)
    xf = x.reshape(N, D)
    for l in range(L):
        sh1, sc1, g1, sh2, sc2, g2 = (mod[l, :, j] for j in range(6))
        qkv = _qkv(xf, norm_mix_g[l], sc1, sh1, w_in[l].astype(BF16), S)
        oa = _attn_a(qkv, slopes_a, B, S, n_heads, heads_a)
        lambda_init = 0.8 - 0.6 * math.exp(-0.3 * l)
        lam_vecs = jnp.stack([lambda_q1[l], lambda_k1[l], lambda_q2[l], lambda_k2[l]]).astype(F32)
        ob = _attn_b(qkv, slopes_b, lam_vecs, subln_g[l], B, S, n_heads, heads_a, heads_b, lambda_init)
        oc = _attn_c(qkv, rpb[l].reshape(-1), B, S, n_heads, heads_a + heads_b, heads_c)
        x1, h2, logits_t = _out_proj(oa, ob, oc, w_out[l].astype(BF16), xf, g1, norm_ffn_g[l],
                                     sc2, sh2, router_w[l].T, S)
        xf = _moe_ffn(h2, logits_t, x1, g2, router_bias[l], wg_all, wu_all, wd_all, l * E,
                      sh_w_gate[l].astype(BF16), sh_w_up[l].astype(BF16),
                      sh_w_down[l].astype(BF16), final_g, S, l == L - 1)
    return xf.reshape(B, S, D)
```

```python
import functools
import math

import jax
import jax.numpy as jnp
from jax import lax
from jax.experimental import pallas as pl
from jax.experimental.pallas import tpu as pltpu

F32 = jnp.float32
BF16 = jnp.bfloat16
I32 = jnp.int32
U32 = jnp.uint32
HIGHEST = lax.Precision.HIGHEST
_NT = (((1,), (1,)), ((), ()))

HEAD_DIM = 128
HEADS_A_FRAC, HEADS_B_FRAC = 2, 4
DILATED_PATTERNS = ((128, 1), (512, 4), (2048, 16))
GRID_W = 64
NA_ROWS = 8
NA_COLS = 16
N_GROUPS = 8
TOPK_GROUPS = 4
TOP_K = 8
ROUTED_SCALE = 2.5
NORM_EPS = 1e-6
SUBLN_EPS = 1e-5
NEG_INF = -1e30
LOG2E = math.log2(math.e)

LANES = 128
VMEM_BYTES_V7X = 64 << 20

MOE_BLOCK = 512


def _params(semantics, vmem_mib):
    return pltpu.CompilerParams(dimension_semantics=semantics,
                                vmem_limit_bytes=min(vmem_mib << 20, VMEM_BYTES_V7X - (4 << 20)))


def _alibi_slopes(n):
    def pow2(m):
        start = 2.0 ** (-8.0 / m)
        return [start ** (i + 1) for i in range(m)]
    p = 2 ** int(math.floor(math.log2(n)))
    return pow2(p) + pow2(2 * p)[0::2][: n - p]


def _rms(x, eps):
    return x * lax.rsqrt(jnp.mean(x * x, axis=-1, keepdims=True) + eps)


def _pack_pairs(x):
    half = x.shape[1] // 2
    hi = lax.bitcast_convert_type(x[:, :half].astype(BF16).astype(F32), U32)
    lo = lax.bitcast_convert_type(x[:, half:].astype(BF16).astype(F32), U32)
    return hi | (lo >> 16)


def _unpack_pairs(w):
    hi = lax.bitcast_convert_type(w & jnp.uint32(0xFFFF0000), F32)
    lo = lax.bitcast_convert_type(w << 16, F32)
    return hi, lo


def _store_row_slabs(ref, packed):
    T, W = packed.shape
    for c in range(W // LANES):
        ref[pl.ds(c, T, stride=W // LANES), :] = packed[:, c * LANES:(c + 1) * LANES]


def _load_row_slab(ref, c, T, G):
    return ref[pl.ds(c, T, stride=G), :]


def _ada_kernel(c_ref, w_ref, b_ref, o_ref):
    c = c_ref[...]
    sc = c * jax.nn.sigmoid(c)
    o_ref[0] = jnp.dot(sc, w_ref[0], preferred_element_type=F32, precision=HIGHEST) + b_ref[0]


def _ada(c, ada_w, ada_b):
    B, D = c.shape
    L, _, D6 = ada_w.shape
    rows = 8
    cp = jnp.zeros((rows, D), F32).at[:B].set(c)
    tn = 1024
    out = pl.pallas_call(
        _ada_kernel,
        grid=(L, D6 // tn),
        in_specs=[pl.BlockSpec((rows, D), lambda l, j: (0, 0)),
                  pl.BlockSpec((1, D, tn), lambda l, j: (l, 0, j)),
                  pl.BlockSpec((1, 1, tn), lambda l, j: (l, 0, j))],
        out_specs=pl.BlockSpec((1, rows, tn), lambda l, j: (l, 0, j)),
        out_shape=jax.ShapeDtypeStruct((L, rows, D6), F32),
        compiler_params=_params(("arbitrary", "arbitrary"), 32),
        name="ada_mod",
    )(cp, ada_w, ada_b.reshape(L, 1, D6))
    return out[:, :B]


def _qkv_kernel(x_ref, g_ref, sc_ref, sh_ref, w_ref, o_ref, h_scr, *, heads_per_step):
    @pl.when(pl.program_id(1) == 0)
    def _():
        y = _rms(x_ref[...], NORM_EPS) * g_ref[...]
        h_scr[...] = (y * (1.0 + sc_ref[0]) + sh_ref[0]).astype(BF16)

    res = jnp.dot(h_scr[...], w_ref[...], preferred_element_type=F32)
    for hh in range(heads_per_step):
        o_ref[hh] = res[:, hh * HEAD_DIM:(hh + 1) * HEAD_DIM].astype(BF16)


def _qkv(x2d, g, sc, sh, w_bf, S, tm=512, tn=2048):
    N, D = x2d.shape
    W3 = w_bf.shape[1]
    hps = tn // HEAD_DIM
    tpb = S // tm
    return pl.pallas_call(
        functools.partial(_qkv_kernel, heads_per_step=hps),
        grid=(N // tm, W3 // tn),
        in_specs=[pl.BlockSpec((tm, D), lambda i, j: (i, 0)),
                  pl.BlockSpec((1, D), lambda i, j: (0, 0)),
                  pl.BlockSpec((1, 1, D), lambda i, j: (i // tpb, 0, 0)),
                  pl.BlockSpec((1, 1, D), lambda i, j: (i // tpb, 0, 0)),
                  pl.BlockSpec((D, tn), lambda i, j: (0, j))],
        out_specs=pl.BlockSpec((hps, tm, HEAD_DIM), lambda i, j: (j, i, 0)),
        out_shape=jax.ShapeDtypeStruct((W3 // HEAD_DIM, N, HEAD_DIM), BF16),
        scratch_shapes=[pltpu.VMEM((tm, D), BF16)],
        compiler_params=_params(("arbitrary", "arbitrary"), 48),
        name="qkv_proj",
    )(x2d, g.reshape(1, D), sc, sh, w_bf)


def _col_to_row(col, eye):
    return jnp.sum(jnp.where(eye, col, 0.0), axis=0, keepdims=True)


def _attn_a_kernel(slopes_ref, q_ref, k_ref, v_ref, o_ref, lse_ref, nat, cq, ck, cv, onat, tmp, *,
                   L, dil, half, Tq):
    h = pl.program_id(1)
    wk = 2 * Tq
    n_tiles = L // Tq
    c_dist = LOG2E * slopes_ref[h] * dil
    a = lax.broadcasted_iota(I32, (Tq, wk), 0)
    u = lax.broadcasted_iota(I32, (Tq, wk), 1)
    eye = lax.broadcasted_iota(I32, (Tq, Tq), 0) == lax.broadcasted_iota(I32, (Tq, Tq), 1)

    def table(off):
        d = u - a - off
        ad = jnp.maximum(d, -d)
        return jnp.where(ad <= half, -c_dist * ad.astype(F32), NEG_INF)

    tabs = {off: table(off) for off in sorted({0, half, wk - Tq})}

    q_scale = LOG2E * HEAD_DIM ** -0.5
    d1 = 4 if dil > 4 else dil
    d2 = dil // d1
    S = L * dil
    if dil > 1:
        for src, dst, scale in ((q_ref, cq, q_scale), (k_ref, ck, None), (v_ref, cv, None)):
            x = src[0].astype(F32)
            nat[...] = x if scale is None else x * scale
            if d2 == 1:
                for r in range(dil):
                    dst[r * L:(r + 1) * L, :] = nat[pl.ds(r, L, stride=dil), :].astype(BF16)
            else:
                for ra in range(d1):
                    tmp[ra * (S // d1):(ra + 1) * (S // d1), :] = nat[pl.ds(ra, S // d1, stride=d1), :]
                for ra in range(d1):
                    for rb in range(d2):
                        r = ra + d1 * rb
                        dst[r * L:(r + 1) * L, :] = tmp[
                            pl.ds(ra * (S // d1) + rb, L, stride=d2), :].astype(BF16)
        kc, vc = ck, cv
    else:
        kc, vc = k_ref.at[0], v_ref.at[0]

    for r in range(dil):
        for j in range(n_tiles):
            start = min(max(j * Tq - half, 0), L - wk)
            rows = slice(r * L + j * Tq, r * L + (j + 1) * Tq)
            q = cq[rows, :] if dil > 1 else (q_ref[0, rows, :].astype(F32) * q_scale).astype(BF16)
            kw = kc[r * L + start:r * L + start + wk, :]
            vw = vc[r * L + start:r * L + start + wk, :]
            s = lax.dot_general(q, kw, _NT, preferred_element_type=F32) + tabs[j * Tq - start]
            m = jnp.max(s, axis=-1, keepdims=True)
            p = jnp.exp2(s - m)
            l = jnp.sum(p, axis=-1, keepdims=True)
            o = jnp.dot(p.astype(BF16), vw, preferred_element_type=F32) / l
            if dil == 1:
                o_ref[0, rows, :] = o.astype(BF16)
            elif d2 == 1:
                onat[pl.ds(r + j * Tq * dil, Tq, stride=dil), :] = o
            else:
                ra, rb = r % d1, r // d1
                tmp[pl.ds(ra * (S // d1) + rb + j * Tq * d2, Tq, stride=d2), :] = o
            lse_ref[0, 0, r:r + 1, j * Tq:(j + 1) * Tq] = _col_to_row(m + jnp.log2(l), eye)
    if d2 > 1:
        for ra in range(d1):
            onat[pl.ds(ra, S // d1, stride=d1), :] = tmp[ra * (S // d1):(ra + 1) * (S // d1), :]
    if dil > 1:
        o_ref[0] = onat[...].astype(BF16)


def _attn_a_pattern(qkv, slopes, B, S, n_heads, heads_a, window, dil, Tq=128):
    N = B * S
    L = S // dil
    half = window // (2 * dil)
    assert L % Tq == 0 and L >= 2 * Tq and 2 * half <= Tq
    blk = (1, S, HEAD_DIM)
    o, lse = pl.pallas_call(
        functools.partial(_attn_a_kernel, L=L, dil=dil, half=half, Tq=Tq),
        grid=(B, heads_a),
        in_specs=[pl.BlockSpec(memory_space=pltpu.SMEM),
                  pl.BlockSpec(blk, lambda b, h: (h, b, 0)),
                  pl.BlockSpec(blk, lambda b, h: (n_heads + h, b, 0)),
                  pl.BlockSpec(blk, lambda b, h: (2 * n_heads + h, b, 0))],
        out_specs=[pl.BlockSpec(blk, lambda b, h: (h, b, 0)),
                   pl.BlockSpec((1, 1, dil, L), lambda b, h: (b, h, 0, 0))],
        out_shape=[jax.ShapeDtypeStruct((heads_a, N, HEAD_DIM), BF16),
                   jax.ShapeDtypeStruct((B, heads_a, dil, L), F32)],
        scratch_shapes=[pltpu.VMEM((S, HEAD_DIM), F32)] + [pltpu.VMEM((S, HEAD_DIM), BF16)] * 3
        + [pltpu.VMEM((S, HEAD_DIM), F32)] * 2,
        compiler_params=_params(("arbitrary", "arbitrary"), 32),
        name=f"attn_dilated_d{dil}",
    )(slopes, qkv, qkv, qkv)
    lse_nat = lse.transpose(1, 0, 3, 2).reshape(heads_a, N)
    return o, lse_nat


def _mix_a_kernel(*refs, n_pat, tm):
    o_refs, lse_refs, out_ref = refs[:n_pat], refs[n_pat:2 * n_pat], refs[2 * n_pat]
    heads = lse_refs[0].shape[0]
    lses = [r[...] for r in lse_refs]
    mx = functools.reduce(jnp.maximum, lses)
    es = [jnp.exp2(x - mx) for x in lses]
    den = functools.reduce(lambda x, y: x + y, es)
    ws = [e / den for e in es]
    pad = jnp.zeros((LANES - n_pat * heads, LANES), F32)
    for c in range(tm // LANES):
        cols = slice(c * LANES, (c + 1) * LANES)
        w_rows = jnp.concatenate([w[:, cols] for w in ws] + [pad], axis=0)
        w_cols = w_rows.T
        for hh in range(heads):
            acc = jnp.zeros((LANES, HEAD_DIM), F32)
            for p in range(n_pat):
                wc = w_cols[:, p * heads + hh:p * heads + hh + 1]
                acc = acc + wc * o_refs[p][hh, cols, :].astype(F32)
            out_ref[cols, hh * HEAD_DIM:(hh + 1) * HEAD_DIM] = acc.astype(BF16)


def _mix_a(os, lses, tm=512):
    heads, N, _ = os[0].shape
    n_pat = len(os)
    assert n_pat * heads <= LANES
    return pl.pallas_call(
        functools.partial(_mix_a_kernel, n_pat=n_pat, tm=tm),
        grid=(N // tm,),
        in_specs=[pl.BlockSpec((heads, tm, HEAD_DIM), lambda i: (0, i, 0))] * n_pat
        + [pl.BlockSpec((heads, tm), lambda i: (0, i))] * n_pat,
        out_specs=pl.BlockSpec((tm, heads * HEAD_DIM), lambda i: (i, 0)),
        out_shape=jax.ShapeDtypeStruct((N, heads * HEAD_DIM), BF16),
        compiler_params=_params(("arbitrary",), 32),
        name="attn_dilated_mix",
    )(*os, *lses)


def _attn_a(qkv, slopes, B, S, n_heads, heads_a):
    parts = [_attn_a_pattern(qkv, slopes, B, S, n_heads, heads_a, w, d) for w, d in DILATED_PATTERNS]
    return _mix_a([p[0] for p in parts], [p[1] for p in parts])


def _attn_b_kernel(slopes_ref, lam_ref, q_ref, k_ref, v_ref, g_ref, o_ref, tab_ref, kmax_ref,
                   m_scr, l_scr, acc_scr, *, S, Tq, Tk, lambda_init):
    h = pl.program_id(1)
    i = pl.program_id(2)
    dh = HEAD_DIM // 2
    n_chunks = S // Tk
    c_dist = LOG2E * slopes_ref[h]

    def sub_norm(x):
        sq = x * x
        return jnp.sqrt(jnp.maximum(jnp.sum(sq[:, :dh], axis=-1, keepdims=True),
                                    jnp.sum(sq[:, dh:], axis=-1, keepdims=True)))

    @pl.when(i == 0)
    def _():
        shape = (Tq, 2 * S - Tq)
        a = lax.broadcasted_iota(I32, shape, 0)
        u = lax.broadcasted_iota(I32, shape, 1)
        d = a - u + (S - Tq)
        tab_ref[...] = -c_dist * jnp.maximum(d, -d).astype(F32)
        kn = sub_norm(k_ref[0].astype(F32))
        for j in range(n_chunks):
            kmax_ref[j] = jnp.max(kn[j * Tk:(j + 1) * Tk])

    lv = lam_ref[...]
    lam = (jnp.exp(jnp.sum(lv[0:1] * lv[1:2], axis=-1, keepdims=True))
           - jnp.exp(jnp.sum(lv[2:3] * lv[3:4], axis=-1, keepdims=True)) + lambda_init)

    t0 = i * Tq
    qf = q_ref[0].astype(F32) * (LOG2E * dh ** -0.5)
    q = qf.astype(BF16)
    qmax = jnp.max(sub_norm(q.astype(F32)))

    def chunk(kstart, first):
        kc = k_ref[0, pl.ds(kstart, Tk), :]
        vc = v_ref[0, pl.ds(kstart, Tk), :]
        bias = tab_ref[:, pl.ds(pl.multiple_of(S - Tq - t0 + kstart, LANES), Tk)]
        s = jnp.concatenate(
            [lax.dot_general(q[:, c * dh:(c + 1) * dh], kc[:, c * dh:(c + 1) * dh], _NT,
                             preferred_element_type=F32) + bias for c in range(2)], axis=0)
        mc = jnp.max(s, axis=-1, keepdims=True)
        if first:
            m_new = mc
        else:
            m_old = m_scr[...]
            m_new = jnp.maximum(m_old, mc)
            alpha = jnp.exp2(m_old - m_new)
        p = jnp.exp2(s - m_new)
        ls = jnp.sum(p, axis=-1, keepdims=True)
        pv = jnp.dot(p.astype(BF16), vc, preferred_element_type=F32)
        m_scr[...] = m_new
        l_scr[...] = ls if first else alpha * l_scr[...] + ls
        acc_scr[...] = pv if first else alpha * acc_scr[...] + pv
        return m_new

    jd = t0 // Tk
    m_diag = chunk(pl.multiple_of(jd * Tk, Tk), True)
    floor = jnp.min(m_diag) - 160.0
    for j in range(n_chunks):
        gap = jnp.maximum(jnp.maximum(j * Tk - (t0 + Tq - 1), t0 - ((j + 1) * Tk - 1)), 0)
        bound = 1.001 * qmax * kmax_ref[j] + 0.01 - c_dist * gap.astype(F32)

        @pl.when((j != jd) & (bound > floor))
        def _():
            chunk(j * Tk, False)

    acc = acc_scr[...]
    l = l_scr[...]
    o = acc[:Tq] / l[:Tq] - acc[Tq:] * (lam / l[Tq:])
    o = _rms(o, SUBLN_EPS) * g_ref[...] * (1.0 - lambda_init)
    o_ref[...] = o.astype(BF16)


def _attn_b(qkv, slopes, lam_vecs, subln_g, B, S, n_heads, head0, heads_b, lambda_init, Tq=256,
            Tk=1024):
    N = B * S
    nq = S // Tq
    Tk = min(Tk, S)
    return pl.pallas_call(
        functools.partial(_attn_b_kernel, S=S, Tq=Tq, Tk=Tk, lambda_init=lambda_init),
        grid=(B, heads_b, nq),
        in_specs=[pl.BlockSpec(memory_space=pltpu.SMEM),
                  pl.BlockSpec((4, HEAD_DIM // 2), lambda b, h, i: (0, 0)),
                  pl.BlockSpec((1, Tq, HEAD_DIM), lambda b, h, i: (head0 + h, b * nq + i, 0)),
                  pl.BlockSpec((1, S, HEAD_DIM), lambda b, h, i: (n_heads + head0 + h, b, 0)),
                  pl.BlockSpec((1, S, HEAD_DIM), lambda b, h, i: (2 * n_heads + head0 + h, b, 0)),
                  pl.BlockSpec((1, HEAD_DIM), lambda b, h, i: (0, 0))],
        out_specs=pl.BlockSpec((Tq, HEAD_DIM), lambda b, h, i: (b * nq + i, h)),
        out_shape=jax.ShapeDtypeStruct((N, heads_b * HEAD_DIM), BF16),
        scratch_shapes=[pltpu.VMEM((Tq, 2 * S - Tq), F32), pltpu.SMEM((S // Tk,), F32),
                        pltpu.VMEM((2 * Tq, 1), F32), pltpu.VMEM((2 * Tq, 1), F32),
                        pltpu.VMEM((2 * Tq, HEAD_DIM), F32)],
        compiler_params=_params(("arbitrary", "arbitrary", "arbitrary"), 48),
        name="attn_diff",
    )(slopes, lam_vecs, qkv, qkv, qkv, subln_g.reshape(1, HEAD_DIM))


def _attn_c_kernel(rpb_ref, q_ref, k_ref, v_ref, o_ref, tab_ref, *, R):
    h = pl.program_id(1)
    W = GRID_W
    kr = min(NA_ROWS, R)
    n_dr = 2 * NA_ROWS - 1
    n_dc = 2 * NA_COLS - 1

    c_io = lax.broadcasted_iota(I32, (W, 2 * W), 0)
    l_io = lax.broadcasted_iota(I32, (W, 2 * W), 1)
    cp = l_io & (W - 1)
    dcm = cp - c_io + (NA_COLS - 1)
    cstart = jnp.clip(c_io - NA_COLS // 2, 0, W - NA_COLS)
    ok = (cp >= cstart) & (cp < cstart + NA_COLS)
    blocks = []
    for dr in range(n_dr):
        blk = jnp.zeros((W, 2 * W), F32)
        for dc in range(n_dc):
            blk = jnp.where(dcm == dc, rpb_ref[(h * n_dr + dr) * n_dc + dc], blk)
        blocks.append(jnp.where(ok, blk, NEG_INF))
    for o in range(NA_ROWS):
        for jj in range(kr // 2):
            tab_ref[o, :, jj * 2 * W:(jj + 1) * 2 * W] = jnp.where(
                l_io < W, blocks[o + 2 * jj], blocks[o + 2 * jj + 1])

    def row(r, carry):
        rs = jnp.clip(r - kr // 2, 0, R - kr)
        q = q_ref[0, pl.ds(pl.multiple_of(r * W, W), W), :]
        kw = k_ref[0, pl.ds(pl.multiple_of(rs * W, W), kr * W), :]
        vw = v_ref[0, pl.ds(pl.multiple_of(rs * W, W), kr * W), :]
        s = lax.dot_general(q, kw, _NT, preferred_element_type=F32) * (HEAD_DIM ** -0.5)
        s = s + tab_ref[rs - r + (NA_ROWS - 1)]
        m = jnp.max(s, axis=-1, keepdims=True)
        p = jnp.exp(s - m)
        l = jnp.sum(p, axis=-1, keepdims=True)
        o = jnp.dot(p.astype(BF16), vw, preferred_element_type=F32) / l
        o_ref[pl.ds(pl.multiple_of(r * W, W), W), :] = o.astype(BF16)
        return carry

    lax.fori_loop(0, R, row, 0, unroll=16 if R % 16 == 0 else 1)


def _attn_c(qkv, rpb_flat, B, S, n_heads, head0, heads_c):
    N = B * S
    R = S // GRID_W
    kr = min(NA_ROWS, R)
    assert kr == NA_ROWS and kr % 2 == 0
    return pl.pallas_call(
        functools.partial(_attn_c_kernel, R=R),
        grid=(B, heads_c),
        in_specs=[pl.BlockSpec(memory_space=pltpu.SMEM),
                  pl.BlockSpec((1, S, HEAD_DIM), lambda b, h: (head0 + h, b, 0)),
                  pl.BlockSpec((1, S, HEAD_DIM), lambda b, h: (n_heads + head0 + h, b, 0)),
                  pl.BlockSpec((1, S, HEAD_DIM), lambda b, h: (2 * n_heads + head0 + h, b, 0))],
        out_specs=pl.BlockSpec((S, HEAD_DIM), lambda b, h: (b, h)),
        out_shape=jax.ShapeDtypeStruct((N, heads_c * HEAD_DIM), BF16),
        scratch_shapes=[pltpu.VMEM((NA_ROWS, GRID_W, kr * GRID_W), F32)],
        compiler_params=_params(("arbitrary", "arbitrary"), 32),
        name="attn_nbr",
    )(rpb_flat, qkv, qkv, qkv)


def _out_kernel(oa_ref, ob_ref, oc_ref, w_ref, x_ref, g1_ref, ng_ref, sc_ref, sh_ref, rw_ref,
                x1_ref, h2_ref, lg_ref):
    o = jnp.concatenate([oa_ref[...], ob_ref[...], oc_ref[...]], axis=1)
    acc = jnp.dot(o, w_ref[...], preferred_element_type=F32)
    x1 = x_ref[...] + g1_ref[0] * acc
    x1_ref[...] = x1
    h2 = (_rms(x1, NORM_EPS) * ng_ref[...]) * (1.0 + sc_ref[0]) + sh_ref[0]
    _store_row_slabs(h2_ref, _pack_pairs(h2))
    lg_ref[...] = lax.dot_general(rw_ref[...], h2, _NT, preferred_element_type=F32,
                                  precision=HIGHEST)


def _out_proj(oa, ob, oc, w_bf, x2d, g1, ng, sc, sh, rw_t, S, tm=512):
    N, D = x2d.shape
    wa, wb, wc = oa.shape[1], ob.shape[1], oc.shape[1]
    E = rw_t.shape[0]
    tpb = S // tm
    const = dict(pipeline_mode=pl.Buffered(1))
    row = lambda i: (i, 0)
    per_b = lambda i: (i // tpb, 0, 0)
    return pl.pallas_call(
        _out_kernel,
        grid=(N // tm,),
        in_specs=[pl.BlockSpec((tm, wa), row), pl.BlockSpec((tm, wb), row), pl.BlockSpec((tm, wc), row),
                  pl.BlockSpec((wa + wb + wc, D), lambda i: (0, 0), **const),
                  pl.BlockSpec((tm, D), row),
                  pl.BlockSpec((1, 1, D), per_b),
                  pl.BlockSpec((1, D), lambda i: (0, 0)),
                  pl.BlockSpec((1, 1, D), per_b),
                  pl.BlockSpec((1, 1, D), per_b),
                  pl.BlockSpec((E, D), lambda i: (0, 0), **const)],
        out_specs=[pl.BlockSpec((tm, D), row), pl.BlockSpec((tm * (D // 2 // LANES), LANES), row),
                   pl.BlockSpec((E, tm), lambda i: (0, i))],
        out_shape=[jax.ShapeDtypeStruct((N, D), F32),
                   jax.ShapeDtypeStruct((N * (D // 2 // LANES), LANES), U32),
                   jax.ShapeDtypeStruct((E, N), F32)],
        compiler_params=_params(("arbitrary",), 52),
        name="out_proj",
    )(oa, ob, oc, w_bf, x2d, g1, ng.reshape(1, D), sc, sh, rw_t)


def _first_argmax(vals, iota, big):
    mx = jnp.max(vals, axis=0, keepdims=True)
    idx = jnp.min(jnp.where(vals == mx, iota, big), axis=0, keepdims=True)
    return mx, idx


def _route_kernel(lg_ref, rb_ref, idx_ref, gate_ref, rank_ref, cnt_ref, tri_ref, carry_ref, *, Tt):
    i = pl.program_id(0)
    E = lg_ref.shape[0]
    gs = E // N_GROUPS

    @pl.when(i == 0)
    def _():
        r = lax.broadcasted_iota(I32, (Tt, Tt), 0)
        c = lax.broadcasted_iota(I32, (Tt, Tt), 1)
        tri_ref[...] = jnp.where(r < c, 1.0, 0.0).astype(BF16)
        carry_ref[...] = jnp.zeros_like(carry_ref)

    scores = jax.nn.sigmoid(lg_ref[...])
    sel = scores + rb_ref[...]
    e_io = lax.broadcasted_iota(I32, (E, Tt), 0).astype(F32)
    s_io = lax.broadcasted_iota(I32, (gs, Tt), 0).astype(F32)
    g_io = lax.broadcasted_iota(I32, (N_GROUPS, Tt), 0).astype(F32)

    grp = []
    for g in range(N_GROUPS):
        blk = sel[g * gs:(g + 1) * gs]
        m1, a1 = _first_argmax(blk, s_io, float(gs))
        m2 = jnp.max(jnp.where(s_io == a1, -jnp.inf, blk), axis=0, keepdims=True)
        grp.append(m1 + m2)
    grp = jnp.concatenate(grp, axis=0)

    gsel = jnp.zeros((N_GROUPS, Tt), F32)
    for _ in range(TOPK_GROUPS):
        _, gi = _first_argmax(grp, g_io, float(N_GROUPS))
        hit = g_io == gi
        gsel = jnp.where(hit, 1.0, gsel)
        grp = jnp.where(hit, -jnp.inf, grp)

    esel = jnp.concatenate(
        [jnp.broadcast_to(gsel[g:g + 1], (gs, Tt)) for g in range(N_GROUPS)], axis=0)
    cur = jnp.where(esel > 0.0, sel, NEG_INF)

    idxs, gates = [], []
    onehot = jnp.zeros((E, Tt), F32)
    for _ in range(TOP_K):
        _, ei = _first_argmax(cur, e_io, float(E))
        hit = e_io == ei
        idxs.append(ei)
        gates.append(jnp.sum(jnp.where(hit, scores, 0.0), axis=0, keepdims=True))
        onehot = jnp.where(hit, 1.0, onehot)
        cur = jnp.where(hit, -jnp.inf, cur)
    gate = jnp.concatenate(gates, axis=0)
    gate = gate / jnp.sum(gate, axis=0, keepdims=True) * ROUTED_SCALE
    idx_ref[...] = jnp.concatenate(idxs, axis=0).astype(I32)
    gate_ref[...] = gate

    oh = onehot.astype(BF16)
    before = jnp.dot(oh, tri_ref[...], preferred_element_type=F32)
    tile_cnt = jnp.dot(oh, jnp.ones((Tt, LANES), BF16), preferred_element_type=F32)
    carry = carry_ref[...]
    rank_mat = before + jnp.concatenate([carry] * (Tt // LANES), axis=1)
    ranks = [jnp.sum(jnp.where(e_io == ei, rank_mat, 0.0), axis=0, keepdims=True) for ei in idxs]
    rank_ref[...] = jnp.concatenate(ranks, axis=0).astype(I32)
    carry_ref[...] = carry + tile_cnt
    cnt_ref[...] = (carry + tile_cnt).astype(I32)


def _route(logits_t, router_bias, Tt=512):
    E, N = logits_t.shape
    tok = lambda i: (0, i)
    return pl.pallas_call(
        functools.partial(_route_kernel, Tt=Tt),
        grid=(N // Tt,),
        in_specs=[pl.BlockSpec((E, Tt), tok), pl.BlockSpec((E, 1), lambda i: (0, 0))],
        out_specs=[pl.BlockSpec((TOP_K, Tt), tok), pl.BlockSpec((TOP_K, Tt), tok),
                   pl.BlockSpec((TOP_K, Tt), tok), pl.BlockSpec((E, LANES), lambda i: (0, 0))],
        out_shape=[jax.ShapeDtypeStruct((TOP_K, N), I32), jax.ShapeDtypeStruct((TOP_K, N), F32),
                   jax.ShapeDtypeStruct((TOP_K, N), I32), jax.ShapeDtypeStruct((E, LANES), I32)],
        scratch_shapes=[pltpu.VMEM((Tt, Tt), BF16), pltpu.VMEM((E, LANES), F32)],
        compiler_params=_params(("arbitrary",), 32),
        name="route_topk",
    )(logits_t, router_bias.reshape(E, 1))


def _dest_kernel(pstart_ref, idx_ref, rank_ref, dest_ref, *, G):
    idx = idx_ref[...]
    base = jnp.zeros(idx.shape, I32)
    for e in range(pstart_ref.shape[0]):
        base = jnp.where(idx == e, pstart_ref[e], base)
    dest_ref[...] = (base + rank_ref[...]) * G


def _dest(pstart, idx, rank, G):
    return pl.pallas_call(
        functools.partial(_dest_kernel, G=G),
        in_specs=[pl.BlockSpec(memory_space=pltpu.SMEM), pl.BlockSpec(memory_space=pltpu.VMEM),
                  pl.BlockSpec(memory_space=pltpu.VMEM)],
        out_specs=pl.BlockSpec(memory_space=pltpu.VMEM),
        out_shape=jax.ShapeDtypeStruct(idx.shape, I32),
        name="route_dest",
    )(pstart, idx, rank)


def _dispatch_kernel(dest_hbm, h_ref, xs_hbm, dest_smem, idx_sem, row_sem, *, Td, G):
    i = pl.program_id(0)
    n = Td * TOP_K
    cp = pltpu.make_async_copy(dest_hbm.at[pl.ds(i * n, n)], dest_smem, idx_sem)
    cp.start()
    cp.wait()

    def row_copy(t, k):
        src = h_ref.at[pl.ds(pl.multiple_of(t * G, G), G)]
        dst = xs_hbm.at[pl.ds(pl.multiple_of(dest_smem[t * TOP_K + k], G), G)]
        return pltpu.make_async_copy(src, dst, row_sem)

    def issue(t, carry):
        for k in range(TOP_K):
            row_copy(t, k).start(priority=k % 2)
        return carry

    lax.fori_loop(0, Td, issue, 0)

    def drain(t, carry):
        for k in range(TOP_K):
            row_copy(t, k).wait()
        return carry

    lax.fori_loop(0, Td, drain, 0)


def _dispatch(dest_flat, h2, P, G, Td=512):
    N = h2.shape[0] // G
    return pl.pallas_call(
        functools.partial(_dispatch_kernel, Td=Td, G=G),
        grid=(N // Td,),
        in_specs=[pl.BlockSpec(memory_space=pl.ANY), pl.BlockSpec((Td * G, LANES), lambda i: (i, 0))],
        out_specs=pl.BlockSpec(memory_space=pl.ANY),
        out_shape=jax.ShapeDtypeStruct((P * G, LANES), h2.dtype),
        scratch_shapes=[pltpu.SMEM((Td * TOP_K,), I32), pltpu.SemaphoreType.DMA,
                        pltpu.SemaphoreType.DMA],
        compiler_params=_params(("arbitrary",), 32),
        name="moe_dispatch",
    )(dest_flat, h2)


def _moe_kernel(be_ref, nv_ref, nb_ref, xs_ref, wg_ref, wu_ref, wd_ref, ys_ref, wg_s, wu_s, wd_s):
    b = pl.program_id(0)
    active = b < nb_ref[0]
    new_expert = (b == 0) | (be_ref[b] != be_ref[jnp.maximum(b - 1, 0)])

    @pl.when(active & new_expert)
    def _():
        wg_s[...] = wg_ref[0].astype(BF16)
        wu_s[...] = wu_ref[0].astype(BF16)
        wd_s[...] = wd_ref[0].astype(BF16)

    @pl.when(active)
    def _():
        T = MOE_BLOCK
        G = xs_ref.shape[0] // T
        valid = lax.broadcasted_iota(I32, (T, LANES), 0) < nv_ref[b]
        his, los = [], []
        for c in range(G):
            w = jnp.where(valid, _load_row_slab(xs_ref, c, T, G), jnp.uint32(0))
            hi, lo = _unpack_pairs(w)
            his.append(hi.astype(BF16))
            los.append(lo.astype(BF16))
        x = jnp.concatenate(his + los, axis=1)
        g = jnp.dot(x, wg_s[...], preferred_element_type=F32)
        u = jnp.dot(x, wu_s[...], preferred_element_type=F32)
        a = (g * jax.nn.sigmoid(g) * u).astype(BF16)
        _store_row_slabs(ys_ref, _pack_pairs(jnp.dot(a, wd_s[...], preferred_element_type=F32)))


def _moe(block_expert, nvalid, nblocks, xs, wg, wu, wd):
    _, D, F = wg.shape
    G = D // 2 // LANES
    P = xs.shape[0] // G
    n_blocks = P // MOE_BLOCK
    blk = lambda b, be, nv, nb: (jnp.minimum(b, nb[0] - 1), 0)
    return pl.pallas_call(
        _moe_kernel,
        grid_spec=pltpu.PrefetchScalarGridSpec(
            num_scalar_prefetch=3,
            grid=(n_blocks,),
            in_specs=[pl.BlockSpec((MOE_BLOCK * G, LANES), blk),
                      pl.BlockSpec((1, D, F), lambda b, be, nv, nb: (be[b], 0, 0)),
                      pl.BlockSpec((1, D, F), lambda b, be, nv, nb: (be[b], 0, 0)),
                      pl.BlockSpec((1, F, D), lambda b, be, nv, nb: (be[b], 0, 0))],
            out_specs=pl.BlockSpec((MOE_BLOCK * G, LANES), blk),
            scratch_shapes=[pltpu.VMEM((D, F), BF16), pltpu.VMEM((D, F), BF16),
                            pltpu.VMEM((F, D), BF16)]),
        out_shape=jax.ShapeDtypeStruct((P * G, LANES), U32),
        compiler_params=_params(("arbitrary",), 52),
        name="moe_experts",
    )(block_expert, nvalid, nblocks, xs, wg, wu, wd)


def _combine_kernel(dest_hbm, ys_hbm, gate_ref, h_ref, x_ref, g2_ref, sg_ref, su_ref, sd_ref, fg_ref,
                    o_ref, dest0, dest1, ybuf0, ybuf1, idx_sem, row_sem, *, Tc, G, final_norm):
    j = pl.program_id(0)
    n = Tc * TOP_K
    slots = ((dest0, ybuf0), (dest1, ybuf1))

    def row_copy(slot, t, k):
        dest, ybuf = slots[slot]
        src = ys_hbm.at[pl.ds(pl.multiple_of(dest[t * TOP_K + k], G), G)]
        dst = ybuf.at[k, pl.ds(pl.multiple_of(t * G, G), G)]
        return pltpu.make_async_copy(src, dst, row_sem.at[slot])

    def request(tile, slot):
        cp = pltpu.make_async_copy(dest_hbm.at[pl.ds(tile * n, n)], slots[slot][0],
                                   idx_sem.at[slot])
        cp.start()
        cp.wait()

        def issue(t, carry):
            for k in range(TOP_K):
                row_copy(slot, t, k).start(priority=k % 2)
            return carry

        lax.fori_loop(0, Tc, issue, 0)

    def reduce(slot):
        rows = slice(slot * Tc, (slot + 1) * Tc)
        h_tile = h_ref.at[pl.ds(slot * Tc * G, Tc * G)]
        halves = [_unpack_pairs(_load_row_slab(h_tile, c, Tc, G)) for c in range(G)]
        h = jnp.concatenate([p[0].astype(BF16) for p in halves]
                            + [p[1].astype(BF16) for p in halves], axis=1)
        g = jnp.dot(h, sg_ref[...], preferred_element_type=F32)
        u = jnp.dot(h, su_ref[...], preferred_element_type=F32)
        a = (g * jax.nn.sigmoid(g) * u).astype(BF16)
        f = jnp.dot(a, sd_ref[...], preferred_element_type=F32)

        def drain(t, carry):
            for k in range(TOP_K):
                row_copy(slot, t, k).wait()
            return carry

        lax.fori_loop(0, Tc, drain, 0)
        gate = gate_ref[rows, :]
        half = f.shape[1] // 2
        his, los = [], []
        for c in range(G):
            f_hi = f[:, c * LANES:(c + 1) * LANES]
            f_lo = f[:, half + c * LANES:half + (c + 1) * LANES]
            for k in range(TOP_K):
                y_hi, y_lo = _unpack_pairs(_load_row_slab(slots[slot][1].at[k], c, Tc, G))
                f_hi = f_hi + gate[:, k:k + 1] * y_hi
                f_lo = f_lo + gate[:, k:k + 1] * y_lo
            his.append(f_hi)
            los.append(f_lo)
        x2 = x_ref[rows, :] + g2_ref[0] * jnp.concatenate(his + los, axis=1)
        if final_norm:
            x2 = _rms(x2, NORM_EPS) * fg_ref[...]
        o_ref[rows, :] = x2

    @pl.when(j == 0)
    def _():
        request(0, 0)

    request(2 * j + 1, 1)
    reduce(0)

    @pl.when(j + 1 < pl.num_programs(0))
    def _():
        request(2 * j + 2, 0)

    reduce(1)


def _combine(dest_flat, ys, gate_t, h2, x1, g2, sg, su, sd, final_g, S, final_norm, Tc=256):
    N, D = x1.shape
    F = sg.shape[1]
    G = D // 2 // LANES
    tb = 2 * Tc
    tpb = S // tb
    row = lambda j: (j, 0)
    const = dict(pipeline_mode=pl.Buffered(1))
    return pl.pallas_call(
        functools.partial(_combine_kernel, Tc=Tc, G=G, final_norm=final_norm),
        grid=(N // tb,),
        in_specs=[pl.BlockSpec(memory_space=pl.ANY), pl.BlockSpec(memory_space=pl.ANY),
                  pl.BlockSpec((tb, TOP_K), row),
                  pl.BlockSpec((tb * G, LANES), row), pl.BlockSpec((tb, D), row),
                  pl.BlockSpec((1, 1, D), lambda j: (j // tpb, 0, 0)),
                  pl.BlockSpec((D, F), lambda j: (0, 0), **const),
                  pl.BlockSpec((D, F), lambda j: (0, 0), **const),
                  pl.BlockSpec((F, D), lambda j: (0, 0), **const),
                  pl.BlockSpec((1, D), lambda j: (0, 0))],
        out_specs=pl.BlockSpec((tb, D), row),
        out_shape=jax.ShapeDtypeStruct((N, D), F32),
        scratch_shapes=[pltpu.SMEM((Tc * TOP_K,), I32), pltpu.SMEM((Tc * TOP_K,), I32),
                        pltpu.VMEM((TOP_K, Tc * G, LANES), U32),
                        pltpu.VMEM((TOP_K, Tc * G, LANES), U32),
                        pltpu.SemaphoreType.DMA((2,)), pltpu.SemaphoreType.DMA((2,))],
        compiler_params=_params(("arbitrary",), 52),
        name="moe_combine",
    )(dest_flat, ys, gate_t, h2, x1, g2, sg, su, sd, final_g.reshape(1, D))


def _moe_ffn(h2, logits_t, x1, g2, router_bias, wg, wu, wd, e_off, sg, su, sd, final_g, S,
             final_norm):
    N, D = x1.shape
    G = D // 2 // LANES
    E = logits_t.shape[0]
    idx, gate, rank, cnt = _route(logits_t, router_bias)
    counts = cnt[:, 0]
    padded = (counts + MOE_BLOCK - 1) // MOE_BLOCK * MOE_BLOCK
    pend = jnp.cumsum(padded)
    pstart = (pend - padded).astype(I32)
    n_blocks = -(-(N * TOP_K + E * (MOE_BLOCK - 1)) // MOE_BLOCK)
    bstart = jnp.arange(n_blocks, dtype=I32) * MOE_BLOCK
    block_expert = jnp.minimum(jnp.sum(pend[None, :] <= bstart[:, None], axis=1), E - 1).astype(I32)
    nvalid = jnp.clip(pstart[block_expert] + counts[block_expert] - bstart, 0, MOE_BLOCK).astype(I32)
    nblocks = (pend[-1:] // MOE_BLOCK).astype(I32)

    dest = _dest(pstart, idx, rank, G)
    dest_flat = dest.T.reshape(N * TOP_K)
    xs = _dispatch(dest_flat, h2, n_blocks * MOE_BLOCK, G)
    ys = _moe(block_expert + e_off, nvalid, nblocks, xs, wg, wu, wd)
    return _combine(dest_flat, ys, gate.T, h2, x1, g2, sg, su, sd, final_g, S, final_norm)


def kernel(x, c, ada_w, ada_b, norm_mix_g, norm_ffn_g, w_in, lambda_q1, lambda_k1, lambda_q2,
           lambda_k2, subln_g, rpb, w_out, router_w, router_bias, exp_w_gate, exp_w_up,
           exp_w_down, sh_w_gate, sh_w_up, sh_w_down, final_g):
    B, S, D = x.shape
    L = ada_w.shape[0]
    N = B * S
    n_heads = w_in.shape[2] // (3 * HEAD_DIM)
    heads_a = n_heads // HEADS_A_FRAC
    heads_b = n_heads // HEADS_B_FRAC
    heads_c = n_heads - heads_a - heads_b
    slopes = _alibi_slopes(heads_a + heads_b)
    slopes_a = jnp.asarray(slopes[:heads_a], F32)
    slopes_b = jnp.asarray(slopes[heads_a:], F32)

    E, F = exp_w_gate.shape[1], exp_w_gate.shape[3]
    wg_all = exp_w_gate.reshape(L * E, D, F)
    wu_all = exp_w_up.reshape(L * E, D, F)
    wd_all = exp_w_down.reshape(L * E, F, D)

    mod = _ada(c, ada_w, ada_b).reshape(L, B, 6, 1, D)
    xf = x.reshape(N, D)
    for l in range(L):
        sh1, sc1, g1, sh2, sc2, g2 = (mod[l, :, j] for j in range(6))
        qkv = _qkv(xf, norm_mix_g[l], sc1, sh1, w_in[l].astype(BF16), S)
        oa = _attn_a(qkv, slopes_a, B, S, n_heads, heads_a)
        lambda_init = 0.8 - 0.6 * math.exp(-0.3 * l)
        lam_vecs = jnp.stack([lambda_q1[l], lambda_k1[l], lambda_q2[l], lambda_k2[l]]).astype(F32)
        ob = _attn_b(qkv, slopes_b, lam_vecs, subln_g[l], B, S, n_heads, heads_a, heads_b, lambda_init)
        oc = _attn_c(qkv, rpb[l].reshape(-1), B, S, n_heads, heads_a + heads_b, heads_c)
        x1, h2, logits_t = _out_proj(oa, ob, oc, w_out[l].astype(BF16), xf, g1, norm_ffn_g[l],
                                     sc2, sh2, router_w[l].T, S)
        xf = _moe_ffn(h2, logits_t, x1, g2, router_bias[l], wg_all, wu_all, wd_all, l * E,
                      sh_w_gate[l].astype(BF16), sh_w_up[l].astype(BF16),
                      sh_w_down[l].astype(BF16), final_g, S, l == L - 1)
    return xf.reshape(B, S, D)
```

```python
import functools
import math

import jax
import jax.numpy as jnp
from jax import lax
from jax.experimental import pallas as pl
from jax.experimental.pallas import tpu as pltpu

F32 = jnp.float32
BF16 = jnp.bfloat16
I32 = jnp.int32
U32 = jnp.uint32
HIGHEST = lax.Precision.HIGHEST
_NT = (((1,), (1,)), ((), ()))

HEAD_DIM = 128
HEADS_A_FRAC, HEADS_B_FRAC = 2, 4
DILATED_PATTERNS = ((128, 1), (512, 4), (2048, 16))
GRID_W = 64
NA_ROWS = 8
NA_COLS = 16
N_GROUPS = 8
TOPK_GROUPS = 4
TOP_K = 8
ROUTED_SCALE = 2.5
NORM_EPS = 1e-6
SUBLN_EPS = 1e-5
NEG_INF = -1e30
LOG2E = math.log2(math.e)

LANES = 128
VMEM_BYTES_V7X = 64 << 20

MOE_BLOCK = 512


def _params(semantics, vmem_mib):
    return pltpu.CompilerParams(dimension_semantics=semantics,
                                vmem_limit_bytes=min(vmem_mib << 20, VMEM_BYTES_V7X - (4 << 20)))


def _alibi_slopes(n):
    def pow2(m):
        start = 2.0 ** (-8.0 / m)
        return [start ** (i + 1) for i in range(m)]
    p = 2 ** int(math.floor(math.log2(n)))
    return pow2(p) + pow2(2 * p)[0::2][: n - p]


def _rms(x, eps):
    return x * lax.rsqrt(jnp.mean(x * x, axis=-1, keepdims=True) + eps)


def _pack_pairs(x):
    half = x.shape[1] // 2
    hi = lax.bitcast_convert_type(x[:, :half].astype(BF16).astype(F32), U32)
    lo = lax.bitcast_convert_type(x[:, half:].astype(BF16).astype(F32), U32)
    return hi | (lo >> 16)


def _unpack_pairs(w):
    hi = lax.bitcast_convert_type(w & jnp.uint32(0xFFFF0000), F32)
    lo = lax.bitcast_convert_type(w << 16, F32)
    return hi, lo


def _store_row_slabs(ref, packed):
    T, W = packed.shape
    for c in range(W // LANES):
        ref[pl.ds(c, T, stride=W // LANES), :] = packed[:, c * LANES:(c + 1) * LANES]


def _load_row_slab(ref, c, T, G):
    return ref[pl.ds(c, T, stride=G), :]


def _ada_kernel(c_ref, w_ref, b_ref, o_ref):
    c = c_ref[...]
    sc = c * jax.nn.sigmoid(c)
    o_ref[0] = jnp.dot(sc, w_ref[0], preferred_element_type=F32, precision=HIGHEST) + b_ref[0]


def _ada(c, ada_w, ada_b):
    B, D = c.shape
    L, _, D6 = ada_w.shape
    rows = 8
    cp = jnp.zeros((rows, D), F32).at[:B].set(c)
    tn = 1024
    out = pl.pallas_call(
        _ada_kernel,
        grid=(L, D6 // tn),
        in_specs=[pl.BlockSpec((rows, D), lambda l, j: (0, 0)),
                  pl.BlockSpec((1, D, tn), lambda l, j: (l, 0, j)),
                  pl.BlockSpec((1, 1, tn), lambda l, j: (l, 0, j))],
        out_specs=pl.BlockSpec((1, rows, tn), lambda l, j: (l, 0, j)),
        out_shape=jax.ShapeDtypeStruct((L, rows, D6), F32),
        compiler_params=_params(("arbitrary", "arbitrary"), 32),
        name="ada_mod",
    )(cp, ada_w, ada_b.reshape(L, 1, D6))
    return out[:, :B]


def _qkv_kernel(x_ref, g_ref, sc_ref, sh_ref, w_ref, o_ref, h_scr, *, heads_per_step):
    @pl.when(pl.program_id(1) == 0)
    def _():
        y = _rms(x_ref[...], NORM_EPS) * g_ref[...]
        h_scr[...] = (y * (1.0 + sc_ref[0]) + sh_ref[0]).astype(BF16)

    res = jnp.dot(h_scr[...], w_ref[...], preferred_element_type=F32)
    for hh in range(heads_per_step):
        o_ref[hh] = res[:, hh * HEAD_DIM:(hh + 1) * HEAD_DIM].astype(BF16)


def _qkv(x2d, g, sc, sh, w_bf, S, tm=512, tn=2048):
    N, D = x2d.shape
    W3 = w_bf.shape[1]
    hps = tn // HEAD_DIM
    tpb = S // tm
    return pl.pallas_call(
        functools.partial(_qkv_kernel, heads_per_step=hps),
        grid=(N // tm, W3 // tn),
        in_specs=[pl.BlockSpec((tm, D), lambda i, j: (i, 0)),
                  pl.BlockSpec((1, D), lambda i, j: (0, 0)),
                  pl.BlockSpec((1, 1, D), lambda i, j: (i // tpb, 0, 0)),
                  pl.BlockSpec((1, 1, D), lambda i, j: (i // tpb, 0, 0)),
                  pl.BlockSpec((D, tn), lambda i, j: (0, j))],
        out_specs=pl.BlockSpec((hps, tm, HEAD_DIM), lambda i, j: (j, i, 0)),
        out_shape=jax.ShapeDtypeStruct((W3 // HEAD_DIM, N, HEAD_DIM), BF16),
        scratch_shapes=[pltpu.VMEM((tm, D), BF16)],
        compiler_params=_params(("arbitrary", "arbitrary"), 48),
        name="qkv_proj",
    )(x2d, g.reshape(1, D), sc, sh, w_bf)


def _col_to_row(col, eye):
    return jnp.sum(jnp.where(eye, col, 0.0), axis=0, keepdims=True)


def _attn_a_kernel(slopes_ref, q_ref, k_ref, v_ref, o_ref, lse_ref, nat, cq, ck, cv, onat, tmp, *,
                   L, dil, half, Tq):
    h = pl.program_id(1)
    wk = 2 * Tq
    n_tiles = L // Tq
    c_dist = LOG2E * slopes_ref[h] * dil
    a = lax.broadcasted_iota(I32, (Tq, wk), 0)
    u = lax.broadcasted_iota(I32, (Tq, wk), 1)
    eye = lax.broadcasted_iota(I32, (Tq, Tq), 0) == lax.broadcasted_iota(I32, (Tq, Tq), 1)

    def table(off):
        d = u - a - off
        ad = jnp.maximum(d, -d)
        return jnp.where(ad <= half, -c_dist * ad.astype(F32), NEG_INF)

    tabs = {off: table(off) for off in sorted({0, half, wk - Tq})}

    q_scale = LOG2E * HEAD_DIM ** -0.5
    d1 = 4 if dil > 4 else dil
    d2 = dil // d1
    S = L * dil
    if dil > 1:
        for src, dst, scale in ((q_ref, cq, q_scale), (k_ref, ck, None), (v_ref, cv, None)):
            x = src[0].astype(F32)
            nat[...] = x if scale is None else x * scale
            if d2 == 1:
                for r in range(dil):
                    dst[r * L:(r + 1) * L, :] = nat[pl.ds(r, L, stride=dil), :].astype(BF16)
            else:
                for ra in range(d1):
                    tmp[ra * (S // d1):(ra + 1) * (S // d1), :] = nat[pl.ds(ra, S // d1, stride=d1), :]
                for ra in range(d1):
                    for rb in range(d2):
                        r = ra + d1 * rb
                        dst[r * L:(r + 1) * L, :] = tmp[
                            pl.ds(ra * (S // d1) + rb, L, stride=d2), :].astype(BF16)
        kc, vc = ck, cv
    else:
        kc, vc = k_ref.at[0], v_ref.at[0]

    for r in range(dil):
        for j in range(n_tiles):
            start = min(max(j * Tq - half, 0), L - wk)
            rows = slice(r * L + j * Tq, r * L + (j + 1) * Tq)
            q = cq[rows, :] if dil > 1 else (q_ref[0, rows, :].astype(F32) * q_scale).astype(BF16)
            kw = kc[r * L + start:r * L + start + wk, :]
            vw = vc[r * L + start:r * L + start + wk, :]
            s = lax.dot_general(q, kw, _NT, preferred_element_type=F32) + tabs[j * Tq - start]
            m = jnp.max(s, axis=-1, keepdims=True)
            p = jnp.exp2(s - m)
            l = jnp.sum(p, axis=-1, keepdims=True)
            o = jnp.dot(p.astype(BF16), vw, preferred_element_type=F32) / l
            if dil == 1:
                o_ref[0, rows, :] = o.astype(BF16)
            elif d2 == 1:
                onat[pl.ds(r + j * Tq * dil, Tq, stride=dil), :] = o
            else:
                ra, rb = r % d1, r // d1
                tmp[pl.ds(ra * (S // d1) + rb + j * Tq * d2, Tq, stride=d2), :] = o
            lse_ref[0, 0, r:r + 1, j * Tq:(j + 1) * Tq] = _col_to_row(m + jnp.log2(l), eye)
    if d2 > 1:
        for ra in range(d1):
            onat[pl.ds(ra, S // d1, stride=d1), :] = tmp[ra * (S // d1):(ra + 1) * (S // d1), :]
    if dil > 1:
        o_ref[0] = onat[...].astype(BF16)


def _attn_a_pattern(qkv, slopes, B, S, n_heads, heads_a, window, dil, Tq=128):
    N = B * S
    L = S // dil
    half = window // (2 * dil)
    assert L % Tq == 0 and L >= 2 * Tq and 2 * half <= Tq
    blk = (1, S, HEAD_DIM)
    o, lse = pl.pallas_call(
        functools.partial(_attn_a_kernel, L=L, dil=dil, half=half, Tq=Tq),
        grid=(B, heads_a),
        in_specs=[pl.BlockSpec(memory_space=pltpu.SMEM),
                  pl.BlockSpec(blk, lambda b, h: (h, b, 0)),
                  pl.BlockSpec(blk, lambda b, h: (n_heads + h, b, 0)),
                  pl.BlockSpec(blk, lambda b, h: (2 * n_heads + h, b, 0))],
        out_specs=[pl.BlockSpec(blk, lambda b, h: (h, b, 0)),
                   pl.BlockSpec((1, 1, dil, L), lambda b, h: (b, h, 0, 0))],
        out_shape=[jax.ShapeDtypeStruct((heads_a, N, HEAD_DIM), BF16),
                   jax.ShapeDtypeStruct((B, heads_a, dil, L), F32)],
        scratch_shapes=[pltpu.VMEM((S, HEAD_DIM), F32)] + [pltpu.VMEM((S, HEAD_DIM), BF16)] * 3
        + [pltpu.VMEM((S, HEAD_DIM), F32)] * 2,
        compiler_params=_params(("arbitrary", "arbitrary"), 32),
        name=f"attn_dilated_d{dil}",
    )(slopes, qkv, qkv, qkv)
    lse_nat = lse.transpose(1, 0, 3, 2).reshape(heads_a, N)
    return o, lse_nat


def _mix_a_kernel(*refs, n_pat, tm):
    o_refs, lse_refs, out_ref = refs[:n_pat], refs[n_pat:2 * n_pat], refs[2 * n_pat]
    heads = lse_refs[0].shape[0]
    lses = [r[...] for r in lse_refs]
    mx = functools.reduce(jnp.maximum, lses)
    es = [jnp.exp2(x - mx) for x in lses]
    den = functools.reduce(lambda x, y: x + y, es)
    ws = [e / den for e in es]
    pad = jnp.zeros((LANES - n_pat * heads, LANES), F32)
    for c in range(tm // LANES):
        cols = slice(c * LANES, (c + 1) * LANES)
        w_rows = jnp.concatenate([w[:, cols] for w in ws] + [pad], axis=0)
        w_cols = w_rows.T
        for hh in range(heads):
            acc = jnp.zeros((LANES, HEAD_DIM), F32)
            for p in range(n_pat):
                wc = w_cols[:, p * heads + hh:p * heads + hh + 1]
                acc = acc + wc * o_refs[p][hh, cols, :].astype(F32)
            out_ref[cols, hh * HEAD_DIM:(hh + 1) * HEAD_DIM] = acc.astype(BF16)


def _mix_a(os, lses, tm=512):
    heads, N, _ = os[0].shape
    n_pat = len(os)
    assert n_pat * heads <= LANES
    return pl.pallas_call(
        functools.partial(_mix_a_kernel, n_pat=n_pat, tm=tm),
        grid=(N // tm,),
        in_specs=[pl.BlockSpec((heads, tm, HEAD_DIM), lambda i: (0, i, 0))] * n_pat
        + [pl.BlockSpec((heads, tm), lambda i: (0, i))] * n_pat,
        out_specs=pl.BlockSpec((tm, heads * HEAD_DIM), lambda i: (i, 0)),
        out_shape=jax.ShapeDtypeStruct((N, heads * HEAD_DIM), BF16),
        compiler_params=_params(("arbitrary",), 32),
        name="attn_dilated_mix",
    )(*os, *lses)


def _attn_a(qkv, slopes, B, S, n_heads, heads_a):
    parts = [_attn_a_pattern(qkv, slopes, B, S, n_heads, heads_a, w, d) for w, d in DILATED_PATTERNS]
    return _mix_a([p[0] for p in parts], [p[1] for p in parts])


def _attn_b_kernel(slopes_ref, lam_ref, q_ref, k_ref, v_ref, g_ref, o_ref, tab_ref, kmax_ref,
                   m_scr, l_scr, acc_scr, *, S, Tq, Tk, lambda_init):
    h = pl.program_id(1)
    i = pl.program_id(2)
    dh = HEAD_DIM // 2
    n_chunks = S // Tk
    c_dist = LOG2E * slopes_ref[h]

    def sub_norm(x):
        sq = x * x
        return jnp.sqrt(jnp.maximum(jnp.sum(sq[:, :dh], axis=-1, keepdims=True),
                                    jnp.sum(sq[:, dh:], axis=-1, keepdims=True)))

    @pl.when(i == 0)
    def _():
        shape = (Tq, 2 * S - Tq)
        a = lax.broadcasted_iota(I32, shape, 0)
        u = lax.broadcasted_iota(I32, shape, 1)
        d = a - u + (S - Tq)
        tab_ref[...] = -c_dist * jnp.maximum(d, -d).astype(F32)
        kn = sub_norm(k_ref[0].astype(F32))
        for j in range(n_chunks):
            kmax_ref[j] = jnp.max(kn[j * Tk:(j + 1) * Tk])

    lv = lam_ref[...]
    lam = (jnp.exp(jnp.sum(lv[0:1] * lv[1:2], axis=-1, keepdims=True))
           - jnp.exp(jnp.sum(lv[2:3] * lv[3:4], axis=-1, keepdims=True)) + lambda_init)

    t0 = i * Tq
    qf = q_ref[0].astype(F32) * (LOG2E * dh ** -0.5)
    q = qf.astype(BF16)
    qmax = jnp.max(sub_norm(q.astype(F32)))

    def chunk(kstart, first):
        kc = k_ref[0, pl.ds(kstart, Tk), :]
        vc = v_ref[0, pl.ds(kstart, Tk), :]
        bias = tab_ref[:, pl.ds(pl.multiple_of(S - Tq - t0 + kstart, LANES), Tk)]
        s = jnp.concatenate(
            [lax.dot_general(q[:, c * dh:(c + 1) * dh], kc[:, c * dh:(c + 1) * dh], _NT,
                             preferred_element_type=F32) + bias for c in range(2)], axis=0)
        mc = jnp.max(s, axis=-1, keepdims=True)
        if first:
            m_new = mc
        else:
            m_old = m_scr[...]
            m_new = jnp.maximum(m_old, mc)
            alpha = jnp.exp2(m_old - m_new)
        p = jnp.exp2(s - m_new)
        ls = jnp.sum(p, axis=-1, keepdims=True)
        pv = jnp.dot(p.astype(BF16), vc, preferred_element_type=F32)
        m_scr[...] = m_new
        l_scr[...] = ls if first else alpha * l_scr[...] + ls
        acc_scr[...] = pv if first else alpha * acc_scr[...] + pv
        return m_new

    jd = t0 // Tk
    m_diag = chunk(pl.multiple_of(jd * Tk, Tk), True)
    floor = jnp.min(m_diag) - 160.0
    for j in range(n_chunks):
        gap = jnp.maximum(jnp.maximum(j * Tk - (t0 + Tq - 1), t0 - ((j + 1) * Tk - 1)), 0)
        bound = 1.001 * qmax * kmax_ref[j] + 0.01 - c_dist * gap.astype(F32)

        @pl.when((j != jd) & (bound > floor))
        def _():
            chunk(j * Tk, False)

    acc = acc_scr[...]
    l = l_scr[...]
    o = acc[:Tq] / l[:Tq] - acc[Tq:] * (lam / l[Tq:])
    o = _rms(o, SUBLN_EPS) * g_ref[...] * (1.0 - lambda_init)
    o_ref[...] = o.astype(BF16)


def _attn_b(qkv, slopes, lam_vecs, subln_g, B, S, n_heads, head0, heads_b, lambda_init, Tq=256,
            Tk=1024):
    N = B * S
    nq = S // Tq
    Tk = min(Tk, S)
    return pl.pallas_call(
        functools.partial(_attn_b_kernel, S=S, Tq=Tq, Tk=Tk, lambda_init=lambda_init),
        grid=(B, heads_b, nq),
        in_specs=[pl.BlockSpec(memory_space=pltpu.SMEM),
                  pl.BlockSpec((4, HEAD_DIM // 2), lambda b, h, i: (0, 0)),
                  pl.BlockSpec((1, Tq, HEAD_DIM), lambda b, h, i: (head0 + h, b * nq + i, 0)),
                  pl.BlockSpec((1, S, HEAD_DIM), lambda b, h, i: (n_heads + head0 + h, b, 0)),
                  pl.BlockSpec((1, S, HEAD_DIM), lambda b, h, i: (2 * n_heads + head0 + h, b, 0)),
                  pl.BlockSpec((1, HEAD_DIM), lambda b, h, i: (0, 0))],
        out_specs=pl.BlockSpec((Tq, HEAD_DIM), lambda b, h, i: (b * nq + i, h)),
        out_shape=jax.ShapeDtypeStruct((N, heads_b * HEAD_DIM), BF16),
        scratch_shapes=[pltpu.VMEM((Tq, 2 * S - Tq), F32), pltpu.SMEM((S // Tk,), F32),
                        pltpu.VMEM((2 * Tq, 1), F32), pltpu.VMEM((2 * Tq, 1), F32),
                        pltpu.VMEM((2 * Tq, HEAD_DIM), F32)],
        compiler_params=_params(("arbitrary", "arbitrary", "arbitrary"), 48),
        name="attn_diff",
    )(slopes, lam_vecs, qkv, qkv, qkv, subln_g.reshape(1, HEAD_DIM))


def _nbr_pairs(R):
    kr, win = NA_ROWS, NA_ROWS + 2
    pairs = []
    for r0 in range(0, R, 2):
        ps = min(max(r0 - kr // 2, 0), R - win)
        rel = tuple(min(max(r0 + a - kr // 2, 0), R - kr) - ps for a in range(2))
        pairs.append((r0, ps, (ps - r0,) + rel))
    return pairs, sorted({p[2] for p in pairs})


def _attn_c_kernel(rpb_ref, q_ref, k_ref, v_ref, o_ref, tab_ref, *, R):
    h = pl.program_id(1)
    W = GRID_W
    kr = min(NA_ROWS, R)
    n_dr = 2 * NA_ROWS - 1
    n_dc = 2 * NA_COLS - 1

    c_io = lax.broadcasted_iota(I32, (W, 2 * W), 0)
    l_io = lax.broadcasted_iota(I32, (W, 2 * W), 1)
    cp = l_io & (W - 1)
    dcm = cp - c_io + (NA_COLS - 1)
    cstart = jnp.clip(c_io - NA_COLS // 2, 0, W - NA_COLS)
    ok = (cp >= cstart) & (cp < cstart + NA_COLS)
    blocks = []
    for dr in range(n_dr):
        blk = jnp.zeros((W, 2 * W), F32)
        for dc in range(n_dc):
            blk = jnp.where(dcm == dc, rpb_ref[(h * n_dr + dr) * n_dc + dc], blk)
        blocks.append(jnp.where(ok, blk, NEG_INF))
    win = kr + 2
    pairs, cfgs = _nbr_pairs(R)
    neg = jnp.full((W, 2 * W), NEG_INF, F32)
    for ti, (off, *rels) in enumerate(cfgs):
        for a, rel in enumerate(rels):
            def key_row_block(j):
                return blocks[off + j - a + (NA_ROWS - 1)] if rel <= j < rel + kr else neg
            for jj in range(win // 2):
                tab_ref[ti, a * W:(a + 1) * W, jj * 2 * W:(jj + 1) * 2 * W] = jnp.where(
                    l_io < W, key_row_block(2 * jj), key_row_block(2 * jj + 1))

    for r0, ps, cfg in pairs:
        rows = slice(r0 * W, (r0 + 2) * W)
        kw = k_ref[0, ps * W:(ps + win) * W, :]
        vw = v_ref[0, ps * W:(ps + win) * W, :]
        s = lax.dot_general(q_ref[0, rows, :], kw, _NT, preferred_element_type=F32)
        s = s * (HEAD_DIM ** -0.5) + tab_ref[cfgs.index(cfg)]
        m = jnp.max(s, axis=-1, keepdims=True)
        p = jnp.exp(s - m)
        l = jnp.sum(p, axis=-1, keepdims=True)
        o = jnp.dot(p.astype(BF16), vw, preferred_element_type=F32) / l
        o_ref[rows, :] = o.astype(BF16)


def _attn_c(qkv, rpb_flat, B, S, n_heads, head0, heads_c):
    N = B * S
    R = S // GRID_W
    kr = NA_ROWS
    assert R % 2 == 0 and R >= kr + 2 and kr % 2 == 0
    n_tabs = len(_nbr_pairs(R)[1])
    return pl.pallas_call(
        functools.partial(_attn_c_kernel, R=R),
        grid=(B, heads_c),
        in_specs=[pl.BlockSpec(memory_space=pltpu.SMEM),
                  pl.BlockSpec((1, S, HEAD_DIM), lambda b, h: (head0 + h, b, 0)),
                  pl.BlockSpec((1, S, HEAD_DIM), lambda b, h: (n_heads + head0 + h, b, 0)),
                  pl.BlockSpec((1, S, HEAD_DIM), lambda b, h: (2 * n_heads + head0 + h, b, 0))],
        out_specs=pl.BlockSpec((S, HEAD_DIM), lambda b, h: (b, h)),
        out_shape=jax.ShapeDtypeStruct((N, heads_c * HEAD_DIM), BF16),
        scratch_shapes=[pltpu.VMEM((n_tabs, 2 * GRID_W, (kr + 2) * GRID_W), F32)],
        compiler_params=_params(("arbitrary", "arbitrary"), 32),
        name="attn_nbr",
    )(rpb_flat, qkv, qkv, qkv)


def _out_kernel(oa_ref, ob_ref, oc_ref, w_ref, x_ref, g1_ref, ng_ref, sc_ref, sh_ref, rw_ref,
                x1_ref, h2_ref, lg_ref):
    o = jnp.concatenate([oa_ref[...], ob_ref[...], oc_ref[...]], axis=1)
    acc = jnp.dot(o, w_ref[...], preferred_element_type=F32)
    x1 = x_ref[...] + g1_ref[0] * acc
    x1_ref[...] = x1
    h2 = (_rms(x1, NORM_EPS) * ng_ref[...]) * (1.0 + sc_ref[0]) + sh_ref[0]
    _store_row_slabs(h2_ref, _pack_pairs(h2))
    lg_ref[...] = lax.dot_general(rw_ref[...], h2, _NT, preferred_element_type=F32,
                                  precision=HIGHEST)


def _out_proj(oa, ob, oc, w_bf, x2d, g1, ng, sc, sh, rw_t, S, tm=512):
    N, D = x2d.shape
    wa, wb, wc = oa.shape[1], ob.shape[1], oc.shape[1]
    E = rw_t.shape[0]
    tpb = S // tm
    const = dict(pipeline_mode=pl.Buffered(1))
    row = lambda i: (i, 0)
    per_b = lambda i: (i // tpb, 0, 0)
    return pl.pallas_call(
        _out_kernel,
        grid=(N // tm,),
        in_specs=[pl.BlockSpec((tm, wa), row), pl.BlockSpec((tm, wb), row), pl.BlockSpec((tm, wc), row),
                  pl.BlockSpec((wa + wb + wc, D), lambda i: (0, 0), **const),
                  pl.BlockSpec((tm, D), row),
                  pl.BlockSpec((1, 1, D), per_b),
                  pl.BlockSpec((1, D), lambda i: (0, 0)),
                  pl.BlockSpec((1, 1, D), per_b),
                  pl.BlockSpec((1, 1, D), per_b),
                  pl.BlockSpec((E, D), lambda i: (0, 0), **const)],
        out_specs=[pl.BlockSpec((tm, D), row), pl.BlockSpec((tm * (D // 2 // LANES), LANES), row),
                   pl.BlockSpec((E, tm), lambda i: (0, i))],
        out_shape=[jax.ShapeDtypeStruct((N, D), F32),
                   jax.ShapeDtypeStruct((N * (D // 2 // LANES), LANES), U32),
                   jax.ShapeDtypeStruct((E, N), F32)],
        compiler_params=_params(("arbitrary",), 52),
        name="out_proj",
    )(oa, ob, oc, w_bf, x2d, g1, ng.reshape(1, D), sc, sh, rw_t)


def _first_argmax(vals, iota, big):
    mx = jnp.max(vals, axis=0, keepdims=True)
    idx = jnp.min(jnp.where(vals == mx, iota, big), axis=0, keepdims=True)
    return mx, idx


def _route_kernel(lg_ref, rb_ref, idx_ref, gate_ref, rank_ref, cnt_ref, tri_ref, carry_ref, *, Tt):
    i = pl.program_id(0)
    E = lg_ref.shape[0]
    gs = E // N_GROUPS

    @pl.when(i == 0)
    def _():
        r = lax.broadcasted_iota(I32, (Tt, Tt), 0)
        c = lax.broadcasted_iota(I32, (Tt, Tt), 1)
        tri_ref[...] = jnp.where(r < c, 1.0, 0.0).astype(BF16)
        carry_ref[...] = jnp.zeros_like(carry_ref)

    scores = jax.nn.sigmoid(lg_ref[...])
    sel = scores + rb_ref[...]
    e_io = lax.broadcasted_iota(I32, (E, Tt), 0).astype(F32)
    s_io = lax.broadcasted_iota(I32, (gs, Tt), 0).astype(F32)
    g_io = lax.broadcasted_iota(I32, (N_GROUPS, Tt), 0).astype(F32)

    grp = []
    for g in range(N_GROUPS):
        blk = sel[g * gs:(g + 1) * gs]
        m1, a1 = _first_argmax(blk, s_io, float(gs))
        m2 = jnp.max(jnp.where(s_io == a1, -jnp.inf, blk), axis=0, keepdims=True)
        grp.append(m1 + m2)
    grp = jnp.concatenate(grp, axis=0)

    gsel = jnp.zeros((N_GROUPS, Tt), F32)
    for _ in range(TOPK_GROUPS):
        _, gi = _first_argmax(grp, g_io, float(N_GROUPS))
        hit = g_io == gi
        gsel = jnp.where(hit, 1.0, gsel)
        grp = jnp.where(hit, -jnp.inf, grp)

    esel = jnp.concatenate(
        [jnp.broadcast_to(gsel[g:g + 1], (gs, Tt)) for g in range(N_GROUPS)], axis=0)
    cur = jnp.where(esel > 0.0, sel, NEG_INF)

    idxs, gates = [], []
    onehot = jnp.zeros((E, Tt), F32)
    for _ in range(TOP_K):
        _, ei = _first_argmax(cur, e_io, float(E))
        hit = e_io == ei
        idxs.append(ei)
        gates.append(jnp.sum(jnp.where(hit, scores, 0.0), axis=0, keepdims=True))
        onehot = jnp.where(hit, 1.0, onehot)
        cur = jnp.where(hit, -jnp.inf, cur)
    gate = jnp.concatenate(gates, axis=0)
    gate = gate / jnp.sum(gate, axis=0, keepdims=True) * ROUTED_SCALE
    idx_ref[...] = jnp.concatenate(idxs, axis=0).astype(I32)
    gate_ref[...] = gate

    oh = onehot.astype(BF16)
    before = jnp.dot(oh, tri_ref[...], preferred_element_type=F32)
    tile_cnt = jnp.dot(oh, jnp.ones((Tt, LANES), BF16), preferred_element_type=F32)
    carry = carry_ref[...]
    rank_mat = before + jnp.concatenate([carry] * (Tt // LANES), axis=1)
    ranks = [jnp.sum(jnp.where(e_io == ei, rank_mat, 0.0), axis=0, keepdims=True) for ei in idxs]
    rank_ref[...] = jnp.concatenate(ranks, axis=0).astype(I32)
    carry_ref[...] = carry + tile_cnt
    cnt_ref[...] = (carry + tile_cnt).astype(I32)


def _route(logits_t, router_bias, Tt=512):
    E, N = logits_t.shape
    tok = lambda i: (0, i)
    return pl.pallas_call(
        functools.partial(_route_kernel, Tt=Tt),
        grid=(N // Tt,),
        in_specs=[pl.BlockSpec((E, Tt), tok), pl.BlockSpec((E, 1), lambda i: (0, 0))],
        out_specs=[pl.BlockSpec((TOP_K, Tt), tok), pl.BlockSpec((TOP_K, Tt), tok),
                   pl.BlockSpec((TOP_K, Tt), tok), pl.BlockSpec((E, LANES), lambda i: (0, 0))],
        out_shape=[jax.ShapeDtypeStruct((TOP_K, N), I32), jax.ShapeDtypeStruct((TOP_K, N), F32),
                   jax.ShapeDtypeStruct((TOP_K, N), I32), jax.ShapeDtypeStruct((E, LANES), I32)],
        scratch_shapes=[pltpu.VMEM((Tt, Tt), BF16), pltpu.VMEM((E, LANES), F32)],
        compiler_params=_params(("arbitrary",), 32),
        name="route_topk",
    )(logits_t, router_bias.reshape(E, 1))


def _dest_kernel(pstart_ref, idx_ref, rank_ref, dest_ref, *, G):
    idx = idx_ref[...]
    base = jnp.zeros(idx.shape, I32)
    for e in range(pstart_ref.shape[0]):
        base = jnp.where(idx == e, pstart_ref[e], base)
    dest_ref[...] = (base + rank_ref[...]) * G


def _dest(pstart, idx, rank, G):
    return pl.pallas_call(
        functools.partial(_dest_kernel, G=G),
        in_specs=[pl.BlockSpec(memory_space=pltpu.SMEM), pl.BlockSpec(memory_space=pltpu.VMEM),
                  pl.BlockSpec(memory_space=pltpu.VMEM)],
        out_specs=pl.BlockSpec(memory_space=pltpu.VMEM),
        out_shape=jax.ShapeDtypeStruct(idx.shape, I32),
        name="route_dest",
    )(pstart, idx, rank)


def _dispatch_kernel(dest_hbm, h_ref, xs_hbm, dest_smem, idx_sem, row_sem, *, Td, G):
    i = pl.program_id(0)
    n = Td * TOP_K
    cp = pltpu.make_async_copy(dest_hbm.at[pl.ds(i * n, n)], dest_smem, idx_sem)
    cp.start()
    cp.wait()

    def row_copy(t, k):
        src = h_ref.at[pl.ds(pl.multiple_of(t * G, G), G)]
        dst = xs_hbm.at[pl.ds(pl.multiple_of(dest_smem[t * TOP_K + k], G), G)]
        return pltpu.make_async_copy(src, dst, row_sem)

    def issue(t, carry):
        for k in range(TOP_K):
            row_copy(t, k).start(priority=k % 2)
        return carry

    lax.fori_loop(0, Td, issue, 0)

    def drain(t, carry):
        for k in range(TOP_K):
            row_copy(t, k).wait()
        return carry

    lax.fori_loop(0, Td, drain, 0)


def _dispatch(dest_flat, h2, P, G, Td=512):
    N = h2.shape[0] // G
    return pl.pallas_call(
        functools.partial(_dispatch_kernel, Td=Td, G=G),
        grid=(N // Td,),
        in_specs=[pl.BlockSpec(memory_space=pl.ANY), pl.BlockSpec((Td * G, LANES), lambda i: (i, 0))],
        out_specs=pl.BlockSpec(memory_space=pl.ANY),
        out_shape=jax.ShapeDtypeStruct((P * G, LANES), h2.dtype),
        scratch_shapes=[pltpu.SMEM((Td * TOP_K,), I32), pltpu.SemaphoreType.DMA,
                        pltpu.SemaphoreType.DMA],
        compiler_params=_params(("arbitrary",), 32),
        name="moe_dispatch",
    )(dest_flat, h2)


def _moe_kernel(be_ref, nv_ref, nb_ref, xs_ref, wg_ref, wu_ref, wd_ref, ys_ref, wg_s, wu_s, wd_s):
    b = pl.program_id(0)
    active = b < nb_ref[0]
    new_expert = (b == 0) | (be_ref[b] != be_ref[jnp.maximum(b - 1, 0)])

    @pl.when(active & new_expert)
    def _():
        wg_s[...] = wg_ref[0].astype(BF16)
        wu_s[...] = wu_ref[0].astype(BF16)
        wd_s[...] = wd_ref[0].astype(BF16)

    @pl.when(active)
    def _():
        T = MOE_BLOCK
        G = xs_ref.shape[0] // T
        valid = lax.broadcasted_iota(I32, (T, LANES), 0) < nv_ref[b]
        his, los = [], []
        for c in range(G):
            w = jnp.where(valid, _load_row_slab(xs_ref, c, T, G), jnp.uint32(0))
            hi, lo = _unpack_pairs(w)
            his.append(hi.astype(BF16))
            los.append(lo.astype(BF16))
        x = jnp.concatenate(his + los, axis=1)
        g = jnp.dot(x, wg_s[...], preferred_element_type=F32)
        u = jnp.dot(x, wu_s[...], preferred_element_type=F32)
        a = (g * jax.nn.sigmoid(g) * u).astype(BF16)
        _store_row_slabs(ys_ref, _pack_pairs(jnp.dot(a, wd_s[...], preferred_element_type=F32)))


def _moe(block_expert, nvalid, nblocks, xs, wg, wu, wd):
    _, D, F = wg.shape
    G = D // 2 // LANES
    P = xs.shape[0] // G
    n_blocks = P // MOE_BLOCK
    blk = lambda b, be, nv, nb: (jnp.minimum(b, nb[0] - 1), 0)
    return pl.pallas_call(
        _moe_kernel,
        grid_spec=pltpu.PrefetchScalarGridSpec(
            num_scalar_prefetch=3,
            grid=(n_blocks,),
            in_specs=[pl.BlockSpec((MOE_BLOCK * G, LANES), blk),
                      pl.BlockSpec((1, D, F), lambda b, be, nv, nb: (be[b], 0, 0)),
                      pl.BlockSpec((1, D, F), lambda b, be, nv, nb: (be[b], 0, 0)),
                      pl.BlockSpec((1, F, D), lambda b, be, nv, nb: (be[b], 0, 0))],
            out_specs=pl.BlockSpec((MOE_BLOCK * G, LANES), blk),
            scratch_shapes=[pltpu.VMEM((D, F), BF16), pltpu.VMEM((D, F), BF16),
                            pltpu.VMEM((F, D), BF16)]),
        out_shape=jax.ShapeDtypeStruct((P * G, LANES), U32),
        compiler_params=_params(("arbitrary",), 52),
        name="moe_experts",
    )(block_expert, nvalid, nblocks, xs, wg, wu, wd)


def _combine_kernel(dest_hbm, ys_hbm, gate_ref, h_ref, x_ref, g2_ref, sg_ref, su_ref, sd_ref, fg_ref,
                    o_ref, dest0, dest1, ybuf0, ybuf1, idx_sem, row_sem, *, Tc, G, final_norm):
    j = pl.program_id(0)
    n = Tc * TOP_K
    slots = ((dest0, ybuf0), (dest1, ybuf1))

    def row_copy(slot, t, k):
        dest, ybuf = slots[slot]
        src = ys_hbm.at[pl.ds(pl.multiple_of(dest[t * TOP_K + k], G), G)]
        dst = ybuf.at[k, pl.ds(pl.multiple_of(t * G, G), G)]
        return pltpu.make_async_copy(src, dst, row_sem.at[slot])

    def request(tile, slot):
        cp = pltpu.make_async_copy(dest_hbm.at[pl.ds(tile * n, n)], slots[slot][0],
                                   idx_sem.at[slot])
        cp.start()
        cp.wait()

        def issue(t, carry):
            for k in range(TOP_K):
                row_copy(slot, t, k).start(priority=k % 2)
            return carry

        lax.fori_loop(0, Tc, issue, 0)

    def reduce(slot):
        rows = slice(slot * Tc, (slot + 1) * Tc)
        h_tile = h_ref.at[pl.ds(slot * Tc * G, Tc * G)]
        halves = [_unpack_pairs(_load_row_slab(h_tile, c, Tc, G)) for c in range(G)]
        h = jnp.concatenate([p[0].astype(BF16) for p in halves]
                            + [p[1].astype(BF16) for p in halves], axis=1)
        g = jnp.dot(h, sg_ref[...], preferred_element_type=F32)
        u = jnp.dot(h, su_ref[...], preferred_element_type=F32)
        a = (g * jax.nn.sigmoid(g) * u).astype(BF16)
        f = jnp.dot(a, sd_ref[...], preferred_element_type=F32)

        def drain(t, carry):
            for k in range(TOP_K):
                row_copy(slot, t, k).wait()
            return carry

        lax.fori_loop(0, Tc, drain, 0)
        gate = gate_ref[rows, :]
        half = f.shape[1] // 2
        his, los = [], []
        for c in range(G):
            f_hi = f[:, c * LANES:(c + 1) * LANES]
            f_lo = f[:, half + c * LANES:half + (c + 1) * LANES]
            for k in range(TOP_K):
                y_hi, y_lo = _unpack_pairs(_load_row_slab(slots[slot][1].at[k], c, Tc, G))
                f_hi = f_hi + gate[:, k:k + 1] * y_hi
                f_lo = f_lo + gate[:, k:k + 1] * y_lo
            his.append(f_hi)
            los.append(f_lo)
        x2 = x_ref[rows, :] + g2_ref[0] * jnp.concatenate(his + los, axis=1)
        if final_norm:
            x2 = _rms(x2, NORM_EPS) * fg_ref[...]
        o_ref[rows, :] = x2

    @pl.when(j == 0)
    def _():
        request(0, 0)

    request(2 * j + 1, 1)
    reduce(0)

    @pl.when(j + 1 < pl.num_programs(0))
    def _():
        request(2 * j + 2, 0)

    reduce(1)


def _combine(dest_flat, ys, gate_t, h2, x1, g2, sg, su, sd, final_g, S, final_norm, Tc=256):
    N, D = x1.shape
    F = sg.shape[1]
    G = D // 2 // LANES
    tb = 2 * Tc
    tpb = S // tb
    row = lambda j: (j, 0)
    const = dict(pipeline_mode=pl.Buffered(1))
    return pl.pallas_call(
        functools.partial(_combine_kernel, Tc=Tc, G=G, final_norm=final_norm),
        grid=(N // tb,),
        in_specs=[pl.BlockSpec(memory_space=pl.ANY), pl.BlockSpec(memory_space=pl.ANY),
                  pl.BlockSpec((tb, TOP_K), row),
                  pl.BlockSpec((tb * G, LANES), row), pl.BlockSpec((tb, D), row),
                  pl.BlockSpec((1, 1, D), lambda j: (j // tpb, 0, 0)),
                  pl.BlockSpec((D, F), lambda j: (0, 0), **const),
                  pl.BlockSpec((D, F), lambda j: (0, 0), **const),
                  pl.BlockSpec((F, D), lambda j: (0, 0), **const),
                  pl.BlockSpec((1, D), lambda j: (0, 0))],
        out_specs=pl.BlockSpec((tb, D), row),
        out_shape=jax.ShapeDtypeStruct((N, D), F32),
        scratch_shapes=[pltpu.SMEM((Tc * TOP_K,), I32), pltpu.SMEM((Tc * TOP_K,), I32),
                        pltpu.VMEM((TOP_K, Tc * G, LANES), U32),
                        pltpu.VMEM((TOP_K, Tc * G, LANES), U32),
                        pltpu.SemaphoreType.DMA((2,)), pltpu.SemaphoreType.DMA((2,))],
        compiler_params=_params(("arbitrary",), 52),
        name="moe_combine",
    )(dest_flat, ys, gate_t, h2, x1, g2, sg, su, sd, final_g.reshape(1, D))


def _moe_ffn(h2, logits_t, x1, g2, router_bias, wg, wu, wd, e_off, sg, su, sd, final_g, S,
             final_norm):
    N, D = x1.shape
    G = D // 2 // LANES
    E = logits_t.shape[0]
    idx, gate, rank, cnt = _route(logits_t, router_bias)
    counts = cnt[:, 0]
    padded = (counts + MOE_BLOCK - 1) // MOE_BLOCK * MOE_BLOCK
    pend = jnp.cumsum(padded)
    pstart = (pend - padded).astype(I32)
    n_blocks = -(-(N * TOP_K + E * (MOE_BLOCK - 1)) // MOE_BLOCK)
    bstart = jnp.arange(n_blocks, dtype=I32) * MOE_BLOCK
    block_expert = jnp.minimum(jnp.sum(pend[None, :] <= bstart[:, None], axis=1), E - 1).astype(I32)
    nvalid = jnp.clip(pstart[block_expert] + counts[block_expert] - bstart, 0, MOE_BLOCK).astype(I32)
    nblocks = (pend[-1:] // MOE_BLOCK).astype(I32)

    dest = _dest(pstart, idx, rank, G)
    dest_flat = dest.T.reshape(N * TOP_K)
    xs = _dispatch(dest_flat, h2, n_blocks * MOE_BLOCK, G)
    ys = _moe(block_expert + e_off, nvalid, nblocks, xs, wg, wu, wd)
    return _combine(dest_flat, ys, gate.T, h2, x1, g2, sg, su, sd, final_g, S, final_norm)


def kernel(x, c, ada_w, ada_b, norm_mix_g, norm_ffn_g, w_in, lambda_q1, lambda_k1, lambda_q2,
           lambda_k2, subln_g, rpb, w_out, router_w, router_bias, exp_w_gate, exp_w_up,
           exp_w_down, sh_w_gate, sh_w_up, sh_w_down, final_g):
    B, S, D = x.shape
    L = ada_w.shape[0]
    N = B * S
    n_heads = w_in.shape[2] // (3 * HEAD_DIM)
    heads_a = n_heads // HEADS_A_FRAC
    heads_b = n_heads // HEADS_B_FRAC
    heads_c = n_heads - heads_a - heads_b
    slopes = _alibi_slopes(heads_a + heads_b)
    slopes_a = jnp.asarray(slopes[:heads_a], F32)
    slopes_b = jnp.asarray(slopes[heads_a:], F32)

    E, F = exp_w_gate.shape[1], exp_w_gate.shape[3]
    wg_all = exp_w_gate.reshape(L * E, D, F)
    wu_all = exp_w_up.reshape(L * E, D, F)
    wd_all = exp_w_down.reshape(L * E, F, D)

    mod = _ada(c, ada_w, ada_b).reshape(L, B, 6, 1, D)
    xf = x.reshape(N, D)
    for l in range(L):
        sh1, sc1, g1, sh2, sc2, g2 = (mod[l, :, j] for j in range(6))
        qkv = _qkv(xf, norm_mix_g[l], sc1, sh1, w_in[l].astype(BF16), S)
        oa = _attn_a(qkv, slopes_a, B, S, n_heads, heads_a)
        lambda_init = 0.8 - 0.6 * math.exp(-0.3 * l)
        lam_vecs = jnp.stack([lambda_q1[l], lambda_k1[l], lambda_q2[l], lambda_k2[l]]).astype(F32)
        ob = _attn_b(qkv, slopes_b, lam_vecs, subln_g[l], B, S, n_heads, heads_a, heads_b, lambda_init)
        oc = _attn_c(qkv, rpb[l].reshape(-1), B, S, n_heads, heads_a + heads_b, heads_c)
        x1, h2, logits_t = _out_proj(oa, ob, oc, w_out[l].astype(BF16), xf, g1, norm_ffn_g[l],
                                     sc2, sh2, router_w[l].T, S)
        xf = _moe_ffn(h2, logits_t, x1, g2, router_bias[l], wg_all, wu_all, wd_all, l * E,
                      sh_w_gate[l].astype(BF16), sh_w_up[l].astype(BF16),
                      sh_w_down[l].astype(BF16), final_g, S, l == L - 1)
    return xf.reshape(B, S, D)
```

```python
import functools
import math

import jax
import jax.numpy as jnp
from jax import lax
from jax.experimental import pallas as pl
from jax.experimental.pallas import tpu as pltpu

F32 = jnp.float32
BF16 = jnp.bfloat16
I32 = jnp.int32
U32 = jnp.uint32
HIGHEST = lax.Precision.HIGHEST
_NT = (((1,), (1,)), ((), ()))

HEAD_DIM = 128
HEADS_A_FRAC, HEADS_B_FRAC = 2, 4
DILATED_PATTERNS = ((128, 1), (512, 4), (2048, 16))
GRID_W = 64
NA_ROWS = 8
NA_COLS = 16
N_GROUPS = 8
TOPK_GROUPS = 4
TOP_K = 8
ROUTED_SCALE = 2.5
NORM_EPS = 1e-6
SUBLN_EPS = 1e-5
NEG_INF = -1e30
LOG2E = math.log2(math.e)

LANES = 128
VMEM_BYTES_V7X = 64 << 20

MOE_BLOCK = 512


def _params(semantics, vmem_mib):
    return pltpu.CompilerParams(dimension_semantics=semantics,
                                vmem_limit_bytes=min(vmem_mib << 20, VMEM_BYTES_V7X - (4 << 20)))


def _alibi_slopes(n):
    def pow2(m):
        start = 2.0 ** (-8.0 / m)
        return [start ** (i + 1) for i in range(m)]
    p = 2 ** int(math.floor(math.log2(n)))
    return pow2(p) + pow2(2 * p)[0::2][: n - p]


def _rms(x, eps):
    return x * lax.rsqrt(jnp.mean(x * x, axis=-1, keepdims=True) + eps)


def _pack_pairs(x):
    half = x.shape[1] // 2
    hi = lax.bitcast_convert_type(x[:, :half].astype(BF16).astype(F32), U32)
    lo = lax.bitcast_convert_type(x[:, half:].astype(BF16).astype(F32), U32)
    return hi | (lo >> 16)


def _unpack_pairs(w):
    hi = lax.bitcast_convert_type(w & jnp.uint32(0xFFFF0000), F32)
    lo = lax.bitcast_convert_type(w << 16, F32)
    return hi, lo


def _store_row_slabs(ref, packed):
    T, W = packed.shape
    for c in range(W // LANES):
        ref[pl.ds(c, T, stride=W // LANES), :] = packed[:, c * LANES:(c + 1) * LANES]


def _load_row_slab(ref, c, T, G):
    return ref[pl.ds(c, T, stride=G), :]


def _ada_kernel(c_ref, w_ref, b_ref, o_ref):
    c = c_ref[...]
    sc = c * jax.nn.sigmoid(c)
    o_ref[0] = jnp.dot(sc, w_ref[0], preferred_element_type=F32, precision=HIGHEST) + b_ref[0]


def _ada(c, ada_w, ada_b):
    B, D = c.shape
    L, _, D6 = ada_w.shape
    rows = 8
    cp = jnp.zeros((rows, D), F32).at[:B].set(c)
    tn = 1024
    out = pl.pallas_call(
        _ada_kernel,
        grid=(L, D6 // tn),
        in_specs=[pl.BlockSpec((rows, D), lambda l, j: (0, 0)),
                  pl.BlockSpec((1, D, tn), lambda l, j: (l, 0, j)),
                  pl.BlockSpec((1, 1, tn), lambda l, j: (l, 0, j))],
        out_specs=pl.BlockSpec((1, rows, tn), lambda l, j: (l, 0, j)),
        out_shape=jax.ShapeDtypeStruct((L, rows, D6), F32),
        compiler_params=_params(("arbitrary", "arbitrary"), 32),
        name="ada_mod",
    )(cp, ada_w, ada_b.reshape(L, 1, D6))
    return out[:, :B]


def _qkv_kernel(x_ref, g_ref, sc_ref, sh_ref, w_ref, o_ref, h_scr, *, heads_per_step):
    @pl.when(pl.program_id(1) == 0)
    def _():
        y = _rms(x_ref[...], NORM_EPS) * g_ref[...]
        h_scr[...] = (y * (1.0 + sc_ref[0]) + sh_ref[0]).astype(BF16)

    res = jnp.dot(h_scr[...], w_ref[...], preferred_element_type=F32)
    for hh in range(heads_per_step):
        o_ref[hh] = res[:, hh * HEAD_DIM:(hh + 1) * HEAD_DIM].astype(BF16)


def _qkv(x2d, g, sc, sh, w_bf, S, tm=512, tn=2048):
    N, D = x2d.shape
    W3 = w_bf.shape[1]
    hps = tn // HEAD_DIM
    tpb = S // tm
    return pl.pallas_call(
        functools.partial(_qkv_kernel, heads_per_step=hps),
        grid=(N // tm, W3 // tn),
        in_specs=[pl.BlockSpec((tm, D), lambda i, j: (i, 0)),
                  pl.BlockSpec((1, D), lambda i, j: (0, 0)),
                  pl.BlockSpec((1, 1, D), lambda i, j: (i // tpb, 0, 0)),
                  pl.BlockSpec((1, 1, D), lambda i, j: (i // tpb, 0, 0)),
                  pl.BlockSpec((D, tn), lambda i, j: (0, j))],
        out_specs=pl.BlockSpec((hps, tm, HEAD_DIM), lambda i, j: (j, i, 0)),
        out_shape=jax.ShapeDtypeStruct((W3 // HEAD_DIM, N, HEAD_DIM), BF16),
        scratch_shapes=[pltpu.VMEM((tm, D), BF16)],
        compiler_params=_params(("arbitrary", "arbitrary"), 48),
        name="qkv_proj",
    )(x2d, g.reshape(1, D), sc, sh, w_bf)


def _col_to_row(col, eye):
    return jnp.sum(jnp.where(eye, col, 0.0), axis=0, keepdims=True)


def _attn_a_kernel(slopes_ref, q_ref, k_ref, v_ref, o_ref, lse_ref, nat, cq, ck, cv, onat, tmp, *,
                   L, dil, half, Tq):
    h = pl.program_id(1)
    wk = 2 * Tq
    n_tiles = L // Tq
    c_dist = LOG2E * slopes_ref[h] * dil
    a = lax.broadcasted_iota(I32, (Tq, wk), 0)
    u = lax.broadcasted_iota(I32, (Tq, wk), 1)
    eye = lax.broadcasted_iota(I32, (Tq, Tq), 0) == lax.broadcasted_iota(I32, (Tq, Tq), 1)

    def table(off):
        d = u - a - off
        ad = jnp.maximum(d, -d)
        return jnp.where(ad <= half, -c_dist * ad.astype(F32), NEG_INF)

    tabs = {off: table(off) for off in sorted({0, half, wk - Tq})}

    q_scale = LOG2E * HEAD_DIM ** -0.5
    d1 = 4 if dil > 4 else dil
    d2 = dil // d1
    S = L * dil
    if dil > 1:
        for src, dst, scale in ((q_ref, cq, q_scale), (k_ref, ck, None), (v_ref, cv, None)):
            x = src[0].astype(F32)
            nat[...] = x if scale is None else x * scale
            if d2 == 1:
                for r in range(dil):
                    dst[r * L:(r + 1) * L, :] = nat[pl.ds(r, L, stride=dil), :].astype(BF16)
            else:
                for ra in range(d1):
                    tmp[ra * (S // d1):(ra + 1) * (S // d1), :] = nat[pl.ds(ra, S // d1, stride=d1), :]
                for ra in range(d1):
                    for rb in range(d2):
                        r = ra + d1 * rb
                        dst[r * L:(r + 1) * L, :] = tmp[
                            pl.ds(ra * (S // d1) + rb, L, stride=d2), :].astype(BF16)
        kc, vc = ck, cv
    else:
        kc, vc = k_ref.at[0], v_ref.at[0]

    for r in range(dil):
        for j in range(n_tiles):
            start = min(max(j * Tq - half, 0), L - wk)
            rows = slice(r * L + j * Tq, r * L + (j + 1) * Tq)
            q = cq[rows, :] if dil > 1 else (q_ref[0, rows, :].astype(F32) * q_scale).astype(BF16)
            kw = kc[r * L + start:r * L + start + wk, :]
            vw = vc[r * L + start:r * L + start + wk, :]
            s = lax.dot_general(q, kw, _NT, preferred_element_type=F32) + tabs[j * Tq - start]
            m = jnp.max(s, axis=-1, keepdims=True)
            p = jnp.exp2(s - m)
            l = jnp.sum(p, axis=-1, keepdims=True)
            o = jnp.dot(p.astype(BF16), vw, preferred_element_type=F32) / l
            if dil == 1:
                o_ref[0, rows, :] = o.astype(BF16)
            elif d2 == 1:
                onat[pl.ds(r + j * Tq * dil, Tq, stride=dil), :] = o
            else:
                ra, rb = r % d1, r // d1
                tmp[pl.ds(ra * (S // d1) + rb + j * Tq * d2, Tq, stride=d2), :] = o
            lse_ref[0, 0, r:r + 1, j * Tq:(j + 1) * Tq] = _col_to_row(m + jnp.log2(l), eye)
    if d2 > 1:
        for ra in range(d1):
            onat[pl.ds(ra, S // d1, stride=d1), :] = tmp[ra * (S // d1):(ra + 1) * (S // d1), :]
    if dil > 1:
        o_ref[0] = onat[...].astype(BF16)


def _attn_a_pattern(qkv, slopes, B, S, n_heads, heads_a, window, dil, Tq=128):
    N = B * S
    L = S // dil
    half = window // (2 * dil)
    assert L % Tq == 0 and L >= 2 * Tq and 2 * half <= Tq
    blk = (1, S, HEAD_DIM)
    o, lse = pl.pallas_call(
        functools.partial(_attn_a_kernel, L=L, dil=dil, half=half, Tq=Tq),
        grid=(B, heads_a),
        in_specs=[pl.BlockSpec(memory_space=pltpu.SMEM),
                  pl.BlockSpec(blk, lambda b, h: (h, b, 0)),
                  pl.BlockSpec(blk, lambda b, h: (n_heads + h, b, 0)),
                  pl.BlockSpec(blk, lambda b, h: (2 * n_heads + h, b, 0))],
        out_specs=[pl.BlockSpec(blk, lambda b, h: (h, b, 0)),
                   pl.BlockSpec((1, 1, dil, L), lambda b, h: (b, h, 0, 0))],
        out_shape=[jax.ShapeDtypeStruct((heads_a, N, HEAD_DIM), BF16),
                   jax.ShapeDtypeStruct((B, heads_a, dil, L), F32)],
        scratch_shapes=[pltpu.VMEM((S, HEAD_DIM), F32)] + [pltpu.VMEM((S, HEAD_DIM), BF16)] * 3
        + [pltpu.VMEM((S, HEAD_DIM), F32)] * 2,
        compiler_params=_params(("arbitrary", "arbitrary"), 32),
        name=f"attn_dilated_d{dil}",
    )(slopes, qkv, qkv, qkv)
    lse_nat = lse.transpose(1, 0, 3, 2).reshape(heads_a, N)
    return o, lse_nat


def _mix_a_kernel(*refs, n_pat, tm):
    o_refs, lse_refs, out_ref = refs[:n_pat], refs[n_pat:2 * n_pat], refs[2 * n_pat]
    heads = lse_refs[0].shape[0]
    lses = [r[...] for r in lse_refs]
    mx = functools.reduce(jnp.maximum, lses)
    es = [jnp.exp2(x - mx) for x in lses]
    den = functools.reduce(lambda x, y: x + y, es)
    ws = [e / den for e in es]
    pad = jnp.zeros((LANES - n_pat * heads, LANES), F32)
    for c in range(tm // LANES):
        cols = slice(c * LANES, (c + 1) * LANES)
        w_rows = jnp.concatenate([w[:, cols] for w in ws] + [pad], axis=0)
        w_cols = w_rows.T
        for hh in range(heads):
            acc = jnp.zeros((LANES, HEAD_DIM), F32)
            for p in range(n_pat):
                wc = w_cols[:, p * heads + hh:p * heads + hh + 1]
                acc = acc + wc * o_refs[p][hh, cols, :].astype(F32)
            out_ref[cols, hh * HEAD_DIM:(hh + 1) * HEAD_DIM] = acc.astype(BF16)


def _mix_a(os, lses, tm=512):
    heads, N, _ = os[0].shape
    n_pat = len(os)
    assert n_pat * heads <= LANES
    return pl.pallas_call(
        functools.partial(_mix_a_kernel, n_pat=n_pat, tm=tm),
        grid=(N // tm,),
        in_specs=[pl.BlockSpec((heads, tm, HEAD_DIM), lambda i: (0, i, 0))] * n_pat
        + [pl.BlockSpec((heads, tm), lambda i: (0, i))] * n_pat,
        out_specs=pl.BlockSpec((tm, heads * HEAD_DIM), lambda i: (i, 0)),
        out_shape=jax.ShapeDtypeStruct((N, heads * HEAD_DIM), BF16),
        compiler_params=_params(("arbitrary",), 32),
        name="attn_dilated_mix",
    )(*os, *lses)


def _attn_a(qkv, slopes, B, S, n_heads, heads_a):
    parts = [_attn_a_pattern(qkv, slopes, B, S, n_heads, heads_a, w, d) for w, d in DILATED_PATTERNS]
    return _mix_a([p[0] for p in parts], [p[1] for p in parts])


def _attn_b_kernel(slopes_ref, lam_ref, q_ref, k_ref, v_ref, g_ref, o_ref, tab_ref, kmax_ref,
                   m_scr, l_scr, acc_scr, *, S, Tq, Tk, lambda_init):
    h = pl.program_id(1)
    i = pl.program_id(2)
    dh = HEAD_DIM // 2
    n_chunks = S // Tk
    c_dist = LOG2E * slopes_ref[h]

    def sub_norm(x):
        sq = x * x
        return jnp.sqrt(jnp.maximum(jnp.sum(sq[:, :dh], axis=-1, keepdims=True),
                                    jnp.sum(sq[:, dh:], axis=-1, keepdims=True)))

    @pl.when(i == 0)
    def _():
        shape = (Tq, 2 * S - Tq)
        a = lax.broadcasted_iota(I32, shape, 0)
        u = lax.broadcasted_iota(I32, shape, 1)
        d = a - u + (S - Tq)
        tab_ref[...] = -c_dist * jnp.maximum(d, -d).astype(F32)
        kn = sub_norm(k_ref[0].astype(F32))
        for j in range(n_chunks):
            kmax_ref[j] = jnp.max(kn[j * Tk:(j + 1) * Tk])

    lv = lam_ref[...]
    lam = (jnp.exp(jnp.sum(lv[0:1] * lv[1:2], axis=-1, keepdims=True))
           - jnp.exp(jnp.sum(lv[2:3] * lv[3:4], axis=-1, keepdims=True)) + lambda_init)

    t0 = i * Tq
    qf = q_ref[0].astype(F32) * (LOG2E * dh ** -0.5)
    q = qf.astype(BF16)
    qmax = jnp.max(sub_norm(q.astype(F32)))

    def chunk(kstart, first):
        kc = k_ref[0, pl.ds(kstart, Tk), :]
        vc = v_ref[0, pl.ds(kstart, Tk), :]
        bias = tab_ref[:, pl.ds(pl.multiple_of(S - Tq - t0 + kstart, LANES), Tk)]
        s = jnp.concatenate(
            [lax.dot_general(q[:, c * dh:(c + 1) * dh], kc[:, c * dh:(c + 1) * dh], _NT,
                             preferred_element_type=F32) + bias for c in range(2)], axis=0)
        mc = jnp.max(s, axis=-1, keepdims=True)
        if first:
            m_new = mc
        else:
            m_old = m_scr[...]
            m_new = jnp.maximum(m_old, mc)
            alpha = jnp.exp2(m_old - m_new)
        p = jnp.exp2(s - m_new)
        ls = jnp.sum(p, axis=-1, keepdims=True)
        pv = jnp.dot(p.astype(BF16), vc, preferred_element_type=F32)
        m_scr[...] = m_new
        l_scr[...] = ls if first else alpha * l_scr[...] + ls
        acc_scr[...] = pv if first else alpha * acc_scr[...] + pv
        return m_new

    jd = t0 // Tk
    m_diag = chunk(pl.multiple_of(jd * Tk, Tk), True)
    floor = jnp.min(m_diag) - 160.0
    for j in range(n_chunks):
        gap = jnp.maximum(jnp.maximum(j * Tk - (t0 + Tq - 1), t0 - ((j + 1) * Tk - 1)), 0)
        bound = 1.001 * qmax * kmax_ref[j] + 0.01 - c_dist * gap.astype(F32)

        @pl.when((j != jd) & (bound > floor))
        def _():
            chunk(j * Tk, False)

    acc = acc_scr[...]
    l = l_scr[...]
    o = acc[:Tq] / l[:Tq] - acc[Tq:] * (lam / l[Tq:])
    o = _rms(o, SUBLN_EPS) * g_ref[...] * (1.0 - lambda_init)
    o_ref[...] = o.astype(BF16)


def _attn_b(qkv, slopes, lam_vecs, subln_g, B, S, n_heads, head0, heads_b, lambda_init, Tq=256,
            Tk=1024):
    N = B * S
    nq = S // Tq
    Tk = min(Tk, S)
    return pl.pallas_call(
        functools.partial(_attn_b_kernel, S=S, Tq=Tq, Tk=Tk, lambda_init=lambda_init),
        grid=(B, heads_b, nq),
        in_specs=[pl.BlockSpec(memory_space=pltpu.SMEM),
                  pl.BlockSpec((4, HEAD_DIM // 2), lambda b, h, i: (0, 0)),
                  pl.BlockSpec((1, Tq, HEAD_DIM), lambda b, h, i: (head0 + h, b * nq + i, 0)),
                  pl.BlockSpec((1, S, HEAD_DIM), lambda b, h, i: (n_heads + head0 + h, b, 0)),
                  pl.BlockSpec((1, S, HEAD_DIM), lambda b, h, i: (2 * n_heads + head0 + h, b, 0)),
                  pl.BlockSpec((1, HEAD_DIM), lambda b, h, i: (0, 0))],
        out_specs=pl.BlockSpec((Tq, HEAD_DIM), lambda b, h, i: (b * nq + i, h)),
        out_shape=jax.ShapeDtypeStruct((N, heads_b * HEAD_DIM), BF16),
        scratch_shapes=[pltpu.VMEM((Tq, 2 * S - Tq), F32), pltpu.SMEM((S // Tk,), F32),
                        pltpu.VMEM((2 * Tq, 1), F32), pltpu.VMEM((2 * Tq, 1), F32),
                        pltpu.VMEM((2 * Tq, HEAD_DIM), F32)],
        compiler_params=_params(("arbitrary", "arbitrary", "arbitrary"), 48),
        name="attn_diff",
    )(slopes, lam_vecs, qkv, qkv, qkv, subln_g.reshape(1, HEAD_DIM))


NBR_GROUP = 4


def _nbr_pairs(R):
    kr, win = NA_ROWS, NA_ROWS + NBR_GROUP
    pairs = []
    for r0 in range(0, R, NBR_GROUP):
        ps = min(max(r0 - kr // 2, 0), R - win)
        rel = tuple(min(max(r0 + a - kr // 2, 0), R - kr) - ps for a in range(NBR_GROUP))
        pairs.append((r0, ps, (ps - r0,) + rel))
    return pairs, sorted({p[2] for p in pairs})


def _attn_c_kernel(rpb_ref, q_ref, k_ref, v_ref, o_ref, tab_ref, *, R):
    h = pl.program_id(1)
    W = GRID_W
    kr = min(NA_ROWS, R)
    n_dr = 2 * NA_ROWS - 1
    n_dc = 2 * NA_COLS - 1

    c_io = lax.broadcasted_iota(I32, (W, 2 * W), 0)
    l_io = lax.broadcasted_iota(I32, (W, 2 * W), 1)
    cp = l_io & (W - 1)
    dcm = cp - c_io + (NA_COLS - 1)
    cstart = jnp.clip(c_io - NA_COLS // 2, 0, W - NA_COLS)
    ok = (cp >= cstart) & (cp < cstart + NA_COLS)
    blocks = []
    for dr in range(n_dr):
        blk = jnp.zeros((W, 2 * W), F32)
        for dc in range(n_dc):
            blk = jnp.where(dcm == dc, rpb_ref[(h * n_dr + dr) * n_dc + dc], blk)
        blocks.append(jnp.where(ok, blk, NEG_INF))
    win = kr + NBR_GROUP
    pairs, cfgs = _nbr_pairs(R)
    neg = jnp.full((W, 2 * W), NEG_INF, F32)
    for ti, (off, *rels) in enumerate(cfgs):
        for a, rel in enumerate(rels):
            def key_row_block(j):
                return blocks[off + j - a + (NA_ROWS - 1)] if rel <= j < rel + kr else neg
            for jj in range(win // 2):
                tab_ref[ti, a * W:(a + 1) * W, jj * 2 * W:(jj + 1) * 2 * W] = jnp.where(
                    l_io < W, key_row_block(2 * jj), key_row_block(2 * jj + 1))

    for r0, ps, cfg in pairs:
        rows = slice(r0 * W, (r0 + NBR_GROUP) * W)
        kw = k_ref[0, ps * W:(ps + win) * W, :]
        vw = v_ref[0, ps * W:(ps + win) * W, :]
        s = lax.dot_general(q_ref[0, rows, :], kw, _NT, preferred_element_type=F32)
        s = s * (HEAD_DIM ** -0.5) + tab_ref[cfgs.index(cfg)]
        m = jnp.max(s, axis=-1, keepdims=True)
        p = jnp.exp(s - m)
        l = jnp.sum(p, axis=-1, keepdims=True)
        o = jnp.dot(p.astype(BF16), vw, preferred_element_type=F32) / l
        o_ref[rows, :] = o.astype(BF16)


def _attn_c(qkv, rpb_flat, B, S, n_heads, head0, heads_c):
    N = B * S
    R = S // GRID_W
    kr = NA_ROWS
    assert R % NBR_GROUP == 0 and R >= kr + NBR_GROUP and (kr + NBR_GROUP) % 2 == 0
    n_tabs = len(_nbr_pairs(R)[1])
    return pl.pallas_call(
        functools.partial(_attn_c_kernel, R=R),
        grid=(B, heads_c),
        in_specs=[pl.BlockSpec(memory_space=pltpu.SMEM),
                  pl.BlockSpec((1, S, HEAD_DIM), lambda b, h: (head0 + h, b, 0)),
                  pl.BlockSpec((1, S, HEAD_DIM), lambda b, h: (n_heads + head0 + h, b, 0)),
                  pl.BlockSpec((1, S, HEAD_DIM), lambda b, h: (2 * n_heads + head0 + h, b, 0))],
        out_specs=pl.BlockSpec((S, HEAD_DIM), lambda b, h: (b, h)),
        out_shape=jax.ShapeDtypeStruct((N, heads_c * HEAD_DIM), BF16),
        scratch_shapes=[pltpu.VMEM((n_tabs, NBR_GROUP * GRID_W, (kr + NBR_GROUP) * GRID_W), F32)],
        compiler_params=_params(("arbitrary", "arbitrary"), 32),
        name="attn_nbr",
    )(rpb_flat, qkv, qkv, qkv)


def _out_kernel(oa_ref, ob_ref, oc_ref, w_ref, x_ref, g1_ref, ng_ref, sc_ref, sh_ref, rw_ref,
                x1_ref, h2_ref, lg_ref):
    o = jnp.concatenate([oa_ref[...], ob_ref[...], oc_ref[...]], axis=1)
    acc = jnp.dot(o, w_ref[...], preferred_element_type=F32)
    x1 = x_ref[...] + g1_ref[0] * acc
    x1_ref[...] = x1
    h2 = (_rms(x1, NORM_EPS) * ng_ref[...]) * (1.0 + sc_ref[0]) + sh_ref[0]
    _store_row_slabs(h2_ref, _pack_pairs(h2))
    lg_ref[...] = lax.dot_general(rw_ref[...], h2, _NT, preferred_element_type=F32,
                                  precision=HIGHEST)


def _out_proj(oa, ob, oc, w_bf, x2d, g1, ng, sc, sh, rw_t, S, tm=512):
    N, D = x2d.shape
    wa, wb, wc = oa.shape[1], ob.shape[1], oc.shape[1]
    E = rw_t.shape[0]
    tpb = S // tm
    const = dict(pipeline_mode=pl.Buffered(1))
    row = lambda i: (i, 0)
    per_b = lambda i: (i // tpb, 0, 0)
    return pl.pallas_call(
        _out_kernel,
        grid=(N // tm,),
        in_specs=[pl.BlockSpec((tm, wa), row), pl.BlockSpec((tm, wb), row), pl.BlockSpec((tm, wc), row),
                  pl.BlockSpec((wa + wb + wc, D), lambda i: (0, 0), **const),
                  pl.BlockSpec((tm, D), row),
                  pl.BlockSpec((1, 1, D), per_b),
                  pl.BlockSpec((1, D), lambda i: (0, 0)),
                  pl.BlockSpec((1, 1, D), per_b),
                  pl.BlockSpec((1, 1, D), per_b),
                  pl.BlockSpec((E, D), lambda i: (0, 0), **const)],
        out_specs=[pl.BlockSpec((tm, D), row), pl.BlockSpec((tm * (D // 2 // LANES), LANES), row),
                   pl.BlockSpec((E, tm), lambda i: (0, i))],
        out_shape=[jax.ShapeDtypeStruct((N, D), F32),
                   jax.ShapeDtypeStruct((N * (D // 2 // LANES), LANES), U32),
                   jax.ShapeDtypeStruct((E, N), F32)],
        compiler_params=_params(("arbitrary",), 52),
        name="out_proj",
    )(oa, ob, oc, w_bf, x2d, g1, ng.reshape(1, D), sc, sh, rw_t)


def _first_argmax(vals, iota, big):
    mx = jnp.max(vals, axis=0, keepdims=True)
    idx = jnp.min(jnp.where(vals == mx, iota, big), axis=0, keepdims=True)
    return mx, idx


def _route_kernel(lg_ref, rb_ref, idx_ref, gate_ref, rank_ref, cnt_ref, tri_ref, carry_ref, *, Tt):
    i = pl.program_id(0)
    E = lg_ref.shape[0]
    gs = E // N_GROUPS

    @pl.when(i == 0)
    def _():
        r = lax.broadcasted_iota(I32, (Tt, Tt), 0)
        c = lax.broadcasted_iota(I32, (Tt, Tt), 1)
        tri_ref[...] = jnp.where(r < c, 1.0, 0.0).astype(BF16)
        carry_ref[...] = jnp.zeros_like(carry_ref)

    scores = jax.nn.sigmoid(lg_ref[...])
    sel = scores + rb_ref[...]
    e_io = lax.broadcasted_iota(I32, (E, Tt), 0).astype(F32)
    s_io = lax.broadcasted_iota(I32, (gs, Tt), 0).astype(F32)
    g_io = lax.broadcasted_iota(I32, (N_GROUPS, Tt), 0).astype(F32)

    grp = []
    for g in range(N_GROUPS):
        blk = sel[g * gs:(g + 1) * gs]
        m1, a1 = _first_argmax(blk, s_io, float(gs))
        m2 = jnp.max(jnp.where(s_io == a1, -jnp.inf, blk), axis=0, keepdims=True)
        grp.append(m1 + m2)
    grp = jnp.concatenate(grp, axis=0)

    gsel = jnp.zeros((N_GROUPS, Tt), F32)
    for _ in range(TOPK_GROUPS):
        _, gi = _first_argmax(grp, g_io, float(N_GROUPS))
        hit = g_io == gi
        gsel = jnp.where(hit, 1.0, gsel)
        grp = jnp.where(hit, -jnp.inf, grp)

    esel = jnp.concatenate(
        [jnp.broadcast_to(gsel[g:g + 1], (gs, Tt)) for g in range(N_GROUPS)], axis=0)
    cur = jnp.where(esel > 0.0, sel, NEG_INF)

    idxs, gates = [], []
    onehot = jnp.zeros((E, Tt), F32)
    for _ in range(TOP_K):
        _, ei = _first_argmax(cur, e_io, float(E))
        hit = e_io == ei
        idxs.append(ei)
        gates.append(jnp.sum(jnp.where(hit, scores, 0.0), axis=0, keepdims=True))
        onehot = jnp.where(hit, 1.0, onehot)
        cur = jnp.where(hit, -jnp.inf, cur)
    gate = jnp.concatenate(gates, axis=0)
    gate = gate / jnp.sum(gate, axis=0, keepdims=True) * ROUTED_SCALE
    idx_ref[...] = jnp.concatenate(idxs, axis=0).astype(I32)
    gate_ref[...] = gate

    oh = onehot.astype(BF16)
    before = jnp.dot(oh, tri_ref[...], preferred_element_type=F32)
    tile_cnt = jnp.dot(oh, jnp.ones((Tt, LANES), BF16), preferred_element_type=F32)
    carry = carry_ref[...]
    rank_mat = before + jnp.concatenate([carry] * (Tt // LANES), axis=1)
    ranks = [jnp.sum(jnp.where(e_io == ei, rank_mat, 0.0), axis=0, keepdims=True) for ei in idxs]
    rank_ref[...] = jnp.concatenate(ranks, axis=0).astype(I32)
    carry_ref[...] = carry + tile_cnt
    cnt_ref[...] = (carry + tile_cnt).astype(I32)


def _route(logits_t, router_bias, Tt=512):
    E, N = logits_t.shape
    tok = lambda i: (0, i)
    return pl.pallas_call(
        functools.partial(_route_kernel, Tt=Tt),
        grid=(N // Tt,),
        in_specs=[pl.BlockSpec((E, Tt), tok), pl.BlockSpec((E, 1), lambda i: (0, 0))],
        out_specs=[pl.BlockSpec((TOP_K, Tt), tok), pl.BlockSpec((TOP_K, Tt), tok),
                   pl.BlockSpec((TOP_K, Tt), tok), pl.BlockSpec((E, LANES), lambda i: (0, 0))],
        out_shape=[jax.ShapeDtypeStruct((TOP_K, N), I32), jax.ShapeDtypeStruct((TOP_K, N), F32),
                   jax.ShapeDtypeStruct((TOP_K, N), I32), jax.ShapeDtypeStruct((E, LANES), I32)],
        scratch_shapes=[pltpu.VMEM((Tt, Tt), BF16), pltpu.VMEM((E, LANES), F32)],
        compiler_params=_params(("arbitrary",), 32),
        name="route_topk",
    )(logits_t, router_bias.reshape(E, 1))


def _dest_kernel(pstart_ref, idx_ref, rank_ref, dest_ref, *, G):
    idx = idx_ref[...]
    base = jnp.zeros(idx.shape, I32)
    for e in range(pstart_ref.shape[0]):
        base = jnp.where(idx == e, pstart_ref[e], base)
    dest_ref[...] = (base + rank_ref[...]) * G


def _dest(pstart, idx, rank, G):
    return pl.pallas_call(
        functools.partial(_dest_kernel, G=G),
        in_specs=[pl.BlockSpec(memory_space=pltpu.SMEM), pl.BlockSpec(memory_space=pltpu.VMEM),
                  pl.BlockSpec(memory_space=pltpu.VMEM)],
        out_specs=pl.BlockSpec(memory_space=pltpu.VMEM),
        out_shape=jax.ShapeDtypeStruct(idx.shape, I32),
        name="route_dest",
    )(pstart, idx, rank)


def _dispatch_kernel(dest_hbm, h_ref, xs_hbm, dest_smem, idx_sem, row_sem, *, Td, G):
    i = pl.program_id(0)
    n = Td * TOP_K
    cp = pltpu.make_async_copy(dest_hbm.at[pl.ds(i * n, n)], dest_smem, idx_sem)
    cp.start()
    cp.wait()

    def row_copy(t, k):
        src = h_ref.at[pl.ds(pl.multiple_of(t * G, G), G)]
        dst = xs_hbm.at[pl.ds(pl.multiple_of(dest_smem[t * TOP_K + k], G), G)]
        return pltpu.make_async_copy(src, dst, row_sem)

    def issue(t, carry):
        for k in range(TOP_K):
            row_copy(t, k).start(priority=k % 2)
        return carry

    lax.fori_loop(0, Td, issue, 0)

    def drain(t, carry):
        for k in range(TOP_K):
            row_copy(t, k).wait()
        return carry

    lax.fori_loop(0, Td, drain, 0)


def _dispatch(dest_flat, h2, P, G, Td=512):
    N = h2.shape[0] // G
    return pl.pallas_call(
        functools.partial(_dispatch_kernel, Td=Td, G=G),
        grid=(N // Td,),
        in_specs=[pl.BlockSpec(memory_space=pl.ANY), pl.BlockSpec((Td * G, LANES), lambda i: (i, 0))],
        out_specs=pl.BlockSpec(memory_space=pl.ANY),
        out_shape=jax.ShapeDtypeStruct((P * G, LANES), h2.dtype),
        scratch_shapes=[pltpu.SMEM((Td * TOP_K,), I32), pltpu.SemaphoreType.DMA,
                        pltpu.SemaphoreType.DMA],
        compiler_params=_params(("arbitrary",), 32),
        name="moe_dispatch",
    )(dest_flat, h2)


def _moe_kernel(be_ref, nv_ref, nb_ref, xs_ref, wg_ref, wu_ref, wd_ref, ys_ref, wg_s, wu_s, wd_s):
    b = pl.program_id(0)
    active = b < nb_ref[0]
    new_expert = (b == 0) | (be_ref[b] != be_ref[jnp.maximum(b - 1, 0)])

    @pl.when(active & new_expert)
    def _():
        wg_s[...] = wg_ref[0].astype(BF16)
        wu_s[...] = wu_ref[0].astype(BF16)
        wd_s[...] = wd_ref[0].astype(BF16)

    @pl.when(active)
    def _():
        T = MOE_BLOCK
        G = xs_ref.shape[0] // T
        valid = lax.broadcasted_iota(I32, (T, LANES), 0) < nv_ref[b]
        his, los = [], []
        for c in range(G):
            w = jnp.where(valid, _load_row_slab(xs_ref, c, T, G), jnp.uint32(0))
            hi, lo = _unpack_pairs(w)
            his.append(hi.astype(BF16))
            los.append(lo.astype(BF16))
        x = jnp.concatenate(his + los, axis=1)
        g = jnp.dot(x, wg_s[...], preferred_element_type=F32)
        u = jnp.dot(x, wu_s[...], preferred_element_type=F32)
        a = (g * jax.nn.sigmoid(g) * u).astype(BF16)
        _store_row_slabs(ys_ref, _pack_pairs(jnp.dot(a, wd_s[...], preferred_element_type=F32)))


def _moe(block_expert, nvalid, nblocks, xs, wg, wu, wd):
    _, D, F = wg.shape
    G = D // 2 // LANES
    P = xs.shape[0] // G
    n_blocks = P // MOE_BLOCK
    blk = lambda b, be, nv, nb: (jnp.minimum(b, nb[0] - 1), 0)
    return pl.pallas_call(
        _moe_kernel,
        grid_spec=pltpu.PrefetchScalarGridSpec(
            num_scalar_prefetch=3,
            grid=(n_blocks,),
            in_specs=[pl.BlockSpec((MOE_BLOCK * G, LANES), blk),
                      pl.BlockSpec((1, D, F), lambda b, be, nv, nb: (be[b], 0, 0)),
                      pl.BlockSpec((1, D, F), lambda b, be, nv, nb: (be[b], 0, 0)),
                      pl.BlockSpec((1, F, D), lambda b, be, nv, nb: (be[b], 0, 0))],
            out_specs=pl.BlockSpec((MOE_BLOCK * G, LANES), blk),
            scratch_shapes=[pltpu.VMEM((D, F), BF16), pltpu.VMEM((D, F), BF16),
                            pltpu.VMEM((F, D), BF16)]),
        out_shape=jax.ShapeDtypeStruct((P * G, LANES), U32),
        compiler_params=_params(("arbitrary",), 52),
        name="moe_experts",
    )(block_expert, nvalid, nblocks, xs, wg, wu, wd)


def _combine_kernel(dest_hbm, ys_hbm, gate_ref, h_ref, x_ref, g2_ref, sg_ref, su_ref, sd_ref, fg_ref,
                    o_ref, dest0, dest1, ybuf0, ybuf1, idx_sem, row_sem, *, Tc, G, final_norm):
    j = pl.program_id(0)
    n = Tc * TOP_K
    slots = ((dest0, ybuf0), (dest1, ybuf1))

    def row_copy(slot, t, k):
        dest, ybuf = slots[slot]
        src = ys_hbm.at[pl.ds(pl.multiple_of(dest[t * TOP_K + k], G), G)]
        dst = ybuf.at[k, pl.ds(pl.multiple_of(t * G, G), G)]
        return pltpu.make_async_copy(src, dst, row_sem.at[slot])

    def request(tile, slot):
        cp = pltpu.make_async_copy(dest_hbm.at[pl.ds(tile * n, n)], slots[slot][0],
                                   idx_sem.at[slot])
        cp.start()
        cp.wait()

        def issue(t, carry):
            for k in range(TOP_K):
                row_copy(slot, t, k).start(priority=k % 2)
            return carry

        lax.fori_loop(0, Tc, issue, 0)

    def reduce(slot):
        rows = slice(slot * Tc, (slot + 1) * Tc)
        h_tile = h_ref.at[pl.ds(slot * Tc * G, Tc * G)]
        halves = [_unpack_pairs(_load_row_slab(h_tile, c, Tc, G)) for c in range(G)]
        h = jnp.concatenate([p[0].astype(BF16) for p in halves]
                            + [p[1].astype(BF16) for p in halves], axis=1)
        g = jnp.dot(h, sg_ref[...], preferred_element_type=F32)
        u = jnp.dot(h, su_ref[...], preferred_element_type=F32)
        a = (g * jax.nn.sigmoid(g) * u).astype(BF16)
        f = jnp.dot(a, sd_ref[...], preferred_element_type=F32)

        def drain(t, carry):
            for k in range(TOP_K):
                row_copy(slot, t, k).wait()
            return carry

        lax.fori_loop(0, Tc, drain, 0)
        gate = gate_ref[rows, :]
        half = f.shape[1] // 2
        his, los = [], []
        for c in range(G):
            f_hi = f[:, c * LANES:(c + 1) * LANES]
            f_lo = f[:, half + c * LANES:half + (c + 1) * LANES]
            for k in range(TOP_K):
                y_hi, y_lo = _unpack_pairs(_load_row_slab(slots[slot][1].at[k], c, Tc, G))
                f_hi = f_hi + gate[:, k:k + 1] * y_hi
                f_lo = f_lo + gate[:, k:k + 1] * y_lo
            his.append(f_hi)
            los.append(f_lo)
        x2 = x_ref[rows, :] + g2_ref[0] * jnp.concatenate(his + los, axis=1)
        if final_norm:
            x2 = _rms(x2, NORM_EPS) * fg_ref[...]
        o_ref[rows, :] = x2

    @pl.when(j == 0)
    def _():
        request(0, 0)

    request(2 * j + 1, 1)
    reduce(0)

    @pl.when(j + 1 < pl.num_programs(0))
    def _():
        request(2 * j + 2, 0)

    reduce(1)


def _combine(dest_flat, ys, gate_t, h2, x1, g2, sg, su, sd, final_g, S, final_norm, Tc=256):
    N, D = x1.shape
    F = sg.shape[1]
    G = D // 2 // LANES
    tb = 2 * Tc
    tpb = S // tb
    row = lambda j: (j, 0)
    const = dict(pipeline_mode=pl.Buffered(1))
    return pl.pallas_call(
        functools.partial(_combine_kernel, Tc=Tc, G=G, final_norm=final_norm),
        grid=(N // tb,),
        in_specs=[pl.BlockSpec(memory_space=pl.ANY), pl.BlockSpec(memory_space=pl.ANY),
                  pl.BlockSpec((tb, TOP_K), row),
                  pl.BlockSpec((tb * G, LANES), row), pl.BlockSpec((tb, D), row),
                  pl.BlockSpec((1, 1, D), lambda j: (j // tpb, 0, 0)),
                  pl.BlockSpec((D, F), lambda j: (0, 0), **const),
                  pl.BlockSpec((D, F), lambda j: (0, 0), **const),
                  pl.BlockSpec((F, D), lambda j: (0, 0), **const),
                  pl.BlockSpec((1, D), lambda j: (0, 0))],
        out_specs=pl.BlockSpec((tb, D), row),
        out_shape=jax.ShapeDtypeStruct((N, D), F32),
        scratch_shapes=[pltpu.SMEM((Tc * TOP_K,), I32), pltpu.SMEM((Tc * TOP_K,), I32),
                        pltpu.VMEM((TOP_K, Tc * G, LANES), U32),
                        pltpu.VMEM((TOP_K, Tc * G, LANES), U32),
                        pltpu.SemaphoreType.DMA((2,)), pltpu.SemaphoreType.DMA((2,))],
        compiler_params=_params(("arbitrary",), 52),
        name="moe_combine",
    )(dest_flat, ys, gate_t, h2, x1, g2, sg, su, sd, final_g.reshape(1, D))


def _moe_ffn(h2, logits_t, x1, g2, router_bias, wg, wu, wd, e_off, sg, su, sd, final_g, S,
             final_norm):
    N, D = x1.shape
    G = D // 2 // LANES
    E = logits_t.shape[0]
    idx, gate, rank, cnt = _route(logits_t, router_bias)
    counts = cnt[:, 0]
    padded = (counts + MOE_BLOCK - 1) // MOE_BLOCK * MOE_BLOCK
    pend = jnp.cumsum(padded)
    pstart = (pend - padded).astype(I32)
    n_blocks = -(-(N * TOP_K + E * (MOE_BLOCK - 1)) // MOE_BLOCK)
    bstart = jnp.arange(n_blocks, dtype=I32) * MOE_BLOCK
    block_expert = jnp.minimum(jnp.sum(pend[None, :] <= bstart[:, None], axis=1), E - 1).astype(I32)
    nvalid = jnp.clip(pstart[block_expert] + counts[block_expert] - bstart, 0, MOE_BLOCK).astype(I32)
    nblocks = (pend[-1:] // MOE_BLOCK).astype(I32)

    dest = _dest(pstart, idx, rank, G)
    dest_flat = dest.T.reshape(N * TOP_K)
    xs = _dispatch(dest_flat, h2, n_blocks * MOE_BLOCK, G)
    ys = _moe(block_expert + e_off, nvalid, nblocks, xs, wg, wu, wd)
    return _combine(dest_flat, ys, gate.T, h2, x1, g2, sg, su, sd, final_g, S, final_norm)


def kernel(x, c, ada_w, ada_b, norm_mix_g, norm_ffn_g, w_in, lambda_q1, lambda_k1, lambda_q2,
           lambda_k2, subln_g, rpb, w_out, router_w, router_bias, exp_w_gate, exp_w_up,
           exp_w_down, sh_w_gate, sh_w_up, sh_w_down, final_g):
    B, S, D = x.shape
    L = ada_w.shape[0]
    N = B * S
    n_heads = w_in.shape[2] // (3 * HEAD_DIM)
    heads_a = n_heads // HEADS_A_FRAC
    heads_b = n_heads // HEADS_B_FRAC
    heads_c = n_heads - heads_a - heads_b
    slopes = _alibi_slopes(heads_a + heads_b)
    slopes_a = jnp.asarray(slopes[:heads_a], F32)
    slopes_b = jnp.asarray(slopes[heads_a:], F32)

    E, F = exp_w_gate.shape[1], exp_w_gate.shape[3]
    wg_all = exp_w_gate.reshape(L * E, D, F)
    wu_all = exp_w_up.reshape(L * E, D, F)
    wd_all = exp_w_down.reshape(L * E, F, D)

    mod = _ada(c, ada_w, ada_b).reshape(L, B, 6, 1, D)
    xf = x.reshape(N, D)
    for l in range(L):
        sh1, sc1, g1, sh2, sc2, g2 = (mod[l, :, j] for j in range(6))
        qkv = _qkv(xf, norm_mix_g[l], sc1, sh1, w_in[l].astype(BF16), S)
        oa = _attn_a(qkv, slopes_a, B, S, n_heads, heads_a)
        lambda_init = 0.8 - 0.6 * math.exp(-0.3 * l)
        lam_vecs = jnp.stack([lambda_q1[l], lambda_k1[l], lambda_q2[l], lambda_k2[l]]).astype(F32)
        ob = _attn_b(qkv, slopes_b, lam_vecs, subln_g[l], B, S, n_heads, heads_a, heads_b, lambda_init)
        oc = _attn_c(qkv, rpb[l].reshape(-1), B, S, n_heads, heads_a + heads_b, heads_c)
        x1, h2, logits_t = _out_proj(oa, ob, oc, w_out[l].astype(BF16), xf, g1, norm_ffn_g[l],
                                     sc2, sh2, router_w[l].T, S)
        xf = _moe_ffn(h2, logits_t, x1, g2, router_bias[l], wg_all, wu_all, wd_all, l * E,
                      sh_w_gate[l].astype(BF16), sh_w_up[l].astype(BF16),
                      sh_w_down[l].astype(BF16), final_g, S, l == L - 1)
    return xf.reshape(B, S, D)
```

```python
import functools
import math

import jax
import jax.numpy as jnp
from jax import lax
from jax.experimental import pallas as pl
from jax.experimental.pallas import tpu as pltpu

F32 = jnp.float32
BF16 = jnp.bfloat16
I32 = jnp.int32
U32 = jnp.uint32
HIGHEST = lax.Precision.HIGHEST
_NT = (((1,), (1,)), ((), ()))

HEAD_DIM = 128
HEADS_A_FRAC, HEADS_B_FRAC = 2, 4
DILATED_PATTERNS = ((128, 1), (512, 4), (2048, 16))
GRID_W = 64
NA_ROWS = 8
NA_COLS = 16
N_GROUPS = 8
TOPK_GROUPS = 4
TOP_K = 8
ROUTED_SCALE = 2.5
NORM_EPS = 1e-6
SUBLN_EPS = 1e-5
NEG_INF = -1e30
LOG2E = math.log2(math.e)

LANES = 128
VMEM_BYTES_V7X = 64 << 20

MOE_BLOCK = 512


def _params(semantics, vmem_mib):
    return pltpu.CompilerParams(dimension_semantics=semantics,
                                vmem_limit_bytes=min(vmem_mib << 20, VMEM_BYTES_V7X - (4 << 20)))


def _alibi_slopes(n):
    def pow2(m):
        start = 2.0 ** (-8.0 / m)
        return [start ** (i + 1) for i in range(m)]
    p = 2 ** int(math.floor(math.log2(n)))
    return pow2(p) + pow2(2 * p)[0::2][: n - p]


def _rms(x, eps):
    return x * lax.rsqrt(jnp.mean(x * x, axis=-1, keepdims=True) + eps)


def _pack_pairs(x):
    half = x.shape[1] // 2
    hi = lax.bitcast_convert_type(x[:, :half].astype(BF16).astype(F32), U32)
    lo = lax.bitcast_convert_type(x[:, half:].astype(BF16).astype(F32), U32)
    return hi | (lo >> 16)


def _unpack_pairs(w):
    hi = lax.bitcast_convert_type(w & jnp.uint32(0xFFFF0000), F32)
    lo = lax.bitcast_convert_type(w << 16, F32)
    return hi, lo


def _store_row_slabs(ref, packed):
    T, W = packed.shape
    for c in range(W // LANES):
        ref[pl.ds(c, T, stride=W // LANES), :] = packed[:, c * LANES:(c + 1) * LANES]


def _load_row_slab(ref, c, T, G):
    return ref[pl.ds(c, T, stride=G), :]


def _cast_kernel(w_ref, o_ref):
    o_ref[...] = w_ref[0].astype(BF16)


def _layer_bf16(w, l, tr=256):
    _, R, C = w.shape
    tr = min(tr, R)
    return pl.pallas_call(
        _cast_kernel,
        grid=(R // tr,),
        in_specs=[pl.BlockSpec((1, tr, C), lambda i: (l, i, 0))],
        out_specs=pl.BlockSpec((tr, C), lambda i: (i, 0)),
        out_shape=jax.ShapeDtypeStruct((R, C), BF16),
        compiler_params=_params(("arbitrary",), 32),
        name="weight_bf16",
    )(w)


def _ada_kernel(c_ref, w_ref, b_ref, o_ref):
    c = c_ref[...]
    sc = c * jax.nn.sigmoid(c)
    o_ref[0] = jnp.dot(sc, w_ref[0], preferred_element_type=F32, precision=HIGHEST) + b_ref[0]


def _ada(c, ada_w, ada_b):
    B, D = c.shape
    L, _, D6 = ada_w.shape
    rows = 8
    cp = jnp.zeros((rows, D), F32).at[:B].set(c)
    tn = 1024
    out = pl.pallas_call(
        _ada_kernel,
        grid=(L, D6 // tn),
        in_specs=[pl.BlockSpec((rows, D), lambda l, j: (0, 0)),
                  pl.BlockSpec((1, D, tn), lambda l, j: (l, 0, j)),
                  pl.BlockSpec((1, 1, tn), lambda l, j: (l, 0, j))],
        out_specs=pl.BlockSpec((1, rows, tn), lambda l, j: (l, 0, j)),
        out_shape=jax.ShapeDtypeStruct((L, rows, D6), F32),
        compiler_params=_params(("arbitrary", "arbitrary"), 32),
        name="ada_mod",
    )(cp, ada_w, ada_b.reshape(L, 1, D6))
    return out[:, :B]


def _qkv_kernel(x_ref, g_ref, sc_ref, sh_ref, w_ref, o_ref, h_scr, *, heads_per_step):
    @pl.when(pl.program_id(1) == 0)
    def _():
        y = _rms(x_ref[...], NORM_EPS) * g_ref[...]
        h_scr[...] = (y * (1.0 + sc_ref[0]) + sh_ref[0]).astype(BF16)

    res = jnp.dot(h_scr[...], w_ref[...], preferred_element_type=F32)
    for hh in range(heads_per_step):
        o_ref[hh] = res[:, hh * HEAD_DIM:(hh + 1) * HEAD_DIM].astype(BF16)


def _qkv(x2d, g, sc, sh, w_bf, S, tm=512, tn=2048):
    N, D = x2d.shape
    W3 = w_bf.shape[1]
    hps = tn // HEAD_DIM
    tpb = S // tm
    return pl.pallas_call(
        functools.partial(_qkv_kernel, heads_per_step=hps),
        grid=(N // tm, W3 // tn),
        in_specs=[pl.BlockSpec((tm, D), lambda i, j: (i, 0)),
                  pl.BlockSpec((1, D), lambda i, j: (0, 0)),
                  pl.BlockSpec((1, 1, D), lambda i, j: (i // tpb, 0, 0)),
                  pl.BlockSpec((1, 1, D), lambda i, j: (i // tpb, 0, 0)),
                  pl.BlockSpec((D, tn), lambda i, j: (0, j))],
        out_specs=pl.BlockSpec((hps, tm, HEAD_DIM), lambda i, j: (j, i, 0)),
        out_shape=jax.ShapeDtypeStruct((W3 // HEAD_DIM, N, HEAD_DIM), BF16),
        scratch_shapes=[pltpu.VMEM((tm, D), BF16)],
        compiler_params=_params(("arbitrary", "arbitrary"), 48),
        name="qkv_proj",
    )(x2d, g.reshape(1, D), sc, sh, w_bf)


def _col_to_row(col, eye):
    return jnp.sum(jnp.where(eye, col, 0.0), axis=0, keepdims=True)


def _attn_a_kernel(slopes_ref, q_ref, k_ref, v_ref, o_ref, lse_ref, nat, cq, ck, cv, onat, tmp, *,
                   L, dil, half, Tq):
    h = pl.program_id(1)
    wk = 2 * Tq
    n_tiles = L // Tq
    c_dist = LOG2E * slopes_ref[h] * dil
    a = lax.broadcasted_iota(I32, (Tq, wk), 0)
    u = lax.broadcasted_iota(I32, (Tq, wk), 1)
    eye = lax.broadcasted_iota(I32, (Tq, Tq), 0) == lax.broadcasted_iota(I32, (Tq, Tq), 1)

    def table(off):
        d = u - a - off
        ad = jnp.maximum(d, -d)
        return jnp.where(ad <= half, -c_dist * ad.astype(F32), NEG_INF)

    tabs = {off: table(off) for off in sorted({0, half, wk - Tq})}

    q_scale = LOG2E * HEAD_DIM ** -0.5
    d1 = 4 if dil > 4 else dil
    d2 = dil // d1
    S = L * dil
    if dil > 1:
        for src, dst, scale in ((q_ref, cq, q_scale), (k_ref, ck, None), (v_ref, cv, None)):
            x = src[0].astype(F32)
            nat[...] = x if scale is None else x * scale
            if d2 == 1:
                for r in range(dil):
                    dst[r * L:(r + 1) * L, :] = nat[pl.ds(r, L, stride=dil), :].astype(BF16)
            else:
                for ra in range(d1):
                    tmp[ra * (S // d1):(ra + 1) * (S // d1), :] = nat[pl.ds(ra, S // d1, stride=d1), :]
                for ra in range(d1):
                    for rb in range(d2):
                        r = ra + d1 * rb
                        dst[r * L:(r + 1) * L, :] = tmp[
                            pl.ds(ra * (S // d1) + rb, L, stride=d2), :].astype(BF16)
        kc, vc = ck, cv
    else:
        kc, vc = k_ref.at[0], v_ref.at[0]

    for r in range(dil):
        for j in range(n_tiles):
            start = min(max(j * Tq - half, 0), L - wk)
            rows = slice(r * L + j * Tq, r * L + (j + 1) * Tq)
            q = cq[rows, :] if dil > 1 else (q_ref[0, rows, :].astype(F32) * q_scale).astype(BF16)
            kw = kc[r * L + start:r * L + start + wk, :]
            vw = vc[r * L + start:r * L + start + wk, :]
            s = lax.dot_general(q, kw, _NT, preferred_element_type=F32) + tabs[j * Tq - start]
            m = jnp.max(s, axis=-1, keepdims=True)
            p = jnp.exp2(s - m)
            l = jnp.sum(p, axis=-1, keepdims=True)
            o = jnp.dot(p.astype(BF16), vw, preferred_element_type=F32) / l
            if dil == 1:
                o_ref[0, rows, :] = o.astype(BF16)
            elif d2 == 1:
                onat[pl.ds(r + j * Tq * dil, Tq, stride=dil), :] = o
            else:
                ra, rb = r % d1, r // d1
                tmp[pl.ds(ra * (S // d1) + rb + j * Tq * d2, Tq, stride=d2), :] = o
            lse_ref[0, 0, r:r + 1, j * Tq:(j + 1) * Tq] = _col_to_row(m + jnp.log2(l), eye)
    if d2 > 1:
        for ra in range(d1):
            onat[pl.ds(ra, S // d1, stride=d1), :] = tmp[ra * (S // d1):(ra + 1) * (S // d1), :]
    if dil > 1:
        o_ref[0] = onat[...].astype(BF16)


def _attn_a_pattern(qkv, slopes, B, S, n_heads, heads_a, window, dil, Tq=128):
    N = B * S
    L = S // dil
    half = window // (2 * dil)
    assert L % Tq == 0 and L >= 2 * Tq and 2 * half <= Tq
    blk = (1, S, HEAD_DIM)
    o, lse = pl.pallas_call(
        functools.partial(_attn_a_kernel, L=L, dil=dil, half=half, Tq=Tq),
        grid=(B, heads_a),
        in_specs=[pl.BlockSpec(memory_space=pltpu.SMEM),
                  pl.BlockSpec(blk, lambda b, h: (h, b, 0)),
                  pl.BlockSpec(blk, lambda b, h: (n_heads + h, b, 0)),
                  pl.BlockSpec(blk, lambda b, h: (2 * n_heads + h, b, 0))],
        out_specs=[pl.BlockSpec(blk, lambda b, h: (h, b, 0)),
                   pl.BlockSpec((1, 1, dil, L), lambda b, h: (b, h, 0, 0))],
        out_shape=[jax.ShapeDtypeStruct((heads_a, N, HEAD_DIM), BF16),
                   jax.ShapeDtypeStruct((B, heads_a, dil, L), F32)],
        scratch_shapes=[pltpu.VMEM((S, HEAD_DIM), F32)] + [pltpu.VMEM((S, HEAD_DIM), BF16)] * 3
        + [pltpu.VMEM((S, HEAD_DIM), F32)] * 2,
        compiler_params=_params(("arbitrary", "arbitrary"), 32),
        name=f"attn_dilated_d{dil}",
    )(slopes, qkv, qkv, qkv)
    lse_nat = lse.transpose(1, 0, 3, 2).reshape(heads_a, N)
    return o, lse_nat


def _mix_a_kernel(*refs, n_pat, tm):
    o_refs, lse_refs, out_ref = refs[:n_pat], refs[n_pat:2 * n_pat], refs[2 * n_pat]
    heads = lse_refs[0].shape[0]
    lses = [r[...] for r in lse_refs]
    mx = functools.reduce(jnp.maximum, lses)
    es = [jnp.exp2(x - mx) for x in lses]
    den = functools.reduce(lambda x, y: x + y, es)
    ws = [e / den for e in es]
    pad = jnp.zeros((LANES - n_pat * heads, LANES), F32)
    for c in range(tm // LANES):
        cols = slice(c * LANES, (c + 1) * LANES)
        w_rows = jnp.concatenate([w[:, cols] for w in ws] + [pad], axis=0)
        w_cols = w_rows.T
        for hh in range(heads):
            acc = jnp.zeros((LANES, HEAD_DIM), F32)
            for p in range(n_pat):
                wc = w_cols[:, p * heads + hh:p * heads + hh + 1]
                acc = acc + wc * o_refs[p][hh, cols, :].astype(F32)
            out_ref[cols, hh * HEAD_DIM:(hh + 1) * HEAD_DIM] = acc.astype(BF16)


def _mix_a(os, lses, tm=512):
    heads, N, _ = os[0].shape
    n_pat = len(os)
    assert n_pat * heads <= LANES
    return pl.pallas_call(
        functools.partial(_mix_a_kernel, n_pat=n_pat, tm=tm),
        grid=(N // tm,),
        in_specs=[pl.BlockSpec((heads, tm, HEAD_DIM), lambda i: (0, i, 0))] * n_pat
        + [pl.BlockSpec((heads, tm), lambda i: (0, i))] * n_pat,
        out_specs=pl.BlockSpec((tm, heads * HEAD_DIM), lambda i: (i, 0)),
        out_shape=jax.ShapeDtypeStruct((N, heads * HEAD_DIM), BF16),
        compiler_params=_params(("arbitrary",), 32),
        name="attn_dilated_mix",
    )(*os, *lses)


def _attn_a(qkv, slopes, B, S, n_heads, heads_a):
    parts = [_attn_a_pattern(qkv, slopes, B, S, n_heads, heads_a, w, d) for w, d in DILATED_PATTERNS]
    return _mix_a([p[0] for p in parts], [p[1] for p in parts])


def _attn_b_kernel(slopes_ref, lam_ref, q_ref, k_ref, v_ref, g_ref, o_ref, tab_ref, kmax_ref,
                   m_scr, l_scr, acc_scr, *, S, Tq, Tk, lambda_init):
    h = pl.program_id(1)
    i = pl.program_id(2)
    dh = HEAD_DIM // 2
    n_chunks = S // Tk
    c_dist = LOG2E * slopes_ref[h]

    def sub_norm(x):
        sq = x * x
        return jnp.sqrt(jnp.maximum(jnp.sum(sq[:, :dh], axis=-1, keepdims=True),
                                    jnp.sum(sq[:, dh:], axis=-1, keepdims=True)))

    @pl.when(i == 0)
    def _():
        shape = (Tq, 2 * S - Tq)
        a = lax.broadcasted_iota(I32, shape, 0)
        u = lax.broadcasted_iota(I32, shape, 1)
        d = a - u + (S - Tq)
        tab_ref[...] = -c_dist * jnp.maximum(d, -d).astype(F32)
        kn = sub_norm(k_ref[0].astype(F32))
        for j in range(n_chunks):
            kmax_ref[j] = jnp.max(kn[j * Tk:(j + 1) * Tk])

    lv = lam_ref[...]
    lam = (jnp.exp(jnp.sum(lv[0:1] * lv[1:2], axis=-1, keepdims=True))
           - jnp.exp(jnp.sum(lv[2:3] * lv[3:4], axis=-1, keepdims=True)) + lambda_init)

    t0 = i * Tq
    qf = q_ref[0].astype(F32) * (LOG2E * dh ** -0.5)
    q = qf.astype(BF16)
    qmax = jnp.max(sub_norm(q.astype(F32)))

    def chunk(kstart, first):
        kc = k_ref[0, pl.ds(kstart, Tk), :]
        vc = v_ref[0, pl.ds(kstart, Tk), :]
        bias = tab_ref[:, pl.ds(pl.multiple_of(S - Tq - t0 + kstart, LANES), Tk)]
        s = jnp.concatenate(
            [lax.dot_general(q[:, c * dh:(c + 1) * dh], kc[:, c * dh:(c + 1) * dh], _NT,
                             preferred_element_type=F32) + bias for c in range(2)], axis=0)
        mc = jnp.max(s, axis=-1, keepdims=True)
        if first:
            m_new = mc
        else:
            m_old = m_scr[...]
            m_new = jnp.maximum(m_old, mc)
            alpha = jnp.exp2(m_old - m_new)
        p = jnp.exp2(s - m_new)
        ls = jnp.sum(p, axis=-1, keepdims=True)
        pv = jnp.dot(p.astype(BF16), vc, preferred_element_type=F32)
        m_scr[...] = m_new
        l_scr[...] = ls if first else alpha * l_scr[...] + ls
        acc_scr[...] = pv if first else alpha * acc_scr[...] + pv
        return m_new

    jd = t0 // Tk
    m_diag = chunk(pl.multiple_of(jd * Tk, Tk), True)
    floor = jnp.min(m_diag) - 160.0
    for j in range(n_chunks):
        gap = jnp.maximum(jnp.maximum(j * Tk - (t0 + Tq - 1), t0 - ((j + 1) * Tk - 1)), 0)
        bound = 1.001 * qmax * kmax_ref[j] + 0.01 - c_dist * gap.astype(F32)

        @pl.when((j != jd) & (bound > floor))
        def _():
            chunk(j * Tk, False)

    acc = acc_scr[...]
    l = l_scr[...]
    o = acc[:Tq] / l[:Tq] - acc[Tq:] * (lam / l[Tq:])
    o = _rms(o, SUBLN_EPS) * g_ref[...] * (1.0 - lambda_init)
    o_ref[...] = o.astype(BF16)


def _attn_b(qkv, slopes, lam_vecs, subln_g, B, S, n_heads, head0, heads_b, lambda_init, Tq=256,
            Tk=1024):
    N = B * S
    nq = S // Tq
    Tk = min(Tk, S)
    return pl.pallas_call(
        functools.partial(_attn_b_kernel, S=S, Tq=Tq, Tk=Tk, lambda_init=lambda_init),
        grid=(B, heads_b, nq),
        in_specs=[pl.BlockSpec(memory_space=pltpu.SMEM),
                  pl.BlockSpec((4, HEAD_DIM // 2), lambda b, h, i: (0, 0)),
                  pl.BlockSpec((1, Tq, HEAD_DIM), lambda b, h, i: (head0 + h, b * nq + i, 0)),
                  pl.BlockSpec((1, S, HEAD_DIM), lambda b, h, i: (n_heads + head0 + h, b, 0)),
                  pl.BlockSpec((1, S, HEAD_DIM), lambda b, h, i: (2 * n_heads + head0 + h, b, 0)),
                  pl.BlockSpec((1, HEAD_DIM), lambda b, h, i: (0, 0))],
        out_specs=pl.BlockSpec((Tq, HEAD_DIM), lambda b, h, i: (b * nq + i, h)),
        out_shape=jax.ShapeDtypeStruct((N, heads_b * HEAD_DIM), BF16),
        scratch_shapes=[pltpu.VMEM((Tq, 2 * S - Tq), F32), pltpu.SMEM((S // Tk,), F32),
                        pltpu.VMEM((2 * Tq, 1), F32), pltpu.VMEM((2 * Tq, 1), F32),
                        pltpu.VMEM((2 * Tq, HEAD_DIM), F32)],
        compiler_params=_params(("arbitrary", "arbitrary", "arbitrary"), 48),
        name="attn_diff",
    )(slopes, lam_vecs, qkv, qkv, qkv, subln_g.reshape(1, HEAD_DIM))


NBR_GROUP = 4


def _nbr_pairs(R):
    kr, win = NA_ROWS, NA_ROWS + NBR_GROUP
    pairs = []
    for r0 in range(0, R, NBR_GROUP):
        ps = min(max(r0 - kr // 2, 0), R - win)
        rel = tuple(min(max(r0 + a - kr // 2, 0), R - kr) - ps for a in range(NBR_GROUP))
        pairs.append((r0, ps, (ps - r0,) + rel))
    return pairs, sorted({p[2] for p in pairs})


def _attn_c_kernel(rpb_ref, q_ref, k_ref, v_ref, o_ref, tab_ref, *, R):
    h = pl.program_id(1)
    W = GRID_W
    kr = min(NA_ROWS, R)
    n_dr = 2 * NA_ROWS - 1
    n_dc = 2 * NA_COLS - 1

    c_io = lax.broadcasted_iota(I32, (W, 2 * W), 0)
    l_io = lax.broadcasted_iota(I32, (W, 2 * W), 1)
    cp = l_io & (W - 1)
    dcm = cp - c_io + (NA_COLS - 1)
    cstart = jnp.clip(c_io - NA_COLS // 2, 0, W - NA_COLS)
    ok = (cp >= cstart) & (cp < cstart + NA_COLS)
    blocks = []
    for dr in range(n_dr):
        blk = jnp.zeros((W, 2 * W), F32)
        for dc in range(n_dc):
            blk = jnp.where(dcm == dc, rpb_ref[(h * n_dr + dr) * n_dc + dc], blk)
        blocks.append(jnp.where(ok, blk, NEG_INF))
    win = kr + NBR_GROUP
    pairs, cfgs = _nbr_pairs(R)
    neg = jnp.full((W, 2 * W), NEG_INF, F32)
    for ti, (off, *rels) in enumerate(cfgs):
        for a, rel in enumerate(rels):
            def key_row_block(j):
                return blocks[off + j - a + (NA_ROWS - 1)] if rel <= j < rel + kr else neg
            for jj in range(win // 2):
                tab_ref[ti, a * W:(a + 1) * W, jj * 2 * W:(jj + 1) * 2 * W] = jnp.where(
                    l_io < W, key_row_block(2 * jj), key_row_block(2 * jj + 1))

    for r0, ps, cfg in pairs:
        rows = slice(r0 * W, (r0 + NBR_GROUP) * W)
        kw = k_ref[0, ps * W:(ps + win) * W, :]
        vw = v_ref[0, ps * W:(ps + win) * W, :]
        s = lax.dot_general(q_ref[0, rows, :], kw, _NT, preferred_element_type=F32)
        s = s * (HEAD_DIM ** -0.5) + tab_ref[cfgs.index(cfg)]
        m = jnp.max(s, axis=-1, keepdims=True)
        p = jnp.exp(s - m)
        l = jnp.sum(p, axis=-1, keepdims=True)
        o = jnp.dot(p.astype(BF16), vw, preferred_element_type=F32) / l
        o_ref[rows, :] = o.astype(BF16)


def _attn_c(qkv, rpb_flat, B, S, n_heads, head0, heads_c):
    N = B * S
    R = S // GRID_W
    kr = NA_ROWS
    assert R % NBR_GROUP == 0 and R >= kr + NBR_GROUP and (kr + NBR_GROUP) % 2 == 0
    n_tabs = len(_nbr_pairs(R)[1])
    return pl.pallas_call(
        functools.partial(_attn_c_kernel, R=R),
        grid=(B, heads_c),
        in_specs=[pl.BlockSpec(memory_space=pltpu.SMEM),
                  pl.BlockSpec((1, S, HEAD_DIM), lambda b, h: (head0 + h, b, 0)),
                  pl.BlockSpec((1, S, HEAD_DIM), lambda b, h: (n_heads + head0 + h, b, 0)),
                  pl.BlockSpec((1, S, HEAD_DIM), lambda b, h: (2 * n_heads + head0 + h, b, 0))],
        out_specs=pl.BlockSpec((S, HEAD_DIM), lambda b, h: (b, h)),
        out_shape=jax.ShapeDtypeStruct((N, heads_c * HEAD_DIM), BF16),
        scratch_shapes=[pltpu.VMEM((n_tabs, NBR_GROUP * GRID_W, (kr + NBR_GROUP) * GRID_W), F32)],
        compiler_params=_params(("arbitrary", "arbitrary"), 32),
        name="attn_nbr",
    )(rpb_flat, qkv, qkv, qkv)


def _out_kernel(oa_ref, ob_ref, oc_ref, w_ref, x_ref, g1_ref, ng_ref, sc_ref, sh_ref, rw_ref,
                x1_ref, h2_ref, lg_ref):
    o = jnp.concatenate([oa_ref[...], ob_ref[...], oc_ref[...]], axis=1)
    acc = jnp.dot(o, w_ref[...], preferred_element_type=F32)
    x1 = x_ref[...] + g1_ref[0] * acc
    x1_ref[...] = x1
    h2 = (_rms(x1, NORM_EPS) * ng_ref[...]) * (1.0 + sc_ref[0]) + sh_ref[0]
    _store_row_slabs(h2_ref, _pack_pairs(h2))
    lg_ref[...] = lax.dot_general(rw_ref[...], h2, _NT, preferred_element_type=F32,
                                  precision=HIGHEST)


def _out_proj(oa, ob, oc, w_bf, x2d, g1, ng, sc, sh, rw_t, S, tm=512):
    N, D = x2d.shape
    wa, wb, wc = oa.shape[1], ob.shape[1], oc.shape[1]
    E = rw_t.shape[0]
    tpb = S // tm
    const = dict(pipeline_mode=pl.Buffered(1))
    row = lambda i: (i, 0)
    per_b = lambda i: (i // tpb, 0, 0)
    return pl.pallas_call(
        _out_kernel,
        grid=(N // tm,),
        in_specs=[pl.BlockSpec((tm, wa), row), pl.BlockSpec((tm, wb), row), pl.BlockSpec((tm, wc), row),
                  pl.BlockSpec((wa + wb + wc, D), lambda i: (0, 0), **const),
                  pl.BlockSpec((tm, D), row),
                  pl.BlockSpec((1, 1, D), per_b),
                  pl.BlockSpec((1, D), lambda i: (0, 0)),
                  pl.BlockSpec((1, 1, D), per_b),
                  pl.BlockSpec((1, 1, D), per_b),
                  pl.BlockSpec((E, D), lambda i: (0, 0), **const)],
        out_specs=[pl.BlockSpec((tm, D), row), pl.BlockSpec((tm * (D // 2 // LANES), LANES), row),
                   pl.BlockSpec((E, tm), lambda i: (0, i))],
        out_shape=[jax.ShapeDtypeStruct((N, D), F32),
                   jax.ShapeDtypeStruct((N * (D // 2 // LANES), LANES), U32),
                   jax.ShapeDtypeStruct((E, N), F32)],
        compiler_params=_params(("arbitrary",), 52),
        name="out_proj",
    )(oa, ob, oc, w_bf, x2d, g1, ng.reshape(1, D), sc, sh, rw_t)


def _first_argmax(vals, iota, big):
    mx = jnp.max(vals, axis=0, keepdims=True)
    idx = jnp.min(jnp.where(vals == mx, iota, big), axis=0, keepdims=True)
    return mx, idx


def _route_kernel(lg_ref, rb_ref, idx_ref, gate_ref, rank_ref, cnt_ref, tri_ref, carry_ref, *, Tt):
    i = pl.program_id(0)
    E = lg_ref.shape[0]
    gs = E // N_GROUPS

    @pl.when(i == 0)
    def _():
        r = lax.broadcasted_iota(I32, (Tt, Tt), 0)
        c = lax.broadcasted_iota(I32, (Tt, Tt), 1)
        tri_ref[...] = jnp.where(r < c, 1.0, 0.0).astype(BF16)
        carry_ref[...] = jnp.zeros_like(carry_ref)

    scores = jax.nn.sigmoid(lg_ref[...])
    sel = scores + rb_ref[...]
    e_io = lax.broadcasted_iota(I32, (E, Tt), 0).astype(F32)
    s_io = lax.broadcasted_iota(I32, (gs, Tt), 0).astype(F32)
    g_io = lax.broadcasted_iota(I32, (N_GROUPS, Tt), 0).astype(F32)

    grp = []
    for g in range(N_GROUPS):
        blk = sel[g * gs:(g + 1) * gs]
        m1, a1 = _first_argmax(blk, s_io, float(gs))
        m2 = jnp.max(jnp.where(s_io == a1, -jnp.inf, blk), axis=0, keepdims=True)
        grp.append(m1 + m2)
    grp = jnp.concatenate(grp, axis=0)

    gsel = jnp.zeros((N_GROUPS, Tt), F32)
    for _ in range(TOPK_GROUPS):
        _, gi = _first_argmax(grp, g_io, float(N_GROUPS))
        hit = g_io == gi
        gsel = jnp.where(hit, 1.0, gsel)
        grp = jnp.where(hit, -jnp.inf, grp)

    esel = jnp.concatenate(
        [jnp.broadcast_to(gsel[g:g + 1], (gs, Tt)) for g in range(N_GROUPS)], axis=0)
    cur = jnp.where(esel > 0.0, sel, NEG_INF)

    idxs, gates = [], []
    onehot = jnp.zeros((E, Tt), F32)
    for _ in range(TOP_K):
        _, ei = _first_argmax(cur, e_io, float(E))
        hit = e_io == ei
        idxs.append(ei)
        gates.append(jnp.sum(jnp.where(hit, scores, 0.0), axis=0, keepdims=True))
        onehot = jnp.where(hit, 1.0, onehot)
        cur = jnp.where(hit, -jnp.inf, cur)
    gate = jnp.concatenate(gates, axis=0)
    gate = gate / jnp.sum(gate, axis=0, keepdims=True) * ROUTED_SCALE
    idx_ref[...] = jnp.concatenate(idxs, axis=0).astype(I32)
    gate_ref[...] = gate

    oh = onehot.astype(BF16)
    before = jnp.dot(oh, tri_ref[...], preferred_element_type=F32)
    tile_cnt = jnp.dot(oh, jnp.ones((Tt, LANES), BF16), preferred_element_type=F32)
    carry = carry_ref[...]
    rank_mat = before + jnp.concatenate([carry] * (Tt // LANES), axis=1)
    ranks = [jnp.sum(jnp.where(e_io == ei, rank_mat, 0.0), axis=0, keepdims=True) for ei in idxs]
    rank_ref[...] = jnp.concatenate(ranks, axis=0).astype(I32)
    carry_ref[...] = carry + tile_cnt
    cnt_ref[...] = (carry + tile_cnt).astype(I32)


def _route(logits_t, router_bias, Tt=512):
    E, N = logits_t.shape
    tok = lambda i: (0, i)
    return pl.pallas_call(
        functools.partial(_route_kernel, Tt=Tt),
        grid=(N // Tt,),
        in_specs=[pl.BlockSpec((E, Tt), tok), pl.BlockSpec((E, 1), lambda i: (0, 0))],
        out_specs=[pl.BlockSpec((TOP_K, Tt), tok), pl.BlockSpec((TOP_K, Tt), tok),
                   pl.BlockSpec((TOP_K, Tt), tok), pl.BlockSpec((E, LANES), lambda i: (0, 0))],
        out_shape=[jax.ShapeDtypeStruct((TOP_K, N), I32), jax.ShapeDtypeStruct((TOP_K, N), F32),
                   jax.ShapeDtypeStruct((TOP_K, N), I32), jax.ShapeDtypeStruct((E, LANES), I32)],
        scratch_shapes=[pltpu.VMEM((Tt, Tt), BF16), pltpu.VMEM((E, LANES), F32)],
        compiler_params=_params(("arbitrary",), 32),
        name="route_topk",
    )(logits_t, router_bias.reshape(E, 1))


def _dest_kernel(pstart_ref, idx_ref, rank_ref, dest_ref, *, G):
    idx = idx_ref[...]
    base = jnp.zeros(idx.shape, I32)
    for e in range(pstart_ref.shape[0]):
        base = jnp.where(idx == e, pstart_ref[e], base)
    dest_ref[...] = (base + rank_ref[...]) * G


def _dest(pstart, idx, rank, G):
    return pl.pallas_call(
        functools.partial(_dest_kernel, G=G),
        in_specs=[pl.BlockSpec(memory_space=pltpu.SMEM), pl.BlockSpec(memory_space=pltpu.VMEM),
                  pl.BlockSpec(memory_space=pltpu.VMEM)],
        out_specs=pl.BlockSpec(memory_space=pltpu.VMEM),
        out_shape=jax.ShapeDtypeStruct(idx.shape, I32),
        name="route_dest",
    )(pstart, idx, rank)


def _dispatch_kernel(dest_hbm, h_ref, xs_hbm, dest_smem, idx_sem, row_sem, *, Td, G):
    i = pl.program_id(0)
    n = Td * TOP_K
    cp = pltpu.make_async_copy(dest_hbm.at[pl.ds(i * n, n)], dest_smem, idx_sem)
    cp.start()
    cp.wait()

    def row_copy(t, k):
        src = h_ref.at[pl.ds(pl.multiple_of(t * G, G), G)]
        dst = xs_hbm.at[pl.ds(pl.multiple_of(dest_smem[t * TOP_K + k], G), G)]
        return pltpu.make_async_copy(src, dst, row_sem)

    def issue(t, carry):
        for k in range(TOP_K):
            row_copy(t, k).start(priority=k % 2)
        return carry

    lax.fori_loop(0, Td, issue, 0)

    def drain(t, carry):
        for k in range(TOP_K):
            row_copy(t, k).wait()
        return carry

    lax.fori_loop(0, Td, drain, 0)


def _dispatch(dest_flat, h2, P, G, Td=512):
    N = h2.shape[0] // G
    return pl.pallas_call(
        functools.partial(_dispatch_kernel, Td=Td, G=G),
        grid=(N // Td,),
        in_specs=[pl.BlockSpec(memory_space=pl.ANY), pl.BlockSpec((Td * G, LANES), lambda i: (i, 0))],
        out_specs=pl.BlockSpec(memory_space=pl.ANY),
        out_shape=jax.ShapeDtypeStruct((P * G, LANES), h2.dtype),
        scratch_shapes=[pltpu.SMEM((Td * TOP_K,), I32), pltpu.SemaphoreType.DMA,
                        pltpu.SemaphoreType.DMA],
        compiler_params=_params(("arbitrary",), 32),
        name="moe_dispatch",
    )(dest_flat, h2)


def _moe_kernel(be_ref, nv_ref, nb_ref, xs_ref, wg_ref, wu_ref, wd_ref, ys_ref, wg_s, wu_s, wd_s):
    b = pl.program_id(0)
    active = b < nb_ref[0]
    new_expert = (b == 0) | (be_ref[b] != be_ref[jnp.maximum(b - 1, 0)])

    @pl.when(active & new_expert)
    def _():
        wg_s[...] = wg_ref[0].astype(BF16)
        wu_s[...] = wu_ref[0].astype(BF16)
        wd_s[...] = wd_ref[0].astype(BF16)

    @pl.when(active)
    def _():
        T = MOE_BLOCK
        G = xs_ref.shape[0] // T
        valid = lax.broadcasted_iota(I32, (T, LANES), 0) < nv_ref[b]
        his, los = [], []
        for c in range(G):
            w = jnp.where(valid, _load_row_slab(xs_ref, c, T, G), jnp.uint32(0))
            hi, lo = _unpack_pairs(w)
            his.append(hi.astype(BF16))
            los.append(lo.astype(BF16))
        x = jnp.concatenate(his + los, axis=1)
        g = jnp.dot(x, wg_s[...], preferred_element_type=F32)
        u = jnp.dot(x, wu_s[...], preferred_element_type=F32)
        a = (g * jax.nn.sigmoid(g) * u).astype(BF16)
        _store_row_slabs(ys_ref, _pack_pairs(jnp.dot(a, wd_s[...], preferred_element_type=F32)))


def _moe(block_expert, nvalid, nblocks, xs, wg, wu, wd):
    _, D, F = wg.shape
    G = D // 2 // LANES
    P = xs.shape[0] // G
    n_blocks = P // MOE_BLOCK
    blk = lambda b, be, nv, nb: (jnp.minimum(b, nb[0] - 1), 0)
    return pl.pallas_call(
        _moe_kernel,
        grid_spec=pltpu.PrefetchScalarGridSpec(
            num_scalar_prefetch=3,
            grid=(n_blocks,),
            in_specs=[pl.BlockSpec((MOE_BLOCK * G, LANES), blk),
                      pl.BlockSpec((1, D, F), lambda b, be, nv, nb: (be[b], 0, 0)),
                      pl.BlockSpec((1, D, F), lambda b, be, nv, nb: (be[b], 0, 0)),
                      pl.BlockSpec((1, F, D), lambda b, be, nv, nb: (be[b], 0, 0))],
            out_specs=pl.BlockSpec((MOE_BLOCK * G, LANES), blk),
            scratch_shapes=[pltpu.VMEM((D, F), BF16), pltpu.VMEM((D, F), BF16),
                            pltpu.VMEM((F, D), BF16)]),
        out_shape=jax.ShapeDtypeStruct((P * G, LANES), U32),
        compiler_params=_params(("arbitrary",), 52),
        name="moe_experts",
    )(block_expert, nvalid, nblocks, xs, wg, wu, wd)


def _combine_kernel(dest_hbm, ys_hbm, gate_ref, h_ref, x_ref, g2_ref, sg_ref, su_ref, sd_ref, fg_ref,
                    o_ref, dest0, dest1, ybuf0, ybuf1, idx_sem, row_sem, *, Tc, G, final_norm):
    j = pl.program_id(0)
    n = Tc * TOP_K
    slots = ((dest0, ybuf0), (dest1, ybuf1))

    def row_copy(slot, t, k):
        dest, ybuf = slots[slot]
        src = ys_hbm.at[pl.ds(pl.multiple_of(dest[t * TOP_K + k], G), G)]
        dst = ybuf.at[k, pl.ds(pl.multiple_of(t * G, G), G)]
        return pltpu.make_async_copy(src, dst, row_sem.at[slot])

    def request(tile, slot):
        cp = pltpu.make_async_copy(dest_hbm.at[pl.ds(tile * n, n)], slots[slot][0],
                                   idx_sem.at[slot])
        cp.start()
        cp.wait()

        def issue(t, carry):
            for k in range(TOP_K):
                row_copy(slot, t, k).start(priority=k % 2)
            return carry

        lax.fori_loop(0, Tc, issue, 0)

    def reduce(slot):
        rows = slice(slot * Tc, (slot + 1) * Tc)
        h_tile = h_ref.at[pl.ds(slot * Tc * G, Tc * G)]
        halves = [_unpack_pairs(_load_row_slab(h_tile, c, Tc, G)) for c in range(G)]
        h = jnp.concatenate([p[0].astype(BF16) for p in halves]
                            + [p[1].astype(BF16) for p in halves], axis=1)
        g = jnp.dot(h, sg_ref[...], preferred_element_type=F32)
        u = jnp.dot(h, su_ref[...], preferred_element_type=F32)
        a = (g * jax.nn.sigmoid(g) * u).astype(BF16)
        f = jnp.dot(a, sd_ref[...], preferred_element_type=F32)

        def drain(t, carry):
            for k in range(TOP_K):
                row_copy(slot, t, k).wait()
            return carry

        lax.fori_loop(0, Tc, drain, 0)
        gate = gate_ref[rows, :]
        half = f.shape[1] // 2
        his, los = [], []
        for c in range(G):
            f_hi = f[:, c * LANES:(c + 1) * LANES]
            f_lo = f[:, half + c * LANES:half + (c + 1) * LANES]
            for k in range(TOP_K):
                y_hi, y_lo = _unpack_pairs(_load_row_slab(slots[slot][1].at[k], c, Tc, G))
                f_hi = f_hi + gate[:, k:k + 1] * y_hi
                f_lo = f_lo + gate[:, k:k + 1] * y_lo
            his.append(f_hi)
            los.append(f_lo)
        x2 = x_ref[rows, :] + g2_ref[0] * jnp.concatenate(his + los, axis=1)
        if final_norm:
            x2 = _rms(x2, NORM_EPS) * fg_ref[...]
        o_ref[rows, :] = x2

    @pl.when(j == 0)
    def _():
        request(0, 0)

    request(2 * j + 1, 1)
    reduce(0)

    @pl.when(j + 1 < pl.num_programs(0))
    def _():
        request(2 * j + 2, 0)

    reduce(1)


def _combine(dest_flat, ys, gate_t, h2, x1, g2, sg, su, sd, final_g, S, final_norm, Tc=256):
    N, D = x1.shape
    F = sg.shape[1]
    G = D // 2 // LANES
    tb = 2 * Tc
    tpb = S // tb
    row = lambda j: (j, 0)
    const = dict(pipeline_mode=pl.Buffered(1))
    return pl.pallas_call(
        functools.partial(_combine_kernel, Tc=Tc, G=G, final_norm=final_norm),
        grid=(N // tb,),
        in_specs=[pl.BlockSpec(memory_space=pl.ANY), pl.BlockSpec(memory_space=pl.ANY),
                  pl.BlockSpec((tb, TOP_K), row),
                  pl.BlockSpec((tb * G, LANES), row), pl.BlockSpec((tb, D), row),
                  pl.BlockSpec((1, 1, D), lambda j: (j // tpb, 0, 0)),
                  pl.BlockSpec((D, F), lambda j: (0, 0), **const),
                  pl.BlockSpec((D, F), lambda j: (0, 0), **const),
                  pl.BlockSpec((F, D), lambda j: (0, 0), **const),
                  pl.BlockSpec((1, D), lambda j: (0, 0))],
        out_specs=pl.BlockSpec((tb, D), row),
        out_shape=jax.ShapeDtypeStruct((N, D), F32),
        scratch_shapes=[pltpu.SMEM((Tc * TOP_K,), I32), pltpu.SMEM((Tc * TOP_K,), I32),
                        pltpu.VMEM((TOP_K, Tc * G, LANES), U32),
                        pltpu.VMEM((TOP_K, Tc * G, LANES), U32),
                        pltpu.SemaphoreType.DMA((2,)), pltpu.SemaphoreType.DMA((2,))],
        compiler_params=_params(("arbitrary",), 52),
        name="moe_combine",
    )(dest_flat, ys, gate_t, h2, x1, g2, sg, su, sd, final_g.reshape(1, D))


def _moe_ffn(h2, logits_t, x1, g2, router_bias, wg, wu, wd, e_off, sg, su, sd, final_g, S,
             final_norm):
    N, D = x1.shape
    G = D // 2 // LANES
    E = logits_t.shape[0]
    idx, gate, rank, cnt = _route(logits_t, router_bias)
    counts = cnt[:, 0]
    padded = (counts + MOE_BLOCK - 1) // MOE_BLOCK * MOE_BLOCK
    pend = jnp.cumsum(padded)
    pstart = (pend - padded).astype(I32)
    n_blocks = -(-(N * TOP_K + E * (MOE_BLOCK - 1)) // MOE_BLOCK)
    bstart = jnp.arange(n_blocks, dtype=I32) * MOE_BLOCK
    block_expert = jnp.minimum(jnp.sum(pend[None, :] <= bstart[:, None], axis=1), E - 1).astype(I32)
    nvalid = jnp.clip(pstart[block_expert] + counts[block_expert] - bstart, 0, MOE_BLOCK).astype(I32)
    nblocks = (pend[-1:] // MOE_BLOCK).astype(I32)

    dest = _dest(pstart, idx, rank, G)
    dest_flat = dest.T.reshape(N * TOP_K)
    xs = _dispatch(dest_flat, h2, n_blocks * MOE_BLOCK, G)
    ys = _moe(block_expert + e_off, nvalid, nblocks, xs, wg, wu, wd)
    return _combine(dest_flat, ys, gate.T, h2, x1, g2, sg, su, sd, final_g, S, final_norm)


def kernel(x, c, ada_w, ada_b, norm_mix_g, norm_ffn_g, w_in, lambda_q1, lambda_k1, lambda_q2,
           lambda_k2, subln_g, rpb, w_out, router_w, router_bias, exp_w_gate, exp_w_up,
           exp_w_down, sh_w_gate, sh_w_up, sh_w_down, final_g):
    B, S, D = x.shape
    L = ada_w.shape[0]
    N = B * S
    n_heads = w_in.shape[2] // (3 * HEAD_DIM)
    heads_a = n_heads // HEADS_A_FRAC
    heads_b = n_heads // HEADS_B_FRAC
    heads_c = n_heads - heads_a - heads_b
    slopes = _alibi_slopes(heads_a + heads_b)
    slopes_a = jnp.asarray(slopes[:heads_a], F32)
    slopes_b = jnp.asarray(slopes[heads_a:], F32)

    E, F = exp_w_gate.shape[1], exp_w_gate.shape[3]
    wg_all = exp_w_gate.reshape(L * E, D, F)
    wu_all = exp_w_up.reshape(L * E, D, F)
    wd_all = exp_w_down.reshape(L * E, F, D)

    mod = _ada(c, ada_w, ada_b).reshape(L, B, 6, 1, D)
    xf = x.reshape(N, D)
    for l in range(L):
        sh1, sc1, g1, sh2, sc2, g2 = (mod[l, :, j] for j in range(6))
        qkv = _qkv(xf, norm_mix_g[l], sc1, sh1, _layer_bf16(w_in, l), S)
        oa = _attn_a(qkv, slopes_a, B, S, n_heads, heads_a)
        lambda_init = 0.8 - 0.6 * math.exp(-0.3 * l)
        lam_vecs = jnp.stack([lambda_q1[l], lambda_k1[l], lambda_q2[l], lambda_k2[l]]).astype(F32)
        ob = _attn_b(qkv, slopes_b, lam_vecs, subln_g[l], B, S, n_heads, heads_a, heads_b, lambda_init)
        oc = _attn_c(qkv, rpb[l].reshape(-1), B, S, n_heads, heads_a + heads_b, heads_c)
        x1, h2, logits_t = _out_proj(oa, ob, oc, _layer_bf16(w_out, l), xf, g1, norm_ffn_g[l],
                                     sc2, sh2, router_w[l].T, S)
        xf = _moe_ffn(h2, logits_t, x1, g2, router_bias[l], wg_all, wu_all, wd_all, l * E,
                      _layer_bf16(sh_w_gate, l), _layer_bf16(sh_w_up, l),
                      _layer_bf16(sh_w_down, l), final_g, S, l == L - 1)
    return xf.reshape(B, S, D)
```

```python
import functools
import math

import jax
import jax.numpy as jnp
from jax import lax
from jax.experimental import pallas as pl
from jax.experimental.pallas import tpu as pltpu

F32 = jnp.float32
BF16 = jnp.bfloat16
I32 = jnp.int32
U32 = jnp.uint32
HIGHEST = lax.Precision.HIGHEST
_NT = (((1,), (1,)), ((), ()))

HEAD_DIM = 128
HEADS_A_FRAC, HEADS_B_FRAC = 2, 4
DILATED_PATTERNS = ((128, 1), (512, 4), (2048, 16))
GRID_W = 64
NA_ROWS = 8
NA_COLS = 16
N_GROUPS = 8
TOPK_GROUPS = 4
TOP_K = 8
ROUTED_SCALE = 2.5
NORM_EPS = 1e-6
SUBLN_EPS = 1e-5
NEG_INF = -1e30
LOG2E = math.log2(math.e)

LANES = 128
VMEM_BYTES_V7X = 64 << 20

MOE_BLOCK = 512


def _params(semantics, vmem_mib):
    return pltpu.CompilerParams(dimension_semantics=semantics,
                                vmem_limit_bytes=min(vmem_mib << 20, VMEM_BYTES_V7X - (4 << 20)))


def _alibi_slopes(n):
    def pow2(m):
        start = 2.0 ** (-8.0 / m)
        return [start ** (i + 1) for i in range(m)]
    p = 2 ** int(math.floor(math.log2(n)))
    return pow2(p) + pow2(2 * p)[0::2][: n - p]


def _rms(x, eps):
    return x * lax.rsqrt(jnp.mean(x * x, axis=-1, keepdims=True) + eps)


def _pack_pairs(x):
    half = x.shape[1] // 2
    hi = lax.bitcast_convert_type(x[:, :half].astype(BF16).astype(F32), U32)
    lo = lax.bitcast_convert_type(x[:, half:].astype(BF16).astype(F32), U32)
    return hi | (lo >> 16)


def _unpack_pairs(w):
    hi = lax.bitcast_convert_type(w & jnp.uint32(0xFFFF0000), F32)
    lo = lax.bitcast_convert_type(w << 16, F32)
    return hi, lo


def _store_row_slabs(ref, packed):
    T, W = packed.shape
    for c in range(W // LANES):
        ref[pl.ds(c, T, stride=W // LANES), :] = packed[:, c * LANES:(c + 1) * LANES]


def _load_row_slab(ref, c, T, G):
    return ref[pl.ds(c, T, stride=G), :]


def _cast_kernel(w_ref, o_ref):
    o_ref[...] = w_ref[0].astype(BF16)


def _layer_bf16(w, l, tr=256):
    _, R, C = w.shape
    tr = min(tr, R)
    return pl.pallas_call(
        _cast_kernel,
        grid=(R // tr,),
        in_specs=[pl.BlockSpec((1, tr, C), lambda i: (l, i, 0))],
        out_specs=pl.BlockSpec((tr, C), lambda i: (i, 0)),
        out_shape=jax.ShapeDtypeStruct((R, C), BF16),
        compiler_params=_params(("arbitrary",), 32),
        name="weight_bf16",
    )(w)


def _ada_kernel(c_ref, w_ref, b_ref, o_ref):
    c = c_ref[...]
    sc = c * jax.nn.sigmoid(c)
    o_ref[0] = jnp.dot(sc, w_ref[0], preferred_element_type=F32, precision=HIGHEST) + b_ref[0]


def _ada(c, ada_w, ada_b):
    B, D = c.shape
    L, _, D6 = ada_w.shape
    rows = 8
    cp = jnp.zeros((rows, D), F32).at[:B].set(c)
    tn = 1024
    out = pl.pallas_call(
        _ada_kernel,
        grid=(L, D6 // tn),
        in_specs=[pl.BlockSpec((rows, D), lambda l, j: (0, 0)),
                  pl.BlockSpec((1, D, tn), lambda l, j: (l, 0, j)),
                  pl.BlockSpec((1, 1, tn), lambda l, j: (l, 0, j))],
        out_specs=pl.BlockSpec((1, rows, tn), lambda l, j: (l, 0, j)),
        out_shape=jax.ShapeDtypeStruct((L, rows, D6), F32),
        compiler_params=_params(("arbitrary", "arbitrary"), 32),
        name="ada_mod",
    )(cp, ada_w, ada_b.reshape(L, 1, D6))
    return out[:, :B]


def _qkv_kernel(x_ref, g_ref, sc_ref, sh_ref, w_ref, o_ref, h_scr, *, heads_per_step):
    @pl.when(pl.program_id(1) == 0)
    def _():
        y = _rms(x_ref[...], NORM_EPS) * g_ref[...]
        h_scr[...] = (y * (1.0 + sc_ref[0]) + sh_ref[0]).astype(BF16)

    res = jnp.dot(h_scr[...], w_ref[...], preferred_element_type=F32)
    for hh in range(heads_per_step):
        o_ref[hh] = res[:, hh * HEAD_DIM:(hh + 1) * HEAD_DIM].astype(BF16)


def _qkv(x2d, g, sc, sh, w_bf, S, tm=512, tn=2048):
    N, D = x2d.shape
    W3 = w_bf.shape[1]
    hps = tn // HEAD_DIM
    tpb = S // tm
    return pl.pallas_call(
        functools.partial(_qkv_kernel, heads_per_step=hps),
        grid=(N // tm, W3 // tn),
        in_specs=[pl.BlockSpec((tm, D), lambda i, j: (i, 0)),
                  pl.BlockSpec((1, D), lambda i, j: (0, 0)),
                  pl.BlockSpec((1, 1, D), lambda i, j: (i // tpb, 0, 0)),
                  pl.BlockSpec((1, 1, D), lambda i, j: (i // tpb, 0, 0)),
                  pl.BlockSpec((D, tn), lambda i, j: (0, j))],
        out_specs=pl.BlockSpec((hps, tm, HEAD_DIM), lambda i, j: (j, i, 0)),
        out_shape=jax.ShapeDtypeStruct((W3 // HEAD_DIM, N, HEAD_DIM), BF16),
        scratch_shapes=[pltpu.VMEM((tm, D), BF16)],
        compiler_params=_params(("arbitrary", "arbitrary"), 48),
        name="qkv_proj",
    )(x2d, g.reshape(1, D), sc, sh, w_bf)


def _col_to_row(col, eye):
    return jnp.sum(jnp.where(eye, col, 0.0), axis=0, keepdims=True)


def _attn_a_kernel(slopes_ref, q_ref, k_ref, v_ref, o_ref, lse_ref, nat, cq, ck, cv, onat, tmp, *,
                   L, dil, half, Tq):
    h = pl.program_id(1)
    wk = 2 * Tq
    n_tiles = L // Tq
    c_dist = LOG2E * slopes_ref[h] * dil
    a = lax.broadcasted_iota(I32, (Tq, wk), 0)
    u = lax.broadcasted_iota(I32, (Tq, wk), 1)
    eye = lax.broadcasted_iota(I32, (Tq, Tq), 0) == lax.broadcasted_iota(I32, (Tq, Tq), 1)

    def table(off):
        d = u - a - off
        ad = jnp.maximum(d, -d)
        return jnp.where(ad <= half, -c_dist * ad.astype(F32), NEG_INF)

    tabs = {off: table(off) for off in sorted({0, half, wk - Tq})}

    q_scale = LOG2E * HEAD_DIM ** -0.5
    d1 = 4 if dil > 4 else dil
    d2 = dil // d1
    S = L * dil
    if dil > 1:
        for src, dst, scale in ((q_ref, cq, q_scale), (k_ref, ck, None), (v_ref, cv, None)):
            x = src[0].astype(F32)
            nat[...] = x if scale is None else x * scale
            if d2 == 1:
                for r in range(dil):
                    dst[r * L:(r + 1) * L, :] = nat[pl.ds(r, L, stride=dil), :].astype(BF16)
            else:
                for ra in range(d1):
                    tmp[ra * (S // d1):(ra + 1) * (S // d1), :] = nat[pl.ds(ra, S // d1, stride=d1), :]
                for ra in range(d1):
                    for rb in range(d2):
                        r = ra + d1 * rb
                        dst[r * L:(r + 1) * L, :] = tmp[
                            pl.ds(ra * (S // d1) + rb, L, stride=d2), :].astype(BF16)
        kc, vc = ck, cv
    else:
        kc, vc = k_ref.at[0], v_ref.at[0]

    for r in range(dil):
        for j in range(n_tiles):
            start = min(max(j * Tq - half, 0), L - wk)
            rows = slice(r * L + j * Tq, r * L + (j + 1) * Tq)
            q = cq[rows, :] if dil > 1 else (q_ref[0, rows, :].astype(F32) * q_scale).astype(BF16)
            kw = kc[r * L + start:r * L + start + wk, :]
            vw = vc[r * L + start:r * L + start + wk, :]
            s = lax.dot_general(q, kw, _NT, preferred_element_type=F32) + tabs[j * Tq - start]
            m = jnp.max(s, axis=-1, keepdims=True)
            p = jnp.exp2(s - m)
            l = jnp.sum(p, axis=-1, keepdims=True)
            o = jnp.dot(p.astype(BF16), vw, preferred_element_type=F32) / l
            if dil == 1:
                o_ref[0, rows, :] = o.astype(BF16)
            elif d2 == 1:
                onat[pl.ds(r + j * Tq * dil, Tq, stride=dil), :] = o
            else:
                ra, rb = r % d1, r // d1
                tmp[pl.ds(ra * (S // d1) + rb + j * Tq * d2, Tq, stride=d2), :] = o
            lse_ref[0, 0, r:r + 1, j * Tq:(j + 1) * Tq] = _col_to_row(m + jnp.log2(l), eye)
    if d2 > 1:
        for ra in range(d1):
            onat[pl.ds(ra, S // d1, stride=d1), :] = tmp[ra * (S // d1):(ra + 1) * (S // d1), :]
    if dil > 1:
        o_ref[0] = onat[...].astype(BF16)


def _attn_a_pattern(qkv, slopes, B, S, n_heads, heads_a, window, dil, Tq=128):
    N = B * S
    L = S // dil
    half = window // (2 * dil)
    assert L % Tq == 0 and L >= 2 * Tq and 2 * half <= Tq
    blk = (1, S, HEAD_DIM)
    o, lse = pl.pallas_call(
        functools.partial(_attn_a_kernel, L=L, dil=dil, half=half, Tq=Tq),
        grid=(B, heads_a),
        in_specs=[pl.BlockSpec(memory_space=pltpu.SMEM),
                  pl.BlockSpec(blk, lambda b, h: (h, b, 0)),
                  pl.BlockSpec(blk, lambda b, h: (n_heads + h, b, 0)),
                  pl.BlockSpec(blk, lambda b, h: (2 * n_heads + h, b, 0))],
        out_specs=[pl.BlockSpec(blk, lambda b, h: (h, b, 0)),
                   pl.BlockSpec((1, 1, dil, L), lambda b, h: (b, h, 0, 0))],
        out_shape=[jax.ShapeDtypeStruct((heads_a, N, HEAD_DIM), BF16),
                   jax.ShapeDtypeStruct((B, heads_a, dil, L), F32)],
        scratch_shapes=[pltpu.VMEM((S, HEAD_DIM), F32)] + [pltpu.VMEM((S, HEAD_DIM), BF16)] * 3
        + [pltpu.VMEM((S, HEAD_DIM), F32)] * 2,
        compiler_params=_params(("arbitrary", "arbitrary"), 32),
        name=f"attn_dilated_d{dil}",
    )(slopes, qkv, qkv, qkv)
    lse_nat = lse.transpose(1, 0, 3, 2).reshape(heads_a, N)
    return o, lse_nat


def _mix_a_kernel(*refs, n_pat, tm):
    o_refs, lse_refs, out_ref = refs[:n_pat], refs[n_pat:2 * n_pat], refs[2 * n_pat]
    heads = lse_refs[0].shape[0]
    lses = [r[...] for r in lse_refs]
    mx = functools.reduce(jnp.maximum, lses)
    es = [jnp.exp2(x - mx) for x in lses]
    den = functools.reduce(lambda x, y: x + y, es)
    ws = [e / den for e in es]
    pad = jnp.zeros((LANES - n_pat * heads, LANES), F32)
    for c in range(tm // LANES):
        cols = slice(c * LANES, (c + 1) * LANES)
        w_rows = jnp.concatenate([w[:, cols] for w in ws] + [pad], axis=0)
        w_cols = w_rows.T
        for hh in range(heads):
            acc = jnp.zeros((LANES, HEAD_DIM), F32)
            for p in range(n_pat):
                wc = w_cols[:, p * heads + hh:p * heads + hh + 1]
                acc = acc + wc * o_refs[p][hh, cols, :].astype(F32)
            out_ref[cols, hh * HEAD_DIM:(hh + 1) * HEAD_DIM] = acc.astype(BF16)


def _mix_a(os, lses, tm=512):
    heads, N, _ = os[0].shape
    n_pat = len(os)
    assert n_pat * heads <= LANES
    return pl.pallas_call(
        functools.partial(_mix_a_kernel, n_pat=n_pat, tm=tm),
        grid=(N // tm,),
        in_specs=[pl.BlockSpec((heads, tm, HEAD_DIM), lambda i: (0, i, 0))] * n_pat
        + [pl.BlockSpec((heads, tm), lambda i: (0, i))] * n_pat,
        out_specs=pl.BlockSpec((tm, heads * HEAD_DIM), lambda i: (i, 0)),
        out_shape=jax.ShapeDtypeStruct((N, heads * HEAD_DIM), BF16),
        compiler_params=_params(("arbitrary",), 32),
        name="attn_dilated_mix",
    )(*os, *lses)


def _attn_a(qkv, slopes, B, S, n_heads, heads_a):
    parts = [_attn_a_pattern(qkv, slopes, B, S, n_heads, heads_a, w, d) for w, d in DILATED_PATTERNS]
    return _mix_a([p[0] for p in parts], [p[1] for p in parts])


def _attn_b_kernel(slopes_ref, lam_ref, q_ref, k_ref, v_ref, g_ref, o_ref, tab_ref, kmax_ref,
                   m_scr, l_scr, acc_scr, *, S, Tq, Tk, lambda_init):
    h = pl.program_id(1)
    i = pl.program_id(2)
    dh = HEAD_DIM // 2
    n_chunks = S // Tk
    c_dist = LOG2E * slopes_ref[h]

    def sub_norm(x):
        sq = x * x
        return jnp.sqrt(jnp.maximum(jnp.sum(sq[:, :dh], axis=-1, keepdims=True),
                                    jnp.sum(sq[:, dh:], axis=-1, keepdims=True)))

    @pl.when(i == 0)
    def _():
        shape = (Tq, 2 * S - Tq)
        a = lax.broadcasted_iota(I32, shape, 0)
        u = lax.broadcasted_iota(I32, shape, 1)
        d = a - u + (S - Tq)
        tab_ref[...] = -c_dist * jnp.maximum(d, -d).astype(F32)
        kn = sub_norm(k_ref[0].astype(F32))
        for j in range(n_chunks):
            kmax_ref[j] = jnp.max(kn[j * Tk:(j + 1) * Tk])

    lv = lam_ref[...]
    lam = (jnp.exp(jnp.sum(lv[0:1] * lv[1:2], axis=-1, keepdims=True))
           - jnp.exp(jnp.sum(lv[2:3] * lv[3:4], axis=-1, keepdims=True)) + lambda_init)

    t0 = i * Tq
    qf = q_ref[0].astype(F32) * (LOG2E * dh ** -0.5)
    q = qf.astype(BF16)
    qmax = jnp.max(sub_norm(q.astype(F32)))

    def chunk(kstart, first):
        kc = k_ref[0, pl.ds(kstart, Tk), :]
        vc = v_ref[0, pl.ds(kstart, Tk), :]
        bias = tab_ref[:, pl.ds(pl.multiple_of(S - Tq - t0 + kstart, LANES), Tk)]
        s = jnp.concatenate(
            [lax.dot_general(q[:, c * dh:(c + 1) * dh], kc[:, c * dh:(c + 1) * dh], _NT,
                             preferred_element_type=F32) + bias for c in range(2)], axis=0)
        mc = jnp.max(s, axis=-1, keepdims=True)
        if first:
            m_new = mc
        else:
            m_old = m_scr[...]
            m_new = jnp.maximum(m_old, mc)
            alpha = jnp.exp2(m_old - m_new)
        p = jnp.exp2(s - m_new)
        ls = jnp.sum(p, axis=-1, keepdims=True)
        pv = jnp.dot(p.astype(BF16), vc, preferred_element_type=F32)
        m_scr[...] = m_new
        l_scr[...] = ls if first else alpha * l_scr[...] + ls
        acc_scr[...] = pv if first else alpha * acc_scr[...] + pv
        return m_new

    jd = t0 // Tk
    m_diag = chunk(pl.multiple_of(jd * Tk, Tk), True)
    floor = jnp.min(m_diag) - 160.0
    for j in range(n_chunks):
        gap = jnp.maximum(jnp.maximum(j * Tk - (t0 + Tq - 1), t0 - ((j + 1) * Tk - 1)), 0)
        bound = 1.001 * qmax * kmax_ref[j] + 0.01 - c_dist * gap.astype(F32)

        @pl.when((j != jd) & (bound > floor))
        def _():
            chunk(j * Tk, False)

    acc = acc_scr[...]
    l = l_scr[...]
    o = acc[:Tq] / l[:Tq] - acc[Tq:] * (lam / l[Tq:])
    o = _rms(o, SUBLN_EPS) * g_ref[...] * (1.0 - lambda_init)
    o_ref[...] = o.astype(BF16)


def _attn_b(qkv, slopes, lam_vecs, subln_g, B, S, n_heads, head0, heads_b, lambda_init, Tq=256,
            Tk=1024):
    N = B * S
    nq = S // Tq
    Tk = min(Tk, S)
    return pl.pallas_call(
        functools.partial(_attn_b_kernel, S=S, Tq=Tq, Tk=Tk, lambda_init=lambda_init),
        grid=(B, heads_b, nq),
        in_specs=[pl.BlockSpec(memory_space=pltpu.SMEM),
                  pl.BlockSpec((4, HEAD_DIM // 2), lambda b, h, i: (0, 0)),
                  pl.BlockSpec((1, Tq, HEAD_DIM), lambda b, h, i: (head0 + h, b * nq + i, 0)),
                  pl.BlockSpec((1, S, HEAD_DIM), lambda b, h, i: (n_heads + head0 + h, b, 0)),
                  pl.BlockSpec((1, S, HEAD_DIM), lambda b, h, i: (2 * n_heads + head0 + h, b, 0)),
                  pl.BlockSpec((1, HEAD_DIM), lambda b, h, i: (0, 0))],
        out_specs=pl.BlockSpec((Tq, HEAD_DIM), lambda b, h, i: (b * nq + i, h)),
        out_shape=jax.ShapeDtypeStruct((N, heads_b * HEAD_DIM), BF16),
        scratch_shapes=[pltpu.VMEM((Tq, 2 * S - Tq), F32), pltpu.SMEM((S // Tk,), F32),
                        pltpu.VMEM((2 * Tq, 1), F32), pltpu.VMEM((2 * Tq, 1), F32),
                        pltpu.VMEM((2 * Tq, HEAD_DIM), F32)],
        compiler_params=_params(("arbitrary", "arbitrary", "arbitrary"), 48),
        name="attn_diff",
    )(slopes, lam_vecs, qkv, qkv, qkv, subln_g.reshape(1, HEAD_DIM))


NBR_GROUP = 4


def _nbr_pairs(R):
    kr, win = NA_ROWS, NA_ROWS + NBR_GROUP
    pairs = []
    for r0 in range(0, R, NBR_GROUP):
        ps = min(max(r0 - kr // 2, 0), R - win)
        rel = tuple(min(max(r0 + a - kr // 2, 0), R - kr) - ps for a in range(NBR_GROUP))
        pairs.append((r0, ps, (ps - r0,) + rel))
    return pairs, sorted({p[2] for p in pairs})


def _attn_c_kernel(rpb_ref, q_ref, k_ref, v_ref, o_ref, tab_ref, *, R):
    h = pl.program_id(1)
    W = GRID_W
    kr = min(NA_ROWS, R)
    n_dr = 2 * NA_ROWS - 1
    n_dc = 2 * NA_COLS - 1

    c_io = lax.broadcasted_iota(I32, (W, 2 * W), 0)
    l_io = lax.broadcasted_iota(I32, (W, 2 * W), 1)
    cp = l_io & (W - 1)
    dcm = cp - c_io + (NA_COLS - 1)
    cstart = jnp.clip(c_io - NA_COLS // 2, 0, W - NA_COLS)
    ok = (cp >= cstart) & (cp < cstart + NA_COLS)
    blocks = []
    for dr in range(n_dr):
        blk = jnp.zeros((W, 2 * W), F32)
        for dc in range(n_dc):
            blk = jnp.where(dcm == dc, rpb_ref[(h * n_dr + dr) * n_dc + dc], blk)
        blocks.append(jnp.where(ok, blk, NEG_INF))
    win = kr + NBR_GROUP
    pairs, cfgs = _nbr_pairs(R)
    neg = jnp.full((W, 2 * W), NEG_INF, F32)
    for ti, (off, *rels) in enumerate(cfgs):
        for a, rel in enumerate(rels):
            def key_row_block(j):
                return blocks[off + j - a + (NA_ROWS - 1)] if rel <= j < rel + kr else neg
            for jj in range(win // 2):
                tab_ref[ti, a * W:(a + 1) * W, jj * 2 * W:(jj + 1) * 2 * W] = jnp.where(
                    l_io < W, key_row_block(2 * jj), key_row_block(2 * jj + 1))

    for r0, ps, cfg in pairs:
        rows = slice(r0 * W, (r0 + NBR_GROUP) * W)
        kw = k_ref[0, ps * W:(ps + win) * W, :]
        vw = v_ref[0, ps * W:(ps + win) * W, :]
        s = lax.dot_general(q_ref[0, rows, :], kw, _NT, preferred_element_type=F32)
        s = s * (HEAD_DIM ** -0.5) + tab_ref[cfgs.index(cfg)]
        m = jnp.max(s, axis=-1, keepdims=True)
        p = jnp.exp(s - m)
        l = jnp.sum(p, axis=-1, keepdims=True)
        o = jnp.dot(p.astype(BF16), vw, preferred_element_type=F32) / l
        o_ref[rows, :] = o.astype(BF16)


def _attn_c(qkv, rpb_flat, B, S, n_heads, head0, heads_c):
    N = B * S
    R = S // GRID_W
    kr = NA_ROWS
    assert R % NBR_GROUP == 0 and R >= kr + NBR_GROUP and (kr + NBR_GROUP) % 2 == 0
    n_tabs = len(_nbr_pairs(R)[1])
    return pl.pallas_call(
        functools.partial(_attn_c_kernel, R=R),
        grid=(B, heads_c),
        in_specs=[pl.BlockSpec(memory_space=pltpu.SMEM),
                  pl.BlockSpec((1, S, HEAD_DIM), lambda b, h: (head0 + h, b, 0)),
                  pl.BlockSpec((1, S, HEAD_DIM), lambda b, h: (n_heads + head0 + h, b, 0)),
                  pl.BlockSpec((1, S, HEAD_DIM), lambda b, h: (2 * n_heads + head0 + h, b, 0))],
        out_specs=pl.BlockSpec((S, HEAD_DIM), lambda b, h: (b, h)),
        out_shape=jax.ShapeDtypeStruct((N, heads_c * HEAD_DIM), BF16),
        scratch_shapes=[pltpu.VMEM((n_tabs, NBR_GROUP * GRID_W, (kr + NBR_GROUP) * GRID_W), F32)],
        compiler_params=_params(("arbitrary", "arbitrary"), 32),
        name="attn_nbr",
    )(rpb_flat, qkv, qkv, qkv)


def _out_kernel(oa_ref, ob_ref, oc_ref, w_ref, x_ref, g1_ref, ng_ref, sc_ref, sh_ref, rw_ref,
                x1_ref, h2_ref, lg_ref):
    o = jnp.concatenate([oa_ref[...], ob_ref[...], oc_ref[...]], axis=1)
    acc = jnp.dot(o, w_ref[...], preferred_element_type=F32)
    x1 = x_ref[...] + g1_ref[0] * acc
    x1_ref[...] = x1
    h2 = (_rms(x1, NORM_EPS) * ng_ref[...]) * (1.0 + sc_ref[0]) + sh_ref[0]
    _store_row_slabs(h2_ref, _pack_pairs(h2))
    lg_ref[...] = lax.dot_general(rw_ref[...], h2, _NT, preferred_element_type=F32,
                                  precision=HIGHEST)


def _out_proj(oa, ob, oc, w_bf, x2d, g1, ng, sc, sh, rw_t, S, tm=512):
    N, D = x2d.shape
    wa, wb, wc = oa.shape[1], ob.shape[1], oc.shape[1]
    E = rw_t.shape[0]
    tpb = S // tm
    const = dict(pipeline_mode=pl.Buffered(1))
    row = lambda i: (i, 0)
    per_b = lambda i: (i // tpb, 0, 0)
    return pl.pallas_call(
        _out_kernel,
        grid=(N // tm,),
        in_specs=[pl.BlockSpec((tm, wa), row), pl.BlockSpec((tm, wb), row), pl.BlockSpec((tm, wc), row),
                  pl.BlockSpec((wa + wb + wc, D), lambda i: (0, 0), **const),
                  pl.BlockSpec((tm, D), row),
                  pl.BlockSpec((1, 1, D), per_b),
                  pl.BlockSpec((1, D), lambda i: (0, 0)),
                  pl.BlockSpec((1, 1, D), per_b),
                  pl.BlockSpec((1, 1, D), per_b),
                  pl.BlockSpec((E, D), lambda i: (0, 0), **const)],
        out_specs=[pl.BlockSpec((tm, D), row), pl.BlockSpec((tm * (D // 2 // LANES), LANES), row),
                   pl.BlockSpec((E, tm), lambda i: (0, i))],
        out_shape=[jax.ShapeDtypeStruct((N, D), F32),
                   jax.ShapeDtypeStruct((N * (D // 2 // LANES), LANES), U32),
                   jax.ShapeDtypeStruct((E, N), F32)],
        compiler_params=_params(("arbitrary",), 52),
        name="out_proj",
    )(oa, ob, oc, w_bf, x2d, g1, ng.reshape(1, D), sc, sh, rw_t)


def _first_argmax(vals, iota, big):
    mx = jnp.max(vals, axis=0, keepdims=True)
    idx = jnp.min(jnp.where(vals == mx, iota, big), axis=0, keepdims=True)
    return mx, idx


def _route_kernel(lg_ref, rb_ref, idx_ref, gate_ref, rank_ref, cnt_ref, tri_ref, carry_ref, *, Tt):
    i = pl.program_id(0)
    E = lg_ref.shape[0]
    gs = E // N_GROUPS

    @pl.when(i == 0)
    def _():
        r = lax.broadcasted_iota(I32, (Tt, Tt), 0)
        c = lax.broadcasted_iota(I32, (Tt, Tt), 1)
        tri_ref[...] = jnp.where(r < c, 1.0, 0.0).astype(BF16)
        carry_ref[...] = jnp.zeros_like(carry_ref)

    scores = jax.nn.sigmoid(lg_ref[...])
    sel = scores + rb_ref[...]
    e_io = lax.broadcasted_iota(I32, (E, Tt), 0).astype(F32)
    s_io = lax.broadcasted_iota(I32, (gs, Tt), 0).astype(F32)
    g_io = lax.broadcasted_iota(I32, (N_GROUPS, Tt), 0).astype(F32)

    grp = []
    for g in range(N_GROUPS):
        blk = sel[g * gs:(g + 1) * gs]
        m1, a1 = _first_argmax(blk, s_io, float(gs))
        m2 = jnp.max(jnp.where(s_io == a1, -jnp.inf, blk), axis=0, keepdims=True)
        grp.append(m1 + m2)
    grp = jnp.concatenate(grp, axis=0)

    gsel = jnp.zeros((N_GROUPS, Tt), F32)
    for _ in range(TOPK_GROUPS):
        _, gi = _first_argmax(grp, g_io, float(N_GROUPS))
        hit = g_io == gi
        gsel = jnp.where(hit, 1.0, gsel)
        grp = jnp.where(hit, -jnp.inf, grp)

    esel = jnp.concatenate(
        [jnp.broadcast_to(gsel[g:g + 1], (gs, Tt)) for g in range(N_GROUPS)], axis=0)
    cur = jnp.where(esel > 0.0, sel, NEG_INF)

    idxs, gates = [], []
    onehot = jnp.zeros((E, Tt), F32)
    for _ in range(TOP_K):
        _, ei = _first_argmax(cur, e_io, float(E))
        hit = e_io == ei
        idxs.append(ei)
        gates.append(jnp.sum(jnp.where(hit, scores, 0.0), axis=0, keepdims=True))
        onehot = jnp.where(hit, 1.0, onehot)
        cur = jnp.where(hit, -jnp.inf, cur)
    gate = jnp.concatenate(gates, axis=0)
    gate = gate / jnp.sum(gate, axis=0, keepdims=True) * ROUTED_SCALE
    idx_ref[...] = jnp.concatenate(idxs, axis=0).astype(I32)
    gate_ref[...] = gate

    oh = onehot.astype(BF16)
    before = jnp.dot(oh, tri_ref[...], preferred_element_type=F32)
    tile_cnt = jnp.dot(oh, jnp.ones((Tt, LANES), BF16), preferred_element_type=F32)
    carry = carry_ref[...]
    rank_mat = before + jnp.concatenate([carry] * (Tt // LANES), axis=1)
    ranks = [jnp.sum(jnp.where(e_io == ei, rank_mat, 0.0), axis=0, keepdims=True) for ei in idxs]
    rank_ref[...] = jnp.concatenate(ranks, axis=0).astype(I32)
    carry_ref[...] = carry + tile_cnt
    cnt_ref[...] = (carry + tile_cnt).astype(I32)


def _route(logits_t, router_bias, Tt=512):
    E, N = logits_t.shape
    tok = lambda i: (0, i)
    return pl.pallas_call(
        functools.partial(_route_kernel, Tt=Tt),
        grid=(N // Tt,),
        in_specs=[pl.BlockSpec((E, Tt), tok), pl.BlockSpec((E, 1), lambda i: (0, 0))],
        out_specs=[pl.BlockSpec((TOP_K, Tt), tok), pl.BlockSpec((TOP_K, Tt), tok),
                   pl.BlockSpec((TOP_K, Tt), tok), pl.BlockSpec((E, LANES), lambda i: (0, 0))],
        out_shape=[jax.ShapeDtypeStruct((TOP_K, N), I32), jax.ShapeDtypeStruct((TOP_K, N), F32),
                   jax.ShapeDtypeStruct((TOP_K, N), I32), jax.ShapeDtypeStruct((E, LANES), I32)],
        scratch_shapes=[pltpu.VMEM((Tt, Tt), BF16), pltpu.VMEM((E, LANES), F32)],
        compiler_params=_params(("arbitrary",), 32),
        name="route_topk",
    )(logits_t, router_bias.reshape(E, 1))


def _dest_kernel(pstart_ref, idx_ref, rank_ref, dest_ref, *, G):
    idx = idx_ref[...]
    base = jnp.zeros(idx.shape, I32)
    for e in range(pstart_ref.shape[0]):
        base = jnp.where(idx == e, pstart_ref[e], base)
    dest_ref[...] = (base + rank_ref[...]) * G


def _dest(pstart, idx, rank, G):
    return pl.pallas_call(
        functools.partial(_dest_kernel, G=G),
        in_specs=[pl.BlockSpec(memory_space=pltpu.SMEM), pl.BlockSpec(memory_space=pltpu.VMEM),
                  pl.BlockSpec(memory_space=pltpu.VMEM)],
        out_specs=pl.BlockSpec(memory_space=pltpu.VMEM),
        out_shape=jax.ShapeDtypeStruct(idx.shape, I32),
        name="route_dest",
    )(pstart, idx, rank)


def _dispatch_kernel(dest_hbm, h_ref, xs_hbm, dest_smem, idx_sem, row_sem, *, Td, G):
    i = pl.program_id(0)
    cp = pltpu.make_async_copy(dest_hbm.at[:, pl.ds(i * Td, Td)], dest_smem, idx_sem)
    cp.start()
    cp.wait()

    def row_copy(t, k):
        src = h_ref.at[pl.ds(pl.multiple_of(t * G, G), G)]
        dst = xs_hbm.at[pl.ds(pl.multiple_of(dest_smem[k, t], G), G)]
        return pltpu.make_async_copy(src, dst, row_sem)

    def issue(t, carry):
        for k in range(TOP_K):
            row_copy(t, k).start(priority=k % 2)
        return carry

    lax.fori_loop(0, Td, issue, 0)

    def drain(t, carry):
        for k in range(TOP_K):
            row_copy(t, k).wait()
        return carry

    lax.fori_loop(0, Td, drain, 0)


def _dispatch(dest, h2, P, G, Td=512):
    N = h2.shape[0] // G
    return pl.pallas_call(
        functools.partial(_dispatch_kernel, Td=Td, G=G),
        grid=(N // Td,),
        in_specs=[pl.BlockSpec(memory_space=pl.ANY), pl.BlockSpec((Td * G, LANES), lambda i: (i, 0))],
        out_specs=pl.BlockSpec(memory_space=pl.ANY),
        out_shape=jax.ShapeDtypeStruct((P * G, LANES), h2.dtype),
        scratch_shapes=[pltpu.SMEM((TOP_K, Td), I32), pltpu.SemaphoreType.DMA,
                        pltpu.SemaphoreType.DMA],
        compiler_params=_params(("arbitrary",), 32),
        name="moe_dispatch",
    )(dest, h2)


def _moe_kernel(be_ref, nv_ref, nb_ref, xs_ref, wg_ref, wu_ref, wd_ref, ys_ref, wg_s, wu_s, wd_s):
    b = pl.program_id(0)
    active = b < nb_ref[0]
    new_expert = (b == 0) | (be_ref[b] != be_ref[jnp.maximum(b - 1, 0)])

    @pl.when(active & new_expert)
    def _():
        wg_s[...] = wg_ref[0].astype(BF16)
        wu_s[...] = wu_ref[0].astype(BF16)
        wd_s[...] = wd_ref[0].astype(BF16)

    @pl.when(active)
    def _():
        T = MOE_BLOCK
        G = xs_ref.shape[0] // T
        valid = lax.broadcasted_iota(I32, (T, LANES), 0) < nv_ref[b]
        his, los = [], []
        for c in range(G):
            w = jnp.where(valid, _load_row_slab(xs_ref, c, T, G), jnp.uint32(0))
            hi, lo = _unpack_pairs(w)
            his.append(hi.astype(BF16))
            los.append(lo.astype(BF16))
        x = jnp.concatenate(his + los, axis=1)
        g = jnp.dot(x, wg_s[...], preferred_element_type=F32)
        u = jnp.dot(x, wu_s[...], preferred_element_type=F32)
        a = (g * jax.nn.sigmoid(g) * u).astype(BF16)
        _store_row_slabs(ys_ref, _pack_pairs(jnp.dot(a, wd_s[...], preferred_element_type=F32)))


def _moe(block_expert, nvalid, nblocks, xs, wg, wu, wd):
    _, D, F = wg.shape
    G = D // 2 // LANES
    P = xs.shape[0] // G
    n_blocks = P // MOE_BLOCK
    blk = lambda b, be, nv, nb: (jnp.minimum(b, nb[0] - 1), 0)
    return pl.pallas_call(
        _moe_kernel,
        grid_spec=pltpu.PrefetchScalarGridSpec(
            num_scalar_prefetch=3,
            grid=(n_blocks,),
            in_specs=[pl.BlockSpec((MOE_BLOCK * G, LANES), blk),
                      pl.BlockSpec((1, D, F), lambda b, be, nv, nb: (be[b], 0, 0)),
                      pl.BlockSpec((1, D, F), lambda b, be, nv, nb: (be[b], 0, 0)),
                      pl.BlockSpec((1, F, D), lambda b, be, nv, nb: (be[b], 0, 0))],
            out_specs=pl.BlockSpec((MOE_BLOCK * G, LANES), blk),
            scratch_shapes=[pltpu.VMEM((D, F), BF16), pltpu.VMEM((D, F), BF16),
                            pltpu.VMEM((F, D), BF16)]),
        out_shape=jax.ShapeDtypeStruct((P * G, LANES), U32),
        compiler_params=_params(("arbitrary",), 52),
        name="moe_experts",
    )(block_expert, nvalid, nblocks, xs, wg, wu, wd)


def _combine_kernel(dest_hbm, ys_hbm, gate_ref, h_ref, x_ref, g2_ref, sg_ref, su_ref, sd_ref, fg_ref,
                    o_ref, dest0, dest1, ybuf0, ybuf1, idx_sem, row_sem, *, Tc, G, final_norm):
    j = pl.program_id(0)
    slots = ((dest0, ybuf0), (dest1, ybuf1))

    def row_copy(slot, t, k):
        dest, ybuf = slots[slot]
        src = ys_hbm.at[pl.ds(pl.multiple_of(dest[k, t], G), G)]
        dst = ybuf.at[k, pl.ds(pl.multiple_of(t * G, G), G)]
        return pltpu.make_async_copy(src, dst, row_sem.at[slot])

    def request(tile, slot):
        cp = pltpu.make_async_copy(dest_hbm.at[:, pl.ds(tile * Tc, Tc)], slots[slot][0],
                                   idx_sem.at[slot])
        cp.start()
        cp.wait()

        def issue(t, carry):
            for k in range(TOP_K):
                row_copy(slot, t, k).start(priority=k % 2)
            return carry

        lax.fori_loop(0, Tc, issue, 0)

    def reduce(slot):
        rows = slice(slot * Tc, (slot + 1) * Tc)
        h_tile = h_ref.at[pl.ds(slot * Tc * G, Tc * G)]
        halves = [_unpack_pairs(_load_row_slab(h_tile, c, Tc, G)) for c in range(G)]
        h = jnp.concatenate([p[0].astype(BF16) for p in halves]
                            + [p[1].astype(BF16) for p in halves], axis=1)
        g = jnp.dot(h, sg_ref[...], preferred_element_type=F32)
        u = jnp.dot(h, su_ref[...], preferred_element_type=F32)
        a = (g * jax.nn.sigmoid(g) * u).astype(BF16)
        f = jnp.dot(a, sd_ref[...], preferred_element_type=F32)

        def drain(t, carry):
            for k in range(TOP_K):
                row_copy(slot, t, k).wait()
            return carry

        lax.fori_loop(0, Tc, drain, 0)
        pad = jnp.zeros((LANES - TOP_K, LANES), F32)
        gate = jnp.concatenate(
            [jnp.concatenate([gate_ref[:, slot * Tc + c * LANES:slot * Tc + (c + 1) * LANES], pad],
                             axis=0).T for c in range(Tc // LANES)], axis=0)
        half = f.shape[1] // 2
        his, los = [], []
        for c in range(G):
            f_hi = f[:, c * LANES:(c + 1) * LANES]
            f_lo = f[:, half + c * LANES:half + (c + 1) * LANES]
            for k in range(TOP_K):
                y_hi, y_lo = _unpack_pairs(_load_row_slab(slots[slot][1].at[k], c, Tc, G))
                f_hi = f_hi + gate[:, k:k + 1] * y_hi
                f_lo = f_lo + gate[:, k:k + 1] * y_lo
            his.append(f_hi)
            los.append(f_lo)
        x2 = x_ref[rows, :] + g2_ref[0] * jnp.concatenate(his + los, axis=1)
        if final_norm:
            x2 = _rms(x2, NORM_EPS) * fg_ref[...]
        o_ref[rows, :] = x2

    @pl.when(j == 0)
    def _():
        request(0, 0)

    request(2 * j + 1, 1)
    reduce(0)

    @pl.when(j + 1 < pl.num_programs(0))
    def _():
        request(2 * j + 2, 0)

    reduce(1)


def _combine(dest, ys, gate, h2, x1, g2, sg, su, sd, final_g, S, final_norm, Tc=256):
    N, D = x1.shape
    F = sg.shape[1]
    G = D // 2 // LANES
    tb = 2 * Tc
    tpb = S // tb
    row = lambda j: (j, 0)
    const = dict(pipeline_mode=pl.Buffered(1))
    return pl.pallas_call(
        functools.partial(_combine_kernel, Tc=Tc, G=G, final_norm=final_norm),
        grid=(N // tb,),
        in_specs=[pl.BlockSpec(memory_space=pl.ANY), pl.BlockSpec(memory_space=pl.ANY),
                  pl.BlockSpec((TOP_K, tb), lambda j: (0, j)),
                  pl.BlockSpec((tb * G, LANES), row), pl.BlockSpec((tb, D), row),
                  pl.BlockSpec((1, 1, D), lambda j: (j // tpb, 0, 0)),
                  pl.BlockSpec((D, F), lambda j: (0, 0), **const),
                  pl.BlockSpec((D, F), lambda j: (0, 0), **const),
                  pl.BlockSpec((F, D), lambda j: (0, 0), **const),
                  pl.BlockSpec((1, D), lambda j: (0, 0))],
        out_specs=pl.BlockSpec((tb, D), row),
        out_shape=jax.ShapeDtypeStruct((N, D), F32),
        scratch_shapes=[pltpu.SMEM((TOP_K, Tc), I32), pltpu.SMEM((TOP_K, Tc), I32),
                        pltpu.VMEM((TOP_K, Tc * G, LANES), U32),
                        pltpu.VMEM((TOP_K, Tc * G, LANES), U32),
                        pltpu.SemaphoreType.DMA((2,)), pltpu.SemaphoreType.DMA((2,))],
        compiler_params=_params(("arbitrary",), 52),
        name="moe_combine",
    )(dest, ys, gate, h2, x1, g2, sg, su, sd, final_g.reshape(1, D))


def _moe_ffn(h2, logits_t, x1, g2, router_bias, wg, wu, wd, e_off, sg, su, sd, final_g, S,
             final_norm):
    N, D = x1.shape
    G = D // 2 // LANES
    E = logits_t.shape[0]
    idx, gate, rank, cnt = _route(logits_t, router_bias)
    counts = cnt[:, 0]
    padded = (counts + MOE_BLOCK - 1) // MOE_BLOCK * MOE_BLOCK
    pend = jnp.cumsum(padded)
    pstart = (pend - padded).astype(I32)
    n_blocks = -(-(N * TOP_K + E * (MOE_BLOCK - 1)) // MOE_BLOCK)
    bstart = jnp.arange(n_blocks, dtype=I32) * MOE_BLOCK
    block_expert = jnp.minimum(jnp.sum(pend[None, :] <= bstart[:, None], axis=1), E - 1).astype(I32)
    nvalid = jnp.clip(pstart[block_expert] + counts[block_expert] - bstart, 0, MOE_BLOCK).astype(I32)
    nblocks = (pend[-1:] // MOE_BLOCK).astype(I32)

    dest = _dest(pstart, idx, rank, G)
    xs = _dispatch(dest, h2, n_blocks * MOE_BLOCK, G)
    ys = _moe(block_expert + e_off, nvalid, nblocks, xs, wg, wu, wd)
    return _combine(dest, ys, gate, h2, x1, g2, sg, su, sd, final_g, S, final_norm)


def kernel(x, c, ada_w, ada_b, norm_mix_g, norm_ffn_g, w_in, lambda_q1, lambda_k1, lambda_q2,
           lambda_k2, subln_g, rpb, w_out, router_w, router_bias, exp_w_gate, exp_w_up,
           exp_w_down, sh_w_gate, sh_w_up, sh_w_down, final_g):
    B, S, D = x.shape
    L = ada_w.shape[0]
    N = B * S
    n_heads = w_in.shape[2] // (3 * HEAD_DIM)
    heads_a = n_heads // HEADS_A_FRAC
    heads_b = n_heads // HEADS_B_FRAC
    heads_c = n_heads - heads_a - heads_b
    slopes = _alibi_slopes(heads_a + heads_b)
    slopes_a = jnp.asarray(slopes[:heads_a], F32)
    slopes_b = jnp.asarray(slopes[heads_a:], F32)

    E, F = exp_w_gate.shape[1], exp_w_gate.shape[3]
    wg_all = exp_w_gate.reshape(L * E, D, F)
    wu_all = exp_w_up.reshape(L * E, D, F)
    wd_all = exp_w_down.reshape(L * E, F, D)

    mod = _ada(c, ada_w, ada_b).reshape(L, B, 6, 1, D)
    xf = x.reshape(N, D)
    for l in range(L):
        sh1, sc1, g1, sh2, sc2, g2 = (mod[l, :, j] for j in range(6))
        qkv = _qkv(xf, norm_mix_g[l], sc1, sh1, _layer_bf16(w_in, l), S)
        oa = _attn_a(qkv, slopes_a, B, S, n_heads, heads_a)
        lambda_init = 0.8 - 0.6 * math.exp(-0.3 * l)
        lam_vecs = jnp.stack([lambda_q1[l], lambda_k1[l], lambda_q2[l], lambda_k2[l]]).astype(F32)
        ob = _attn_b(qkv, slopes_b, lam_vecs, subln_g[l], B, S, n_heads, heads_a, heads_b, lambda_init)
        oc = _attn_c(qkv, rpb[l].reshape(-1), B, S, n_heads, heads_a + heads_b, heads_c)
        x1, h2, logits_t = _out_proj(oa, ob, oc, _layer_bf16(w_out, l), xf, g1, norm_ffn_g[l],
                                     sc2, sh2, router_w[l].T, S)
        xf = _moe_ffn(h2, logits_t, x1, g2, router_bias[l], wg_all, wu_all, wd_all, l * E,
                      _layer_bf16(sh_w_gate, l), _layer_bf16(sh_w_up, l),
                      _layer_bf16(sh_w_down, l), final_g, S, l == L - 1)
    return xf.reshape(B, S, D)
```
